```python
import jax, jax.numpy as jnp
from jax import lax
import numpy as np

D_MODEL = 1024
BATCH = 8
SEQ = 2048
DEPTH = 4
DEC_BATCH = 16
DEC_SEQ = 2048
PAST_LEN = 128

N_MIXERS = 3
ALPHA = (2 * DEPTH) ** 0.25
BETA = (8 * DEPTH) ** -0.25
LN_EPS = 1e-5
ADA_SCALE = 0.5
N_CONV_LAYERS = (DEPTH + 2) // 3
N_ATTN_LAYERS = (DEPTH + 1) // 3
N_RWKV_LAYERS = DEPTH // 3
N_DENSE_LAYERS = (DEPTH + 1) // 2
N_MOE_LAYERS = DEPTH // 2

CONV_WIDTH = 3

GRID_W = 64
NA_HEADS = 16
NA_HEAD_DIM = D_MODEL // NA_HEADS
NA_WIN_ROWS = 8
NA_WIN_COLS = 16
NA_COL_BLOCK = 16
NA_N_COL_BLOCKS = GRID_W // NA_COL_BLOCK
NA_KEY_COLS = NA_COL_BLOCK + NA_WIN_COLS
NEG_INF = -1e30

RK_HEAD_DIM = 64
RK_HEADS = D_MODEL // RK_HEAD_DIM
RK_DECAY_LORA = 64
RK_AAA_LORA = 64
RK_GATE_LORA = 128
RK_GN_EPS = 64e-5

D_FF = 2816
N_EXPERTS = 8
TOP_K = 2
D_FF_EXPERT = D_FF // TOP_K

kernel_name = 'hybrid_conv_natten_rwkv7_encoder'


def layer_norm(x, g, b):
    xf = x.astype(jnp.float32)
    mu = jnp.mean(xf, axis=-1, keepdims=True)
    var = jnp.mean(jnp.square(xf - mu), axis=-1, keepdims=True)
    return ((xf - mu) * lax.rsqrt(var + LN_EPS) * g + b).astype(x.dtype)


def swiglu(h, w_gu, w_down):
    g, u = jnp.split(h @ w_gu, 2, axis=-1)
    return (jax.nn.silu(g) * u) @ w_down


def moe_swiglu(h, w_router, b_router, w_gu, w_down):
    logits = (h @ w_router + b_router).astype(jnp.float32)
    probs = jax.nn.softmax(logits, axis=-1)
    top_p, top_i = lax.top_k(probs, TOP_K)
    top_p = top_p / jnp.sum(top_p, axis=-1, keepdims=True)
    comb = jnp.sum(jax.nn.one_hot(top_i, N_EXPERTS, dtype=jnp.float32) * top_p[..., None], axis=-2).astype(h.dtype)
    out = comb[..., 0:1] * swiglu(h, w_gu[0], w_down[0])
    for e in range(1, N_EXPERTS):
        out = out + comb[..., e:e + 1] * swiglu(h, w_gu[e], w_down[e])
    return out


def short_conv_mixer(h, w_in, conv_w, w_out):
    b_gate, c_gate, u = jnp.split(h @ w_in, 3, axis=-1)
    z = c_gate * u
    n = z.shape[1]
    half = CONV_WIDTH // 2
    zp = jnp.pad(z, ((0, 0), (half, half), (0, 0)))
    conv = zp[:, 0:n] * conv_w[0]
    for j in range(1, CONV_WIDTH):
        conv = conv + zp[:, j:j + n] * conv_w[j]
    return (b_gate * conv) @ w_out


def neighbourhood_attention(h, w_qkv, rpb, w_o):
    bsz, n, _ = h.shape
    rows = n // GRID_W
    win_r = min(NA_WIN_ROWS, rows)
    q, k, v = jnp.split(h @ w_qkv, 3, axis=-1)
    grid = (bsz, rows, GRID_W, NA_HEADS, NA_HEAD_DIM)
    q = q.reshape(grid) * (NA_HEAD_DIM ** -0.5)
    k = k.reshape(grid)
    v = v.reshape(grid)
    qcol = jnp.arange(GRID_W).reshape(NA_N_COL_BLOCKS, NA_COL_BLOCK)
    col_start = jnp.clip(qcol - NA_WIN_COLS // 2, 0, GRID_W - NA_WIN_COLS)
    key_col0 = jnp.clip(qcol[:, 0] - NA_WIN_COLS // 2, 0, GRID_W - NA_KEY_COLS)
    key_cols = key_col0[:, None] + jnp.arange(NA_KEY_COLS)
    kc = key_cols[:, None, :]
    col_ok = (kc >= col_start[:, :, None]) & (kc < col_start[:, :, None] + NA_WIN_COLS)
    dc_idx = jnp.clip(kc - qcol[:, :, None] + NA_WIN_COLS - 1, 0, 2 * NA_WIN_COLS - 2)
    mask = col_ok[:, None, :, None, :]

    def row_block(r):
        r0 = jnp.clip(r - win_r // 2, 0, rows - win_r)
        q_r = lax.dynamic_index_in_dim(q, r, axis=1, keepdims=False)
        q_r = q_r.reshape(bsz, NA_N_COL_BLOCKS, NA_COL_BLOCK, NA_HEADS, NA_HEAD_DIM)
        k_g = lax.dynamic_slice_in_dim(k, r0, win_r, axis=1)[:, :, key_cols]
        v_g = lax.dynamic_slice_in_dim(v, r0, win_r, axis=1)[:, :, key_cols]
        dr_idx = r0 + jnp.arange(win_r) - r + NA_WIN_ROWS - 1
        bias = rpb[:, dr_idx[None, None, :, None], dc_idx[:, :, None, :]]
        bias = bias.transpose(1, 0, 2, 3, 4)
        s = jnp.einsum('bcqhd,bwckhd->bchqwk', q_r, k_g).astype(jnp.float32) + bias
        s = jnp.where(mask, s, NEG_INF)
        p = jax.nn.softmax(s.reshape(s.shape[:-2] + (win_r * NA_KEY_COLS,)), axis=-1)
        p = p.reshape(s.shape).astype(v.dtype)
        o = jnp.einsum('bchqwk,bwckhd->bcqhd', p, v_g)
        return o.reshape(bsz, GRID_W, NA_HEADS * NA_HEAD_DIM)

    out = lax.map(row_block, jnp.arange(rows))
    out = jnp.moveaxis(out, 0, 1).reshape(bsz, n, NA_HEADS * NA_HEAD_DIM)
    return out @ w_o


def _heads(t):
    return t.reshape(t.shape[:-1] + (RK_HEADS, RK_HEAD_DIM))


def _rwkv_step(state, inp):
    r_t, w_t, k_t, v_t, a_t, b_t = inp
    sa = jnp.einsum('zbhvk,zbhk->zbhv', state, a_t)
    state = state * w_t[..., None, :] + sa[..., None] * b_t[..., None, :] + v_t[..., None] * k_t[..., None, :]
    return state, jnp.einsum('zbhvk,zbhk->zbhv', state, r_t)


def rwkv7_mixer(h, mu, w_rkv, w0, w1, w2, a0, a1, a2, g1, g2, k_k, k_a, r_k, lnx_g, lnx_b, w_o):
    f32 = jnp.float32
    hp = jnp.pad(h, ((0, 0), (1, 1), (0, 0)))
    xx = 0.5 * (hp[:, :-2] + hp[:, 2:]) - h
    mixed = h[None] + xx[None] * mu[:, None, None, :]
    rkv = jnp.einsum('pbtd,pde->pbte', mixed[:3], w_rkv)
    r, k, v = rkv[0], rkv[1], rkv[2]

    def lora(x_in, down, up, inner):
        return jnp.einsum('zbtr,zrd->zbtd', inner(jnp.einsum('btd,zdr->zbtr', x_in, down)), up)

    w_log = -jax.nn.softplus(-(w0[:, None, None, :] + lora(mixed[3], w1, w2, jnp.tanh))) - 0.5
    decay = jnp.exp(-jnp.exp(w_log.astype(f32)))
    a_lr = jax.nn.sigmoid(a0[:, None, None, :] + lora(mixed[4], a1, a2, lambda t: t))
    gate = lora(mixed[5], g1, g2, jax.nn.sigmoid)
    k_mod = k[None] * (1 + (a_lr - 1) * k_a)
    kk = _heads(k * k_k).astype(f32)
    kk = kk / jnp.maximum(jnp.sqrt(jnp.sum(kk * kk, axis=-1, keepdims=True)), 1e-12)
    r_h, v_h, k_h = _heads(r), _heads(v), _heads(k_mod)

    def shared(t):
        return jnp.stack([t, jnp.flip(t, axis=1)]).astype(f32)

    def per_dir(t):
        return jnp.stack([t[0], jnp.flip(t[1], axis=1)]).astype(f32)

    seq_in = tuple(jnp.moveaxis(t, 2, 0) for t in (
        shared(r_h), per_dir(_heads(decay)), per_dir(k_h), shared(v_h),
        shared(-kk), per_dir(kk[None] * _heads(a_lr))))
    s0 = jnp.zeros((2, h.shape[0], RK_HEADS, RK_HEAD_DIM, RK_HEAD_DIM), f32)
    _, ys = lax.scan(_rwkv_step, s0, seq_in)
    ys = per_dir(jnp.moveaxis(ys, 0, 2))
    mean = jnp.mean(ys, axis=-1, keepdims=True)
    var = jnp.mean(jnp.square(ys - mean), axis=-1, keepdims=True)
    yn = ((ys - mean) * lax.rsqrt(var + RK_GN_EPS)).reshape(k_mod.shape)
    yn = yn * lnx_g[:, None, None, :] + lnx_b[:, None, None, :]
    bonus = (jnp.sum(r_h[None] * k_h * r_k, axis=-1, keepdims=True) * v_h[None]).reshape(k_mod.shape)
    o = jnp.sum(gate * (yn.astype(h.dtype) + bonus), axis=0)
    return o @ w_o


def trunk(x, c, p):
    ada_in = jax.nn.silu(c)
    for i in range(DEPTH):
        mod = (ada_in @ p['w_ada'][i] + p['b_ada'][i])[:, None, :]
        sh_m, sc_m, g_m, sh_f, sc_f, g_f = jnp.split(mod, 6, axis=-1)
        h = x * (1 + sc_m) + sh_m
        kind, j = i % N_MIXERS, i // N_MIXERS
        if kind == 0:
            mix = short_conv_mixer(h, p['conv_w_in'][j], p['conv_w'][j], p['conv_w_out'][j])
        elif kind == 1:
            mix = neighbourhood_attention(h, p['na_w_qkv'][j], p['na_rpb'][j], p['na_w_o'][j])
        else:
            mix = rwkv7_mixer(h, p['rk_mu'][j], p['rk_w_rkv'][j], p['rk_w0'][j], p['rk_w1'][j],
                              p['rk_w2'][j], p['rk_a0'][j], p['rk_a1'][j], p['rk_a2'][j],
                              p['rk_g1'][j], p['rk_g2'][j], p['rk_k_k'][j], p['rk_k_a'][j],
                              p['rk_r_k'][j], p['rk_lnx_g'][j], p['rk_lnx_b'][j], p['rk_w_o'][j])
        x = layer_norm(ALPHA * x + g_m * mix, p['ln_g'][i, 0], p['ln_b'][i, 0])
        h = x * (1 + sc_f) + sh_f
        if i % 2 == 0:
            ffn = swiglu(h, p['ffn_w_gu'][i // 2], p['ffn_w_down'][i // 2])
        else:
            ffn = moe_swiglu(h, p['moe_w_router'][i // 2], p['moe_b_router'][i // 2],
                             p['moe_w_gu'][i // 2], p['moe_w_down'][i // 2])
        x = layer_norm(ALPHA * x + g_f * ffn, p['ln_g'][i, 1], p['ln_b'][i, 1])
    return x


def setup_inputs(seed: int = 0) -> dict:
    key = jax.random.key(seed)
    keys = iter(jax.random.split(key, 40))
    f32 = jnp.float32
    d = D_MODEL

    def nrm(shape, scale):
        return jax.random.normal(next(keys), shape, f32) * scale

    def unif(shape, lo, hi):
        return jax.random.uniform(next(keys), shape, f32, lo, hi)

    gate_offset = jnp.array([0.0, 0.0, 1.0, 0.0, 0.0, 1.0], f32)[None, :, None]
    return {
        'x_prompt': nrm((BATCH, SEQ, d), 1.0),
        'x_sample': nrm((DEC_BATCH, DEC_SEQ, d), 1.0),
        'c_prompt': nrm((BATCH, d), 1.0),
        'c_sample': nrm((DEC_BATCH, d), 1.0),
        'w_ada': nrm((DEPTH, d, 6 * d), ADA_SCALE * d ** -0.5),
        'b_ada': (nrm((DEPTH, 6, d), 0.02) + gate_offset).reshape(DEPTH, 6 * d),
        'ln_g': 1.0 + nrm((DEPTH, 2, d), 0.02),
        'ln_b': nrm((DEPTH, 2, d), 0.02),
        'conv_w_in': nrm((N_CONV_LAYERS, d, 3 * d), d ** -0.5),
        'conv_w': nrm((N_CONV_LAYERS, CONV_WIDTH, d), CONV_WIDTH ** -0.5),
        'conv_w_out': nrm((N_CONV_LAYERS, d, d), BETA * d ** -0.5),
        'na_w_qkv': nrm((N_ATTN_LAYERS, d, 3 * d), d ** -0.5),
        'na_rpb': nrm((N_ATTN_LAYERS, NA_HEADS, 2 * NA_WIN_ROWS - 1, 2 * NA_WIN_COLS - 1), 0.5),
        'na_w_o': nrm((N_ATTN_LAYERS, d, d), BETA * d ** -0.5),
        'rk_mu': unif((N_RWKV_LAYERS, 6, d), 0.0, 1.0),
        'rk_w_rkv': nrm((N_RWKV_LAYERS, 3, d, d), d ** -0.5),
        'rk_w0': unif((N_RWKV_LAYERS, 2, d), -3.0, 1.0),
        'rk_w1': nrm((N_RWKV_LAYERS, 2, d, RK_DECAY_LORA), d ** -0.5),
        'rk_w2': nrm((N_RWKV_LAYERS, 2, RK_DECAY_LORA, d), 0.5 * RK_DECAY_LORA ** -0.5),
        'rk_a0': nrm((N_RWKV_LAYERS, 2, d), 0.5),
        'rk_a1': nrm((N_RWKV_LAYERS, 2, d, RK_AAA_LORA), d ** -0.5),
        'rk_a2': nrm((N_RWKV_LAYERS, 2, RK_AAA_LORA, d), 0.5 * RK_AAA_LORA ** -0.5),
        'rk_g1': nrm((N_RWKV_LAYERS, 2, d, RK_GATE_LORA), d ** -0.5),
        'rk_g2': nrm((N_RWKV_LAYERS, 2, RK_GATE_LORA, d), RK_GATE_LORA ** -0.5),
        'rk_k_k': 0.85 + nrm((N_RWKV_LAYERS, d), 0.02),
        'rk_k_a': 1.0 + nrm((N_RWKV_LAYERS, d), 0.02),
        'rk_r_k': nrm((N_RWKV_LAYERS, RK_HEADS, RK_HEAD_DIM), 0.1),
        'rk_lnx_g': 1.0 + nrm((N_RWKV_LAYERS, 2, d), 0.02),
        'rk_lnx_b': nrm((N_RWKV_LAYERS, 2, d), 0.02),
        'rk_w_o': nrm((N_RWKV_LAYERS, d, d), BETA * d ** -0.5),
        'ffn_w_gu': nrm((N_DENSE_LAYERS, d, 2 * D_FF), d ** -0.5),
        'ffn_w_down': nrm((N_DENSE_LAYERS, D_FF, d), BETA * D_FF ** -0.5),
        'moe_w_router': nrm((N_MOE_LAYERS, d, N_EXPERTS), d ** -0.5),
        'moe_b_router': nrm((N_MOE_LAYERS, N_EXPERTS), 0.01),
        'moe_w_gu': nrm((N_MOE_LAYERS, N_EXPERTS, d, 2 * D_FF_EXPERT), d ** -0.5),
        'moe_w_down': nrm((N_MOE_LAYERS, N_EXPERTS, D_FF_EXPERT, d), BETA * D_FF_EXPERT ** -0.5),
    }


def reference(x_prompt, x_sample, c_prompt, c_sample, w_ada, b_ada, ln_g, ln_b,
              conv_w_in, conv_w, conv_w_out, na_w_qkv, na_rpb, na_w_o,
              rk_mu, rk_w_rkv, rk_w0, rk_w1, rk_w2, rk_a0, rk_a1, rk_a2, rk_g1, rk_g2,
              rk_k_k, rk_k_a, rk_r_k, rk_lnx_g, rk_lnx_b, rk_w_o,
              ffn_w_gu, ffn_w_down, moe_w_router, moe_b_router, moe_w_gu, moe_w_down):
    params = dict(
        w_ada=w_ada, b_ada=b_ada, ln_g=ln_g, ln_b=ln_b,
        conv_w_in=conv_w_in, conv_w=conv_w, conv_w_out=conv_w_out,
        na_w_qkv=na_w_qkv, na_rpb=na_rpb, na_w_o=na_w_o,
        rk_mu=rk_mu, rk_w_rkv=rk_w_rkv, rk_w0=rk_w0, rk_w1=rk_w1, rk_w2=rk_w2,
        rk_a0=rk_a0, rk_a1=rk_a1, rk_a2=rk_a2, rk_g1=rk_g1, rk_g2=rk_g2,
        rk_k_k=rk_k_k, rk_k_a=rk_k_a, rk_r_k=rk_r_k, rk_lnx_g=rk_lnx_g, rk_lnx_b=rk_lnx_b,
        rk_w_o=rk_w_o, ffn_w_gu=ffn_w_gu, ffn_w_down=ffn_w_down,
        moe_w_router=moe_w_router, moe_b_router=moe_b_router,
        moe_w_gu=moe_w_gu, moe_w_down=moe_w_down)
    y_prompt = trunk(x_prompt, c_prompt, params)
    y_sample = trunk(x_sample, c_sample, params)
    return (y_prompt, y_sample)
```

```python
import functools

import jax
import jax.numpy as jnp
from jax import lax
from jax.experimental import pallas as pl
from jax.experimental.pallas import tpu as pltpu

F32 = jnp.float32
BF16 = jnp.bfloat16

D_MODEL = 1024
DEPTH = 4
ALPHA = (2 * DEPTH) ** 0.25
LN_EPS = 1e-5

GRID_W = 64
NA_HEADS = 16
NA_HEAD_DIM = D_MODEL // NA_HEADS
NA_WIN_ROWS = 8
NA_WIN_COLS = 16
NEG_INF = -1e30

RK_HEAD_DIM = 64
RK_GN_EPS = 64e-5
RK_CHUNK = 64
RK_TOK_BLOCK = 256
RK_LANE_BLOCK = 256

N_EXPERTS = 8
ROUTER_LANES = 128

VMEM_LIMIT = 52 * 1024 * 1024


def _cparams(*sem):
    return pltpu.CompilerParams(dimension_semantics=sem, vmem_limit_bytes=VMEM_LIMIT)


def _const_spec(shape):
    nd = len(shape)
    return pl.BlockSpec(shape, lambda *_: (0,) * nd, pipeline_mode=pl.Buffered(1))


def _dot(a, b):
    return jnp.dot(a, b, preferred_element_type=F32)


def _dot_nt(a, b):
    return lax.dot_general(a, b, (((1,), (1,)), ((), ())), preferred_element_type=F32)


def _dot_tn(a, b):
    return lax.dot_general(a, b, (((0,), (0,)), ((), ())), preferred_element_type=F32)


def _sigmoid(x):
    return 1.0 / (1.0 + jnp.exp(-x))


def _layer_norm(y, g, b):
    mu = jnp.mean(y, axis=-1, keepdims=True)
    d = y - mu
    var = jnp.mean(d * d, axis=-1, keepdims=True)
    return d * lax.rsqrt(var + LN_EPS) * g + b


def _ada_kernel(c_ref, w_ref, b_ref, o_ref):
    c = c_ref[...]
    s = (c * _sigmoid(c)).astype(BF16)
    o_ref[...] = _dot(s, w_ref[...].astype(BF16)) + b_ref[...]


def ada_modulation(c, w_ada, b_ada):
    bsz, d = c.shape
    depth = w_ada.shape[0]
    out = pl.pallas_call(
        _ada_kernel,
        grid=(depth, 6),
        in_specs=[
            pl.BlockSpec((bsz, d), lambda i, j: (0, 0)),
            pl.BlockSpec((None, d, d), lambda i, j: (i, 0, j)),
            pl.BlockSpec((None, None, 1, d), lambda i, j: (i, j, 0, 0)),
        ],
        out_specs=pl.BlockSpec((None, None, bsz, d), lambda i, j: (i, j, 0, 0)),
        out_shape=jax.ShapeDtypeStruct((depth, 6, bsz, d), F32),
        compiler_params=_cparams("arbitrary", "arbitrary"),
        name="ada_modulation",
    )(c, w_ada, b_ada.reshape(depth, 6, 1, d))
    return jnp.transpose(out, (0, 2, 1, 3))


def _tile_specs(seq, tt, d):
    nb8 = seq // 8
    per = tt // 8
    x_spec = pl.BlockSpec((None, tt, d), lambda b, t: (b, t, 0))
    prev_spec = pl.BlockSpec((None, 8, d), lambda b, t: (b, jnp.maximum(t * per - 1, 0), 0))
    next_spec = pl.BlockSpec((None, 8, d), lambda b, t: (b, jnp.minimum((t + 1) * per, nb8 - 1), 0))
    return x_spec, prev_spec, next_spec


def _mod_spec(d):
    return pl.BlockSpec((None, 6, d), lambda b, t: (b, 0, 0))


def _shifted(cur, prev_row, next_row, tt):
    row = lax.broadcasted_iota(jnp.int32, (tt, 1), 0)
    m1 = jnp.where(row == 0, prev_row, pltpu.roll(cur, 1, 0))
    p1 = jnp.where(row == tt - 1, next_row, pltpu.roll(cur, tt - 1, 0))
    return m1, p1


def _conv_kernel(x_ref, xp_ref, xn_ref, mod_ref, win_ref, cw_ref, wout_ref, lng_ref, lnb_ref,
                 o_ref, *, nt, tt, d):
    t = pl.program_id(1)
    sh, sc, gate = mod_ref[0:1, :], mod_ref[1:2, :], mod_ref[2:3, :]
    x = x_ref[...]
    h = (x * (1.0 + sc) + sh).astype(BF16)
    p = _dot(h, win_ref[...])
    z = p[:, d:2 * d] * p[:, 2 * d:]
    halo = jnp.concatenate([xp_ref[...], xn_ref[...]], axis=0)
    hh = (halo * (1.0 + sc) + sh).astype(BF16)
    ph = _dot(hh, win_ref[:, d:])
    zh = ph[:, :d] * ph[:, d:]
    z_prev = jnp.where(t > 0, zh[7:8, :], 0.0)
    z_next = jnp.where(t < nt - 1, zh[8:9, :], 0.0)
    z_m1, z_p1 = _shifted(z, z_prev, z_next, tt)
    conv = z_m1 * cw_ref[0:1, :] + z * cw_ref[1:2, :] + z_p1 * cw_ref[2:3, :]
    y = (p[:, :d] * conv).astype(BF16)
    mix = _dot(y, wout_ref[...])
    o_ref[...] = _layer_norm(ALPHA * x + gate * mix, lng_ref[...], lnb_ref[...])


def conv_mixer_layer(x, mod, w_in, conv_w, w_out, ln_g, ln_b, tt=512):
    bsz, seq, d = x.shape
    tt = min(tt, seq)
    nt = seq // tt
    x_spec, prev_spec, next_spec = _tile_specs(seq, tt, d)
    return pl.pallas_call(
        functools.partial(_conv_kernel, nt=nt, tt=tt, d=d),
        grid=(bsz, nt),
        in_specs=[x_spec, prev_spec, next_spec, _mod_spec(d),
                  _const_spec((d, 3 * d)), _const_spec((3, d)), _const_spec((d, d)),
                  _const_spec((1, d)), _const_spec((1, d))],
        out_specs=pl.BlockSpec((None, tt, d), lambda b, t: (b, t, 0)),
        out_shape=jax.ShapeDtypeStruct(x.shape, F32),
        compiler_params=_cparams("parallel", "arbitrary"),
        name="conv_mixer",
    )(x, x, x, mod, w_in, conv_w, w_out, ln_g, ln_b)


def _ffn_kernel(x_ref, mod_ref, wgu_ref, wd_ref, lng_ref, lnb_ref, o_ref, *, ff, fc):
    sh, sc, gate = mod_ref[3:4, :], mod_ref[4:5, :], mod_ref[5:6, :]
    x = x_ref[...]
    h = (x * (1.0 + sc) + sh).astype(BF16)
    acc = jnp.zeros(x.shape, F32)
    for c in range(ff // fc):
        g = _dot(h, wgu_ref[:, c * fc:(c + 1) * fc])
        u = _dot(h, wgu_ref[:, ff + c * fc:ff + (c + 1) * fc])
        act = (g * _sigmoid(g) * u).astype(BF16)
        acc = acc + _dot(act, wd_ref[c * fc:(c + 1) * fc, :])
    o_ref[...] = _layer_norm(ALPHA * x + gate * acc, lng_ref[...], lnb_ref[...])


def dense_ffn_layer(x, mod, w_gu, w_down, ln_g, ln_b, tt=512):
    bsz, seq, d = x.shape
    tt = min(tt, seq)
    ff = w_down.shape[0]
    fc = ff // 2
    return pl.pallas_call(
        functools.partial(_ffn_kernel, ff=ff, fc=fc),
        grid=(bsz, seq // tt),
        in_specs=[pl.BlockSpec((None, tt, d), lambda b, t: (b, t, 0)), _mod_spec(d),
                  _const_spec((d, 2 * ff)), _const_spec((ff, d)),
                  _const_spec((1, d)), _const_spec((1, d))],
        out_specs=pl.BlockSpec((None, tt, d), lambda b, t: (b, t, 0)),
        out_shape=jax.ShapeDtypeStruct(x.shape, F32),
        compiler_params=_cparams("parallel", "arbitrary"),
        name="dense_ffn",
    )(x, mod, w_gu, w_down, ln_g, ln_b)


def _moe_kernel(x_ref, mod_ref, wr_ref, br_ref, wgu_ref, wd_ref, lng_ref, lnb_ref, o_ref,
                h_sc, comb_sc, acc_sc, *, fe):
    e = pl.program_id(2)
    lane = lax.broadcasted_iota(jnp.int32, comb_sc.shape, 1)

    @pl.when(e == 0)
    def _():
        sh, sc = mod_ref[3:4, :], mod_ref[4:5, :]
        h = x_ref[...] * (1.0 + sc) + sh
        h_sc[...] = h.astype(BF16)
        logits = jnp.dot(h, wr_ref[...], preferred_element_type=F32,
                         precision=lax.Precision.HIGHEST) + br_ref[...]
        m = jnp.max(logits, axis=-1, keepdims=True)
        ex = jnp.exp(logits - m)
        probs = ex / jnp.sum(ex, axis=-1, keepdims=True)
        valid = lane < N_EXPERTS
        p = jnp.where(valid, probs, -1.0)
        p1 = jnp.max(p, axis=-1, keepdims=True)
        i1 = jnp.min(jnp.where(p == p1, lane, ROUTER_LANES), axis=-1, keepdims=True)
        pr = jnp.where(lane == i1, -1.0, p)
        p2 = jnp.max(pr, axis=-1, keepdims=True)
        i2 = jnp.min(jnp.where(pr == p2, lane, ROUTER_LANES), axis=-1, keepdims=True)
        tot = p1 + p2
        comb_sc[...] = jnp.where(lane == i1, p1 / tot, 0.0) + jnp.where(lane == i2, p2 / tot, 0.0)
        acc_sc[...] = jnp.zeros(acc_sc.shape, F32)

    h = h_sc[...]
    g = _dot(h, wgu_ref[:, :fe])
    u = _dot(h, wgu_ref[:, fe:])
    act = (g * _sigmoid(g) * u).astype(BF16)
    y = _dot(act, wd_ref[...])
    w_e = jnp.sum(jnp.where(lane == e, comb_sc[...], 0.0), axis=-1, keepdims=True)
    acc_sc[...] += w_e * y

    @pl.when(e == N_EXPERTS - 1)
    def _():
        gate = mod_ref[5:6, :]
        o_ref[...] = _layer_norm(ALPHA * x_ref[...] + gate * acc_sc[...], lng_ref[...], lnb_ref[...])


def moe_ffn_layer(x, mod, w_router, b_router, w_gu, w_down, ln_g, ln_b, tt=512):
    bsz, seq, d = x.shape
    tt = min(tt, seq)
    n_e, fe = w_down.shape[0], w_down.shape[1]
    return pl.pallas_call(
        functools.partial(_moe_kernel, fe=fe),
        grid=(bsz, seq // tt, n_e),
        in_specs=[pl.BlockSpec((None, tt, d), lambda b, t, e: (b, t, 0)),
                  pl.BlockSpec((None, 6, d), lambda b, t, e: (b, 0, 0)),
                  _const_spec((d, ROUTER_LANES)), _const_spec((1, ROUTER_LANES)),
                  pl.BlockSpec((None, d, 2 * fe), lambda b, t, e: (e, 0, 0)),
                  pl.BlockSpec((None, fe, d), lambda b, t, e: (e, 0, 0)),
                  _const_spec((1, d)), _const_spec((1, d))],
        out_specs=pl.BlockSpec((None, tt, d), lambda b, t, e: (b, t, 0)),
        out_shape=jax.ShapeDtypeStruct(x.shape, F32),
        scratch_shapes=[pltpu.VMEM((tt, d), BF16), pltpu.VMEM((tt, ROUTER_LANES), F32),
                        pltpu.VMEM((tt, d), F32)],
        compiler_params=_cparams("parallel", "arbitrary", "arbitrary"),
        name="moe_ffn",
    )(x, mod, w_router, b_router, w_gu, w_down, ln_g, ln_b)


def _pad_router(w_router, b_router):
    d, n_e = w_router.shape
    w = jnp.zeros((d, ROUTER_LANES), F32).at[:, :n_e].set(w_router)
    b = jnp.full((1, ROUTER_LANES), NEG_INF, F32).at[0, :n_e].set(b_router)
    return w, b


def _qkv_kernel(x_ref, mod_ref, w_ref, q_ref, k_ref, v_ref, *, d):
    sh, sc = mod_ref[0:1, :], mod_ref[1:2, :]
    h = (x_ref[...] * (1.0 + sc) + sh).astype(BF16)
    p = _dot(h, w_ref[...])
    q_ref[...] = (p[:, :d] * (NA_HEAD_DIM ** -0.5)).astype(BF16)
    k_ref[...] = p[:, d:2 * d].astype(BF16)
    v_ref[...] = p[:, 2 * d:].astype(BF16)


def _na_kernel(q_ref, k_ref, v_ref, bias_ref, o_ref, *, rows):
    r = pl.program_id(1)
    r0 = jnp.clip(r - NA_WIN_ROWS // 2, 0, rows - NA_WIN_ROWS)
    start = pl.multiple_of(r0 * GRID_W, GRID_W)
    nk = NA_WIN_ROWS * GRID_W
    for hd in range(NA_HEADS):
        sl = slice(hd * NA_HEAD_DIM, (hd + 1) * NA_HEAD_DIM)
        s = _dot_nt(q_ref[:, sl], k_ref[pl.ds(start, nk), sl]) + bias_ref[hd]
        m = jnp.max(s, axis=-1, keepdims=True)
        p = jnp.exp(s - m)
        l = jnp.sum(p, axis=-1, keepdims=True)
        o = _dot(p.astype(BF16), v_ref[pl.ds(start, nk), sl]) / l
        o_ref[:, sl] = o.astype(BF16)


def _na_bias_table(rpb):
    qc = jnp.arange(GRID_W)[:, None]
    kc = jnp.arange(GRID_W)[None, :]
    cs = jnp.clip(qc - NA_WIN_COLS // 2, 0, GRID_W - NA_WIN_COLS)
    col_ok = (kc >= cs) & (kc < cs + NA_WIN_COLS)
    dc = jnp.clip(kc - qc + NA_WIN_COLS - 1, 0, 2 * NA_WIN_COLS - 2)
    dr = jnp.arange(NA_WIN_ROWS)[:, None] + jnp.arange(NA_WIN_ROWS)[None, :]
    tab = rpb[:, dr[:, None, :, None], dc[None, :, None, :]]
    tab = jnp.where(col_ok[None, None, :, None, :], tab, NEG_INF)
    return tab.reshape(rpb.shape[0], NA_WIN_ROWS, GRID_W, NA_WIN_ROWS * GRID_W)


def _proj_ln_kernel(*refs, n_in):
    a_refs = refs[:n_in]
    x_ref, mod_ref, w_ref, lng_ref, lnb_ref, o_ref = refs[n_in:]
    a = a_refs[0][...]
    if n_in == 2:
        a = (a.astype(F32) + a_refs[1][...].astype(F32)).astype(BF16)
    mix = _dot(a, w_ref[...])
    gate = mod_ref[2:3, :]
    o_ref[...] = _layer_norm(ALPHA * x_ref[...] + gate * mix, lng_ref[...], lnb_ref[...])


def proj_residual_ln(acts, x, mod, w_o, ln_g, ln_b, tt=512):
    bsz, seq, d = x.shape
    tt = min(tt, seq)
    tile = pl.BlockSpec((None, tt, d), lambda b, t: (b, t, 0))
    return pl.pallas_call(
        functools.partial(_proj_ln_kernel, n_in=len(acts)),
        grid=(bsz, seq // tt),
        in_specs=[tile] * len(acts) + [tile, _mod_spec(d), _const_spec((d, d)),
                                       _const_spec((1, d)), _const_spec((1, d))],
        out_specs=tile,
        out_shape=jax.ShapeDtypeStruct(x.shape, F32),
        compiler_params=_cparams("parallel", "arbitrary"),
        name="proj_residual_ln",
    )(*acts, x, mod, w_o, ln_g, ln_b)


def na_mixer_layer(x, mod, w_qkv, bias_tab, w_o, ln_g, ln_b, tt=512):
    bsz, seq, d = x.shape
    tt = min(tt, seq)
    rows = seq // GRID_W
    assert rows >= NA_WIN_ROWS and seq % GRID_W == 0
    tile = pl.BlockSpec((None, tt, d), lambda b, t: (b, t, 0))
    q, k, v = pl.pallas_call(
        functools.partial(_qkv_kernel, d=d),
        grid=(bsz, seq // tt),
        in_specs=[tile, _mod_spec(d), _const_spec((d, 3 * d))],
        out_specs=[tile] * 3,
        out_shape=[jax.ShapeDtypeStruct(x.shape, BF16)] * 3,
        compiler_params=_cparams("parallel", "arbitrary"),
        name="na_qkv",
    )(x, mod, w_qkv)

    def delta(r):
        return jnp.clip(r - NA_WIN_ROWS // 2, 0, rows - NA_WIN_ROWS) - r + NA_WIN_ROWS - 1

    row_spec = pl.BlockSpec((None, GRID_W, d), lambda b, r: (b, r, 0))
    seq_spec = pl.BlockSpec((None, seq, d), lambda b, r: (b, 0, 0))
    att = pl.pallas_call(
        functools.partial(_na_kernel, rows=rows),
        grid=(bsz, rows),
        in_specs=[row_spec, seq_spec, seq_spec,
                  pl.BlockSpec((NA_HEADS, None, GRID_W, NA_WIN_ROWS * GRID_W),
                               lambda b, r: (0, delta(r), 0, 0))],
        out_specs=row_spec,
        out_shape=jax.ShapeDtypeStruct(x.shape, BF16),
        compiler_params=_cparams("parallel", "arbitrary"),
        name="na_attention",
    )(q, k, v, bias_tab)
    return proj_residual_ln([att], x, mod, w_o, ln_g, ln_b, tt)


def _softplus(y):
    return jnp.maximum(y, 0.0) + jnp.log(1.0 + jnp.exp(-jnp.abs(y)))


def _rk_proj_kernel(x_ref, xp_ref, xn_ref, mod_ref, mu_ref, wrkv_ref, w1_ref, w2_ref, a1_ref, a2_ref,
                    g1_ref, g2_ref, w0_ref, a0_ref, kk_ref, ka_ref, seg_ref,
                    r_o, v_o, kkn_o, lw_o, km_o, b_o, gate_o, *, nt, tt):
    t = pl.program_id(1)
    sh, sc = mod_ref[0:1, :], mod_ref[1:2, :]
    h = x_ref[...] * (1.0 + sc) + sh
    h_prev = jnp.where(t > 0, xp_ref[7:8, :] * (1.0 + sc) + sh, 0.0)
    h_next = jnp.where(t < nt - 1, xn_ref[0:1, :] * (1.0 + sc) + sh, 0.0)
    h_m1, h_p1 = _shifted(h, h_prev, h_next, tt)
    xx = 0.5 * (h_m1 + h_p1) - h

    def mixed(p):
        return (h + xx * mu_ref[p:p + 1, :]).astype(BF16)

    r = _dot(mixed(0), wrkv_ref[0])
    k = _dot(mixed(1), wrkv_ref[1])
    v = _dot(mixed(2), wrkv_ref[2])
    r_o[...] = r.astype(BF16)
    v_o[...] = v.astype(BF16)
    lw_in = jnp.tanh(_dot(mixed(3), w1_ref[...])).astype(BF16)
    a_in = _dot(mixed(4), a1_ref[...]).astype(BF16)
    g_in = _sigmoid(_dot(mixed(5), g1_ref[...])).astype(BF16)

    kk = k * kk_ref[...]
    sq = kk * kk
    sq_hi = sq.astype(BF16)
    sq_lo = (sq - sq_hi.astype(F32)).astype(BF16)
    ss = _dot(sq_hi, seg_ref[...]) + _dot(sq_lo, seg_ref[...])
    kkn = kk / jnp.maximum(jnp.sqrt(ss), 1e-12)
    kkn_o[...] = kkn.astype(BF16)
    rw = w1_ref.shape[1] // 2
    rg = g1_ref.shape[1] // 2
    for z in range(2):
        wl = w0_ref[z:z + 1, :] + _dot(lw_in[:, z * rw:(z + 1) * rw], w2_ref[z])
        w_log = -_softplus(-wl) - 0.5
        lw_o[z] = -jnp.exp(w_log)
        a_lr = _sigmoid(a0_ref[z:z + 1, :] + _dot(a_in[:, z * rw:(z + 1) * rw], a2_ref[z]))
        km_o[z] = (k * (1.0 + (a_lr - 1.0) * ka_ref[...])).astype(BF16)
        b_o[z] = (kkn * a_lr).astype(BF16)
        gate_o[z] = _dot(g_in[:, z * rg:(z + 1) * rg], g2_ref[z]).astype(BF16)


def _rk_scan_kernel(r_ref, v_ref, kkn_ref, lw_ref, km_ref, b_ref, gate_ref, rk_ref, lg_ref, lb_ref,
                    o_ref, s_ref, *, reverse):
    tb, hb = r_ref.shape
    cl = RK_CHUNK
    nc, nh = tb // cl, hb // RK_HEAD_DIM

    @pl.when(pl.program_id(2) == 0)
    def _():
        s_ref[...] = jnp.zeros(s_ref.shape, F32)

    ri = lax.broadcasted_iota(jnp.int32, (tb, tb), 0)
    ci = lax.broadcasted_iota(jnp.int32, (tb, tb), 1)
    ordered = (ci >= ri) if reverse else (ci <= ri)
    tri = jnp.where((ri // cl == ci // cl) & ordered, 1.0, 0.0).astype(BF16)
    lw = lw_ref[...]
    lw_hi = lw.astype(BF16)
    lw_lo = (lw - lw_hi.astype(F32)).astype(BF16)
    cum = _dot(tri, lw_hi) + _dot(tri, lw_lo)
    last = 0 if reverse else cl - 1
    tot_rows = [cum[c * cl + last:c * cl + last + 1, :] for c in range(nc)]
    tot = jnp.concatenate([jnp.broadcast_to(tr, (cl, hb)) for tr in tot_rows], axis=0)

    rr = r_ref[...].astype(F32)
    vv = v_ref[...]
    kkn = kkn_ref[...].astype(F32)
    km = km_ref[...].astype(F32)
    bb = b_ref[...].astype(F32)
    e_neg = jnp.exp(-cum)
    e_rem = jnp.exp(tot - cum)
    a_t = (-kkn * jnp.exp(cum - lw)).astype(BF16)
    r_t = (rr * jnp.exp(cum)).astype(BF16)
    b_t = (bb * e_neg).astype(BF16)
    k_t = (km * e_neg).astype(BF16)
    b_h = (bb * e_rem).astype(BF16)
    k_h = (km * e_rem).astype(BF16)
    bonus_w = rr * km * rk_ref[...]

    i0 = lax.broadcasted_iota(jnp.int32, (cl, cl), 0)
    i1 = lax.broadcasted_iota(jnp.int32, (cl, cl), 1)
    strict = (i0 < i1) if reverse else (i0 > i1)
    incl = (i0 <= i1) if reverse else (i0 >= i1)
    eye = jnp.where(i0 == i1, 1.0, 0.0)

    pre = {}
    for c in range(nc):
        rs = slice(c * cl, (c + 1) * cl)
        for j in range(nh):
            ls = slice(j * RK_HEAD_DIM, (j + 1) * RK_HEAD_DIM)
            at, rt, vt = a_t[rs, ls], r_t[rs, ls], vv[rs, ls]
            bk = jnp.concatenate([b_t[rs, ls], k_t[rs, ls]], axis=0)
            ga = _dot_nt(at, bk)
            gr = _dot_nt(rt, bk)
            m_ab = jnp.where(strict, ga[:, :cl], 0.0)
            m_ak = jnp.where(strict, ga[:, cl:], 0.0).astype(BF16)
            m_rb = jnp.where(incl, gr[:, :cl], 0.0).astype(BF16)
            m_rk = jnp.where(incl, gr[:, cl:], 0.0).astype(BF16)
            pw = m_ab
            inv = eye + m_ab
            n = 2
            while n < cl:
                pwb = pw.astype(BF16)
                pw = _dot(pwb, pwb)
                inv = inv + _dot(inv.astype(BF16), pw.astype(BF16))
                n *= 2
            invb = inv.astype(BF16)
            a_hat = _dot(invb, at).astype(BF16)
            u0 = _dot(invb, _dot(m_ak, vt).astype(BF16))
            y0 = _dot(m_rk, vt)
            kv = _dot_tn(k_h[rs, ls], vt)
            pre[c, j] = (a_hat, rt, u0, y0, m_rb, b_h[rs, ls], kv)

    order = range(nc - 1, -1, -1) if reverse else range(nc)
    for c in order:
        rs = slice(c * cl, (c + 1) * cl)
        for j in range(nh):
            ls = slice(j * RK_HEAD_DIM, (j + 1) * RK_HEAD_DIM)
            a_hat, rt, u0, y0, m_rb, bh, kv = pre[c, j]
            st = s_ref[j]
            stb = st.astype(BF16)
            u = u0 + _dot(a_hat, stb)
            ub = u.astype(BF16)
            y = y0 + _dot(rt, stb) + _dot(m_rb, ub)
            decay = jnp.exp(tot_rows[c][:, ls])
            s_ref[j] = st * jnp.transpose(jnp.broadcast_to(decay, (RK_HEAD_DIM, RK_HEAD_DIM))) \
                + _dot_tn(bh, ub) + kv
            mean = jnp.mean(y, axis=-1, keepdims=True)
            dy = y - mean
            var = jnp.mean(dy * dy, axis=-1, keepdims=True)
            yn = dy * lax.rsqrt(var + RK_GN_EPS) * lg_ref[:, ls] + lb_ref[:, ls]
            bonus = jnp.sum(bonus_w[rs, ls], axis=-1, keepdims=True) * vt_f32(vv[rs, ls])
            o_ref[rs, ls] = (gate_ref[rs, ls].astype(F32) * (yn + bonus)).astype(BF16)


def vt_f32(v):
    return v.astype(F32)


def rwkv_mixer_layer(x, mod, p, ln_g, ln_b, tt=256):
    bsz, seq, d = x.shape
    tt = min(tt, seq)
    nt = seq // tt
    x_spec, prev_spec, next_spec = _tile_specs(seq, tt, d)
    tile = pl.BlockSpec((None, tt, d), lambda b, t: (b, t, 0))
    tile2 = pl.BlockSpec((2, None, tt, d), lambda b, t: (0, b, t, 0))
    rw, rg = p['w1'].shape[1] // 2, p['g1'].shape[1] // 2
    sh_bf = jax.ShapeDtypeStruct(x.shape, BF16)
    sh2_bf = jax.ShapeDtypeStruct((2,) + x.shape, BF16)
    sh2_f32 = jax.ShapeDtypeStruct((2,) + x.shape, F32)
    r, v, kkn, lw, km, bz, gate = pl.pallas_call(
        functools.partial(_rk_proj_kernel, nt=nt, tt=tt),
        grid=(bsz, nt),
        in_specs=[x_spec, prev_spec, next_spec, _mod_spec(d), _const_spec((6, d)),
                  _const_spec((3, d, d)),
                  _const_spec((d, 2 * rw)), _const_spec((2, rw, d)),
                  _const_spec((d, 2 * rw)), _const_spec((2, rw, d)),
                  _const_spec((d, 2 * rg)), _const_spec((2, rg, d)),
                  _const_spec((2, d)), _const_spec((2, d)), _const_spec((1, d)), _const_spec((1, d)),
                  _const_spec((d, d))],
        out_specs=[tile, tile, tile, tile2, tile2, tile2, tile2],
        out_shape=[sh_bf, sh_bf, sh_bf, sh2_f32, sh2_bf, sh2_bf, sh2_bf],
        compiler_params=_cparams("parallel", "arbitrary"),
        name="rwkv_proj",
    )(x, x, x, mod, p['mu'], p['w_rkv'], p['w1'], p['w2'], p['a1'], p['a2'], p['g1'], p['g2'],
      p['w0'], p['a0'], p['k_k'], p['k_a'], p['seg'])

    tb = min(RK_TOK_BLOCK, seq)
    hb = RK_LANE_BLOCK
    nb = seq // tb
    outs = []
    for z in range(2):
        rev = z == 1

        def tok(i, rev=rev):
            return nb - 1 - i if rev else i

        blk = pl.BlockSpec((None, tb, hb), lambda b, g, i: (b, tok(i), g))
        blk2 = pl.BlockSpec((None, None, tb, hb), lambda b, g, i, z=z: (z, b, tok(i), g))
        vec = pl.BlockSpec((1, hb), lambda b, g, i: (0, g))
        vec2 = pl.BlockSpec((None, 1, hb), lambda b, g, i, z=z: (z, 0, g))
        outs.append(pl.pallas_call(
            functools.partial(_rk_scan_kernel, reverse=rev),
            grid=(bsz, d // hb, nb),
            in_specs=[blk, blk, blk, blk2, blk2, blk2, blk2, vec, vec2, vec2],
            out_specs=blk,
            out_shape=sh_bf,
            scratch_shapes=[pltpu.VMEM((hb // RK_HEAD_DIM, RK_HEAD_DIM, RK_HEAD_DIM), F32)],
            compiler_params=_cparams("parallel", "parallel", "arbitrary"),
            name="rwkv_scan_rev" if rev else "rwkv_scan_fwd",
        )(r, v, kkn, lw, km, bz, gate, p['r_k'], p['lnx_g'], p['lnx_b']))
    return proj_residual_ln(outs, x, mod, p['w_o'], ln_g, ln_b)


def _prepare_params(w):
    d = D_MODEL
    bf = lambda a: a.astype(BF16)
    row = lambda a: a.reshape(1, d)
    lane = jnp.arange(d) // RK_HEAD_DIM
    p = {
        'w_ada': w['w_ada'], 'b_ada': w['b_ada'],
        'ln_g': w['ln_g'], 'ln_b': w['ln_b'],
        'conv_w_in': bf(w['conv_w_in']), 'conv_w': w['conv_w'], 'conv_w_out': bf(w['conv_w_out']),
        'na_w_qkv': bf(w['na_w_qkv']), 'na_w_o': bf(w['na_w_o']),
        'na_bias': [_na_bias_table(w['na_rpb'][j]) for j in range(w['na_rpb'].shape[0])],
        'ffn_w_gu': bf(w['ffn_w_gu']), 'ffn_w_down': bf(w['ffn_w_down']),
        'moe_router': [_pad_router(w['moe_w_router'][j], w['moe_b_router'][j])
                       for j in range(w['moe_w_router'].shape[0])],
        'moe_w_gu': bf(w['moe_w_gu']), 'moe_w_down': bf(w['moe_w_down']),
        'rk': [],
    }
    cat = lambda a: jnp.concatenate([a[0], a[1]], axis=1)
    for j in range(w['rk_mu'].shape[0]):
        p['rk'].append({
            'mu': w['rk_mu'][j], 'w_rkv': bf(w['rk_w_rkv'][j]),
            'w1': bf(cat(w['rk_w1'][j])), 'w2': bf(w['rk_w2'][j]),
            'a1': bf(cat(w['rk_a1'][j])), 'a2': bf(w['rk_a2'][j]),
            'g1': bf(cat(w['rk_g1'][j])), 'g2': bf(w['rk_g2'][j]),
            'w0': w['rk_w0'][j], 'a0': w['rk_a0'][j],
            'k_k': row(w['rk_k_k'][j]), 'k_a': row(w['rk_k_a'][j]),
            'r_k': w['rk_r_k'][j].reshape(1, d),
            'lnx_g': w['rk_lnx_g'][j].reshape(2, 1, d), 'lnx_b': w['rk_lnx_b'][j].reshape(2, 1, d),
            'w_o': bf(w['rk_w_o'][j]),
            'seg': (lane[:, None] == lane[None, :]).astype(BF16),
        })
    return p


def _trunk(x, c, p):
    d = D_MODEL
    mod_all = ada_modulation(c, p['w_ada'], p['b_ada'])
    for i in range(DEPTH):
        mod = mod_all[i]
        lng = lambda s: p['ln_g'][i, s].reshape(1, d)
        lnb = lambda s: p['ln_b'][i, s].reshape(1, d)
        kind, j = i % 3, i // 3
        if kind == 0:
            x = conv_mixer_layer(x, mod, p['conv_w_in'][j], p['conv_w'][j], p['conv_w_out'][j],
                                 lng(0), lnb(0))
        elif kind == 1:
            x = na_mixer_layer(x, mod, p['na_w_qkv'][j], p['na_bias'][j], p['na_w_o'][j], lng(0), lnb(0))
        else:
            x = rwkv_mixer_layer(x, mod, p['rk'][j], lng(0), lnb(0))
        if i % 2 == 0:
            x = dense_ffn_layer(x, mod, p['ffn_w_gu'][i // 2], p['ffn_w_down'][i // 2], lng(1), lnb(1))
        else:
            wr, br = p['moe_router'][i // 2]
            x = moe_ffn_layer(x, mod, wr, br, p['moe_w_gu'][i // 2], p['moe_w_down'][i // 2],
                              lng(1), lnb(1))
    return x


def kernel(x_prompt, x_sample, c_prompt, c_sample, w_ada, b_ada, ln_g, ln_b, conv_w_in, conv_w, conv_w_out, na_w_qkv, na_rpb, na_w_o, rk_mu, rk_w_rkv, rk_w0, rk_w1, rk_w2, rk_a0, rk_a1, rk_a2, rk_g1, rk_g2, rk_k_k, rk_k_a, rk_r_k, rk_lnx_g, rk_lnx_b, rk_w_o, ffn_w_gu, ffn_w_down, moe_w_router, moe_b_router, moe_w_gu, moe_w_down):
    p = _prepare_params(dict(
        w_ada=w_ada, b_ada=b_ada, ln_g=ln_g, ln_b=ln_b,
        conv_w_in=conv_w_in, conv_w=conv_w, conv_w_out=conv_w_out,
        na_w_qkv=na_w_qkv, na_rpb=na_rpb, na_w_o=na_w_o,
        rk_mu=rk_mu, rk_w_rkv=rk_w_rkv, rk_w0=rk_w0, rk_w1=rk_w1, rk_w2=rk_w2,
        rk_a0=rk_a0, rk_a1=rk_a1, rk_a2=rk_a2, rk_g1=rk_g1, rk_g2=rk_g2,
        rk_k_k=rk_k_k, rk_k_a=rk_k_a, rk_r_k=rk_r_k, rk_lnx_g=rk_lnx_g, rk_lnx_b=rk_lnx_b,
        rk_w_o=rk_w_o, ffn_w_gu=ffn_w_gu, ffn_w_down=ffn_w_down,
        moe_w_router=moe_w_router, moe_b_router=moe_b_router,
        moe_w_gu=moe_w_gu, moe_w_down=moe_w_down))
    return (_trunk(x_prompt, c_prompt, p), _trunk(x_sample, c_sample, p))
```

```python
import functools

import jax
import jax.numpy as jnp
from jax import lax
from jax.experimental import pallas as pl
from jax.experimental.pallas import tpu as pltpu

F32 = jnp.float32
BF16 = jnp.bfloat16

D_MODEL = 1024
DEPTH = 4
ALPHA = (2 * DEPTH) ** 0.25
LN_EPS = 1e-5

GRID_W = 64
NA_HEADS = 16
NA_HEAD_DIM = D_MODEL // NA_HEADS
NA_WIN_ROWS = 8
NA_WIN_COLS = 16
NEG_INF = -1e30

RK_HEAD_DIM = 64
RK_GN_EPS = 64e-5
RK_CHUNK = 64
RK_TOK_BLOCK = 256
RK_LANE_BLOCK = 256

N_EXPERTS = 8
ROUTER_LANES = 128

VMEM_LIMIT = 52 * 1024 * 1024


def _cparams(*sem):
    return pltpu.CompilerParams(dimension_semantics=sem, vmem_limit_bytes=VMEM_LIMIT)


def _const_spec(shape):
    nd = len(shape)
    return pl.BlockSpec(shape, lambda *_: (0,) * nd, pipeline_mode=pl.Buffered(1))


def _dot(a, b):
    return jnp.dot(a, b, preferred_element_type=F32)


def _dot_nt(a, b):
    return lax.dot_general(a, b, (((1,), (1,)), ((), ())), preferred_element_type=F32)


def _dot_tn(a, b):
    return lax.dot_general(a, b, (((0,), (0,)), ((), ())), preferred_element_type=F32)


def _sigmoid(x):
    return 1.0 / (1.0 + jnp.exp(-x))


def _layer_norm(y, g, b):
    mu = jnp.mean(y, axis=-1, keepdims=True)
    d = y - mu
    var = jnp.mean(d * d, axis=-1, keepdims=True)
    return d * lax.rsqrt(var + LN_EPS) * g + b


def _ada_kernel(c_ref, w_ref, b_ref, o_ref):
    c = c_ref[...]
    s = (c * _sigmoid(c)).astype(BF16)
    o_ref[...] = _dot(s, w_ref[...].astype(BF16)) + b_ref[...]


def ada_modulation(c, w_ada, b_ada):
    bsz, d = c.shape
    depth = w_ada.shape[0]
    out = pl.pallas_call(
        _ada_kernel,
        grid=(depth, 6),
        in_specs=[
            pl.BlockSpec((bsz, d), lambda i, j: (0, 0)),
            pl.BlockSpec((None, d, d), lambda i, j: (i, 0, j)),
            pl.BlockSpec((None, None, 1, d), lambda i, j: (i, j, 0, 0)),
        ],
        out_specs=pl.BlockSpec((None, None, bsz, d), lambda i, j: (i, j, 0, 0)),
        out_shape=jax.ShapeDtypeStruct((depth, 6, bsz, d), F32),
        compiler_params=_cparams("arbitrary", "arbitrary"),
        name="ada_modulation",
    )(c, w_ada, b_ada.reshape(depth, 6, 1, d))
    return jnp.transpose(out, (0, 2, 1, 3))


def _tile_specs(seq, tt, d):
    nb8 = seq // 8
    per = tt // 8
    x_spec = pl.BlockSpec((None, tt, d), lambda b, t: (b, t, 0))
    prev_spec = pl.BlockSpec((None, 8, d), lambda b, t: (b, jnp.maximum(t * per - 1, 0), 0))
    next_spec = pl.BlockSpec((None, 8, d), lambda b, t: (b, jnp.minimum((t + 1) * per, nb8 - 1), 0))
    return x_spec, prev_spec, next_spec


def _mod_spec(d):
    return pl.BlockSpec((None, 6, d), lambda b, t: (b, 0, 0))


def _shifted(cur, prev_row, next_row, tt):
    row = lax.broadcasted_iota(jnp.int32, (tt, 1), 0)
    m1 = jnp.where(row == 0, prev_row, pltpu.roll(cur, 1, 0))
    p1 = jnp.where(row == tt - 1, next_row, pltpu.roll(cur, tt - 1, 0))
    return m1, p1


def _conv_kernel(x_ref, xp_ref, xn_ref, mod_ref, win_ref, cw_ref, wout_ref, lng_ref, lnb_ref,
                 o_ref, *, nt, tt, d):
    t = pl.program_id(1)
    sh, sc, gate = mod_ref[0:1, :], mod_ref[1:2, :], mod_ref[2:3, :]
    x = x_ref[...]
    h = (x * (1.0 + sc) + sh).astype(BF16)
    p = _dot(h, win_ref[...])
    z = p[:, d:2 * d] * p[:, 2 * d:]
    halo = jnp.concatenate([xp_ref[...], xn_ref[...]], axis=0)
    hh = (halo * (1.0 + sc) + sh).astype(BF16)
    ph = _dot(hh, win_ref[:, d:])
    zh = ph[:, :d] * ph[:, d:]
    z_prev = jnp.where(t > 0, zh[7:8, :], 0.0)
    z_next = jnp.where(t < nt - 1, zh[8:9, :], 0.0)
    z_m1, z_p1 = _shifted(z, z_prev, z_next, tt)
    conv = z_m1 * cw_ref[0:1, :] + z * cw_ref[1:2, :] + z_p1 * cw_ref[2:3, :]
    y = (p[:, :d] * conv).astype(BF16)
    mix = _dot(y, wout_ref[...])
    o_ref[...] = _layer_norm(ALPHA * x + gate * mix, lng_ref[...], lnb_ref[...])


def conv_mixer_layer(x, mod, w_in, conv_w, w_out, ln_g, ln_b, tt=512):
    bsz, seq, d = x.shape
    tt = min(tt, seq)
    nt = seq // tt
    x_spec, prev_spec, next_spec = _tile_specs(seq, tt, d)
    return pl.pallas_call(
        functools.partial(_conv_kernel, nt=nt, tt=tt, d=d),
        grid=(bsz, nt),
        in_specs=[x_spec, prev_spec, next_spec, _mod_spec(d),
                  _const_spec((d, 3 * d)), _const_spec((3, d)), _const_spec((d, d)),
                  _const_spec((1, d)), _const_spec((1, d))],
        out_specs=pl.BlockSpec((None, tt, d), lambda b, t: (b, t, 0)),
        out_shape=jax.ShapeDtypeStruct(x.shape, F32),
        compiler_params=_cparams("parallel", "arbitrary"),
        name="conv_mixer",
    )(x, x, x, mod, w_in, conv_w, w_out, ln_g, ln_b)


def _ffn_kernel(x_ref, mod_ref, wgu_ref, wd_ref, lng_ref, lnb_ref, o_ref, *, ff, fc):
    sh, sc, gate = mod_ref[3:4, :], mod_ref[4:5, :], mod_ref[5:6, :]
    x = x_ref[...]
    h = (x * (1.0 + sc) + sh).astype(BF16)
    acc = jnp.zeros(x.shape, F32)
    for c in range(ff // fc):
        g = _dot(h, wgu_ref[:, c * fc:(c + 1) * fc])
        u = _dot(h, wgu_ref[:, ff + c * fc:ff + (c + 1) * fc])
        act = (g * _sigmoid(g) * u).astype(BF16)
        acc = acc + _dot(act, wd_ref[c * fc:(c + 1) * fc, :])
    o_ref[...] = _layer_norm(ALPHA * x + gate * acc, lng_ref[...], lnb_ref[...])


def dense_ffn_layer(x, mod, w_gu, w_down, ln_g, ln_b, tt=512):
    bsz, seq, d = x.shape
    tt = min(tt, seq)
    ff = w_down.shape[0]
    fc = ff // 2
    return pl.pallas_call(
        functools.partial(_ffn_kernel, ff=ff, fc=fc),
        grid=(bsz, seq // tt),
        in_specs=[pl.BlockSpec((None, tt, d), lambda b, t: (b, t, 0)), _mod_spec(d),
                  _const_spec((d, 2 * ff)), _const_spec((ff, d)),
                  _const_spec((1, d)), _const_spec((1, d))],
        out_specs=pl.BlockSpec((None, tt, d), lambda b, t: (b, t, 0)),
        out_shape=jax.ShapeDtypeStruct(x.shape, F32),
        compiler_params=_cparams("parallel", "arbitrary"),
        name="dense_ffn",
    )(x, mod, w_gu, w_down, ln_g, ln_b)


def _moe_kernel(x_ref, mod_ref, wr_ref, br_ref, wgu_ref, wd_ref, lng_ref, lnb_ref, o_ref,
                h_sc, comb_sc, acc_sc, *, fe):
    e = pl.program_id(2)
    lane = lax.broadcasted_iota(jnp.int32, comb_sc.shape, 1)

    @pl.when(e == 0)
    def _():
        sh, sc = mod_ref[3:4, :], mod_ref[4:5, :]
        h = x_ref[...] * (1.0 + sc) + sh
        h_sc[...] = h.astype(BF16)
        logits = jnp.dot(h, wr_ref[...], preferred_element_type=F32,
                         precision=lax.Precision.HIGHEST) + br_ref[...]
        m = jnp.max(logits, axis=-1, keepdims=True)
        ex = jnp.exp(logits - m)
        probs = ex / jnp.sum(ex, axis=-1, keepdims=True)
        valid = lane < N_EXPERTS
        p = jnp.where(valid, probs, -1.0)
        p1 = jnp.max(p, axis=-1, keepdims=True)
        i1 = jnp.min(jnp.where(p == p1, lane, ROUTER_LANES), axis=-1, keepdims=True)
        pr = jnp.where(lane == i1, -1.0, p)
        p2 = jnp.max(pr, axis=-1, keepdims=True)
        i2 = jnp.min(jnp.where(pr == p2, lane, ROUTER_LANES), axis=-1, keepdims=True)
        tot = p1 + p2
        comb_sc[...] = jnp.where(lane == i1, p1 / tot, 0.0) + jnp.where(lane == i2, p2 / tot, 0.0)
        acc_sc[...] = jnp.zeros(acc_sc.shape, F32)

    h = h_sc[...]
    g = _dot(h, wgu_ref[:, :fe])
    u = _dot(h, wgu_ref[:, fe:])
    act = (g * _sigmoid(g) * u).astype(BF16)
    y = _dot(act, wd_ref[...])
    w_e = jnp.sum(jnp.where(lane == e, comb_sc[...], 0.0), axis=-1, keepdims=True)
    acc_sc[...] += w_e * y

    @pl.when(e == N_EXPERTS - 1)
    def _():
        gate = mod_ref[5:6, :]
        o_ref[...] = _layer_norm(ALPHA * x_ref[...] + gate * acc_sc[...], lng_ref[...], lnb_ref[...])


def moe_ffn_layer(x, mod, w_router, b_router, w_gu, w_down, ln_g, ln_b, tt=512):
    bsz, seq, d = x.shape
    tt = min(tt, seq)
    n_e, fe = w_down.shape[0], w_down.shape[1]
    return pl.pallas_call(
        functools.partial(_moe_kernel, fe=fe),
        grid=(bsz, seq // tt, n_e),
        in_specs=[pl.BlockSpec((None, tt, d), lambda b, t, e: (b, t, 0)),
                  pl.BlockSpec((None, 6, d), lambda b, t, e: (b, 0, 0)),
                  _const_spec((d, ROUTER_LANES)), _const_spec((1, ROUTER_LANES)),
                  pl.BlockSpec((None, d, 2 * fe), lambda b, t, e: (e, 0, 0)),
                  pl.BlockSpec((None, fe, d), lambda b, t, e: (e, 0, 0)),
                  _const_spec((1, d)), _const_spec((1, d))],
        out_specs=pl.BlockSpec((None, tt, d), lambda b, t, e: (b, t, 0)),
        out_shape=jax.ShapeDtypeStruct(x.shape, F32),
        scratch_shapes=[pltpu.VMEM((tt, d), BF16), pltpu.VMEM((tt, ROUTER_LANES), F32),
                        pltpu.VMEM((tt, d), F32)],
        compiler_params=_cparams("parallel", "arbitrary", "arbitrary"),
        name="moe_ffn",
    )(x, mod, w_router, b_router, w_gu, w_down, ln_g, ln_b)


def _pad_router(w_router, b_router):
    d, n_e = w_router.shape
    w = jnp.zeros((d, ROUTER_LANES), F32).at[:, :n_e].set(w_router)
    b = jnp.full((1, ROUTER_LANES), NEG_INF, F32).at[0, :n_e].set(b_router)
    return w, b


def _qkv_kernel(x_ref, mod_ref, w_ref, q_ref, k_ref, v_ref, *, d):
    sh, sc = mod_ref[0:1, :], mod_ref[1:2, :]
    h = (x_ref[...] * (1.0 + sc) + sh).astype(BF16)
    p = _dot(h, w_ref[...])
    q_ref[...] = (p[:, :d] * (NA_HEAD_DIM ** -0.5)).astype(BF16)
    k_ref[...] = p[:, d:2 * d].astype(BF16)
    v_ref[...] = p[:, 2 * d:].astype(BF16)


def _na_kernel(q_ref, k_ref, v_ref, bias_ref, o_ref, *, rows):
    r = pl.program_id(1)
    r0 = jnp.clip(r - NA_WIN_ROWS // 2, 0, rows - NA_WIN_ROWS)
    start = pl.multiple_of(r0 * GRID_W, GRID_W)
    nk = NA_WIN_ROWS * GRID_W
    for hd in range(NA_HEADS):
        sl = slice(hd * NA_HEAD_DIM, (hd + 1) * NA_HEAD_DIM)
        s = _dot_nt(q_ref[:, sl], k_ref[pl.ds(start, nk), sl]) + bias_ref[hd]
        m = jnp.max(s, axis=-1, keepdims=True)
        p = jnp.exp(s - m)
        l = jnp.sum(p, axis=-1, keepdims=True)
        o = _dot(p.astype(BF16), v_ref[pl.ds(start, nk), sl]) / l
        o_ref[:, sl] = o.astype(BF16)


def _na_bias_table(rpb):
    nh, nr, nc = rpb.shape
    w = GRID_W
    qc = jnp.arange(w)[:, None]
    kc = jnp.arange(w)[None, :]
    cs = jnp.clip(qc - NA_WIN_COLS // 2, 0, w - NA_WIN_COLS)
    col_ok = (kc >= cs) & (kc < cs + NA_WIN_COLS)
    lo = w - NA_WIN_COLS
    e = jnp.pad(rpb, ((0, 0), (0, 0), (lo, 2 * w - lo - nc)))
    flat = jnp.tile(e, (1, 1, w))
    toep = flat[:, :, w - 1:w - 1 + w * (2 * w - 1)].reshape(nh, nr, w, 2 * w - 1)[..., :w]
    toep = jnp.where(col_ok[None, None], toep, NEG_INF)
    tab = jnp.stack([toep[:, dl:dl + NA_WIN_ROWS] for dl in range(NA_WIN_ROWS)], axis=1)
    tab = jnp.transpose(tab, (0, 1, 3, 2, 4))
    return tab.reshape(nh, NA_WIN_ROWS, w, NA_WIN_ROWS * w)


def _proj_ln_kernel(*refs, n_in):
    a_refs = refs[:n_in]
    x_ref, mod_ref, w_ref, lng_ref, lnb_ref, o_ref = refs[n_in:]
    a = a_refs[0][...]
    if n_in == 2:
        a = (a.astype(F32) + a_refs[1][...].astype(F32)).astype(BF16)
    mix = _dot(a, w_ref[...])
    gate = mod_ref[2:3, :]
    o_ref[...] = _layer_norm(ALPHA * x_ref[...] + gate * mix, lng_ref[...], lnb_ref[...])


def proj_residual_ln(acts, x, mod, w_o, ln_g, ln_b, tt=512):
    bsz, seq, d = x.shape
    tt = min(tt, seq)
    tile = pl.BlockSpec((None, tt, d), lambda b, t: (b, t, 0))
    return pl.pallas_call(
        functools.partial(_proj_ln_kernel, n_in=len(acts)),
        grid=(bsz, seq // tt),
        in_specs=[tile] * len(acts) + [tile, _mod_spec(d), _const_spec((d, d)),
                                       _const_spec((1, d)), _const_spec((1, d))],
        out_specs=tile,
        out_shape=jax.ShapeDtypeStruct(x.shape, F32),
        compiler_params=_cparams("parallel", "arbitrary"),
        name="proj_residual_ln",
    )(*acts, x, mod, w_o, ln_g, ln_b)


def na_mixer_layer(x, mod, w_qkv, bias_tab, w_o, ln_g, ln_b, tt=512):
    bsz, seq, d = x.shape
    tt = min(tt, seq)
    rows = seq // GRID_W
    assert rows >= NA_WIN_ROWS and seq % GRID_W == 0
    tile = pl.BlockSpec((None, tt, d), lambda b, t: (b, t, 0))
    q, k, v = pl.pallas_call(
        functools.partial(_qkv_kernel, d=d),
        grid=(bsz, seq // tt),
        in_specs=[tile, _mod_spec(d), _const_spec((d, 3 * d))],
        out_specs=[tile] * 3,
        out_shape=[jax.ShapeDtypeStruct(x.shape, BF16)] * 3,
        compiler_params=_cparams("parallel", "arbitrary"),
        name="na_qkv",
    )(x, mod, w_qkv)

    def delta(r):
        return jnp.clip(r - NA_WIN_ROWS // 2, 0, rows - NA_WIN_ROWS) - r + NA_WIN_ROWS - 1

    row_spec = pl.BlockSpec((None, GRID_W, d), lambda b, r: (b, r, 0))
    seq_spec = pl.BlockSpec((None, seq, d), lambda b, r: (b, 0, 0))
    att = pl.pallas_call(
        functools.partial(_na_kernel, rows=rows),
        grid=(bsz, rows),
        in_specs=[row_spec, seq_spec, seq_spec,
                  pl.BlockSpec((NA_HEADS, None, GRID_W, NA_WIN_ROWS * GRID_W),
                               lambda b, r: (0, delta(r), 0, 0))],
        out_specs=row_spec,
        out_shape=jax.ShapeDtypeStruct(x.shape, BF16),
        compiler_params=_cparams("parallel", "arbitrary"),
        name="na_attention",
    )(q, k, v, bias_tab)
    return proj_residual_ln([att], x, mod, w_o, ln_g, ln_b, tt)


def _softplus(y):
    return jnp.maximum(y, 0.0) + jnp.log(1.0 + jnp.exp(-jnp.abs(y)))


def _rk_proj_kernel(x_ref, xp_ref, xn_ref, mod_ref, mu_ref, wrkv_ref, w1_ref, w2_ref, a1_ref, a2_ref,
                    g1_ref, g2_ref, w0_ref, a0_ref, kk_ref, ka_ref, seg_ref,
                    r_o, v_o, kkn_o, lw_o, km_o, b_o, gate_o, *, nt, tt):
    t = pl.program_id(1)
    sh, sc = mod_ref[0:1, :], mod_ref[1:2, :]
    h = x_ref[...] * (1.0 + sc) + sh
    h_prev = jnp.where(t > 0, xp_ref[7:8, :] * (1.0 + sc) + sh, 0.0)
    h_next = jnp.where(t < nt - 1, xn_ref[0:1, :] * (1.0 + sc) + sh, 0.0)
    h_m1, h_p1 = _shifted(h, h_prev, h_next, tt)
    xx = 0.5 * (h_m1 + h_p1) - h

    def mixed(p):
        return (h + xx * mu_ref[p:p + 1, :]).astype(BF16)

    r = _dot(mixed(0), wrkv_ref[0])
    k = _dot(mixed(1), wrkv_ref[1])
    v = _dot(mixed(2), wrkv_ref[2])
    r_o[...] = r.astype(BF16)
    v_o[...] = v.astype(BF16)
    lw_in = jnp.tanh(_dot(mixed(3), w1_ref[...])).astype(BF16)
    a_in = _dot(mixed(4), a1_ref[...]).astype(BF16)
    g_in = _sigmoid(_dot(mixed(5), g1_ref[...])).astype(BF16)

    kk = k * kk_ref[...]
    sq = kk * kk
    sq_hi = sq.astype(BF16)
    sq_lo = (sq - sq_hi.astype(F32)).astype(BF16)
    ss = _dot(sq_hi, seg_ref[...]) + _dot(sq_lo, seg_ref[...])
    kkn = kk / jnp.maximum(jnp.sqrt(ss), 1e-12)
    kkn_o[...] = kkn.astype(BF16)
    rw = w1_ref.shape[1] // 2
    rg = g1_ref.shape[1] // 2
    for z in range(2):
        wl = w0_ref[z:z + 1, :] + _dot(lw_in[:, z * rw:(z + 1) * rw], w2_ref[z])
        w_log = -_softplus(-wl) - 0.5
        lw_o[z] = -jnp.exp(w_log)
        a_lr = _sigmoid(a0_ref[z:z + 1, :] + _dot(a_in[:, z * rw:(z + 1) * rw], a2_ref[z]))
        km_o[z] = (k * (1.0 + (a_lr - 1.0) * ka_ref[...])).astype(BF16)
        b_o[z] = (kkn * a_lr).astype(BF16)
        gate_o[z] = _dot(g_in[:, z * rg:(z + 1) * rg], g2_ref[z]).astype(BF16)


def _rk_scan_kernel(r_ref, v_ref, kkn_ref, lw_ref, km_ref, b_ref, gate_ref, rk_ref, lg_ref, lb_ref,
                    o_ref, s_ref, *, reverse):
    tb, hb = r_ref.shape
    cl = RK_CHUNK
    nc, nh = tb // cl, hb // RK_HEAD_DIM

    @pl.when(pl.program_id(2) == 0)
    def _():
        s_ref[...] = jnp.zeros(s_ref.shape, F32)

    ri = lax.broadcasted_iota(jnp.int32, (tb, tb), 0)
    ci = lax.broadcasted_iota(jnp.int32, (tb, tb), 1)
    ordered = (ci >= ri) if reverse else (ci <= ri)
    tri = jnp.where((ri // cl == ci // cl) & ordered, 1.0, 0.0).astype(BF16)
    lw = lw_ref[...]
    lw_hi = lw.astype(BF16)
    lw_lo = (lw - lw_hi.astype(F32)).astype(BF16)
    cum = _dot(tri, lw_hi) + _dot(tri, lw_lo)
    last = 0 if reverse else cl - 1
    tot_rows = [cum[c * cl + last:c * cl + last + 1, :] for c in range(nc)]
    tot = jnp.concatenate([jnp.broadcast_to(tr, (cl, hb)) for tr in tot_rows], axis=0)

    rr = r_ref[...].astype(F32)
    vv = v_ref[...]
    kkn = kkn_ref[...].astype(F32)
    km = km_ref[...].astype(F32)
    bb = b_ref[...].astype(F32)
    e_neg = jnp.exp(-cum)
    e_rem = jnp.exp(tot - cum)
    a_t = (-kkn * jnp.exp(cum - lw)).astype(BF16)
    r_t = (rr * jnp.exp(cum)).astype(BF16)
    b_t = (bb * e_neg).astype(BF16)
    k_t = (km * e_neg).astype(BF16)
    b_h = (bb * e_rem).astype(BF16)
    k_h = (km * e_rem).astype(BF16)
    bonus_w = rr * km * rk_ref[...]

    i0 = lax.broadcasted_iota(jnp.int32, (cl, cl), 0)
    i1 = lax.broadcasted_iota(jnp.int32, (cl, cl), 1)
    strict = (i0 < i1) if reverse else (i0 > i1)
    incl = (i0 <= i1) if reverse else (i0 >= i1)
    eye = jnp.where(i0 == i1, 1.0, 0.0)

    keys = [(c, j) for c in range(nc) for j in range(nh)]
    rsl = {k: slice(k[0] * cl, (k[0] + 1) * cl) for k in keys}
    lsl = {k: slice(k[1] * RK_HEAD_DIM, (k[1] + 1) * RK_HEAD_DIM) for k in keys}
    blk = lambda arr, k: arr[rsl[k], lsl[k]]

    bk = {k: jnp.concatenate([blk(b_t, k), blk(k_t, k)], axis=0) for k in keys}
    ga = {k: _dot_nt(blk(a_t, k), bk[k]) for k in keys}
    gr = {k: _dot_nt(blk(r_t, k), bk[k]) for k in keys}
    m_ab = {k: jnp.where(strict, ga[k][:, :cl], 0.0) for k in keys}
    m_ak = {k: jnp.where(strict, ga[k][:, cl:], 0.0).astype(BF16) for k in keys}
    m_rb = {k: jnp.where(incl, gr[k][:, :cl], 0.0).astype(BF16) for k in keys}
    m_rk = {k: jnp.where(incl, gr[k][:, cl:], 0.0).astype(BF16) for k in keys}
    mv = {k: _dot(m_ak[k], blk(vv, k)).astype(BF16) for k in keys}
    y0 = {k: _dot(m_rk[k], blk(vv, k)) for k in keys}
    kv = {k: _dot_tn(blk(k_h, k), blk(vv, k)) for k in keys}
    inv = {k: eye + m_ab[k] for k in keys}
    pwb = {k: m_ab[k].astype(BF16) for k in keys}
    pw = {k: _dot(pwb[k], pwb[k]) for k in keys}
    n = 4
    while n < cl:
        pwb = {k: pw[k].astype(BF16) for k in keys}
        res = {k: _dot(pwb[k], jnp.concatenate([pwb[k], inv[k].astype(BF16)], axis=1)) for k in keys}
        pw = {k: res[k][:, :cl] for k in keys}
        inv = {k: inv[k] + res[k][:, cl:] for k in keys}
        n *= 2
    inv = {k: (inv[k] + _dot(pw[k].astype(BF16), inv[k].astype(BF16))).astype(BF16) for k in keys}
    au = {k: _dot(inv[k], jnp.concatenate([blk(a_t, k), mv[k]], axis=1)) for k in keys}
    ar = {k: jnp.concatenate([au[k][:, :RK_HEAD_DIM].astype(BF16), blk(r_t, k)], axis=0) for k in keys}

    heads = range(nh)
    for c in (range(nc - 1, -1, -1) if reverse else range(nc)):
        st = {j: s_ref[j] for j in heads}
        res = {j: _dot(ar[c, j], st[j].astype(BF16)) for j in heads}
        ub = {j: (au[c, j][:, RK_HEAD_DIM:] + res[j][:cl]).astype(BF16) for j in heads}
        y = {j: y0[c, j] + res[j][cl:] + _dot(m_rb[c, j], ub[j]) for j in heads}
        for j in heads:
            k = (c, j)
            decay = jnp.exp(tot_rows[c][:, lsl[k]])
            s_ref[j] = st[j] * jnp.transpose(jnp.broadcast_to(decay, (RK_HEAD_DIM, RK_HEAD_DIM))) \
                + _dot_tn(blk(b_h, k), ub[j]) + kv[k]
        for j in heads:
            k = (c, j)
            mean = jnp.mean(y[j], axis=-1, keepdims=True)
            dy = y[j] - mean
            var = jnp.mean(dy * dy, axis=-1, keepdims=True)
            yn = dy * lax.rsqrt(var + RK_GN_EPS) * lg_ref[:, lsl[k]] + lb_ref[:, lsl[k]]
            bonus = jnp.sum(blk(bonus_w, k), axis=-1, keepdims=True) * blk(vv, k).astype(F32)
            o_ref[rsl[k], lsl[k]] = (blk(gate_ref, k).astype(F32) * (yn + bonus)).astype(BF16)


def rwkv_mixer_layer(x, mod, p, ln_g, ln_b, tt=256):
    bsz, seq, d = x.shape
    tt = min(tt, seq)
    nt = seq // tt
    x_spec, prev_spec, next_spec = _tile_specs(seq, tt, d)
    tile = pl.BlockSpec((None, tt, d), lambda b, t: (b, t, 0))
    tile2 = pl.BlockSpec((2, None, tt, d), lambda b, t: (0, b, t, 0))
    rw, rg = p['w1'].shape[1] // 2, p['g1'].shape[1] // 2
    sh_bf = jax.ShapeDtypeStruct(x.shape, BF16)
    sh2_bf = jax.ShapeDtypeStruct((2,) + x.shape, BF16)
    sh2_f32 = jax.ShapeDtypeStruct((2,) + x.shape, F32)
    r, v, kkn, lw, km, bz, gate = pl.pallas_call(
        functools.partial(_rk_proj_kernel, nt=nt, tt=tt),
        grid=(bsz, nt),
        in_specs=[x_spec, prev_spec, next_spec, _mod_spec(d), _const_spec((6, d)),
                  _const_spec((3, d, d)),
                  _const_spec((d, 2 * rw)), _const_spec((2, rw, d)),
                  _const_spec((d, 2 * rw)), _const_spec((2, rw, d)),
                  _const_spec((d, 2 * rg)), _const_spec((2, rg, d)),
                  _const_spec((2, d)), _const_spec((2, d)), _const_spec((1, d)), _const_spec((1, d)),
                  _const_spec((d, d))],
        out_specs=[tile, tile, tile, tile2, tile2, tile2, tile2],
        out_shape=[sh_bf, sh_bf, sh_bf, sh2_f32, sh2_bf, sh2_bf, sh2_bf],
        compiler_params=_cparams("parallel", "arbitrary"),
        name="rwkv_proj",
    )(x, x, x, mod, p['mu'], p['w_rkv'], p['w1'], p['w2'], p['a1'], p['a2'], p['g1'], p['g2'],
      p['w0'], p['a0'], p['k_k'], p['k_a'], p['seg'])

    tb = min(RK_TOK_BLOCK, seq)
    hb = RK_LANE_BLOCK
    nb = seq // tb
    outs = []
    for z in range(2):
        rev = z == 1

        def tok(i, rev=rev):
            return nb - 1 - i if rev else i

        blk = pl.BlockSpec((None, tb, hb), lambda b, g, i: (b, tok(i), g))
        blk2 = pl.BlockSpec((None, None, tb, hb), lambda b, g, i, z=z: (z, b, tok(i), g))
        vec = pl.BlockSpec((1, hb), lambda b, g, i: (0, g))
        vec2 = pl.BlockSpec((None, 1, hb), lambda b, g, i, z=z: (z, 0, g))
        outs.append(pl.pallas_call(
            functools.partial(_rk_scan_kernel, reverse=rev),
            grid=(bsz, d // hb, nb),
            in_specs=[blk, blk, blk, blk2, blk2, blk2, blk2, vec, vec2, vec2],
            out_specs=blk,
            out_shape=sh_bf,
            scratch_shapes=[pltpu.VMEM((hb // RK_HEAD_DIM, RK_HEAD_DIM, RK_HEAD_DIM), F32)],
            compiler_params=_cparams("parallel", "parallel", "arbitrary"),
            name="rwkv_scan_rev" if rev else "rwkv_scan_fwd",
        )(r, v, kkn, lw, km, bz, gate, p['r_k'], p['lnx_g'], p['lnx_b']))
    return proj_residual_ln(outs, x, mod, p['w_o'], ln_g, ln_b)


def _prepare_params(w):
    d = D_MODEL
    bf = lambda a: a.astype(BF16)
    row = lambda a: a.reshape(1, d)
    lane = jnp.arange(d) // RK_HEAD_DIM
    p = {
        'w_ada': w['w_ada'], 'b_ada': w['b_ada'],
        'ln_g': w['ln_g'], 'ln_b': w['ln_b'],
        'conv_w_in': bf(w['conv_w_in']), 'conv_w': w['conv_w'], 'conv_w_out': bf(w['conv_w_out']),
        'na_w_qkv': bf(w['na_w_qkv']), 'na_w_o': bf(w['na_w_o']),
        'na_bias': [_na_bias_table(w['na_rpb'][j]) for j in range(w['na_rpb'].shape[0])],
        'ffn_w_gu': bf(w['ffn_w_gu']), 'ffn_w_down': bf(w['ffn_w_down']),
        'moe_router': [_pad_router(w['moe_w_router'][j], w['moe_b_router'][j])
                       for j in range(w['moe_w_router'].shape[0])],
        'moe_w_gu': bf(w['moe_w_gu']), 'moe_w_down': bf(w['moe_w_down']),
        'rk': [],
    }
    cat = lambda a: jnp.concatenate([a[0], a[1]], axis=1)
    for j in range(w['rk_mu'].shape[0]):
        p['rk'].append({
            'mu': w['rk_mu'][j], 'w_rkv': bf(w['rk_w_rkv'][j]),
            'w1': bf(cat(w['rk_w1'][j])), 'w2': bf(w['rk_w2'][j]),
            'a1': bf(cat(w['rk_a1'][j])), 'a2': bf(w['rk_a2'][j]),
            'g1': bf(cat(w['rk_g1'][j])), 'g2': bf(w['rk_g2'][j]),
            'w0': w['rk_w0'][j], 'a0': w['rk_a0'][j],
            'k_k': row(w['rk_k_k'][j]), 'k_a': row(w['rk_k_a'][j]),
            'r_k': w['rk_r_k'][j].reshape(1, d),
            'lnx_g': w['rk_lnx_g'][j].reshape(2, 1, d), 'lnx_b': w['rk_lnx_b'][j].reshape(2, 1, d),
            'w_o': bf(w['rk_w_o'][j]),
            'seg': (lane[:, None] == lane[None, :]).astype(BF16),
        })
    return p


def _trunk(x, c, p):
    d = D_MODEL
    mod_all = ada_modulation(c, p['w_ada'], p['b_ada'])
    for i in range(DEPTH):
        mod = mod_all[i]
        lng = lambda s: p['ln_g'][i, s].reshape(1, d)
        lnb = lambda s: p['ln_b'][i, s].reshape(1, d)
        kind, j = i % 3, i // 3
        if kind == 0:
            x = conv_mixer_layer(x, mod, p['conv_w_in'][j], p['conv_w'][j], p['conv_w_out'][j],
                                 lng(0), lnb(0))
        elif kind == 1:
            x = na_mixer_layer(x, mod, p['na_w_qkv'][j], p['na_bias'][j], p['na_w_o'][j], lng(0), lnb(0))
        else:
            x = rwkv_mixer_layer(x, mod, p['rk'][j], lng(0), lnb(0))
        if i % 2 == 0:
            x = dense_ffn_layer(x, mod, p['ffn_w_gu'][i // 2], p['ffn_w_down'][i // 2], lng(1), lnb(1))
        else:
            wr, br = p['moe_router'][i // 2]
            x = moe_ffn_layer(x, mod, wr, br, p['moe_w_gu'][i // 2], p['moe_w_down'][i // 2],
                              lng(1), lnb(1))
    return x


def kernel(x_prompt, x_sample, c_prompt, c_sample, w_ada, b_ada, ln_g, ln_b, conv_w_in, conv_w, conv_w_out, na_w_qkv, na_rpb, na_w_o, rk_mu, rk_w_rkv, rk_w0, rk_w1, rk_w2, rk_a0, rk_a1, rk_a2, rk_g1, rk_g2, rk_k_k, rk_k_a, rk_r_k, rk_lnx_g, rk_lnx_b, rk_w_o, ffn_w_gu, ffn_w_down, moe_w_router, moe_b_router, moe_w_gu, moe_w_down):
    p = _prepare_params(dict(
        w_ada=w_ada, b_ada=b_ada, ln_g=ln_g, ln_b=ln_b,
        conv_w_in=conv_w_in, conv_w=conv_w, conv_w_out=conv_w_out,
        na_w_qkv=na_w_qkv, na_rpb=na_rpb, na_w_o=na_w_o,
        rk_mu=rk_mu, rk_w_rkv=rk_w_rkv, rk_w0=rk_w0, rk_w1=rk_w1, rk_w2=rk_w2,
        rk_a0=rk_a0, rk_a1=rk_a1, rk_a2=rk_a2, rk_g1=rk_g1, rk_g2=rk_g2,
        rk_k_k=rk_k_k, rk_k_a=rk_k_a, rk_r_k=rk_r_k, rk_lnx_g=rk_lnx_g, rk_lnx_b=rk_lnx_b,
        rk_w_o=rk_w_o, ffn_w_gu=ffn_w_gu, ffn_w_down=ffn_w_down,
        moe_w_router=moe_w_router, moe_b_router=moe_b_router,
        moe_w_gu=moe_w_gu, moe_w_down=moe_w_down))
    return (_trunk(x_prompt, c_prompt, p), _trunk(x_sample, c_sample, p))
```

```python
import functools

import jax
import jax.numpy as jnp
from jax import lax
from jax.experimental import pallas as pl
from jax.experimental.pallas import tpu as pltpu

F32 = jnp.float32
BF16 = jnp.bfloat16

D_MODEL = 1024
DEPTH = 4
ALPHA = (2 * DEPTH) ** 0.25
LN_EPS = 1e-5

GRID_W = 64
NA_HEADS = 16
NA_HEAD_DIM = D_MODEL // NA_HEADS
NA_WIN_ROWS = 8
NA_WIN_COLS = 16
NEG_INF = -1e30

RK_HEAD_DIM = 64
RK_GN_EPS = 64e-5
RK_CHUNK = 64
RK_TOK_BLOCK = 128
RK_LANE_BLOCK = 1024

N_EXPERTS = 8
ROUTER_LANES = 128

VMEM_LIMIT = 52 * 1024 * 1024


def _cparams(*sem):
    return pltpu.CompilerParams(dimension_semantics=sem, vmem_limit_bytes=VMEM_LIMIT)


def _const_spec(shape):
    nd = len(shape)
    return pl.BlockSpec(shape, lambda *_: (0,) * nd, pipeline_mode=pl.Buffered(1))


def _dot(a, b):
    return jnp.dot(a, b, preferred_element_type=F32)


def _dot_nt(a, b):
    return lax.dot_general(a, b, (((1,), (1,)), ((), ())), preferred_element_type=F32)


def _dot_tn(a, b):
    return lax.dot_general(a, b, (((0,), (0,)), ((), ())), preferred_element_type=F32)


def _sigmoid(x):
    return 1.0 / (1.0 + jnp.exp(-x))


def _layer_norm(y, g, b):
    mu = jnp.mean(y, axis=-1, keepdims=True)
    d = y - mu
    var = jnp.mean(d * d, axis=-1, keepdims=True)
    return d * lax.rsqrt(var + LN_EPS) * g + b


def _ada_kernel(c_ref, w_ref, b_ref, o_ref):
    c = c_ref[...]
    s = (c * _sigmoid(c)).astype(BF16)
    o_ref[...] = _dot(s, w_ref[...].astype(BF16)) + b_ref[...]


def ada_modulation(c, w_ada, b_ada):
    bsz, d = c.shape
    depth = w_ada.shape[0]
    out = pl.pallas_call(
        _ada_kernel,
        grid=(depth, 6),
        in_specs=[
            pl.BlockSpec((bsz, d), lambda i, j: (0, 0)),
            pl.BlockSpec((None, d, d), lambda i, j: (i, 0, j)),
            pl.BlockSpec((None, None, 1, d), lambda i, j: (i, j, 0, 0)),
        ],
        out_specs=pl.BlockSpec((None, None, bsz, d), lambda i, j: (i, j, 0, 0)),
        out_shape=jax.ShapeDtypeStruct((depth, 6, bsz, d), F32),
        compiler_params=_cparams("arbitrary", "arbitrary"),
        name="ada_modulation",
    )(c, w_ada, b_ada.reshape(depth, 6, 1, d))
    return jnp.transpose(out, (0, 2, 1, 3))


def _tile_specs(seq, tt, d):
    nb8 = seq // 8
    per = tt // 8
    x_spec = pl.BlockSpec((None, tt, d), lambda b, t: (b, t, 0))
    prev_spec = pl.BlockSpec((None, 8, d), lambda b, t: (b, jnp.maximum(t * per - 1, 0), 0))
    next_spec = pl.BlockSpec((None, 8, d), lambda b, t: (b, jnp.minimum((t + 1) * per, nb8 - 1), 0))
    return x_spec, prev_spec, next_spec


def _mod_spec(d):
    return pl.BlockSpec((None, 6, d), lambda b, t: (b, 0, 0))


def _shifted(cur, prev_row, next_row, tt):
    row = lax.broadcasted_iota(jnp.int32, (tt, 1), 0)
    m1 = jnp.where(row == 0, prev_row, pltpu.roll(cur, 1, 0))
    p1 = jnp.where(row == tt - 1, next_row, pltpu.roll(cur, tt - 1, 0))
    return m1, p1


def _conv_kernel(x_ref, xp_ref, xn_ref, mod_ref, win_ref, cw_ref, wout_ref, lng_ref, lnb_ref,
                 o_ref, *, nt, tt, d):
    t = pl.program_id(1)
    sh, sc, gate = mod_ref[0:1, :], mod_ref[1:2, :], mod_ref[2:3, :]
    x = x_ref[...]
    h = (x * (1.0 + sc) + sh).astype(BF16)
    p = _dot(h, win_ref[...])
    z = p[:, d:2 * d] * p[:, 2 * d:]
    halo = jnp.concatenate([xp_ref[...], xn_ref[...]], axis=0)
    hh = (halo * (1.0 + sc) + sh).astype(BF16)
    ph = _dot(hh, win_ref[:, d:])
    zh = ph[:, :d] * ph[:, d:]
    z_prev = jnp.where(t > 0, zh[7:8, :], 0.0)
    z_next = jnp.where(t < nt - 1, zh[8:9, :], 0.0)
    z_m1, z_p1 = _shifted(z, z_prev, z_next, tt)
    conv = z_m1 * cw_ref[0:1, :] + z * cw_ref[1:2, :] + z_p1 * cw_ref[2:3, :]
    y = (p[:, :d] * conv).astype(BF16)
    mix = _dot(y, wout_ref[...])
    o_ref[...] = _layer_norm(ALPHA * x + gate * mix, lng_ref[...], lnb_ref[...])


def conv_mixer_layer(x, mod, w_in, conv_w, w_out, ln_g, ln_b, tt=512):
    bsz, seq, d = x.shape
    tt = min(tt, seq)
    nt = seq // tt
    x_spec, prev_spec, next_spec = _tile_specs(seq, tt, d)
    return pl.pallas_call(
        functools.partial(_conv_kernel, nt=nt, tt=tt, d=d),
        grid=(bsz, nt),
        in_specs=[x_spec, prev_spec, next_spec, _mod_spec(d),
                  _const_spec((d, 3 * d)), _const_spec((3, d)), _const_spec((d, d)),
                  _const_spec((1, d)), _const_spec((1, d))],
        out_specs=pl.BlockSpec((None, tt, d), lambda b, t: (b, t, 0)),
        out_shape=jax.ShapeDtypeStruct(x.shape, F32),
        compiler_params=_cparams("parallel", "arbitrary"),
        name="conv_mixer",
    )(x, x, x, mod, w_in, conv_w, w_out, ln_g, ln_b)


def _ffn_kernel(x_ref, mod_ref, wgu_ref, wd_ref, lng_ref, lnb_ref, o_ref, *, ff, fc):
    sh, sc, gate = mod_ref[3:4, :], mod_ref[4:5, :], mod_ref[5:6, :]
    x = x_ref[...]
    h = (x * (1.0 + sc) + sh).astype(BF16)
    acc = jnp.zeros(x.shape, F32)
    for c in range(ff // fc):
        g = _dot(h, wgu_ref[:, c * fc:(c + 1) * fc])
        u = _dot(h, wgu_ref[:, ff + c * fc:ff + (c + 1) * fc])
        act = (g * _sigmoid(g) * u).astype(BF16)
        acc = acc + _dot(act, wd_ref[c * fc:(c + 1) * fc, :])
    o_ref[...] = _layer_norm(ALPHA * x + gate * acc, lng_ref[...], lnb_ref[...])


def dense_ffn_layer(x, mod, w_gu, w_down, ln_g, ln_b, tt=512):
    bsz, seq, d = x.shape
    tt = min(tt, seq)
    ff = w_down.shape[0]
    fc = ff // 2
    return pl.pallas_call(
        functools.partial(_ffn_kernel, ff=ff, fc=fc),
        grid=(bsz, seq // tt),
        in_specs=[pl.BlockSpec((None, tt, d), lambda b, t: (b, t, 0)), _mod_spec(d),
                  _const_spec((d, 2 * ff)), _const_spec((ff, d)),
                  _const_spec((1, d)), _const_spec((1, d))],
        out_specs=pl.BlockSpec((None, tt, d), lambda b, t: (b, t, 0)),
        out_shape=jax.ShapeDtypeStruct(x.shape, F32),
        compiler_params=_cparams("parallel", "arbitrary"),
        name="dense_ffn",
    )(x, mod, w_gu, w_down, ln_g, ln_b)


def _moe_kernel(x_ref, mod_ref, wr_ref, br_ref, wgu_ref, wd_ref, lng_ref, lnb_ref, o_ref,
                h_sc, comb_sc, acc_sc, *, fe):
    e = pl.program_id(2)
    lane = lax.broadcasted_iota(jnp.int32, comb_sc.shape, 1)

    @pl.when(e == 0)
    def _():
        sh, sc = mod_ref[3:4, :], mod_ref[4:5, :]
        h = x_ref[...] * (1.0 + sc) + sh
        h_sc[...] = h.astype(BF16)
        logits = jnp.dot(h, wr_ref[...], preferred_element_type=F32,
                         precision=lax.Precision.HIGHEST) + br_ref[...]
        m = jnp.max(logits, axis=-1, keepdims=True)
        ex = jnp.exp(logits - m)
        probs = ex / jnp.sum(ex, axis=-1, keepdims=True)
        valid = lane < N_EXPERTS
        p = jnp.where(valid, probs, -1.0)
        p1 = jnp.max(p, axis=-1, keepdims=True)
        i1 = jnp.min(jnp.where(p == p1, lane, ROUTER_LANES), axis=-1, keepdims=True)
        pr = jnp.where(lane == i1, -1.0, p)
        p2 = jnp.max(pr, axis=-1, keepdims=True)
        i2 = jnp.min(jnp.where(pr == p2, lane, ROUTER_LANES), axis=-1, keepdims=True)
        tot = p1 + p2
        comb_sc[...] = jnp.where(lane == i1, p1 / tot, 0.0) + jnp.where(lane == i2, p2 / tot, 0.0)
        acc_sc[...] = jnp.zeros(acc_sc.shape, F32)

    h = h_sc[...]
    g = _dot(h, wgu_ref[:, :fe])
    u = _dot(h, wgu_ref[:, fe:])
    act = (g * _sigmoid(g) * u).astype(BF16)
    y = _dot(act, wd_ref[...])
    w_e = jnp.sum(jnp.where(lane == e, comb_sc[...], 0.0), axis=-1, keepdims=True)
    acc_sc[...] += w_e * y

    @pl.when(e == N_EXPERTS - 1)
    def _():
        gate = mod_ref[5:6, :]
        o_ref[...] = _layer_norm(ALPHA * x_ref[...] + gate * acc_sc[...], lng_ref[...], lnb_ref[...])


def moe_ffn_layer(x, mod, w_router, b_router, w_gu, w_down, ln_g, ln_b, tt=512):
    bsz, seq, d = x.shape
    tt = min(tt, seq)
    n_e, fe = w_down.shape[0], w_down.shape[1]
    return pl.pallas_call(
        functools.partial(_moe_kernel, fe=fe),
        grid=(bsz, seq // tt, n_e),
        in_specs=[pl.BlockSpec((None, tt, d), lambda b, t, e: (b, t, 0)),
                  pl.BlockSpec((None, 6, d), lambda b, t, e: (b, 0, 0)),
                  _const_spec((d, ROUTER_LANES)), _const_spec((1, ROUTER_LANES)),
                  pl.BlockSpec((None, d, 2 * fe), lambda b, t, e: (e, 0, 0)),
                  pl.BlockSpec((None, fe, d), lambda b, t, e: (e, 0, 0)),
                  _const_spec((1, d)), _const_spec((1, d))],
        out_specs=pl.BlockSpec((None, tt, d), lambda b, t, e: (b, t, 0)),
        out_shape=jax.ShapeDtypeStruct(x.shape, F32),
        scratch_shapes=[pltpu.VMEM((tt, d), BF16), pltpu.VMEM((tt, ROUTER_LANES), F32),
                        pltpu.VMEM((tt, d), F32)],
        compiler_params=_cparams("parallel", "arbitrary", "arbitrary"),
        name="moe_ffn",
    )(x, mod, w_router, b_router, w_gu, w_down, ln_g, ln_b)


def _pad_router(w_router, b_router):
    d, n_e = w_router.shape
    w = jnp.zeros((d, ROUTER_LANES), F32).at[:, :n_e].set(w_router)
    b = jnp.full((1, ROUTER_LANES), NEG_INF, F32).at[0, :n_e].set(b_router)
    return w, b


def _qkv_kernel(x_ref, mod_ref, w_ref, q_ref, k_ref, v_ref, *, d):
    sh, sc = mod_ref[0:1, :], mod_ref[1:2, :]
    h = (x_ref[...] * (1.0 + sc) + sh).astype(BF16)
    p = _dot(h, w_ref[...])
    q_ref[...] = (p[:, :d] * (NA_HEAD_DIM ** -0.5)).astype(BF16)
    k_ref[...] = p[:, d:2 * d].astype(BF16)
    v_ref[...] = p[:, 2 * d:].astype(BF16)


def _na_kernel(q_ref, k_ref, v_ref, bias_ref, o_ref, *, rows):
    r = pl.program_id(1)
    r0 = jnp.clip(r - NA_WIN_ROWS // 2, 0, rows - NA_WIN_ROWS)
    start = pl.multiple_of(r0 * GRID_W, GRID_W)
    nk = NA_WIN_ROWS * GRID_W
    pw_ = 2 * NA_HEAD_DIM
    pairs = range(NA_HEADS // 2)
    lanes = [slice(p * pw_, (p + 1) * pw_) for p in pairs]
    head0 = lax.broadcasted_iota(jnp.int32, (GRID_W, pw_), 1) < NA_HEAD_DIM
    q2 = {}
    for p in pairs:
        q = q_ref[:, lanes[p]]
        z = jnp.zeros_like(q)
        q2[p] = jnp.concatenate([jnp.where(head0, q, z), jnp.where(head0, z, q)], axis=0)
    s = {p: _dot_nt(q2[p], k_ref[pl.ds(start, nk), lanes[p]])
         + bias_ref[2 * p:2 * p + 2].reshape(2 * GRID_W, nk) for p in pairs}
    m = {p: jnp.max(s[p], axis=-1, keepdims=True) for p in pairs}
    e = {p: jnp.exp(s[p] - m[p]) for p in pairs}
    l = {p: jnp.sum(e[p], axis=-1, keepdims=True) for p in pairs}
    o = {p: _dot(e[p].astype(BF16), v_ref[pl.ds(start, nk), lanes[p]]) / l[p] for p in pairs}
    for p in pairs:
        o_ref[:, lanes[p]] = jnp.where(head0, o[p][:GRID_W], o[p][GRID_W:]).astype(BF16)


def _na_bias_table(rpb):
    nh, nr, nc = rpb.shape
    w = GRID_W
    qc = jnp.arange(w)[:, None]
    kc = jnp.arange(w)[None, :]
    cs = jnp.clip(qc - NA_WIN_COLS // 2, 0, w - NA_WIN_COLS)
    col_ok = (kc >= cs) & (kc < cs + NA_WIN_COLS)
    lo = w - NA_WIN_COLS
    e = jnp.pad(rpb, ((0, 0), (0, 0), (lo, 2 * w - lo - nc)))
    flat = jnp.tile(e, (1, 1, w))
    toep = flat[:, :, w - 1:w - 1 + w * (2 * w - 1)].reshape(nh, nr, w, 2 * w - 1)[..., :w]
    toep = jnp.where(col_ok[None, None], toep, NEG_INF)
    tab = jnp.stack([toep[:, dl:dl + NA_WIN_ROWS] for dl in range(NA_WIN_ROWS)], axis=1)
    tab = jnp.transpose(tab, (0, 1, 3, 2, 4))
    return tab.reshape(nh, NA_WIN_ROWS, w, NA_WIN_ROWS * w)


def _proj_ln_kernel(*refs, n_in):
    a_refs = refs[:n_in]
    x_ref, mod_ref, w_ref, lng_ref, lnb_ref, o_ref = refs[n_in:]
    a = a_refs[0][...]
    if n_in == 2:
        a = (a.astype(F32) + a_refs[1][...].astype(F32)).astype(BF16)
    mix = _dot(a, w_ref[...])
    gate = mod_ref[2:3, :]
    o_ref[...] = _layer_norm(ALPHA * x_ref[...] + gate * mix, lng_ref[...], lnb_ref[...])


def proj_residual_ln(acts, x, mod, w_o, ln_g, ln_b, tt=512):
    bsz, seq, d = x.shape
    tt = min(tt, seq)
    tile = pl.BlockSpec((None, tt, d), lambda b, t: (b, t, 0))
    return pl.pallas_call(
        functools.partial(_proj_ln_kernel, n_in=len(acts)),
        grid=(bsz, seq // tt),
        in_specs=[tile] * len(acts) + [tile, _mod_spec(d), _const_spec((d, d)),
                                       _const_spec((1, d)), _const_spec((1, d))],
        out_specs=tile,
        out_shape=jax.ShapeDtypeStruct(x.shape, F32),
        compiler_params=_cparams("parallel", "arbitrary"),
        name="proj_residual_ln",
    )(*acts, x, mod, w_o, ln_g, ln_b)


def na_mixer_layer(x, mod, w_qkv, bias_tab, w_o, ln_g, ln_b, tt=512):
    bsz, seq, d = x.shape
    tt = min(tt, seq)
    rows = seq // GRID_W
    assert rows >= NA_WIN_ROWS and seq % GRID_W == 0
    tile = pl.BlockSpec((None, tt, d), lambda b, t: (b, t, 0))
    q, k, v = pl.pallas_call(
        functools.partial(_qkv_kernel, d=d),
        grid=(bsz, seq // tt),
        in_specs=[tile, _mod_spec(d), _const_spec((d, 3 * d))],
        out_specs=[tile] * 3,
        out_shape=[jax.ShapeDtypeStruct(x.shape, BF16)] * 3,
        compiler_params=_cparams("parallel", "arbitrary"),
        name="na_qkv",
    )(x, mod, w_qkv)

    def delta(r):
        return jnp.clip(r - NA_WIN_ROWS // 2, 0, rows - NA_WIN_ROWS) - r + NA_WIN_ROWS - 1

    row_spec = pl.BlockSpec((None, GRID_W, d), lambda b, r: (b, r, 0))
    seq_spec = pl.BlockSpec((None, seq, d), lambda b, r: (b, 0, 0))
    att = pl.pallas_call(
        functools.partial(_na_kernel, rows=rows),
        grid=(bsz, rows),
        in_specs=[row_spec, seq_spec, seq_spec,
                  pl.BlockSpec((NA_HEADS, None, GRID_W, NA_WIN_ROWS * GRID_W),
                               lambda b, r: (0, delta(r), 0, 0))],
        out_specs=row_spec,
        out_shape=jax.ShapeDtypeStruct(x.shape, BF16),
        compiler_params=_cparams("parallel", "arbitrary"),
        name="na_attention",
    )(q, k, v, bias_tab)
    return proj_residual_ln([att], x, mod, w_o, ln_g, ln_b, tt)


def _softplus(y):
    return jnp.maximum(y, 0.0) + jnp.log(1.0 + jnp.exp(-jnp.abs(y)))


def _rk_proj_kernel(x_ref, xp_ref, xn_ref, mod_ref, mu_ref, wrkv_ref, w1_ref, w2_ref, a1_ref, a2_ref,
                    g1_ref, g2_ref, w0_ref, a0_ref, kk_ref, ka_ref, seg_ref,
                    r_o, v_o, kkn_o, lw_o, km_o, b_o, gate_o, *, nt, tt):
    t = pl.program_id(1)
    sh, sc = mod_ref[0:1, :], mod_ref[1:2, :]
    h = x_ref[...] * (1.0 + sc) + sh
    h_prev = jnp.where(t > 0, xp_ref[7:8, :] * (1.0 + sc) + sh, 0.0)
    h_next = jnp.where(t < nt - 1, xn_ref[0:1, :] * (1.0 + sc) + sh, 0.0)
    h_m1, h_p1 = _shifted(h, h_prev, h_next, tt)
    xx = 0.5 * (h_m1 + h_p1) - h

    def mixed(p):
        return (h + xx * mu_ref[p:p + 1, :]).astype(BF16)

    r = _dot(mixed(0), wrkv_ref[0])
    k = _dot(mixed(1), wrkv_ref[1])
    v = _dot(mixed(2), wrkv_ref[2])
    r_o[...] = r.astype(BF16)
    v_o[...] = v.astype(BF16)
    lw_in = jnp.tanh(_dot(mixed(3), w1_ref[...])).astype(BF16)
    a_in = _dot(mixed(4), a1_ref[...]).astype(BF16)
    g_in = _sigmoid(_dot(mixed(5), g1_ref[...])).astype(BF16)

    kk = k * kk_ref[...]
    sq = kk * kk
    sq_hi = sq.astype(BF16)
    sq_lo = (sq - sq_hi.astype(F32)).astype(BF16)
    ss = _dot(sq_hi, seg_ref[...]) + _dot(sq_lo, seg_ref[...])
    kkn = kk / jnp.maximum(jnp.sqrt(ss), 1e-12)
    kkn_o[...] = kkn.astype(BF16)
    rw = w1_ref.shape[1] // 2
    rg = g1_ref.shape[1] // 2
    for z in range(2):
        wl = w0_ref[z:z + 1, :] + _dot(lw_in[:, z * rw:(z + 1) * rw], w2_ref[z])
        w_log = -_softplus(-wl) - 0.5
        lw_o[z] = -jnp.exp(w_log)
        a_lr = _sigmoid(a0_ref[z:z + 1, :] + _dot(a_in[:, z * rw:(z + 1) * rw], a2_ref[z]))
        km_o[z] = (k * (1.0 + (a_lr - 1.0) * ka_ref[...])).astype(BF16)
        b_o[z] = (kkn * a_lr).astype(BF16)
        gate_o[z] = _dot(g_in[:, z * rg:(z + 1) * rg], g2_ref[z]).astype(BF16)


def _rk_scan_kernel(r_ref, v_ref, kkn_ref, lw_ref, km_ref, b_ref, gate_ref, rk_ref, lg_ref, lb_ref,
                    o_ref, s_ref, *, reverse):
    tb, hb = r_ref.shape
    cl = RK_CHUNK
    nc, nh = tb // cl, hb // RK_HEAD_DIM

    @pl.when(pl.program_id(2) == 0)
    def _():
        s_ref[...] = jnp.zeros(s_ref.shape, F32)

    ri = lax.broadcasted_iota(jnp.int32, (tb, tb), 0)
    ci = lax.broadcasted_iota(jnp.int32, (tb, tb), 1)
    ordered = (ci >= ri) if reverse else (ci <= ri)
    tri = jnp.where((ri // cl == ci // cl) & ordered, 1.0, 0.0).astype(BF16)
    lw = lw_ref[...]
    lw_hi = lw.astype(BF16)
    lw_lo = (lw - lw_hi.astype(F32)).astype(BF16)
    cum = _dot(tri, lw_hi) + _dot(tri, lw_lo)
    last = 0 if reverse else cl - 1
    tot_rows = [cum[c * cl + last:c * cl + last + 1, :] for c in range(nc)]
    tot = jnp.concatenate([jnp.broadcast_to(tr, (cl, hb)) for tr in tot_rows], axis=0)

    rr = r_ref[...].astype(F32)
    vv = v_ref[...]
    kkn = kkn_ref[...].astype(F32)
    km = km_ref[...].astype(F32)
    bb = b_ref[...].astype(F32)
    e_neg = jnp.exp(-cum)
    e_rem = jnp.exp(tot - cum)
    a_t = (-kkn * jnp.exp(cum - lw)).astype(BF16)
    r_t = (rr * jnp.exp(cum)).astype(BF16)
    b_t = (bb * e_neg).astype(BF16)
    k_t = (km * e_neg).astype(BF16)
    b_h = (bb * e_rem).astype(BF16)
    k_h = (km * e_rem).astype(BF16)
    bonus_w = rr * km * rk_ref[...]

    pw_ = 2 * RK_HEAD_DIM
    npair = hb // pw_
    t_i = lax.broadcasted_iota(jnp.int32, (cl, pw_), 0)
    s_i = lax.broadcasted_iota(jnp.int32, (cl, pw_), 1) % cl
    strict = (t_i < s_i) if reverse else (t_i > s_i)
    incl = (t_i <= s_i) if reverse else (t_i >= s_i)
    eye = jnp.where(t_i == s_i, 1.0, 0.0)
    head0 = lax.broadcasted_iota(jnp.int32, (cl, pw_), 1) < RK_HEAD_DIM
    q0 = lax.broadcasted_iota(jnp.int32, (pw_, pw_), 0)
    q1 = lax.broadcasted_iota(jnp.int32, (pw_, pw_), 1)
    same_head = (q0 // RK_HEAD_DIM) == (q1 // RK_HEAD_DIM)
    ones_bd = jnp.where(same_head, 1.0, 0.0).astype(BF16)
    ident = jnp.where(q0 == q1, 1.0, 0.0).astype(BF16)

    def bd(x):
        z = jnp.zeros_like(x)
        return jnp.concatenate([jnp.where(head0, x, z), jnp.where(head0, z, x)], axis=0)

    keys = [(c, p) for c in range(nc) for p in range(npair)]
    rsl = {k: slice(k[0] * cl, (k[0] + 1) * cl) for k in keys}
    lsl = {k: slice(k[1] * pw_, (k[1] + 1) * pw_) for k in keys}
    blk = lambda arr, k: arr[rsl[k], lsl[k]]

    gram = {k: _dot_nt(jnp.concatenate([blk(a_t, k), blk(r_t, k)], axis=0),
                       jnp.concatenate([bd(blk(b_t, k)), bd(blk(k_t, k))], axis=0)) for k in keys}
    m_ab = {k: jnp.where(strict, gram[k][:cl, :pw_], 0.0) for k in keys}
    m_ak = {k: jnp.where(strict, gram[k][:cl, pw_:], 0.0).astype(BF16) for k in keys}
    m_rb = {k: jnp.where(incl, gram[k][cl:, :pw_], 0.0).astype(BF16) for k in keys}
    m_rk = {k: jnp.where(incl, gram[k][cl:, pw_:], 0.0).astype(BF16) for k in keys}
    vbd = {k: bd(blk(vv, k)) for k in keys}
    my = {k: _dot(jnp.concatenate([m_ak[k], m_rk[k]], axis=0), vbd[k]) for k in keys}
    v_t = {k: _dot_nt(ident, blk(vv, k)).astype(BF16) for k in keys}
    vk = {k: jnp.where(same_head, _dot(v_t[k], blk(k_h, k)), 0.0) for k in keys}
    inv = {k: eye + m_ab[k] for k in keys}
    pwb = {k: m_ab[k].astype(BF16) for k in keys}
    pw = {k: _dot(pwb[k], bd(pwb[k])) for k in keys}
    n = 4
    while n < cl:
        pwb = {k: pw[k].astype(BF16) for k in keys}
        res = {k: _dot(pwb[k], jnp.concatenate([bd(pwb[k]), bd(inv[k].astype(BF16))], axis=1)) for k in keys}
        pw = {k: res[k][:, :pw_] for k in keys}
        inv = {k: inv[k] + res[k][:, pw_:] for k in keys}
        n *= 2
    inv = {k: (inv[k] + _dot(pw[k].astype(BF16), bd(inv[k].astype(BF16)))).astype(BF16) for k in keys}
    au = {k: _dot(inv[k], jnp.concatenate([bd(blk(a_t, k)), bd(my[k][:cl].astype(BF16))], axis=1))
          for k in keys}
    a_hat = {k: au[k][:, :pw_].astype(BF16) for k in keys}
    u0_t = {k: _dot_nt(ident, au[k][:, pw_:].astype(BF16)) for k in keys}
    ar = {k: jnp.concatenate([a_hat[k], blk(r_t, k)], axis=0) for k in keys}

    pairs = range(npair)
    for c in (range(nc - 1, -1, -1) if reverse else range(nc)):
        st = {p: s_ref[p] for p in pairs}
        stb = {p: st[p].astype(BF16) for p in pairs}
        res = {p: _dot_nt(ar[c, p], stb[p]) for p in pairs}
        res_t = {p: _dot_nt(stb[p], a_hat[c, p]) for p in pairs}
        ub = {p: (au[c, p][:, pw_:] + res[p][:cl]).astype(BF16) for p in pairs}
        ub_t = {p: (u0_t[c, p] + res_t[p]).astype(BF16) for p in pairs}
        y = {p: my[c, p][cl:] + res[p][cl:] + _dot(m_rb[c, p], bd(ub[p])) for p in pairs}
        for p in pairs:
            k = (c, p)
            decay = jnp.exp(tot_rows[c][:, lsl[k]])
            s_ref[p] = jnp.where(same_head, st[p] * decay + _dot(ub_t[p], blk(b_h, k)), 0.0) + vk[k]
        sums = {p: _dot(jnp.concatenate([y[p].astype(BF16), blk(bonus_w, (c, p)).astype(BF16)], axis=0), ones_bd)
                for p in pairs}
        dy = {p: y[p] - sums[p][:cl] * (1.0 / RK_HEAD_DIM) for p in pairs}
        var = {p: _dot((dy[p] * dy[p]).astype(BF16), ones_bd) * (1.0 / RK_HEAD_DIM) for p in pairs}
        for p in pairs:
            k = (c, p)
            yn = dy[p] * lax.rsqrt(var[p] + RK_GN_EPS) * lg_ref[:, lsl[k]] + lb_ref[:, lsl[k]]
            bonus = sums[p][cl:] * blk(vv, k).astype(F32)
            o_ref[rsl[k], lsl[k]] = (blk(gate_ref, k).astype(F32) * (yn + bonus)).astype(BF16)


def rwkv_mixer_layer(x, mod, p, ln_g, ln_b, tt=256):
    bsz, seq, d = x.shape
    tt = min(tt, seq)
    nt = seq // tt
    x_spec, prev_spec, next_spec = _tile_specs(seq, tt, d)
    tile = pl.BlockSpec((None, tt, d), lambda b, t: (b, t, 0))
    tile2 = pl.BlockSpec((2, None, tt, d), lambda b, t: (0, b, t, 0))
    rw, rg = p['w1'].shape[1] // 2, p['g1'].shape[1] // 2
    sh_bf = jax.ShapeDtypeStruct(x.shape, BF16)
    sh2_bf = jax.ShapeDtypeStruct((2,) + x.shape, BF16)
    sh2_f32 = jax.ShapeDtypeStruct((2,) + x.shape, F32)
    r, v, kkn, lw, km, bz, gate = pl.pallas_call(
        functools.partial(_rk_proj_kernel, nt=nt, tt=tt),
        grid=(bsz, nt),
        in_specs=[x_spec, prev_spec, next_spec, _mod_spec(d), _const_spec((6, d)),
                  _const_spec((3, d, d)),
                  _const_spec((d, 2 * rw)), _const_spec((2, rw, d)),
                  _const_spec((d, 2 * rw)), _const_spec((2, rw, d)),
                  _const_spec((d, 2 * rg)), _const_spec((2, rg, d)),
                  _const_spec((2, d)), _const_spec((2, d)), _const_spec((1, d)), _const_spec((1, d)),
                  _const_spec((d, d))],
        out_specs=[tile, tile, tile, tile2, tile2, tile2, tile2],
        out_shape=[sh_bf, sh_bf, sh_bf, sh2_f32, sh2_bf, sh2_bf, sh2_bf],
        compiler_params=_cparams("parallel", "arbitrary"),
        name="rwkv_proj",
    )(x, x, x, mod, p['mu'], p['w_rkv'], p['w1'], p['w2'], p['a1'], p['a2'], p['g1'], p['g2'],
      p['w0'], p['a0'], p['k_k'], p['k_a'], p['seg'])

    tb = min(RK_TOK_BLOCK, seq)
    hb = RK_LANE_BLOCK
    nb = seq // tb
    outs = []
    for z in range(2):
        rev = z == 1

        def tok(i, rev=rev):
            return nb - 1 - i if rev else i

        blk = pl.BlockSpec((None, tb, hb), lambda b, g, i: (b, tok(i), g))
        blk2 = pl.BlockSpec((None, None, tb, hb), lambda b, g, i, z=z: (z, b, tok(i), g))
        vec = pl.BlockSpec((1, hb), lambda b, g, i: (0, g))
        vec2 = pl.BlockSpec((None, 1, hb), lambda b, g, i, z=z: (z, 0, g))
        outs.append(pl.pallas_call(
            functools.partial(_rk_scan_kernel, reverse=rev),
            grid=(bsz, d // hb, nb),
            in_specs=[blk, blk, blk, blk2, blk2, blk2, blk2, vec, vec2, vec2],
            out_specs=blk,
            out_shape=sh_bf,
            scratch_shapes=[pltpu.VMEM((hb // (2 * RK_HEAD_DIM), 2 * RK_HEAD_DIM, 2 * RK_HEAD_DIM), F32)],
            compiler_params=_cparams("parallel", "parallel", "arbitrary"),
            name="rwkv_scan_rev" if rev else "rwkv_scan_fwd",
        )(r, v, kkn, lw, km, bz, gate, p['r_k'], p['lnx_g'], p['lnx_b']))
    return proj_residual_ln(outs, x, mod, p['w_o'], ln_g, ln_b)


def _prepare_params(w):
    d = D_MODEL
    bf = lambda a: a.astype(BF16)
    row = lambda a: a.reshape(1, d)
    lane = jnp.arange(d) // RK_HEAD_DIM
    p = {
        'w_ada': w['w_ada'], 'b_ada': w['b_ada'],
        'ln_g': w['ln_g'], 'ln_b': w['ln_b'],
        'conv_w_in': bf(w['conv_w_in']), 'conv_w': w['conv_w'], 'conv_w_out': bf(w['conv_w_out']),
        'na_w_qkv': bf(w['na_w_qkv']), 'na_w_o': bf(w['na_w_o']),
        'na_bias': [_na_bias_table(w['na_rpb'][j]) for j in range(w['na_rpb'].shape[0])],
        'ffn_w_gu': bf(w['ffn_w_gu']), 'ffn_w_down': bf(w['ffn_w_down']),
        'moe_router': [_pad_router(w['moe_w_router'][j], w['moe_b_router'][j])
                       for j in range(w['moe_w_router'].shape[0])],
        'moe_w_gu': bf(w['moe_w_gu']), 'moe_w_down': bf(w['moe_w_down']),
        'rk': [],
    }
    cat = lambda a: jnp.concatenate([a[0], a[1]], axis=1)
    for j in range(w['rk_mu'].shape[0]):
        p['rk'].append({
            'mu': w['rk_mu'][j], 'w_rkv': bf(w['rk_w_rkv'][j]),
            'w1': bf(cat(w['rk_w1'][j])), 'w2': bf(w['rk_w2'][j]),
            'a1': bf(cat(w['rk_a1'][j])), 'a2': bf(w['rk_a2'][j]),
            'g1': bf(cat(w['rk_g1'][j])), 'g2': bf(w['rk_g2'][j]),
            'w0': w['rk_w0'][j], 'a0': w['rk_a0'][j],
            'k_k': row(w['rk_k_k'][j]), 'k_a': row(w['rk_k_a'][j]),
            'r_k': w['rk_r_k'][j].reshape(1, d),
            'lnx_g': w['rk_lnx_g'][j].reshape(2, 1, d), 'lnx_b': w['rk_lnx_b'][j].reshape(2, 1, d),
            'w_o': bf(w['rk_w_o'][j]),
            'seg': (lane[:, None] == lane[None, :]).astype(BF16),
        })
    return p


def _trunk(x, c, p):
    d = D_MODEL
    mod_all = ada_modulation(c, p['w_ada'], p['b_ada'])
    for i in range(DEPTH):
        mod = mod_all[i]
        lng = lambda s: p['ln_g'][i, s].reshape(1, d)
        lnb = lambda s: p['ln_b'][i, s].reshape(1, d)
        kind, j = i % 3, i // 3
        if kind == 0:
            x = conv_mixer_layer(x, mod, p['conv_w_in'][j], p['conv_w'][j], p['conv_w_out'][j],
                                 lng(0), lnb(0))
        elif kind == 1:
            x = na_mixer_layer(x, mod, p['na_w_qkv'][j], p['na_bias'][j], p['na_w_o'][j], lng(0), lnb(0))
        else:
            x = rwkv_mixer_layer(x, mod, p['rk'][j], lng(0), lnb(0))
        if i % 2 == 0:
            x = dense_ffn_layer(x, mod, p['ffn_w_gu'][i // 2], p['ffn_w_down'][i // 2], lng(1), lnb(1))
        else:
            wr, br = p['moe_router'][i // 2]
            x = moe_ffn_layer(x, mod, wr, br, p['moe_w_gu'][i // 2], p['moe_w_down'][i // 2],
                              lng(1), lnb(1))
    return x


def kernel(x_prompt, x_sample, c_prompt, c_sample, w_ada, b_ada, ln_g, ln_b, conv_w_in, conv_w, conv_w_out, na_w_qkv, na_rpb, na_w_o, rk_mu, rk_w_rkv, rk_w0, rk_w1, rk_w2, rk_a0, rk_a1, rk_a2, rk_g1, rk_g2, rk_k_k, rk_k_a, rk_r_k, rk_lnx_g, rk_lnx_b, rk_w_o, ffn_w_gu, ffn_w_down, moe_w_router, moe_b_router, moe_w_gu, moe_w_down):
    p = _prepare_params(dict(
        w_ada=w_ada, b_ada=b_ada, ln_g=ln_g, ln_b=ln_b,
        conv_w_in=conv_w_in, conv_w=conv_w, conv_w_out=conv_w_out,
        na_w_qkv=na_w_qkv, na_rpb=na_rpb, na_w_o=na_w_o,
        rk_mu=rk_mu, rk_w_rkv=rk_w_rkv, rk_w0=rk_w0, rk_w1=rk_w1, rk_w2=rk_w2,
        rk_a0=rk_a0, rk_a1=rk_a1, rk_a2=rk_a2, rk_g1=rk_g1, rk_g2=rk_g2,
        rk_k_k=rk_k_k, rk_k_a=rk_k_a, rk_r_k=rk_r_k, rk_lnx_g=rk_lnx_g, rk_lnx_b=rk_lnx_b,
        rk_w_o=rk_w_o, ffn_w_gu=ffn_w_gu, ffn_w_down=ffn_w_down,
        moe_w_router=moe_w_router, moe_b_router=moe_b_router,
        moe_w_gu=moe_w_gu, moe_w_down=moe_w_down))
    return (_trunk(x_prompt, c_prompt, p), _trunk(x_sample, c_sample, p))
```

```python
import functools

import jax
import jax.numpy as jnp
from jax import lax
from jax.experimental import pallas as pl
from jax.experimental.pallas import tpu as pltpu

F32 = jnp.float32
BF16 = jnp.bfloat16

D_MODEL = 1024
DEPTH = 4
ALPHA = (2 * DEPTH) ** 0.25
LN_EPS = 1e-5

GRID_W = 64
NA_HEADS = 16
NA_HEAD_DIM = D_MODEL // NA_HEADS
NA_WIN_ROWS = 8
NA_WIN_COLS = 16
NEG_INF = -1e30

RK_HEAD_DIM = 64
RK_GN_EPS = 64e-5
RK_CHUNK = 64
RK_TOK_BLOCK = 128
RK_LANE_BLOCK = 1024

N_EXPERTS = 8
ROUTER_LANES = 128

VMEM_LIMIT = 52 * 1024 * 1024
MOE_VMEM_LIMIT = 58 * 1024 * 1024


def _cparams(*sem):
    return pltpu.CompilerParams(dimension_semantics=sem, vmem_limit_bytes=VMEM_LIMIT)


def _const_spec(shape):
    nd = len(shape)
    return pl.BlockSpec(shape, lambda *_: (0,) * nd, pipeline_mode=pl.Buffered(1))


def _dot(a, b):
    return jnp.dot(a, b, preferred_element_type=F32)


def _dot_nt(a, b):
    return lax.dot_general(a, b, (((1,), (1,)), ((), ())), preferred_element_type=F32)


def _dot_tn(a, b):
    return lax.dot_general(a, b, (((0,), (0,)), ((), ())), preferred_element_type=F32)


def _sigmoid(x):
    return 1.0 / (1.0 + jnp.exp(-x))


def _layer_norm(y, g, b):
    mu = jnp.mean(y, axis=-1, keepdims=True)
    d = y - mu
    var = jnp.mean(d * d, axis=-1, keepdims=True)
    return d * lax.rsqrt(var + LN_EPS) * g + b


def _ada_kernel(c_ref, w_ref, b_ref, o_ref):
    c = c_ref[...]
    s = (c * _sigmoid(c)).astype(BF16)
    o_ref[...] = _dot(s, w_ref[...].astype(BF16)) + b_ref[...]


def ada_modulation(c, w_ada, b_ada):
    bsz, d = c.shape
    depth = w_ada.shape[0]
    out = pl.pallas_call(
        _ada_kernel,
        grid=(depth, 6),
        in_specs=[
            pl.BlockSpec((bsz, d), lambda i, j: (0, 0)),
            pl.BlockSpec((None, d, d), lambda i, j: (i, 0, j)),
            pl.BlockSpec((None, None, 1, d), lambda i, j: (i, j, 0, 0)),
        ],
        out_specs=pl.BlockSpec((None, None, bsz, d), lambda i, j: (i, j, 0, 0)),
        out_shape=jax.ShapeDtypeStruct((depth, 6, bsz, d), F32),
        compiler_params=_cparams("arbitrary", "arbitrary"),
        name="ada_modulation",
    )(c, w_ada, b_ada.reshape(depth, 6, 1, d))
    return jnp.transpose(out, (0, 2, 1, 3))


def _tile_specs(seq, tt, d):
    nb8 = seq // 8
    per = tt // 8
    x_spec = pl.BlockSpec((None, tt, d), lambda b, t: (b, t, 0))
    prev_spec = pl.BlockSpec((None, 8, d), lambda b, t: (b, jnp.maximum(t * per - 1, 0), 0))
    next_spec = pl.BlockSpec((None, 8, d), lambda b, t: (b, jnp.minimum((t + 1) * per, nb8 - 1), 0))
    return x_spec, prev_spec, next_spec


def _mod_spec(d):
    return pl.BlockSpec((None, 6, d), lambda b, t: (b, 0, 0))


def _shifted(cur, prev_row, next_row, tt):
    row = lax.broadcasted_iota(jnp.int32, (tt, 1), 0)
    m1 = jnp.where(row == 0, prev_row, pltpu.roll(cur, 1, 0))
    p1 = jnp.where(row == tt - 1, next_row, pltpu.roll(cur, tt - 1, 0))
    return m1, p1


def _conv_kernel(x_ref, xp_ref, xn_ref, mod_ref, win_ref, cw_ref, wout_ref, lng_ref, lnb_ref,
                 o_ref, *, nt, tt, d):
    t = pl.program_id(1)
    sh, sc, gate = mod_ref[0:1, :], mod_ref[1:2, :], mod_ref[2:3, :]
    x = x_ref[...]
    h = (x * (1.0 + sc) + sh).astype(BF16)
    p = _dot(h, win_ref[...])
    z = p[:, d:2 * d] * p[:, 2 * d:]
    halo = jnp.concatenate([xp_ref[...], xn_ref[...]], axis=0)
    hh = (halo * (1.0 + sc) + sh).astype(BF16)
    ph = _dot(hh, win_ref[:, d:])
    zh = ph[:, :d] * ph[:, d:]
    z_prev = jnp.where(t > 0, zh[7:8, :], 0.0)
    z_next = jnp.where(t < nt - 1, zh[8:9, :], 0.0)
    z_m1, z_p1 = _shifted(z, z_prev, z_next, tt)
    conv = z_m1 * cw_ref[0:1, :] + z * cw_ref[1:2, :] + z_p1 * cw_ref[2:3, :]
    y = (p[:, :d] * conv).astype(BF16)
    mix = _dot(y, wout_ref[...])
    o_ref[...] = _layer_norm(ALPHA * x + gate * mix, lng_ref[...], lnb_ref[...])


def conv_mixer_layer(x, mod, w_in, conv_w, w_out, ln_g, ln_b, tt=512):
    bsz, seq, d = x.shape
    tt = min(tt, seq)
    nt = seq // tt
    x_spec, prev_spec, next_spec = _tile_specs(seq, tt, d)
    return pl.pallas_call(
        functools.partial(_conv_kernel, nt=nt, tt=tt, d=d),
        grid=(bsz, nt),
        in_specs=[x_spec, prev_spec, next_spec, _mod_spec(d),
                  _const_spec((d, 3 * d)), _const_spec((3, d)), _const_spec((d, d)),
                  _const_spec((1, d)), _const_spec((1, d))],
        out_specs=pl.BlockSpec((None, tt, d), lambda b, t: (b, t, 0)),
        out_shape=jax.ShapeDtypeStruct(x.shape, F32),
        compiler_params=_cparams("parallel", "arbitrary"),
        name="conv_mixer",
    )(x, x, x, mod, w_in, conv_w, w_out, ln_g, ln_b)


def _ffn_kernel(x_ref, mod_ref, wgu_ref, wd_ref, lng_ref, lnb_ref, o_ref, *, ff, fc):
    sh, sc, gate = mod_ref[3:4, :], mod_ref[4:5, :], mod_ref[5:6, :]
    x = x_ref[...]
    h = (x * (1.0 + sc) + sh).astype(BF16)
    acc = jnp.zeros(x.shape, F32)
    for c in range(ff // fc):
        g = _dot(h, wgu_ref[:, c * fc:(c + 1) * fc])
        u = _dot(h, wgu_ref[:, ff + c * fc:ff + (c + 1) * fc])
        act = (g * _sigmoid(g) * u).astype(BF16)
        acc = acc + _dot(act, wd_ref[c * fc:(c + 1) * fc, :])
    o_ref[...] = _layer_norm(ALPHA * x + gate * acc, lng_ref[...], lnb_ref[...])


def dense_ffn_layer(x, mod, w_gu, w_down, ln_g, ln_b, tt=512):
    bsz, seq, d = x.shape
    tt = min(tt, seq)
    ff = w_down.shape[0]
    fc = ff // 2
    return pl.pallas_call(
        functools.partial(_ffn_kernel, ff=ff, fc=fc),
        grid=(bsz, seq // tt),
        in_specs=[pl.BlockSpec((None, tt, d), lambda b, t: (b, t, 0)), _mod_spec(d),
                  _const_spec((d, 2 * ff)), _const_spec((ff, d)),
                  _const_spec((1, d)), _const_spec((1, d))],
        out_specs=pl.BlockSpec((None, tt, d), lambda b, t: (b, t, 0)),
        out_shape=jax.ShapeDtypeStruct(x.shape, F32),
        compiler_params=_cparams("parallel", "arbitrary"),
        name="dense_ffn",
    )(x, mod, w_gu, w_down, ln_g, ln_b)


MOE_ROW_CHUNK = 256
MOE_SPARSE_ROWS = (256, 384, 512)


def _moe_kernel(x_ref, mod_ref, wr_ref, br_ref, wgu_ref, wd_ref, lng_ref, lnb_ref, o_ref,
                h_sc, comb_sc, acc_sc, rankc_sc, rankr_sc, maskr_sc, *, fe, caps):
    e = pl.program_id(2)
    tt = comb_sc.shape[0]
    lane = lax.broadcasted_iota(jnp.int32, comb_sc.shape, 1)

    @pl.when(e == 0)
    def _():
        sh, sc = mod_ref[3:4, :], mod_ref[4:5, :]
        h = x_ref[...] * (1.0 + sc) + sh
        h_sc[...] = h.astype(BF16)
        logits = jnp.dot(h, wr_ref[...], preferred_element_type=F32,
                         precision=lax.Precision.HIGHEST) + br_ref[...]
        m = jnp.max(logits, axis=-1, keepdims=True)
        ex = jnp.exp(logits - m)
        probs = ex / jnp.sum(ex, axis=-1, keepdims=True)
        valid = lane < N_EXPERTS
        p = jnp.where(valid, probs, -1.0)
        p1 = jnp.max(p, axis=-1, keepdims=True)
        i1 = jnp.min(jnp.where(p == p1, lane, ROUTER_LANES), axis=-1, keepdims=True)
        pr = jnp.where(lane == i1, -1.0, p)
        p2 = jnp.max(pr, axis=-1, keepdims=True)
        i2 = jnp.min(jnp.where(pr == p2, lane, ROUTER_LANES), axis=-1, keepdims=True)
        tot = p1 + p2
        comb = jnp.where(lane == i1, p1 / tot, 0.0) + jnp.where(lane == i2, p2 / tot, 0.0)
        comb_sc[...] = comb
        acc_sc[...] = jnp.zeros(acc_sc.shape, F32)
        routed = jnp.where(comb > 0.0, 1.0, 0.0).astype(BF16)
        q0 = lax.broadcasted_iota(jnp.int32, (ROUTER_LANES, ROUTER_LANES), 0)
        q1 = lax.broadcasted_iota(jnp.int32, (ROUTER_LANES, ROUTER_LANES), 1)
        routed_t = _dot_nt(jnp.where(q0 == q1, 1.0, 0.0).astype(BF16), routed)
        maskr_sc[...] = routed_t
        routed_tb = routed_t.astype(BF16)
        rc = min(MOE_ROW_CHUNK, tt)
        for c in range(tt // rc):
            r_i = lax.broadcasted_iota(jnp.int32, (rc, tt), 0) + c * rc
            c_i = lax.broadcasted_iota(jnp.int32, (rc, tt), 1)
            rankc_sc[c * rc:(c + 1) * rc, :] = _dot(jnp.where(c_i < r_i, 1.0, 0.0).astype(BF16), routed)
            r_j = lax.broadcasted_iota(jnp.int32, (tt, rc), 0)
            c_j = lax.broadcasted_iota(jnp.int32, (tt, rc), 1) + c * rc
            rankr_sc[:, c * rc:(c + 1) * rc] = _dot(routed_tb, jnp.where(r_j < c_j, 1.0, 0.0).astype(BF16))

    def expert(hb):
        g = _dot(hb, wgu_ref[:, :fe])
        u = _dot(hb, wgu_ref[:, fe:])
        act = (g * _sigmoid(g) * u).astype(BF16)
        return _dot(act, wd_ref[...])

    w_col = jnp.sum(jnp.where(lane == e, comb_sc[...], 0.0), axis=-1, keepdims=True)
    rk_col = jnp.sum(jnp.where(lane == e, rankc_sc[...], 0.0), axis=-1, keepdims=True).astype(jnp.int32)
    rk_row = rankr_sc[pl.ds(e, 1), :].astype(jnp.int32)
    mk_row = maskr_sc[pl.ds(e, 1), :]
    n_routed = jnp.sum(mk_row).astype(jnp.int32)
    half = tt // 2

    def sparse(cap):
        sub = lax.broadcasted_iota(jnp.int32, (cap, tt), 0)
        pick = jnp.where((rk_row == sub) & (mk_row > 0.0), 1.0, 0.0).astype(BF16)
        y = expert(_dot(pick, h_sc[...]).astype(BF16)).astype(BF16)
        for r0 in (0, half):
            rows = slice(r0, r0 + half)
            ln_i = lax.broadcasted_iota(jnp.int32, (half, cap), 1)
            put = jnp.where((rk_col[rows] == ln_i) & (w_col[rows] > 0.0), 1.0, 0.0).astype(BF16)
            acc_sc[rows, :] += w_col[rows] * _dot(put, y)

    def dense():
        for r0 in (0, half):
            rows = slice(r0, r0 + half)
            acc_sc[rows, :] += w_col[rows] * expert(h_sc[rows, :])

    lo = 0
    for cap in caps:
        pl.when((n_routed > lo) & (n_routed <= cap))(functools.partial(sparse, cap))
        lo = cap
    pl.when(n_routed > lo)(dense)

    @pl.when(e == N_EXPERTS - 1)
    def _():
        gate = mod_ref[5:6, :]
        o_ref[...] = _layer_norm(ALPHA * x_ref[...] + gate * acc_sc[...], lng_ref[...], lnb_ref[...])


def moe_ffn_layer(x, mod, w_router, b_router, w_gu, w_down, ln_g, ln_b, tt=1024, caps=MOE_SPARSE_ROWS):
    bsz, seq, d = x.shape
    tt = min(tt, seq)
    n_e, fe = w_down.shape[0], w_down.shape[1]
    return pl.pallas_call(
        functools.partial(_moe_kernel, fe=fe, caps=caps),
        grid=(bsz, seq // tt, n_e),
        in_specs=[pl.BlockSpec((None, tt, d), lambda b, t, e: (b, t, 0), pipeline_mode=pl.Buffered(1)),
                  pl.BlockSpec((None, 6, d), lambda b, t, e: (b, 0, 0)),
                  _const_spec((d, ROUTER_LANES)), _const_spec((1, ROUTER_LANES)),
                  pl.BlockSpec((None, d, 2 * fe), lambda b, t, e: (e, 0, 0)),
                  pl.BlockSpec((None, fe, d), lambda b, t, e: (e, 0, 0)),
                  _const_spec((1, d)), _const_spec((1, d))],
        out_specs=pl.BlockSpec((None, tt, d), lambda b, t, e: (b, t, 0)),
        out_shape=jax.ShapeDtypeStruct(x.shape, F32),
        scratch_shapes=[pltpu.VMEM((tt, d), BF16), pltpu.VMEM((tt, ROUTER_LANES), F32),
                        pltpu.VMEM((tt, d), F32), pltpu.VMEM((tt, ROUTER_LANES), F32),
                        pltpu.VMEM((ROUTER_LANES, tt), F32), pltpu.VMEM((ROUTER_LANES, tt), F32)],
        compiler_params=pltpu.CompilerParams(
            dimension_semantics=("parallel", "arbitrary", "arbitrary"), vmem_limit_bytes=MOE_VMEM_LIMIT),
        name="moe_ffn",
    )(x, mod, w_router, b_router, w_gu, w_down, ln_g, ln_b)


def _pad_router(w_router, b_router):
    d, n_e = w_router.shape
    w = jnp.zeros((d, ROUTER_LANES), F32).at[:, :n_e].set(w_router)
    b = jnp.full((1, ROUTER_LANES), NEG_INF, F32).at[0, :n_e].set(b_router)
    return w, b


def _qkv_kernel(x_ref, mod_ref, w_ref, q_ref, k_ref, v_ref, *, d):
    sh, sc = mod_ref[0:1, :], mod_ref[1:2, :]
    h = (x_ref[...] * (1.0 + sc) + sh).astype(BF16)
    p = _dot(h, w_ref[...])
    q_ref[...] = (p[:, :d] * (NA_HEAD_DIM ** -0.5)).astype(BF16)
    k_ref[...] = p[:, d:2 * d].astype(BF16)
    v_ref[...] = p[:, 2 * d:].astype(BF16)


def _na_kernel(q_ref, k_ref, v_ref, bias_ref, o_ref, *, rows):
    r = pl.program_id(1)
    r0 = jnp.clip(r - NA_WIN_ROWS // 2, 0, rows - NA_WIN_ROWS)
    start = pl.multiple_of(r0 * GRID_W, GRID_W)
    nk = NA_WIN_ROWS * GRID_W
    pw_ = 2 * NA_HEAD_DIM
    pairs = range(NA_HEADS // 2)
    lanes = [slice(p * pw_, (p + 1) * pw_) for p in pairs]
    head0 = lax.broadcasted_iota(jnp.int32, (GRID_W, pw_), 1) < NA_HEAD_DIM
    q2 = {}
    for p in pairs:
        q = q_ref[:, lanes[p]]
        z = jnp.zeros_like(q)
        q2[p] = jnp.concatenate([jnp.where(head0, q, z), jnp.where(head0, z, q)], axis=0)
    s = {p: _dot_nt(q2[p], k_ref[pl.ds(start, nk), lanes[p]])
         + bias_ref[2 * p:2 * p + 2].reshape(2 * GRID_W, nk) for p in pairs}
    m = {p: jnp.max(s[p], axis=-1, keepdims=True) for p in pairs}
    e = {p: jnp.exp(s[p] - m[p]) for p in pairs}
    l = {p: jnp.sum(e[p], axis=-1, keepdims=True) for p in pairs}
    o = {p: _dot(e[p].astype(BF16), v_ref[pl.ds(start, nk), lanes[p]]) / l[p] for p in pairs}
    for p in pairs:
        o_ref[:, lanes[p]] = jnp.where(head0, o[p][:GRID_W], o[p][GRID_W:]).astype(BF16)


def _na_bias_table(rpb):
    nh, nr, nc = rpb.shape
    w = GRID_W
    qc = jnp.arange(w)[:, None]
    kc = jnp.arange(w)[None, :]
    cs = jnp.clip(qc - NA_WIN_COLS // 2, 0, w - NA_WIN_COLS)
    col_ok = (kc >= cs) & (kc < cs + NA_WIN_COLS)
    lo = w - NA_WIN_COLS
    e = jnp.pad(rpb, ((0, 0), (0, 0), (lo, 2 * w - lo - nc)))
    flat = jnp.tile(e, (1, 1, w))
    toep = flat[:, :, w - 1:w - 1 + w * (2 * w - 1)].reshape(nh, nr, w, 2 * w - 1)[..., :w]
    toep = jnp.where(col_ok[None, None], toep, NEG_INF)
    tab = jnp.stack([toep[:, dl:dl + NA_WIN_ROWS] for dl in range(NA_WIN_ROWS)], axis=1)
    tab = jnp.transpose(tab, (0, 1, 3, 2, 4))
    return tab.reshape(nh, NA_WIN_ROWS, w, NA_WIN_ROWS * w)


def _proj_ln_kernel(*refs, n_in):
    a_refs = refs[:n_in]
    x_ref, mod_ref, w_ref, lng_ref, lnb_ref, o_ref = refs[n_in:]
    a = a_refs[0][...]
    if n_in == 2:
        a = (a.astype(F32) + a_refs[1][...].astype(F32)).astype(BF16)
    mix = _dot(a, w_ref[...])
    gate = mod_ref[2:3, :]
    o_ref[...] = _layer_norm(ALPHA * x_ref[...] + gate * mix, lng_ref[...], lnb_ref[...])


def proj_residual_ln(acts, x, mod, w_o, ln_g, ln_b, tt=512):
    bsz, seq, d = x.shape
    tt = min(tt, seq)
    tile = pl.BlockSpec((None, tt, d), lambda b, t: (b, t, 0))
    return pl.pallas_call(
        functools.partial(_proj_ln_kernel, n_in=len(acts)),
        grid=(bsz, seq // tt),
        in_specs=[tile] * len(acts) + [tile, _mod_spec(d), _const_spec((d, d)),
                                       _const_spec((1, d)), _const_spec((1, d))],
        out_specs=tile,
        out_shape=jax.ShapeDtypeStruct(x.shape, F32),
        compiler_params=_cparams("parallel", "arbitrary"),
        name="proj_residual_ln",
    )(*acts, x, mod, w_o, ln_g, ln_b)


def na_mixer_layer(x, mod, w_qkv, bias_tab, w_o, ln_g, ln_b, tt=512):
    bsz, seq, d = x.shape
    tt = min(tt, seq)
    rows = seq // GRID_W
    assert rows >= NA_WIN_ROWS and seq % GRID_W == 0
    tile = pl.BlockSpec((None, tt, d), lambda b, t: (b, t, 0))
    q, k, v = pl.pallas_call(
        functools.partial(_qkv_kernel, d=d),
        grid=(bsz, seq // tt),
        in_specs=[tile, _mod_spec(d), _const_spec((d, 3 * d))],
        out_specs=[tile] * 3,
        out_shape=[jax.ShapeDtypeStruct(x.shape, BF16)] * 3,
        compiler_params=_cparams("parallel", "arbitrary"),
        name="na_qkv",
    )(x, mod, w_qkv)

    def delta(r):
        return jnp.clip(r - NA_WIN_ROWS // 2, 0, rows - NA_WIN_ROWS) - r + NA_WIN_ROWS - 1

    row_spec = pl.BlockSpec((None, GRID_W, d), lambda b, r: (b, r, 0))
    seq_spec = pl.BlockSpec((None, seq, d), lambda b, r: (b, 0, 0))
    att = pl.pallas_call(
        functools.partial(_na_kernel, rows=rows),
        grid=(bsz, rows),
        in_specs=[row_spec, seq_spec, seq_spec,
                  pl.BlockSpec((NA_HEADS, None, GRID_W, NA_WIN_ROWS * GRID_W),
                               lambda b, r: (0, delta(r), 0, 0))],
        out_specs=row_spec,
        out_shape=jax.ShapeDtypeStruct(x.shape, BF16),
        compiler_params=_cparams("parallel", "arbitrary"),
        name="na_attention",
    )(q, k, v, bias_tab)
    return proj_residual_ln([att], x, mod, w_o, ln_g, ln_b, tt)


def _softplus(y):
    return jnp.maximum(y, 0.0) + jnp.log(1.0 + jnp.exp(-jnp.abs(y)))


def _rk_proj_kernel(x_ref, xp_ref, xn_ref, mod_ref, mu_ref, wrkv_ref, w1_ref, w2_ref, a1_ref, a2_ref,
                    g1_ref, g2_ref, w0_ref, a0_ref, kk_ref, ka_ref, seg_ref,
                    r_o, v_o, kkn_o, lw_o, km_o, b_o, gate_o, *, nt, tt):
    t = pl.program_id(1)
    sh, sc = mod_ref[0:1, :], mod_ref[1:2, :]
    h = x_ref[...] * (1.0 + sc) + sh
    h_prev = jnp.where(t > 0, xp_ref[7:8, :] * (1.0 + sc) + sh, 0.0)
    h_next = jnp.where(t < nt - 1, xn_ref[0:1, :] * (1.0 + sc) + sh, 0.0)
    h_m1, h_p1 = _shifted(h, h_prev, h_next, tt)
    xx = 0.5 * (h_m1 + h_p1) - h

    def mixed(p):
        return (h + xx * mu_ref[p:p + 1, :]).astype(BF16)

    r = _dot(mixed(0), wrkv_ref[0])
    k = _dot(mixed(1), wrkv_ref[1])
    v = _dot(mixed(2), wrkv_ref[2])
    r_o[...] = r.astype(BF16)
    v_o[...] = v.astype(BF16)
    lw_in = jnp.tanh(_dot(mixed(3), w1_ref[...])).astype(BF16)
    a_in = _dot(mixed(4), a1_ref[...]).astype(BF16)
    g_in = _sigmoid(_dot(mixed(5), g1_ref[...])).astype(BF16)

    kk = k * kk_ref[...]
    sq = kk * kk
    sq_hi = sq.astype(BF16)
    sq_lo = (sq - sq_hi.astype(F32)).astype(BF16)
    ss = _dot(sq_hi, seg_ref[...]) + _dot(sq_lo, seg_ref[...])
    kkn = kk / jnp.maximum(jnp.sqrt(ss), 1e-12)
    kkn_o[...] = kkn.astype(BF16)
    rw = w1_ref.shape[1] // 2
    rg = g1_ref.shape[1] // 2
    for z in range(2):
        wl = w0_ref[z:z + 1, :] + _dot(lw_in[:, z * rw:(z + 1) * rw], w2_ref[z])
        w_log = -_softplus(-wl) - 0.5
        lw_o[z] = -jnp.exp(w_log)
        a_lr = _sigmoid(a0_ref[z:z + 1, :] + _dot(a_in[:, z * rw:(z + 1) * rw], a2_ref[z]))
        km_o[z] = (k * (1.0 + (a_lr - 1.0) * ka_ref[...])).astype(BF16)
        b_o[z] = (kkn * a_lr).astype(BF16)
        gate_o[z] = _dot(g_in[:, z * rg:(z + 1) * rg], g2_ref[z]).astype(BF16)


def _rk_scan_kernel(r_ref, v_ref, kkn_ref, lw_ref, km_ref, b_ref, gate_ref, rk_ref, lg_ref, lb_ref,
                    o_ref, s_ref, *, reverse):
    tb, hb = r_ref.shape
    cl = RK_CHUNK
    nc, nh = tb // cl, hb // RK_HEAD_DIM

    @pl.when(pl.program_id(2) == 0)
    def _():
        s_ref[...] = jnp.zeros(s_ref.shape, F32)

    ri = lax.broadcasted_iota(jnp.int32, (tb, tb), 0)
    ci = lax.broadcasted_iota(jnp.int32, (tb, tb), 1)
    ordered = (ci >= ri) if reverse else (ci <= ri)
    tri = jnp.where((ri // cl == ci // cl) & ordered, 1.0, 0.0).astype(BF16)
    lw = lw_ref[...]
    lw_hi = lw.astype(BF16)
    lw_lo = (lw - lw_hi.astype(F32)).astype(BF16)
    cum = _dot(tri, lw_hi) + _dot(tri, lw_lo)
    last = 0 if reverse else cl - 1
    tot_rows = [cum[c * cl + last:c * cl + last + 1, :] for c in range(nc)]
    tot = jnp.concatenate([jnp.broadcast_to(tr, (cl, hb)) for tr in tot_rows], axis=0)

    rr = r_ref[...].astype(F32)
    vv = v_ref[...]
    kkn = kkn_ref[...].astype(F32)
    km = km_ref[...].astype(F32)
    bb = b_ref[...].astype(F32)
    e_neg = jnp.exp(-cum)
    e_rem = jnp.exp(tot - cum)
    a_t = (-kkn * jnp.exp(cum - lw)).astype(BF16)
    r_t = (rr * jnp.exp(cum)).astype(BF16)
    b_t = (bb * e_neg).astype(BF16)
    k_t = (km * e_neg).astype(BF16)
    b_h = (bb * e_rem).astype(BF16)
    k_h = (km * e_rem).astype(BF16)
    bonus_w = rr * km * rk_ref[...]

    pw_ = 2 * RK_HEAD_DIM
    npair = hb // pw_
    t_i = lax.broadcasted_iota(jnp.int32, (cl, pw_), 0)
    s_i = lax.broadcasted_iota(jnp.int32, (cl, pw_), 1) % cl
    strict = (t_i < s_i) if reverse else (t_i > s_i)
    incl = (t_i <= s_i) if reverse else (t_i >= s_i)
    eye = jnp.where(t_i == s_i, 1.0, 0.0)
    head0 = lax.broadcasted_iota(jnp.int32, (cl, pw_), 1) < RK_HEAD_DIM
    q0 = lax.broadcasted_iota(jnp.int32, (pw_, pw_), 0)
    q1 = lax.broadcasted_iota(jnp.int32, (pw_, pw_), 1)
    same_head = (q0 // RK_HEAD_DIM) == (q1 // RK_HEAD_DIM)
    ones_bd = jnp.where(same_head, 1.0, 0.0).astype(BF16)
    ident = jnp.where(q0 == q1, 1.0, 0.0).astype(BF16)

    def bd(x):
        z = jnp.zeros_like(x)
        return jnp.concatenate([jnp.where(head0, x, z), jnp.where(head0, z, x)], axis=0)

    keys = [(c, p) for c in range(nc) for p in range(npair)]
    rsl = {k: slice(k[0] * cl, (k[0] + 1) * cl) for k in keys}
    lsl = {k: slice(k[1] * pw_, (k[1] + 1) * pw_) for k in keys}
    blk = lambda arr, k: arr[rsl[k], lsl[k]]

    gram = {k: _dot_nt(jnp.concatenate([blk(a_t, k), blk(r_t, k)], axis=0),
                       jnp.concatenate([bd(blk(b_t, k)), bd(blk(k_t, k))], axis=0)) for k in keys}
    m_ab = {k: jnp.where(strict, gram[k][:cl, :pw_], 0.0) for k in keys}
    m_ak = {k: jnp.where(strict, gram[k][:cl, pw_:], 0.0).astype(BF16) for k in keys}
    m_rb = {k: jnp.where(incl, gram[k][cl:, :pw_], 0.0).astype(BF16) for k in keys}
    m_rk = {k: jnp.where(incl, gram[k][cl:, pw_:], 0.0).astype(BF16) for k in keys}
    vbd = {k: bd(blk(vv, k)) for k in keys}
    my = {k: _dot(jnp.concatenate([m_ak[k], m_rk[k]], axis=0), vbd[k]) for k in keys}
    v_t = {k: _dot_nt(ident, blk(vv, k)).astype(BF16) for k in keys}
    vk = {k: jnp.where(same_head, _dot(v_t[k], blk(k_h, k)), 0.0) for k in keys}
    inv = {k: eye + m_ab[k] for k in keys}
    pwb = {k: m_ab[k].astype(BF16) for k in keys}
    pw = {k: _dot(pwb[k], bd(pwb[k])) for k in keys}
    n = 4
    while n < cl:
        pwb = {k: pw[k].astype(BF16) for k in keys}
        res = {k: _dot(pwb[k], jnp.concatenate([bd(pwb[k]), bd(inv[k].astype(BF16))], axis=1)) for k in keys}
        pw = {k: res[k][:, :pw_] for k in keys}
        inv = {k: inv[k] + res[k][:, pw_:] for k in keys}
        n *= 2
    inv = {k: (inv[k] + _dot(pw[k].astype(BF16), bd(inv[k].astype(BF16)))).astype(BF16) for k in keys}
    au = {k: _dot(inv[k], jnp.concatenate([bd(blk(a_t, k)), bd(my[k][:cl].astype(BF16))], axis=1))
          for k in keys}
    a_hat = {k: au[k][:, :pw_].astype(BF16) for k in keys}
    u0_t = {k: _dot_nt(ident, au[k][:, pw_:].astype(BF16)) for k in keys}
    ar = {k: jnp.concatenate([a_hat[k], blk(r_t, k)], axis=0) for k in keys}

    pairs = range(npair)
    for c in (range(nc - 1, -1, -1) if reverse else range(nc)):
        st = {p: s_ref[p] for p in pairs}
        stb = {p: st[p].astype(BF16) for p in pairs}
        res = {p: _dot_nt(ar[c, p], stb[p]) for p in pairs}
        res_t = {p: _dot_nt(stb[p], a_hat[c, p]) for p in pairs}
        ub = {p: (au[c, p][:, pw_:] + res[p][:cl]).astype(BF16) for p in pairs}
        ub_t = {p: (u0_t[c, p] + res_t[p]).astype(BF16) for p in pairs}
        y = {p: my[c, p][cl:] + res[p][cl:] + _dot(m_rb[c, p], bd(ub[p])) for p in pairs}
        for p in pairs:
            k = (c, p)
            decay = jnp.exp(tot_rows[c][:, lsl[k]])
            s_ref[p] = jnp.where(same_head, st[p] * decay + _dot(ub_t[p], blk(b_h, k)), 0.0) + vk[k]
        sums = {p: _dot(jnp.concatenate([y[p].astype(BF16), blk(bonus_w, (c, p)).astype(BF16)], axis=0), ones_bd)
                for p in pairs}
        dy = {p: y[p] - sums[p][:cl] * (1.0 / RK_HEAD_DIM) for p in pairs}
        var = {p: _dot((dy[p] * dy[p]).astype(BF16), ones_bd) * (1.0 / RK_HEAD_DIM) for p in pairs}
        for p in pairs:
            k = (c, p)
            yn = dy[p] * lax.rsqrt(var[p] + RK_GN_EPS) * lg_ref[:, lsl[k]] + lb_ref[:, lsl[k]]
            bonus = sums[p][cl:] * blk(vv, k).astype(F32)
            o_ref[rsl[k], lsl[k]] = (blk(gate_ref, k).astype(F32) * (yn + bonus)).astype(BF16)


def rwkv_mixer_layer(x, mod, p, ln_g, ln_b, tt=256):
    bsz, seq, d = x.shape
    tt = min(tt, seq)
    nt = seq // tt
    x_spec, prev_spec, next_spec = _tile_specs(seq, tt, d)
    tile = pl.BlockSpec((None, tt, d), lambda b, t: (b, t, 0))
    tile2 = pl.BlockSpec((2, None, tt, d), lambda b, t: (0, b, t, 0))
    rw, rg = p['w1'].shape[1] // 2, p['g1'].shape[1] // 2
    sh_bf = jax.ShapeDtypeStruct(x.shape, BF16)
    sh2_bf = jax.ShapeDtypeStruct((2,) + x.shape, BF16)
    sh2_f32 = jax.ShapeDtypeStruct((2,) + x.shape, F32)
    r, v, kkn, lw, km, bz, gate = pl.pallas_call(
        functools.partial(_rk_proj_kernel, nt=nt, tt=tt),
        grid=(bsz, nt),
        in_specs=[x_spec, prev_spec, next_spec, _mod_spec(d), _const_spec((6, d)),
                  _const_spec((3, d, d)),
                  _const_spec((d, 2 * rw)), _const_spec((2, rw, d)),
                  _const_spec((d, 2 * rw)), _const_spec((2, rw, d)),
                  _const_spec((d, 2 * rg)), _const_spec((2, rg, d)),
                  _const_spec((2, d)), _const_spec((2, d)), _const_spec((1, d)), _const_spec((1, d)),
                  _const_spec((d, d))],
        out_specs=[tile, tile, tile, tile2, tile2, tile2, tile2],
        out_shape=[sh_bf, sh_bf, sh_bf, sh2_f32, sh2_bf, sh2_bf, sh2_bf],
        compiler_params=_cparams("parallel", "arbitrary"),
        name="rwkv_proj",
    )(x, x, x, mod, p['mu'], p['w_rkv'], p['w1'], p['w2'], p['a1'], p['a2'], p['g1'], p['g2'],
      p['w0'], p['a0'], p['k_k'], p['k_a'], p['seg'])

    tb = min(RK_TOK_BLOCK, seq)
    hb = RK_LANE_BLOCK
    nb = seq // tb
    outs = []
    for z in range(2):
        rev = z == 1

        def tok(i, rev=rev):
            return nb - 1 - i if rev else i

        blk = pl.BlockSpec((None, tb, hb), lambda b, g, i: (b, tok(i), g))
        blk2 = pl.BlockSpec((None, None, tb, hb), lambda b, g, i, z=z: (z, b, tok(i), g))
        vec = pl.BlockSpec((1, hb), lambda b, g, i: (0, g))
        vec2 = pl.BlockSpec((None, 1, hb), lambda b, g, i, z=z: (z, 0, g))
        outs.append(pl.pallas_call(
            functools.partial(_rk_scan_kernel, reverse=rev),
            grid=(bsz, d // hb, nb),
            in_specs=[blk, blk, blk, blk2, blk2, blk2, blk2, vec, vec2, vec2],
            out_specs=blk,
            out_shape=sh_bf,
            scratch_shapes=[pltpu.VMEM((hb // (2 * RK_HEAD_DIM), 2 * RK_HEAD_DIM, 2 * RK_HEAD_DIM), F32)],
            compiler_params=_cparams("parallel", "parallel", "arbitrary"),
            name="rwkv_scan_rev" if rev else "rwkv_scan_fwd",
        )(r, v, kkn, lw, km, bz, gate, p['r_k'], p['lnx_g'], p['lnx_b']))
    return proj_residual_ln(outs, x, mod, p['w_o'], ln_g, ln_b)


def _prepare_params(w):
    d = D_MODEL
    bf = lambda a: a.astype(BF16)
    row = lambda a: a.reshape(1, d)
    lane = jnp.arange(d) // RK_HEAD_DIM
    p = {
        'w_ada': w['w_ada'], 'b_ada': w['b_ada'],
        'ln_g': w['ln_g'], 'ln_b': w['ln_b'],
        'conv_w_in': bf(w['conv_w_in']), 'conv_w': w['conv_w'], 'conv_w_out': bf(w['conv_w_out']),
        'na_w_qkv': bf(w['na_w_qkv']), 'na_w_o': bf(w['na_w_o']),
        'na_bias': [_na_bias_table(w['na_rpb'][j]) for j in range(w['na_rpb'].shape[0])],
        'ffn_w_gu': bf(w['ffn_w_gu']), 'ffn_w_down': bf(w['ffn_w_down']),
        'moe_router': [_pad_router(w['moe_w_router'][j], w['moe_b_router'][j])
                       for j in range(w['moe_w_router'].shape[0])],
        'moe_w_gu': bf(w['moe_w_gu']), 'moe_w_down': bf(w['moe_w_down']),
        'rk': [],
    }
    cat = lambda a: jnp.concatenate([a[0], a[1]], axis=1)
    for j in range(w['rk_mu'].shape[0]):
        p['rk'].append({
            'mu': w['rk_mu'][j], 'w_rkv': bf(w['rk_w_rkv'][j]),
            'w1': bf(cat(w['rk_w1'][j])), 'w2': bf(w['rk_w2'][j]),
            'a1': bf(cat(w['rk_a1'][j])), 'a2': bf(w['rk_a2'][j]),
            'g1': bf(cat(w['rk_g1'][j])), 'g2': bf(w['rk_g2'][j]),
            'w0': w['rk_w0'][j], 'a0': w['rk_a0'][j],
            'k_k': row(w['rk_k_k'][j]), 'k_a': row(w['rk_k_a'][j]),
            'r_k': w['rk_r_k'][j].reshape(1, d),
            'lnx_g': w['rk_lnx_g'][j].reshape(2, 1, d), 'lnx_b': w['rk_lnx_b'][j].reshape(2, 1, d),
            'w_o': bf(w['rk_w_o'][j]),
            'seg': (lane[:, None] == lane[None, :]).astype(BF16),
        })
    return p


def _trunk(x, c, p):
    d = D_MODEL
    mod_all = ada_modulation(c, p['w_ada'], p['b_ada'])
    for i in range(DEPTH):
        mod = mod_all[i]
        lng = lambda s: p['ln_g'][i, s].reshape(1, d)
        lnb = lambda s: p['ln_b'][i, s].reshape(1, d)
        kind, j = i % 3, i // 3
        if kind == 0:
            x = conv_mixer_layer(x, mod, p['conv_w_in'][j], p['conv_w'][j], p['conv_w_out'][j],
                                 lng(0), lnb(0))
        elif kind == 1:
            x = na_mixer_layer(x, mod, p['na_w_qkv'][j], p['na_bias'][j], p['na_w_o'][j], lng(0), lnb(0))
        else:
            x = rwkv_mixer_layer(x, mod, p['rk'][j], lng(0), lnb(0))
        if i % 2 == 0:
            x = dense_ffn_layer(x, mod, p['ffn_w_gu'][i // 2], p['ffn_w_down'][i // 2], lng(1), lnb(1))
        else:
            wr, br = p['moe_router'][i // 2]
            x = moe_ffn_layer(x, mod, wr, br, p['moe_w_gu'][i // 2], p['moe_w_down'][i // 2],
                              lng(1), lnb(1))
    return x


def kernel(x_prompt, x_sample, c_prompt, c_sample, w_ada, b_ada, ln_g, ln_b, conv_w_in, conv_w, conv_w_out, na_w_qkv, na_rpb, na_w_o, rk_mu, rk_w_rkv, rk_w0, rk_w1, rk_w2, rk_a0, rk_a1, rk_a2, rk_g1, rk_g2, rk_k_k, rk_k_a, rk_r_k, rk_lnx_g, rk_lnx_b, rk_w_o, ffn_w_gu, ffn_w_down, moe_w_router, moe_b_router, moe_w_gu, moe_w_down):
    p = _prepare_params(dict(
        w_ada=w_ada, b_ada=b_ada, ln_g=ln_g, ln_b=ln_b,
        conv_w_in=conv_w_in, conv_w=conv_w, conv_w_out=conv_w_out,
        na_w_qkv=na_w_qkv, na_rpb=na_rpb, na_w_o=na_w_o,
        rk_mu=rk_mu, rk_w_rkv=rk_w_rkv, rk_w0=rk_w0, rk_w1=rk_w1, rk_w2=rk_w2,
        rk_a0=rk_a0, rk_a1=rk_a1, rk_a2=rk_a2, rk_g1=rk_g1, rk_g2=rk_g2,
        rk_k_k=rk_k_k, rk_k_a=rk_k_a, rk_r_k=rk_r_k, rk_lnx_g=rk_lnx_g, rk_lnx_b=rk_lnx_b,
        rk_w_o=rk_w_o, ffn_w_gu=ffn_w_gu, ffn_w_down=ffn_w_down,
        moe_w_router=moe_w_router, moe_b_router=moe_b_router,
        moe_w_gu=moe_w_gu, moe_w_down=moe_w_down))
    return (_trunk(x_prompt, c_prompt, p), _trunk(x_sample, c_sample, p))
```

```python
import functools

import jax
import jax.numpy as jnp
from jax import lax
from jax.experimental import pallas as pl
from jax.experimental.pallas import tpu as pltpu

F32 = jnp.float32
BF16 = jnp.bfloat16

D_MODEL = 1024
DEPTH = 4
ALPHA = (2 * DEPTH) ** 0.25
LN_EPS = 1e-5

GRID_W = 64
NA_HEADS = 16
NA_HEAD_DIM = D_MODEL // NA_HEADS
NA_WIN_ROWS = 8
NA_WIN_COLS = 16
NEG_INF = -1e30

RK_HEAD_DIM = 64
RK_GN_EPS = 64e-5
RK_CHUNK = 64
RK_TOK_BLOCK = 128
RK_LANE_BLOCK = 1024

N_EXPERTS = 8
ROUTER_LANES = 128

VMEM_LIMIT = 52 * 1024 * 1024
MOE_VMEM_LIMIT = 58 * 1024 * 1024


def _cparams(*sem):
    return pltpu.CompilerParams(dimension_semantics=sem, vmem_limit_bytes=VMEM_LIMIT)


def _const_spec(shape):
    nd = len(shape)
    return pl.BlockSpec(shape, lambda *_: (0,) * nd, pipeline_mode=pl.Buffered(1))


def _dot(a, b):
    return jnp.dot(a, b, preferred_element_type=F32)


def _dot_nt(a, b):
    return lax.dot_general(a, b, (((1,), (1,)), ((), ())), preferred_element_type=F32)


def _dot_tn(a, b):
    return lax.dot_general(a, b, (((0,), (0,)), ((), ())), preferred_element_type=F32)


def _sigmoid(x):
    return 1.0 / (1.0 + jnp.exp(-x))


def _layer_norm(y, g, b):
    mu = jnp.mean(y, axis=-1, keepdims=True)
    d = y - mu
    var = jnp.mean(d * d, axis=-1, keepdims=True)
    return d * lax.rsqrt(var + LN_EPS) * g + b


def _ada_kernel(c_ref, w_ref, b_ref, o_ref):
    c = c_ref[...]
    s = (c * _sigmoid(c)).astype(BF16)
    o_ref[...] = _dot(s, w_ref[...].astype(BF16)) + b_ref[...]


def ada_modulation(c, w_ada, b_ada):
    bsz, d = c.shape
    depth = w_ada.shape[0]
    out = pl.pallas_call(
        _ada_kernel,
        grid=(depth, 6),
        in_specs=[
            pl.BlockSpec((bsz, d), lambda i, j: (0, 0)),
            pl.BlockSpec((None, d, d), lambda i, j: (i, 0, j)),
            pl.BlockSpec((None, None, 1, d), lambda i, j: (i, j, 0, 0)),
        ],
        out_specs=pl.BlockSpec((None, None, bsz, d), lambda i, j: (i, j, 0, 0)),
        out_shape=jax.ShapeDtypeStruct((depth, 6, bsz, d), F32),
        compiler_params=_cparams("arbitrary", "arbitrary"),
        name="ada_modulation",
    )(c, w_ada, b_ada.reshape(depth, 6, 1, d))
    return jnp.transpose(out, (0, 2, 1, 3))


def _tile_specs(seq, tt, d):
    nb8 = seq // 8
    per = tt // 8
    x_spec = pl.BlockSpec((None, tt, d), lambda b, t: (b, t, 0))
    prev_spec = pl.BlockSpec((None, 8, d), lambda b, t: (b, jnp.maximum(t * per - 1, 0), 0))
    next_spec = pl.BlockSpec((None, 8, d), lambda b, t: (b, jnp.minimum((t + 1) * per, nb8 - 1), 0))
    return x_spec, prev_spec, next_spec


def _mod_spec(d):
    return pl.BlockSpec((None, 6, d), lambda b, t: (b, 0, 0))


def _shifted(cur, prev_row, next_row, tt):
    row = lax.broadcasted_iota(jnp.int32, (tt, 1), 0)
    m1 = jnp.where(row == 0, prev_row, pltpu.roll(cur, 1, 0))
    p1 = jnp.where(row == tt - 1, next_row, pltpu.roll(cur, tt - 1, 0))
    return m1, p1


def _conv_kernel(x_ref, xp_ref, xn_ref, mod_ref, win_ref, cw_ref, wout_ref, lng_ref, lnb_ref,
                 o_ref, *, nt, tt, d):
    t = pl.program_id(1)
    sh, sc, gate = mod_ref[0:1, :], mod_ref[1:2, :], mod_ref[2:3, :]
    x = x_ref[...]
    h = (x * (1.0 + sc) + sh).astype(BF16)
    p = _dot(h, win_ref[...])
    z = p[:, d:2 * d] * p[:, 2 * d:]
    halo = jnp.concatenate([xp_ref[...], xn_ref[...]], axis=0)
    hh = (halo * (1.0 + sc) + sh).astype(BF16)
    ph = _dot(hh, win_ref[:, d:])
    zh = ph[:, :d] * ph[:, d:]
    z_prev = jnp.where(t > 0, zh[7:8, :], 0.0)
    z_next = jnp.where(t < nt - 1, zh[8:9, :], 0.0)
    z_m1, z_p1 = _shifted(z, z_prev, z_next, tt)
    conv = z_m1 * cw_ref[0:1, :] + z * cw_ref[1:2, :] + z_p1 * cw_ref[2:3, :]
    y = (p[:, :d] * conv).astype(BF16)
    mix = _dot(y, wout_ref[...])
    o_ref[...] = _layer_norm(ALPHA * x + gate * mix, lng_ref[...], lnb_ref[...])


def conv_mixer_layer(x, mod, w_in, conv_w, w_out, ln_g, ln_b, tt=512):
    bsz, seq, d = x.shape
    tt = min(tt, seq)
    nt = seq // tt
    x_spec, prev_spec, next_spec = _tile_specs(seq, tt, d)
    return pl.pallas_call(
        functools.partial(_conv_kernel, nt=nt, tt=tt, d=d),
        grid=(bsz, nt),
        in_specs=[x_spec, prev_spec, next_spec, _mod_spec(d),
                  _const_spec((d, 3 * d)), _const_spec((3, d)), _const_spec((d, d)),
                  _const_spec((1, d)), _const_spec((1, d))],
        out_specs=pl.BlockSpec((None, tt, d), lambda b, t: (b, t, 0)),
        out_shape=jax.ShapeDtypeStruct(x.shape, F32),
        compiler_params=_cparams("parallel", "arbitrary"),
        name="conv_mixer",
    )(x, x, x, mod, w_in, conv_w, w_out, ln_g, ln_b)


def _ffn_kernel(x_ref, mod_ref, wgu_ref, wd_ref, lng_ref, lnb_ref, o_ref, *, ff, fc):
    sh, sc, gate = mod_ref[3:4, :], mod_ref[4:5, :], mod_ref[5:6, :]
    x = x_ref[...]
    h = (x * (1.0 + sc) + sh).astype(BF16)
    acc = jnp.zeros(x.shape, F32)
    for c in range(ff // fc):
        g = _dot(h, wgu_ref[:, c * fc:(c + 1) * fc])
        u = _dot(h, wgu_ref[:, ff + c * fc:ff + (c + 1) * fc])
        act = (g * _sigmoid(g) * u).astype(BF16)
        acc = acc + _dot(act, wd_ref[c * fc:(c + 1) * fc, :])
    o_ref[...] = _layer_norm(ALPHA * x + gate * acc, lng_ref[...], lnb_ref[...])


def dense_ffn_layer(x, mod, w_gu, w_down, ln_g, ln_b, tt=512):
    bsz, seq, d = x.shape
    tt = min(tt, seq)
    ff = w_down.shape[0]
    fc = ff // 2
    return pl.pallas_call(
        functools.partial(_ffn_kernel, ff=ff, fc=fc),
        grid=(bsz, seq // tt),
        in_specs=[pl.BlockSpec((None, tt, d), lambda b, t: (b, t, 0)), _mod_spec(d),
                  _const_spec((d, 2 * ff)), _const_spec((ff, d)),
                  _const_spec((1, d)), _const_spec((1, d))],
        out_specs=pl.BlockSpec((None, tt, d), lambda b, t: (b, t, 0)),
        out_shape=jax.ShapeDtypeStruct(x.shape, F32),
        compiler_params=_cparams("parallel", "arbitrary"),
        name="dense_ffn",
    )(x, mod, w_gu, w_down, ln_g, ln_b)


MOE_ROW_CHUNK = 256
MOE_SPARSE_ROWS = (240, 256, 272, 288, 320, 512)


def _moe_kernel(x_ref, mod_ref, wr_ref, br_ref, wgu_ref, wd_ref, lng_ref, lnb_ref, o_ref,
                h_sc, comb_sc, acc_sc, rankc_sc, rankr_sc, maskr_sc, *, fe, caps):
    e = pl.program_id(2)
    tt = comb_sc.shape[0]
    lane = lax.broadcasted_iota(jnp.int32, comb_sc.shape, 1)

    @pl.when(e == 0)
    def _():
        sh, sc = mod_ref[3:4, :], mod_ref[4:5, :]
        h = x_ref[...] * (1.0 + sc) + sh
        h_hi = h.astype(BF16)
        h_sc[...] = h_hi
        h_lo = (h - h_hi.astype(F32)).astype(BF16)
        wr = wr_ref[...]
        w_hi = wr.astype(BF16)
        w_lo = (wr - w_hi.astype(F32)).astype(BF16)
        logits = _dot(h_hi, w_hi) + _dot(h_lo, w_hi) + _dot(h_hi, w_lo) + br_ref[...]
        m = jnp.max(logits, axis=-1, keepdims=True)
        ex = jnp.exp(logits - m)
        probs = ex / jnp.sum(ex, axis=-1, keepdims=True)
        valid = lane < N_EXPERTS
        p = jnp.where(valid, probs, -1.0)
        p1 = jnp.max(p, axis=-1, keepdims=True)
        i1 = jnp.min(jnp.where(p == p1, lane, ROUTER_LANES), axis=-1, keepdims=True)
        pr = jnp.where(lane == i1, -1.0, p)
        p2 = jnp.max(pr, axis=-1, keepdims=True)
        i2 = jnp.min(jnp.where(pr == p2, lane, ROUTER_LANES), axis=-1, keepdims=True)
        tot = p1 + p2
        comb = jnp.where(lane == i1, p1 / tot, 0.0) + jnp.where(lane == i2, p2 / tot, 0.0)
        comb_sc[...] = comb
        acc_sc[...] = jnp.zeros(acc_sc.shape, F32)
        routed = jnp.where(comb > 0.0, 1.0, 0.0).astype(BF16)
        q0 = lax.broadcasted_iota(jnp.int32, (ROUTER_LANES, ROUTER_LANES), 0)
        q1 = lax.broadcasted_iota(jnp.int32, (ROUTER_LANES, ROUTER_LANES), 1)
        routed_t = _dot_nt(jnp.where(q0 == q1, 1.0, 0.0).astype(BF16), routed)
        maskr_sc[...] = routed_t
        routed_tb = routed_t.astype(BF16)
        rc = min(MOE_ROW_CHUNK, tt)
        for c in range(tt // rc):
            r_i = lax.broadcasted_iota(jnp.int32, (rc, tt), 0) + c * rc
            c_i = lax.broadcasted_iota(jnp.int32, (rc, tt), 1)
            rankc_sc[c * rc:(c + 1) * rc, :] = _dot(jnp.where(c_i < r_i, 1.0, 0.0).astype(BF16), routed)
            r_j = lax.broadcasted_iota(jnp.int32, (tt, rc), 0)
            c_j = lax.broadcasted_iota(jnp.int32, (tt, rc), 1) + c * rc
            rankr_sc[:, c * rc:(c + 1) * rc] = _dot(routed_tb, jnp.where(r_j < c_j, 1.0, 0.0).astype(BF16))

    def expert(hb):
        g = _dot(hb, wgu_ref[:, :fe])
        u = _dot(hb, wgu_ref[:, fe:])
        act = (g * _sigmoid(g) * u).astype(BF16)
        return _dot(act, wd_ref[...])

    w_col = jnp.sum(jnp.where(lane == e, comb_sc[...], 0.0), axis=-1, keepdims=True)
    rk_col = jnp.sum(jnp.where(lane == e, rankc_sc[...], 0.0), axis=-1, keepdims=True).astype(jnp.int32)
    rk_row = rankr_sc[pl.ds(e, 1), :].astype(jnp.int32)
    mk_row = maskr_sc[pl.ds(e, 1), :]
    n_routed = jnp.sum(mk_row).astype(jnp.int32)
    half = tt // 2

    def sparse(cap):
        sub = lax.broadcasted_iota(jnp.int32, (cap, tt), 0)
        pick = jnp.where((rk_row == sub) & (mk_row > 0.0), 1.0, 0.0).astype(BF16)
        y = expert(_dot(pick, h_sc[...]).astype(BF16)).astype(BF16)
        for r0 in (0, half):
            rows = slice(r0, r0 + half)
            ln_i = lax.broadcasted_iota(jnp.int32, (half, cap), 1)
            put = jnp.where((rk_col[rows] == ln_i) & (w_col[rows] > 0.0), 1.0, 0.0).astype(BF16)
            acc_sc[rows, :] += w_col[rows] * _dot(put, y)

    def dense():
        for r0 in (0, half):
            rows = slice(r0, r0 + half)
            acc_sc[rows, :] += w_col[rows] * expert(h_sc[rows, :])

    lo = 0
    for cap in caps:
        pl.when((n_routed > lo) & (n_routed <= cap))(functools.partial(sparse, cap))
        lo = cap
    pl.when(n_routed > lo)(dense)

    @pl.when(e == N_EXPERTS - 1)
    def _():
        gate = mod_ref[5:6, :]
        o_ref[...] = _layer_norm(ALPHA * x_ref[...] + gate * acc_sc[...], lng_ref[...], lnb_ref[...])


def moe_ffn_layer(x, mod, w_router, b_router, w_gu, w_down, ln_g, ln_b, tt=1024, caps=MOE_SPARSE_ROWS):
    bsz, seq, d = x.shape
    tt = min(tt, seq)
    n_e, fe = w_down.shape[0], w_down.shape[1]
    return pl.pallas_call(
        functools.partial(_moe_kernel, fe=fe, caps=caps),
        grid=(bsz, seq // tt, n_e),
        in_specs=[pl.BlockSpec((None, tt, d), lambda b, t, e: (b, t, 0), pipeline_mode=pl.Buffered(1)),
                  pl.BlockSpec((None, 6, d), lambda b, t, e: (b, 0, 0)),
                  _const_spec((d, ROUTER_LANES)), _const_spec((1, ROUTER_LANES)),
                  pl.BlockSpec((None, d, 2 * fe), lambda b, t, e: (e, 0, 0)),
                  pl.BlockSpec((None, fe, d), lambda b, t, e: (e, 0, 0)),
                  _const_spec((1, d)), _const_spec((1, d))],
        out_specs=pl.BlockSpec((None, tt, d), lambda b, t, e: (b, t, 0)),
        out_shape=jax.ShapeDtypeStruct(x.shape, F32),
        scratch_shapes=[pltpu.VMEM((tt, d), BF16), pltpu.VMEM((tt, ROUTER_LANES), F32),
                        pltpu.VMEM((tt, d), F32), pltpu.VMEM((tt, ROUTER_LANES), F32),
                        pltpu.VMEM((ROUTER_LANES, tt), F32), pltpu.VMEM((ROUTER_LANES, tt), F32)],
        compiler_params=pltpu.CompilerParams(
            dimension_semantics=("parallel", "arbitrary", "arbitrary"), vmem_limit_bytes=MOE_VMEM_LIMIT),
        name="moe_ffn",
    )(x, mod, w_router, b_router, w_gu, w_down, ln_g, ln_b)


def _pad_router(w_router, b_router):
    d, n_e = w_router.shape
    w = jnp.zeros((d, ROUTER_LANES), F32).at[:, :n_e].set(w_router)
    b = jnp.full((1, ROUTER_LANES), NEG_INF, F32).at[0, :n_e].set(b_router)
    return w, b


def _qkv_kernel(x_ref, mod_ref, w_ref, q_ref, k_ref, v_ref, *, d):
    sh, sc = mod_ref[0:1, :], mod_ref[1:2, :]
    h = (x_ref[...] * (1.0 + sc) + sh).astype(BF16)
    p = _dot(h, w_ref[...])
    q_ref[...] = (p[:, :d] * (NA_HEAD_DIM ** -0.5)).astype(BF16)
    k_ref[...] = p[:, d:2 * d].astype(BF16)
    v_ref[...] = p[:, 2 * d:].astype(BF16)


def _na_kernel(q_ref, k_ref, v_ref, bias_ref, o_ref, *, rows):
    r = pl.program_id(1)
    r0 = jnp.clip(r - NA_WIN_ROWS // 2, 0, rows - NA_WIN_ROWS)
    start = pl.multiple_of(r0 * GRID_W, GRID_W)
    nk = NA_WIN_ROWS * GRID_W
    pw_ = 2 * NA_HEAD_DIM
    pairs = range(NA_HEADS // 2)
    lanes = [slice(p * pw_, (p + 1) * pw_) for p in pairs]
    head0 = lax.broadcasted_iota(jnp.int32, (GRID_W, pw_), 1) < NA_HEAD_DIM
    q2 = {}
    for p in pairs:
        q = q_ref[:, lanes[p]]
        z = jnp.zeros_like(q)
        q2[p] = jnp.concatenate([jnp.where(head0, q, z), jnp.where(head0, z, q)], axis=0)
    s = {p: _dot_nt(q2[p], k_ref[pl.ds(start, nk), lanes[p]])
         + bias_ref[2 * p:2 * p + 2].reshape(2 * GRID_W, nk) for p in pairs}
    m = {p: jnp.max(s[p], axis=-1, keepdims=True) for p in pairs}
    e = {p: jnp.exp(s[p] - m[p]) for p in pairs}
    l = {p: jnp.sum(e[p], axis=-1, keepdims=True) for p in pairs}
    o = {p: _dot(e[p].astype(BF16), v_ref[pl.ds(start, nk), lanes[p]]) / l[p] for p in pairs}
    for p in pairs:
        o_ref[:, lanes[p]] = jnp.where(head0, o[p][:GRID_W], o[p][GRID_W:]).astype(BF16)


def _na_bias_table(rpb):
    nh, nr, nc = rpb.shape
    w = GRID_W
    qc = jnp.arange(w)[:, None]
    kc = jnp.arange(w)[None, :]
    cs = jnp.clip(qc - NA_WIN_COLS // 2, 0, w - NA_WIN_COLS)
    col_ok = (kc >= cs) & (kc < cs + NA_WIN_COLS)
    lo = w - NA_WIN_COLS
    e = jnp.pad(rpb, ((0, 0), (0, 0), (lo, 2 * w - lo - nc)))
    flat = jnp.tile(e, (1, 1, w))
    toep = flat[:, :, w - 1:w - 1 + w * (2 * w - 1)].reshape(nh, nr, w, 2 * w - 1)[..., :w]
    toep = jnp.where(col_ok[None, None], toep, NEG_INF)
    tab = jnp.stack([toep[:, dl:dl + NA_WIN_ROWS] for dl in range(NA_WIN_ROWS)], axis=1)
    tab = jnp.transpose(tab, (0, 1, 3, 2, 4))
    return tab.reshape(nh, NA_WIN_ROWS, w, NA_WIN_ROWS * w)


def _proj_ln_kernel(*refs, n_in):
    a_refs = refs[:n_in]
    x_ref, mod_ref, w_ref, lng_ref, lnb_ref, o_ref = refs[n_in:]
    a = a_refs[0][...]
    if n_in == 2:
        a = (a.astype(F32) + a_refs[1][...].astype(F32)).astype(BF16)
    mix = _dot(a, w_ref[...])
    gate = mod_ref[2:3, :]
    o_ref[...] = _layer_norm(ALPHA * x_ref[...] + gate * mix, lng_ref[...], lnb_ref[...])


def proj_residual_ln(acts, x, mod, w_o, ln_g, ln_b, tt=512):
    bsz, seq, d = x.shape
    tt = min(tt, seq)
    tile = pl.BlockSpec((None, tt, d), lambda b, t: (b, t, 0))
    return pl.pallas_call(
        functools.partial(_proj_ln_kernel, n_in=len(acts)),
        grid=(bsz, seq // tt),
        in_specs=[tile] * len(acts) + [tile, _mod_spec(d), _const_spec((d, d)),
                                       _const_spec((1, d)), _const_spec((1, d))],
        out_specs=tile,
        out_shape=jax.ShapeDtypeStruct(x.shape, F32),
        compiler_params=_cparams("parallel", "arbitrary"),
        name="proj_residual_ln",
    )(*acts, x, mod, w_o, ln_g, ln_b)


def na_mixer_layer(x, mod, w_qkv, bias_tab, w_o, ln_g, ln_b, tt=512):
    bsz, seq, d = x.shape
    tt = min(tt, seq)
    rows = seq // GRID_W
    assert rows >= NA_WIN_ROWS and seq % GRID_W == 0
    tile = pl.BlockSpec((None, tt, d), lambda b, t: (b, t, 0))
    q, k, v = pl.pallas_call(
        functools.partial(_qkv_kernel, d=d),
        grid=(bsz, seq // tt),
        in_specs=[tile, _mod_spec(d), _const_spec((d, 3 * d))],
        out_specs=[tile] * 3,
        out_shape=[jax.ShapeDtypeStruct(x.shape, BF16)] * 3,
        compiler_params=_cparams("parallel", "arbitrary"),
        name="na_qkv",
    )(x, mod, w_qkv)

    def delta(r):
        return jnp.clip(r - NA_WIN_ROWS // 2, 0, rows - NA_WIN_ROWS) - r + NA_WIN_ROWS - 1

    row_spec = pl.BlockSpec((None, GRID_W, d), lambda b, r: (b, r, 0))
    seq_spec = pl.BlockSpec((None, seq, d), lambda b, r: (b, 0, 0))
    att = pl.pallas_call(
        functools.partial(_na_kernel, rows=rows),
        grid=(bsz, rows),
        in_specs=[row_spec, seq_spec, seq_spec,
                  pl.BlockSpec((NA_HEADS, None, GRID_W, NA_WIN_ROWS * GRID_W),
                               lambda b, r: (0, delta(r), 0, 0))],
        out_specs=row_spec,
        out_shape=jax.ShapeDtypeStruct(x.shape, BF16),
        compiler_params=_cparams("parallel", "arbitrary"),
        name="na_attention",
    )(q, k, v, bias_tab)
    return proj_residual_ln([att], x, mod, w_o, ln_g, ln_b, tt)


def _softplus(y):
    return jnp.maximum(y, 0.0) + jnp.log(1.0 + jnp.exp(-jnp.abs(y)))


def _rk_proj_kernel(x_ref, xp_ref, xn_ref, mod_ref, mu_ref, wrkv_ref, w1_ref, w2_ref, a1_ref, a2_ref,
                    g1_ref, g2_ref, w0_ref, a0_ref, kk_ref, ka_ref,
                    r_o, v_o, kk_o, lw_o, km_o, b_o, gate_o, *, nt, tt):
    t = pl.program_id(1)
    sh, sc = mod_ref[0:1, :], mod_ref[1:2, :]
    h = x_ref[...] * (1.0 + sc) + sh
    h_prev = jnp.where(t > 0, xp_ref[7:8, :] * (1.0 + sc) + sh, 0.0)
    h_next = jnp.where(t < nt - 1, xn_ref[0:1, :] * (1.0 + sc) + sh, 0.0)
    h_m1, h_p1 = _shifted(h, h_prev, h_next, tt)
    xx = 0.5 * (h_m1 + h_p1) - h

    def mixed(p):
        return (h + xx * mu_ref[p:p + 1, :]).astype(BF16)

    r = _dot(mixed(0), wrkv_ref[0])
    k = _dot(mixed(1), wrkv_ref[1])
    v = _dot(mixed(2), wrkv_ref[2])
    r_o[...] = r.astype(BF16)
    v_o[...] = v.astype(BF16)
    lw_in = jnp.tanh(_dot(mixed(3), w1_ref[...])).astype(BF16)
    a_in = _dot(mixed(4), a1_ref[...]).astype(BF16)
    g_in = _sigmoid(_dot(mixed(5), g1_ref[...])).astype(BF16)

    kk = k * kk_ref[...]
    kk_o[...] = kk.astype(BF16)
    rw = w1_ref.shape[1] // 2
    rg = g1_ref.shape[1] // 2
    for z in range(2):
        wl = w0_ref[z:z + 1, :] + _dot(lw_in[:, z * rw:(z + 1) * rw], w2_ref[z])
        w_log = -_softplus(-wl) - 0.5
        lw_o[z] = -jnp.exp(w_log)
        a_lr = _sigmoid(a0_ref[z:z + 1, :] + _dot(a_in[:, z * rw:(z + 1) * rw], a2_ref[z]))
        km_o[z] = (k * (1.0 + (a_lr - 1.0) * ka_ref[...])).astype(BF16)
        b_o[z] = (kk * a_lr).astype(BF16)
        gate_o[z] = _dot(g_in[:, z * rg:(z + 1) * rg], g2_ref[z]).astype(BF16)


def _head_sum(x):
    head0 = lax.broadcasted_iota(jnp.int32, x.shape, 1) < RK_HEAD_DIM
    s0 = jnp.sum(jnp.where(head0, x, 0.0), axis=-1, keepdims=True)
    s1 = jnp.sum(jnp.where(head0, 0.0, x), axis=-1, keepdims=True)
    return jnp.where(head0, s0, s1)


def _rk_scan_kernel(r_ref, v_ref, kk_ref, lw_ref, km_ref, b_ref, gate_ref, rk_ref, lg_ref, lb_ref,
                    o_ref, s_ref, *, reverse):
    tb, hb = r_ref.shape
    cl = RK_CHUNK
    nc, nh = tb // cl, hb // RK_HEAD_DIM

    @pl.when(pl.program_id(2) == 0)
    def _():
        s_ref[...] = jnp.zeros(s_ref.shape, F32)

    ri = lax.broadcasted_iota(jnp.int32, (tb, tb), 0)
    ci = lax.broadcasted_iota(jnp.int32, (tb, tb), 1)
    ordered = (ci >= ri) if reverse else (ci <= ri)
    tri = jnp.where((ri // cl == ci // cl) & ordered, 1.0, 0.0).astype(BF16)
    lw = lw_ref[...]
    lw_hi = lw.astype(BF16)
    lw_lo = (lw - lw_hi.astype(F32)).astype(BF16)
    cum = _dot(tri, lw_hi) + _dot(tri, lw_lo)
    last = 0 if reverse else cl - 1
    tot_rows = [cum[c * cl + last:c * cl + last + 1, :] for c in range(nc)]
    tot = jnp.concatenate([jnp.broadcast_to(tr, (cl, hb)) for tr in tot_rows], axis=0)

    rr = r_ref[...].astype(F32)
    vv = v_ref[...]
    kk = kk_ref[...].astype(F32)
    pair = 2 * RK_HEAD_DIM
    rnorm = jnp.concatenate(
        [1.0 / jnp.maximum(jnp.sqrt(_head_sum(jnp.square(kk[:, i:i + pair]))), 1e-12) for i in range(0, hb, pair)],
        axis=1)
    kkn = kk * rnorm
    km = km_ref[...].astype(F32)
    bb = b_ref[...].astype(F32) * rnorm
    e_neg = jnp.exp(-cum)
    e_rem = jnp.exp(tot - cum)
    a_t = (-kkn * jnp.exp(cum - lw)).astype(BF16)
    r_t = (rr * jnp.exp(cum)).astype(BF16)
    b_t = (bb * e_neg).astype(BF16)
    k_t = (km * e_neg).astype(BF16)
    b_h = (bb * e_rem).astype(BF16)
    k_h = (km * e_rem).astype(BF16)
    bonus_w = rr * km * rk_ref[...]

    pw_ = 2 * RK_HEAD_DIM
    npair = hb // pw_
    t_i = lax.broadcasted_iota(jnp.int32, (cl, pw_), 0)
    s_i = lax.broadcasted_iota(jnp.int32, (cl, pw_), 1) % cl
    strict = (t_i < s_i) if reverse else (t_i > s_i)
    incl = (t_i <= s_i) if reverse else (t_i >= s_i)
    eye = jnp.where(t_i == s_i, 1.0, 0.0)
    head0 = lax.broadcasted_iota(jnp.int32, (cl, pw_), 1) < RK_HEAD_DIM
    q0 = lax.broadcasted_iota(jnp.int32, (pw_, pw_), 0)
    q1 = lax.broadcasted_iota(jnp.int32, (pw_, pw_), 1)
    same_head = (q0 // RK_HEAD_DIM) == (q1 // RK_HEAD_DIM)

    head_sum = _head_sum

    def bd(x):
        z = jnp.zeros_like(x)
        return jnp.concatenate([jnp.where(head0, x, z), jnp.where(head0, z, x)], axis=0)

    keys = [(c, p) for c in range(nc) for p in range(npair)]
    rsl = {k: slice(k[0] * cl, (k[0] + 1) * cl) for k in keys}
    lsl = {k: slice(k[1] * pw_, (k[1] + 1) * pw_) for k in keys}
    blk = lambda arr, k: arr[rsl[k], lsl[k]]

    gram = {k: _dot_nt(jnp.concatenate([blk(a_t, k), blk(r_t, k)], axis=0),
                       jnp.concatenate([bd(blk(b_t, k)), bd(blk(k_t, k))], axis=0)) for k in keys}
    m_ab = {k: jnp.where(strict, gram[k][:cl, :pw_], 0.0) for k in keys}
    m_ak = {k: jnp.where(strict, gram[k][:cl, pw_:], 0.0).astype(BF16) for k in keys}
    m_rb = {k: jnp.where(incl, gram[k][cl:, :pw_], 0.0).astype(BF16) for k in keys}
    m_rk = {k: jnp.where(incl, gram[k][cl:, pw_:], 0.0).astype(BF16) for k in keys}
    vbd = {k: bd(blk(vv, k)) for k in keys}
    my = {k: _dot(jnp.concatenate([m_ak[k], m_rk[k]], axis=0), vbd[k]) for k in keys}
    v_t = {k: jnp.transpose(blk(vv, k).astype(F32)).astype(BF16) for k in keys}
    vk = {k: jnp.where(same_head, _dot(v_t[k], blk(k_h, k)), 0.0) for k in keys}
    inv = {k: eye + m_ab[k] for k in keys}
    pwb = {k: m_ab[k].astype(BF16) for k in keys}
    pw = {k: _dot(pwb[k], bd(pwb[k])) for k in keys}
    n = 4
    while n < cl:
        pwb = {k: pw[k].astype(BF16) for k in keys}
        res = {k: _dot(pwb[k], jnp.concatenate([bd(pwb[k]), bd(inv[k].astype(BF16))], axis=1)) for k in keys}
        pw = {k: res[k][:, :pw_] for k in keys}
        inv = {k: inv[k] + res[k][:, pw_:] for k in keys}
        n *= 2
    inv = {k: (inv[k] + _dot(pw[k].astype(BF16), bd(inv[k].astype(BF16)))).astype(BF16) for k in keys}
    au = {k: _dot(inv[k], jnp.concatenate([bd(blk(a_t, k)), bd(my[k][:cl].astype(BF16))], axis=1))
          for k in keys}
    a_hat = {k: au[k][:, :pw_].astype(BF16) for k in keys}
    u0_t = {k: jnp.transpose(au[k][:, pw_:]) for k in keys}
    ar = {k: jnp.concatenate([a_hat[k], blk(r_t, k)], axis=0) for k in keys}

    pairs = range(npair)
    for c in (range(nc - 1, -1, -1) if reverse else range(nc)):
        st = {p: s_ref[p] for p in pairs}
        stb = {p: st[p].astype(BF16) for p in pairs}
        res = {p: _dot_nt(ar[c, p], stb[p]) for p in pairs}
        res_t = {p: _dot_nt(stb[p], a_hat[c, p]) for p in pairs}
        ub = {p: (au[c, p][:, pw_:] + res[p][:cl]).astype(BF16) for p in pairs}
        ub_t = {p: (u0_t[c, p] + res_t[p]).astype(BF16) for p in pairs}
        y = {p: my[c, p][cl:] + res[p][cl:] + _dot(m_rb[c, p], bd(ub[p])) for p in pairs}
        for p in pairs:
            k = (c, p)
            decay = jnp.exp(tot_rows[c][:, lsl[k]])
            s_ref[p] = jnp.where(same_head, st[p] * decay + _dot(ub_t[p], blk(b_h, k)), 0.0) + vk[k]
        dy = {p: y[p] - head_sum(y[p]) * (1.0 / RK_HEAD_DIM) for p in pairs}
        var = {p: head_sum(dy[p] * dy[p]) * (1.0 / RK_HEAD_DIM) for p in pairs}
        for p in pairs:
            k = (c, p)
            yn = dy[p] * lax.rsqrt(var[p] + RK_GN_EPS) * lg_ref[:, lsl[k]] + lb_ref[:, lsl[k]]
            bonus = head_sum(blk(bonus_w, k)) * blk(vv, k).astype(F32)
            o_ref[rsl[k], lsl[k]] = (blk(gate_ref, k).astype(F32) * (yn + bonus)).astype(BF16)


def rwkv_mixer_layer(x, mod, p, ln_g, ln_b, tt=256):
    bsz, seq, d = x.shape
    tt = min(tt, seq)
    nt = seq // tt
    x_spec, prev_spec, next_spec = _tile_specs(seq, tt, d)
    tile = pl.BlockSpec((None, tt, d), lambda b, t: (b, t, 0))
    tile2 = pl.BlockSpec((2, None, tt, d), lambda b, t: (0, b, t, 0))
    rw, rg = p['w1'].shape[1] // 2, p['g1'].shape[1] // 2
    sh_bf = jax.ShapeDtypeStruct(x.shape, BF16)
    sh2_bf = jax.ShapeDtypeStruct((2,) + x.shape, BF16)
    sh2_f32 = jax.ShapeDtypeStruct((2,) + x.shape, F32)
    r, v, kkn, lw, km, bz, gate = pl.pallas_call(
        functools.partial(_rk_proj_kernel, nt=nt, tt=tt),
        grid=(bsz, nt),
        in_specs=[x_spec, prev_spec, next_spec, _mod_spec(d), _const_spec((6, d)),
                  _const_spec((3, d, d)),
                  _const_spec((d, 2 * rw)), _const_spec((2, rw, d)),
                  _const_spec((d, 2 * rw)), _const_spec((2, rw, d)),
                  _const_spec((d, 2 * rg)), _const_spec((2, rg, d)),
                  _const_spec((2, d)), _const_spec((2, d)), _const_spec((1, d)), _const_spec((1, d))],
        out_specs=[tile, tile, tile, tile2, tile2, tile2, tile2],
        out_shape=[sh_bf, sh_bf, sh_bf, sh2_f32, sh2_bf, sh2_bf, sh2_bf],
        compiler_params=_cparams("parallel", "arbitrary"),
        name="rwkv_proj",
    )(x, x, x, mod, p['mu'], p['w_rkv'], p['w1'], p['w2'], p['a1'], p['a2'], p['g1'], p['g2'],
      p['w0'], p['a0'], p['k_k'], p['k_a'])

    tb = min(RK_TOK_BLOCK, seq)
    hb = RK_LANE_BLOCK
    nb = seq // tb
    outs = []
    for z in range(2):
        rev = z == 1

        def tok(i, rev=rev):
            return nb - 1 - i if rev else i

        blk = pl.BlockSpec((None, tb, hb), lambda b, g, i: (b, tok(i), g))
        blk2 = pl.BlockSpec((None, None, tb, hb), lambda b, g, i, z=z: (z, b, tok(i), g))
        vec = pl.BlockSpec((1, hb), lambda b, g, i: (0, g))
        vec2 = pl.BlockSpec((None, 1, hb), lambda b, g, i, z=z: (z, 0, g))
        outs.append(pl.pallas_call(
            functools.partial(_rk_scan_kernel, reverse=rev),
            grid=(bsz, d // hb, nb),
            in_specs=[blk, blk, blk, blk2, blk2, blk2, blk2, vec, vec2, vec2],
            out_specs=blk,
            out_shape=sh_bf,
            scratch_shapes=[pltpu.VMEM((hb // (2 * RK_HEAD_DIM), 2 * RK_HEAD_DIM, 2 * RK_HEAD_DIM), F32)],
            compiler_params=_cparams("parallel", "parallel", "arbitrary"),
            name="rwkv_scan_rev" if rev else "rwkv_scan_fwd",
        )(r, v, kkn, lw, km, bz, gate, p['r_k'], p['lnx_g'], p['lnx_b']))
    return proj_residual_ln(outs, x, mod, p['w_o'], ln_g, ln_b)


def _prepare_params(w):
    d = D_MODEL
    bf = lambda a: a.astype(BF16)
    row = lambda a: a.reshape(1, d)
    lane = jnp.arange(d) // RK_HEAD_DIM
    p = {
        'w_ada': w['w_ada'], 'b_ada': w['b_ada'],
        'ln_g': w['ln_g'], 'ln_b': w['ln_b'],
        'conv_w_in': bf(w['conv_w_in']), 'conv_w': w['conv_w'], 'conv_w_out': bf(w['conv_w_out']),
        'na_w_qkv': bf(w['na_w_qkv']), 'na_w_o': bf(w['na_w_o']),
        'na_bias': [_na_bias_table(w['na_rpb'][j]) for j in range(w['na_rpb'].shape[0])],
        'ffn_w_gu': bf(w['ffn_w_gu']), 'ffn_w_down': bf(w['ffn_w_down']),
        'moe_router': [_pad_router(w['moe_w_router'][j], w['moe_b_router'][j])
                       for j in range(w['moe_w_router'].shape[0])],
        'moe_w_gu': bf(w['moe_w_gu']), 'moe_w_down': bf(w['moe_w_down']),
        'rk': [],
    }
    cat = lambda a: jnp.concatenate([a[0], a[1]], axis=1)
    for j in range(w['rk_mu'].shape[0]):
        p['rk'].append({
            'mu': w['rk_mu'][j], 'w_rkv': bf(w['rk_w_rkv'][j]),
            'w1': bf(cat(w['rk_w1'][j])), 'w2': bf(w['rk_w2'][j]),
            'a1': bf(cat(w['rk_a1'][j])), 'a2': bf(w['rk_a2'][j]),
            'g1': bf(cat(w['rk_g1'][j])), 'g2': bf(w['rk_g2'][j]),
            'w0': w['rk_w0'][j], 'a0': w['rk_a0'][j],
            'k_k': row(w['rk_k_k'][j]), 'k_a': row(w['rk_k_a'][j]),
            'r_k': w['rk_r_k'][j].reshape(1, d),
            'lnx_g': w['rk_lnx_g'][j].reshape(2, 1, d), 'lnx_b': w['rk_lnx_b'][j].reshape(2, 1, d),
            'w_o': bf(w['rk_w_o'][j]),
        })
    return p


def _trunk(x, c, p):
    d = D_MODEL
    mod_all = ada_modulation(c, p['w_ada'], p['b_ada'])
    for i in range(DEPTH):
        mod = mod_all[i]
        lng = lambda s: p['ln_g'][i, s].reshape(1, d)
        lnb = lambda s: p['ln_b'][i, s].reshape(1, d)
        kind, j = i % 3, i // 3
        if kind == 0:
            x = conv_mixer_layer(x, mod, p['conv_w_in'][j], p['conv_w'][j], p['conv_w_out'][j],
                                 lng(0), lnb(0))
        elif kind == 1:
            x = na_mixer_layer(x, mod, p['na_w_qkv'][j], p['na_bias'][j], p['na_w_o'][j], lng(0), lnb(0))
        else:
            x = rwkv_mixer_layer(x, mod, p['rk'][j], lng(0), lnb(0))
        if i % 2 == 0:
            x = dense_ffn_layer(x, mod, p['ffn_w_gu'][i // 2], p['ffn_w_down'][i // 2], lng(1), lnb(1))
        else:
            wr, br = p['moe_router'][i // 2]
            x = moe_ffn_layer(x, mod, wr, br, p['moe_w_gu'][i // 2], p['moe_w_down'][i // 2],
                              lng(1), lnb(1))
    return x


def kernel(x_prompt, x_sample, c_prompt, c_sample, w_ada, b_ada, ln_g, ln_b, conv_w_in, conv_w, conv_w_out, na_w_qkv, na_rpb, na_w_o, rk_mu, rk_w_rkv, rk_w0, rk_w1, rk_w2, rk_a0, rk_a1, rk_a2, rk_g1, rk_g2, rk_k_k, rk_k_a, rk_r_k, rk_lnx_g, rk_lnx_b, rk_w_o, ffn_w_gu, ffn_w_down, moe_w_router, moe_b_router, moe_w_gu, moe_w_down):
    p = _prepare_params(dict(
        w_ada=w_ada, b_ada=b_ada, ln_g=ln_g, ln_b=ln_b,
        conv_w_in=conv_w_in, conv_w=conv_w, conv_w_out=conv_w_out,
        na_w_qkv=na_w_qkv, na_rpb=na_rpb, na_w_o=na_w_o,
        rk_mu=rk_mu, rk_w_rkv=rk_w_rkv, rk_w0=rk_w0, rk_w1=rk_w1, rk_w2=rk_w2,
        rk_a0=rk_a0, rk_a1=rk_a1, rk_a2=rk_a2, rk_g1=rk_g1, rk_g2=rk_g2,
        rk_k_k=rk_k_k, rk_k_a=rk_k_a, rk_r_k=rk_r_k, rk_lnx_g=rk_lnx_g, rk_lnx_b=rk_lnx_b,
        rk_w_o=rk_w_o, ffn_w_gu=ffn_w_gu, ffn_w_down=ffn_w_down,
        moe_w_router=moe_w_router, moe_b_router=moe_b_router,
        moe_w_gu=moe_w_gu, moe_w_down=moe_w_down))
    return (_trunk(x_prompt, c_prompt, p), _trunk(x_sample, c_sample, p))
```

```python
import functools

import jax
import jax.numpy as jnp
from jax import lax
from jax.experimental import pallas as pl
from jax.experimental.pallas import tpu as pltpu

F32 = jnp.float32
BF16 = jnp.bfloat16

D_MODEL = 1024
DEPTH = 4
ALPHA = (2 * DEPTH) ** 0.25
LN_EPS = 1e-5

GRID_W = 64
NA_HEADS = 16
NA_HEAD_DIM = D_MODEL // NA_HEADS
NA_WIN_ROWS = 8
NA_WIN_COLS = 16
NEG_INF = -1e30

RK_HEAD_DIM = 64
RK_GN_EPS = 64e-5
RK_CHUNK = 64
RK_TOK_BLOCK = 128
RK_LANE_BLOCK = 1024

N_EXPERTS = 8
ROUTER_LANES = 128

VMEM_LIMIT = 52 * 1024 * 1024
MOE_VMEM_LIMIT = 58 * 1024 * 1024


def _cparams(*sem):
    return pltpu.CompilerParams(dimension_semantics=sem, vmem_limit_bytes=VMEM_LIMIT)


def _const_spec(shape):
    nd = len(shape)
    return pl.BlockSpec(shape, lambda *_: (0,) * nd, pipeline_mode=pl.Buffered(1))


def _dot(a, b):
    return jnp.dot(a, b, preferred_element_type=F32)


def _dot_nt(a, b):
    return lax.dot_general(a, b, (((1,), (1,)), ((), ())), preferred_element_type=F32)


def _dot_tn(a, b):
    return lax.dot_general(a, b, (((0,), (0,)), ((), ())), preferred_element_type=F32)


def _sigmoid(x):
    return 1.0 / (1.0 + jnp.exp(-x))


def _layer_norm(y, g, b):
    mu = jnp.mean(y, axis=-1, keepdims=True)
    d = y - mu
    var = jnp.mean(d * d, axis=-1, keepdims=True)
    return d * lax.rsqrt(var + LN_EPS) * g + b


def _ada_kernel(c_ref, w_ref, b_ref, o_ref):
    c = c_ref[...]
    s = (c * _sigmoid(c)).astype(BF16)
    o_ref[...] = _dot(s, w_ref[...].astype(BF16)) + b_ref[...]


def ada_modulation(c, w_ada, b_ada):
    bsz, d = c.shape
    depth = w_ada.shape[0]
    out = pl.pallas_call(
        _ada_kernel,
        grid=(depth, 6),
        in_specs=[
            pl.BlockSpec((bsz, d), lambda i, j: (0, 0)),
            pl.BlockSpec((None, d, d), lambda i, j: (i, 0, j)),
            pl.BlockSpec((None, None, 1, d), lambda i, j: (i, j, 0, 0)),
        ],
        out_specs=pl.BlockSpec((None, None, bsz, d), lambda i, j: (i, j, 0, 0)),
        out_shape=jax.ShapeDtypeStruct((depth, 6, bsz, d), F32),
        compiler_params=_cparams("arbitrary", "arbitrary"),
        name="ada_modulation",
    )(c, w_ada, b_ada.reshape(depth, 6, 1, d))
    return jnp.transpose(out, (0, 2, 1, 3))


def _tile_specs(seq, tt, d):
    nb8 = seq // 8
    per = tt // 8
    x_spec = pl.BlockSpec((None, tt, d), lambda b, t: (b, t, 0))
    prev_spec = pl.BlockSpec((None, 8, d), lambda b, t: (b, jnp.maximum(t * per - 1, 0), 0))
    next_spec = pl.BlockSpec((None, 8, d), lambda b, t: (b, jnp.minimum((t + 1) * per, nb8 - 1), 0))
    return x_spec, prev_spec, next_spec


def _mod_spec(d):
    return pl.BlockSpec((None, 6, d), lambda b, t: (b, 0, 0))


def _shifted(cur, prev_row, next_row, tt):
    row = lax.broadcasted_iota(jnp.int32, (tt, 1), 0)
    m1 = jnp.where(row == 0, prev_row, pltpu.roll(cur, 1, 0))
    p1 = jnp.where(row == tt - 1, next_row, pltpu.roll(cur, tt - 1, 0))
    return m1, p1


def _conv_kernel(x_ref, xp_ref, xn_ref, mod_ref, win_ref, cw_ref, wout_ref, lng_ref, lnb_ref,
                 o_ref, *, nt, tt, d):
    t = pl.program_id(1)
    sh, sc, gate = mod_ref[0:1, :], mod_ref[1:2, :], mod_ref[2:3, :]
    x = x_ref[...]
    h = (x * (1.0 + sc) + sh).astype(BF16)
    p = _dot(h, win_ref[...])
    z = p[:, d:2 * d] * p[:, 2 * d:]
    halo = jnp.concatenate([xp_ref[...], xn_ref[...]], axis=0)
    hh = (halo * (1.0 + sc) + sh).astype(BF16)
    ph = _dot(hh, win_ref[:, d:])
    zh = ph[:, :d] * ph[:, d:]
    z_prev = jnp.where(t > 0, zh[7:8, :], 0.0)
    z_next = jnp.where(t < nt - 1, zh[8:9, :], 0.0)
    z_m1, z_p1 = _shifted(z, z_prev, z_next, tt)
    conv = z_m1 * cw_ref[0:1, :] + z * cw_ref[1:2, :] + z_p1 * cw_ref[2:3, :]
    y = (p[:, :d] * conv).astype(BF16)
    mix = _dot(y, wout_ref[...])
    o_ref[...] = _layer_norm(ALPHA * x + gate * mix, lng_ref[...], lnb_ref[...])


def conv_mixer_layer(x, mod, w_in, conv_w, w_out, ln_g, ln_b, tt=512):
    bsz, seq, d = x.shape
    tt = min(tt, seq)
    nt = seq // tt
    x_spec, prev_spec, next_spec = _tile_specs(seq, tt, d)
    return pl.pallas_call(
        functools.partial(_conv_kernel, nt=nt, tt=tt, d=d),
        grid=(bsz, nt),
        in_specs=[x_spec, prev_spec, next_spec, _mod_spec(d),
                  _const_spec((d, 3 * d)), _const_spec((3, d)), _const_spec((d, d)),
                  _const_spec((1, d)), _const_spec((1, d))],
        out_specs=pl.BlockSpec((None, tt, d), lambda b, t: (b, t, 0)),
        out_shape=jax.ShapeDtypeStruct(x.shape, F32),
        compiler_params=_cparams("parallel", "arbitrary"),
        name="conv_mixer",
    )(x, x, x, mod, w_in, conv_w, w_out, ln_g, ln_b)


def _ffn_kernel(x_ref, mod_ref, wgu_ref, wd_ref, lng_ref, lnb_ref, o_ref, *, ff, fc):
    sh, sc, gate = mod_ref[3:4, :], mod_ref[4:5, :], mod_ref[5:6, :]
    x = x_ref[...]
    h = (x * (1.0 + sc) + sh).astype(BF16)
    acc = jnp.zeros(x.shape, F32)
    for c in range(ff // fc):
        g = _dot(h, wgu_ref[:, c * fc:(c + 1) * fc])
        u = _dot(h, wgu_ref[:, ff + c * fc:ff + (c + 1) * fc])
        act = (g * _sigmoid(g) * u).astype(BF16)
        acc = acc + _dot(act, wd_ref[c * fc:(c + 1) * fc, :])
    o_ref[...] = _layer_norm(ALPHA * x + gate * acc, lng_ref[...], lnb_ref[...])


def dense_ffn_layer(x, mod, w_gu, w_down, ln_g, ln_b, tt=512):
    bsz, seq, d = x.shape
    tt = min(tt, seq)
    ff = w_down.shape[0]
    fc = ff // 2
    return pl.pallas_call(
        functools.partial(_ffn_kernel, ff=ff, fc=fc),
        grid=(bsz, seq // tt),
        in_specs=[pl.BlockSpec((None, tt, d), lambda b, t: (b, t, 0)), _mod_spec(d),
                  _const_spec((d, 2 * ff)), _const_spec((ff, d)),
                  _const_spec((1, d)), _const_spec((1, d))],
        out_specs=pl.BlockSpec((None, tt, d), lambda b, t: (b, t, 0)),
        out_shape=jax.ShapeDtypeStruct(x.shape, F32),
        compiler_params=_cparams("parallel", "arbitrary"),
        name="dense_ffn",
    )(x, mod, w_gu, w_down, ln_g, ln_b)


MOE_ROW_CHUNK = 256
MOE_SPARSE_ROWS = (256, 288)


def _moe_kernel(x_ref, mod_ref, wr_ref, br_ref, wgu_ref, wd_ref, lng_ref, lnb_ref, o_ref,
                h_sc, comb_sc, acc_sc, rankc_sc, rankr_sc, maskr_sc, *, fe, caps):
    e = pl.program_id(2)
    tt = comb_sc.shape[0]
    lane = lax.broadcasted_iota(jnp.int32, comb_sc.shape, 1)

    @pl.when(e == 0)
    def _():
        sh, sc = mod_ref[3:4, :], mod_ref[4:5, :]
        h = x_ref[...] * (1.0 + sc) + sh
        h_hi = h.astype(BF16)
        h_sc[...] = h_hi
        h_lo = (h - h_hi.astype(F32)).astype(BF16)
        wr = wr_ref[...]
        w_hi = wr.astype(BF16)
        w_lo = (wr - w_hi.astype(F32)).astype(BF16)
        logits = _dot(h_hi, w_hi) + _dot(h_lo, w_hi) + _dot(h_hi, w_lo) + br_ref[...]
        m = jnp.max(logits, axis=-1, keepdims=True)
        ex = jnp.exp(logits - m)
        probs = ex / jnp.sum(ex, axis=-1, keepdims=True)
        valid = lane < N_EXPERTS
        p = jnp.where(valid, probs, -1.0)
        p1 = jnp.max(p, axis=-1, keepdims=True)
        i1 = jnp.min(jnp.where(p == p1, lane, ROUTER_LANES), axis=-1, keepdims=True)
        pr = jnp.where(lane == i1, -1.0, p)
        p2 = jnp.max(pr, axis=-1, keepdims=True)
        i2 = jnp.min(jnp.where(pr == p2, lane, ROUTER_LANES), axis=-1, keepdims=True)
        tot = p1 + p2
        comb = jnp.where(lane == i1, p1 / tot, 0.0) + jnp.where(lane == i2, p2 / tot, 0.0)
        comb_sc[...] = comb
        acc_sc[...] = jnp.zeros(acc_sc.shape, F32)
        routed = jnp.where(comb > 0.0, 1.0, 0.0).astype(BF16)
        q0 = lax.broadcasted_iota(jnp.int32, (ROUTER_LANES, ROUTER_LANES), 0)
        q1 = lax.broadcasted_iota(jnp.int32, (ROUTER_LANES, ROUTER_LANES), 1)
        routed_t = _dot_nt(jnp.where(q0 == q1, 1.0, 0.0).astype(BF16), routed)
        maskr_sc[...] = routed_t
        routed_tb = routed_t.astype(BF16)
        rc = min(MOE_ROW_CHUNK, tt)
        for c in range(tt // rc):
            r_i = lax.broadcasted_iota(jnp.int32, (rc, tt), 0) + c * rc
            c_i = lax.broadcasted_iota(jnp.int32, (rc, tt), 1)
            rankc_sc[c * rc:(c + 1) * rc, :] = _dot(jnp.where(c_i < r_i, 1.0, 0.0).astype(BF16), routed)
            r_j = lax.broadcasted_iota(jnp.int32, (tt, rc), 0)
            c_j = lax.broadcasted_iota(jnp.int32, (tt, rc), 1) + c * rc
            rankr_sc[:, c * rc:(c + 1) * rc] = _dot(routed_tb, jnp.where(r_j < c_j, 1.0, 0.0).astype(BF16))

    def expert(hb):
        g = _dot(hb, wgu_ref[:, :fe])
        u = _dot(hb, wgu_ref[:, fe:])
        act = (g * _sigmoid(g) * u).astype(BF16)
        return _dot(act, wd_ref[...])

    w_col = jnp.sum(jnp.where(lane == e, comb_sc[...], 0.0), axis=-1, keepdims=True)
    rk_col = jnp.sum(jnp.where(lane == e, rankc_sc[...], 0.0), axis=-1, keepdims=True).astype(jnp.int32)
    rk_row = rankr_sc[pl.ds(e, 1), :].astype(jnp.int32)
    mk_row = maskr_sc[pl.ds(e, 1), :]
    n_routed = jnp.sum(mk_row).astype(jnp.int32)
    half = tt // 2

    def sparse(cap):
        sub = lax.broadcasted_iota(jnp.int32, (cap, tt), 0)
        pick = jnp.where((rk_row == sub) & (mk_row > 0.0), 1.0, 0.0).astype(BF16)
        y = expert(_dot(pick, h_sc[...]).astype(BF16)).astype(BF16)
        for r0 in (0, half):
            rows = slice(r0, r0 + half)
            ln_i = lax.broadcasted_iota(jnp.int32, (half, cap), 1)
            put = jnp.where((rk_col[rows] == ln_i) & (w_col[rows] > 0.0), 1.0, 0.0).astype(BF16)
            acc_sc[rows, :] += w_col[rows] * _dot(put, y)

    def dense():
        for r0 in (0, half):
            rows = slice(r0, r0 + half)
            acc_sc[rows, :] += w_col[rows] * expert(h_sc[rows, :])

    lo = 0
    for cap in caps:
        pl.when((n_routed > lo) & (n_routed <= cap))(functools.partial(sparse, cap))
        lo = cap
    pl.when(n_routed > lo)(dense)

    @pl.when(e == N_EXPERTS - 1)
    def _():
        gate = mod_ref[5:6, :]
        o_ref[...] = _layer_norm(ALPHA * x_ref[...] + gate * acc_sc[...], lng_ref[...], lnb_ref[...])


def moe_ffn_layer(x, mod, w_router, b_router, w_gu, w_down, ln_g, ln_b, tt=1024, caps=MOE_SPARSE_ROWS):
    bsz, seq, d = x.shape
    tt = min(tt, seq)
    n_e, fe = w_down.shape[0], w_down.shape[1]
    return pl.pallas_call(
        functools.partial(_moe_kernel, fe=fe, caps=caps),
        grid=(bsz, seq // tt, n_e),
        in_specs=[pl.BlockSpec((None, tt, d), lambda b, t, e: (b, t, 0), pipeline_mode=pl.Buffered(1)),
                  pl.BlockSpec((None, 6, d), lambda b, t, e: (b, 0, 0)),
                  _const_spec((d, ROUTER_LANES)), _const_spec((1, ROUTER_LANES)),
                  pl.BlockSpec((None, d, 2 * fe), lambda b, t, e: (e, 0, 0)),
                  pl.BlockSpec((None, fe, d), lambda b, t, e: (e, 0, 0)),
                  _const_spec((1, d)), _const_spec((1, d))],
        out_specs=pl.BlockSpec((None, tt, d), lambda b, t, e: (b, t, 0)),
        out_shape=jax.ShapeDtypeStruct(x.shape, F32),
        scratch_shapes=[pltpu.VMEM((tt, d), BF16), pltpu.VMEM((tt, ROUTER_LANES), F32),
                        pltpu.VMEM((tt, d), F32), pltpu.VMEM((tt, ROUTER_LANES), F32),
                        pltpu.VMEM((ROUTER_LANES, tt), F32), pltpu.VMEM((ROUTER_LANES, tt), F32)],
        compiler_params=pltpu.CompilerParams(
            dimension_semantics=("parallel", "arbitrary", "arbitrary"), vmem_limit_bytes=MOE_VMEM_LIMIT),
        name="moe_ffn",
    )(x, mod, w_router, b_router, w_gu, w_down, ln_g, ln_b)


def _pad_router(w_router, b_router):
    d, n_e = w_router.shape
    w = jnp.zeros((d, ROUTER_LANES), F32).at[:, :n_e].set(w_router)
    b = jnp.full((1, ROUTER_LANES), NEG_INF, F32).at[0, :n_e].set(b_router)
    return w, b


def _qkv_kernel(x_ref, mod_ref, w_ref, q_ref, k_ref, v_ref, *, d):
    sh, sc = mod_ref[0:1, :], mod_ref[1:2, :]
    h = (x_ref[...] * (1.0 + sc) + sh).astype(BF16)
    p = _dot(h, w_ref[...])
    q_ref[...] = (p[:, :d] * (NA_HEAD_DIM ** -0.5)).astype(BF16)
    k_ref[...] = p[:, d:2 * d].astype(BF16)
    v_ref[...] = p[:, 2 * d:].astype(BF16)


def _na_kernel(q_ref, k_ref, v_ref, bias_ref, o_ref, *, rows):
    r = pl.program_id(1)
    r0 = jnp.clip(r - NA_WIN_ROWS // 2, 0, rows - NA_WIN_ROWS)
    start = pl.multiple_of(r0 * GRID_W, GRID_W)
    nk = NA_WIN_ROWS * GRID_W
    pw_ = 2 * NA_HEAD_DIM
    pairs = range(NA_HEADS // 2)
    lanes = [slice(p * pw_, (p + 1) * pw_) for p in pairs]
    head0 = lax.broadcasted_iota(jnp.int32, (GRID_W, pw_), 1) < NA_HEAD_DIM
    q2 = {}
    for p in pairs:
        q = q_ref[:, lanes[p]]
        z = jnp.zeros_like(q)
        q2[p] = jnp.concatenate([jnp.where(head0, q, z), jnp.where(head0, z, q)], axis=0)
    s = {p: _dot_nt(q2[p], k_ref[pl.ds(start, nk), lanes[p]])
         + bias_ref[2 * p:2 * p + 2].reshape(2 * GRID_W, nk) for p in pairs}
    m = {p: jnp.max(s[p], axis=-1, keepdims=True) for p in pairs}
    e = {p: jnp.exp(s[p] - m[p]) for p in pairs}
    l = {p: jnp.sum(e[p], axis=-1, keepdims=True) for p in pairs}
    o = {p: _dot(e[p].astype(BF16), v_ref[pl.ds(start, nk), lanes[p]]) / l[p] for p in pairs}
    for p in pairs:
        o_ref[:, lanes[p]] = jnp.where(head0, o[p][:GRID_W], o[p][GRID_W:]).astype(BF16)


def _na_bias_table(rpb):
    nh, nr, nc = rpb.shape
    w = GRID_W
    qc = jnp.arange(w)[:, None]
    kc = jnp.arange(w)[None, :]
    cs = jnp.clip(qc - NA_WIN_COLS // 2, 0, w - NA_WIN_COLS)
    col_ok = (kc >= cs) & (kc < cs + NA_WIN_COLS)
    lo = w - NA_WIN_COLS
    e = jnp.pad(rpb, ((0, 0), (0, 0), (lo, 2 * w - lo - nc)))
    flat = jnp.tile(e, (1, 1, w))
    toep = flat[:, :, w - 1:w - 1 + w * (2 * w - 1)].reshape(nh, nr, w, 2 * w - 1)[..., :w]
    toep = jnp.where(col_ok[None, None], toep, NEG_INF)
    tab = jnp.stack([toep[:, dl:dl + NA_WIN_ROWS] for dl in range(NA_WIN_ROWS)], axis=1)
    tab = jnp.transpose(tab, (0, 1, 3, 2, 4))
    return tab.reshape(nh, NA_WIN_ROWS, w, NA_WIN_ROWS * w)


def _proj_ln_kernel(*refs, n_in):
    a_refs = refs[:n_in]
    x_ref, mod_ref, w_ref, lng_ref, lnb_ref, o_ref = refs[n_in:]
    a = a_refs[0][...]
    if n_in == 2:
        a = (a.astype(F32) + a_refs[1][...].astype(F32)).astype(BF16)
    mix = _dot(a, w_ref[...])
    gate = mod_ref[2:3, :]
    o_ref[...] = _layer_norm(ALPHA * x_ref[...] + gate * mix, lng_ref[...], lnb_ref[...])


def proj_residual_ln(acts, x, mod, w_o, ln_g, ln_b, tt=512):
    bsz, seq, d = x.shape
    tt = min(tt, seq)
    tile = pl.BlockSpec((None, tt, d), lambda b, t: (b, t, 0))
    return pl.pallas_call(
        functools.partial(_proj_ln_kernel, n_in=len(acts)),
        grid=(bsz, seq // tt),
        in_specs=[tile] * len(acts) + [tile, _mod_spec(d), _const_spec((d, d)),
                                       _const_spec((1, d)), _const_spec((1, d))],
        out_specs=tile,
        out_shape=jax.ShapeDtypeStruct(x.shape, F32),
        compiler_params=_cparams("parallel", "arbitrary"),
        name="proj_residual_ln",
    )(*acts, x, mod, w_o, ln_g, ln_b)


def na_mixer_layer(x, mod, w_qkv, bias_tab, w_o, ln_g, ln_b, tt=512):
    bsz, seq, d = x.shape
    tt = min(tt, seq)
    rows = seq // GRID_W
    assert rows >= NA_WIN_ROWS and seq % GRID_W == 0
    tile = pl.BlockSpec((None, tt, d), lambda b, t: (b, t, 0))
    q, k, v = pl.pallas_call(
        functools.partial(_qkv_kernel, d=d),
        grid=(bsz, seq // tt),
        in_specs=[tile, _mod_spec(d), _const_spec((d, 3 * d))],
        out_specs=[tile] * 3,
        out_shape=[jax.ShapeDtypeStruct(x.shape, BF16)] * 3,
        compiler_params=_cparams("parallel", "arbitrary"),
        name="na_qkv",
    )(x, mod, w_qkv)

    def delta(r):
        return jnp.clip(r - NA_WIN_ROWS // 2, 0, rows - NA_WIN_ROWS) - r + NA_WIN_ROWS - 1

    row_spec = pl.BlockSpec((None, GRID_W, d), lambda b, r: (b, r, 0))
    seq_spec = pl.BlockSpec((None, seq, d), lambda b, r: (b, 0, 0))
    att = pl.pallas_call(
        functools.partial(_na_kernel, rows=rows),
        grid=(bsz, rows),
        in_specs=[row_spec, seq_spec, seq_spec,
                  pl.BlockSpec((NA_HEADS, None, GRID_W, NA_WIN_ROWS * GRID_W),
                               lambda b, r: (0, delta(r), 0, 0))],
        out_specs=row_spec,
        out_shape=jax.ShapeDtypeStruct(x.shape, BF16),
        compiler_params=_cparams("parallel", "arbitrary"),
        name="na_attention",
    )(q, k, v, bias_tab)
    return proj_residual_ln([att], x, mod, w_o, ln_g, ln_b, tt)


def _softplus(y):
    return jnp.maximum(y, 0.0) + jnp.log(1.0 + jnp.exp(-jnp.abs(y)))


def _rk_proj_kernel(x_ref, xp_ref, xn_ref, mod_ref, mu_ref, wrkv_ref, w1_ref, w2_ref, a1_ref, a2_ref,
                    g1_ref, g2_ref, w0_ref, a0_ref, kk_ref, ka_ref,
                    r_o, v_o, kk_o, lw_o, km_o, b_o, gate_o, *, nt, tt):
    t = pl.program_id(1)
    sh, sc = mod_ref[0:1, :], mod_ref[1:2, :]
    h = x_ref[...] * (1.0 + sc) + sh
    h_prev = jnp.where(t > 0, xp_ref[7:8, :] * (1.0 + sc) + sh, 0.0)
    h_next = jnp.where(t < nt - 1, xn_ref[0:1, :] * (1.0 + sc) + sh, 0.0)
    h_m1, h_p1 = _shifted(h, h_prev, h_next, tt)
    xx = 0.5 * (h_m1 + h_p1) - h

    def mixed(p):
        return (h + xx * mu_ref[p:p + 1, :]).astype(BF16)

    r = _dot(mixed(0), wrkv_ref[0])
    k = _dot(mixed(1), wrkv_ref[1])
    v = _dot(mixed(2), wrkv_ref[2])
    r_o[...] = r.astype(BF16)
    v_o[...] = v.astype(BF16)
    lw_in = jnp.tanh(_dot(mixed(3), w1_ref[...])).astype(BF16)
    a_in = _dot(mixed(4), a1_ref[...]).astype(BF16)
    g_in = _sigmoid(_dot(mixed(5), g1_ref[...])).astype(BF16)

    kk = k * kk_ref[...]
    kk_o[...] = kk.astype(BF16)
    rw = w1_ref.shape[1] // 2
    rg = g1_ref.shape[1] // 2
    for z in range(2):
        wl = w0_ref[z:z + 1, :] + _dot(lw_in[:, z * rw:(z + 1) * rw], w2_ref[z])
        w_log = -_softplus(-wl) - 0.5
        lw_o[z] = -jnp.exp(w_log)
        a_lr = _sigmoid(a0_ref[z:z + 1, :] + _dot(a_in[:, z * rw:(z + 1) * rw], a2_ref[z]))
        km_o[z] = (k * (1.0 + (a_lr - 1.0) * ka_ref[...])).astype(BF16)
        b_o[z] = (kk * a_lr).astype(BF16)
        gate_o[z] = _dot(g_in[:, z * rg:(z + 1) * rg], g2_ref[z]).astype(BF16)


def _head_sum(x):
    head0 = lax.broadcasted_iota(jnp.int32, x.shape, 1) < RK_HEAD_DIM
    s0 = jnp.sum(jnp.where(head0, x, 0.0), axis=-1, keepdims=True)
    s1 = jnp.sum(jnp.where(head0, 0.0, x), axis=-1, keepdims=True)
    return jnp.where(head0, s0, s1)


def _rk_scan_kernel(r_ref, v_ref, kk_ref, lw_ref, km_ref, b_ref, gate_ref, rk_ref, lg_ref, lb_ref,
                    o_ref, s_ref, *, reverse):
    tb, hb = r_ref.shape
    cl = RK_CHUNK
    nc, nh = tb // cl, hb // RK_HEAD_DIM

    @pl.when(pl.program_id(2) == 0)
    def _():
        s_ref[...] = jnp.zeros(s_ref.shape, F32)

    ri = lax.broadcasted_iota(jnp.int32, (tb, tb), 0)
    ci = lax.broadcasted_iota(jnp.int32, (tb, tb), 1)
    ordered = (ci >= ri) if reverse else (ci <= ri)
    tri = jnp.where((ri // cl == ci // cl) & ordered, 1.0, 0.0).astype(BF16)
    lw = lw_ref[...]
    lw_hi = lw.astype(BF16)
    lw_lo = (lw - lw_hi.astype(F32)).astype(BF16)
    cum = _dot(tri, lw_hi) + _dot(tri, lw_lo)
    last = 0 if reverse else cl - 1
    tot_rows = [cum[c * cl + last:c * cl + last + 1, :] for c in range(nc)]
    tot = jnp.concatenate([jnp.broadcast_to(tr, (cl, hb)) for tr in tot_rows], axis=0)

    rr = r_ref[...].astype(F32)
    vv = v_ref[...]
    kk = kk_ref[...].astype(F32)
    pair = 2 * RK_HEAD_DIM
    rnorm = jnp.concatenate(
        [1.0 / jnp.maximum(jnp.sqrt(_head_sum(jnp.square(kk[:, i:i + pair]))), 1e-12) for i in range(0, hb, pair)],
        axis=1)
    kkn = kk * rnorm
    km = km_ref[...].astype(F32)
    bb = b_ref[...].astype(F32) * rnorm
    e_neg = jnp.exp(-cum)
    e_rem = jnp.exp(tot - cum)
    a_t = (-kkn * jnp.exp(cum - lw)).astype(BF16)
    r_t = (rr * jnp.exp(cum)).astype(BF16)
    b_t = (bb * e_neg).astype(BF16)
    k_t = (km * e_neg).astype(BF16)
    b_h = (bb * e_rem).astype(BF16)
    k_h = (km * e_rem).astype(BF16)
    bonus_w = rr * km * rk_ref[...]

    pw_ = 2 * RK_HEAD_DIM
    npair = hb // pw_
    t_i = lax.broadcasted_iota(jnp.int32, (cl, pw_), 0)
    s_i = lax.broadcasted_iota(jnp.int32, (cl, pw_), 1) % cl
    strict = (t_i < s_i) if reverse else (t_i > s_i)
    incl = (t_i <= s_i) if reverse else (t_i >= s_i)
    eye = jnp.where(t_i == s_i, 1.0, 0.0)
    head0 = lax.broadcasted_iota(jnp.int32, (cl, pw_), 1) < RK_HEAD_DIM
    q0 = lax.broadcasted_iota(jnp.int32, (pw_, pw_), 0)
    q1 = lax.broadcasted_iota(jnp.int32, (pw_, pw_), 1)
    same_head = (q0 // RK_HEAD_DIM) == (q1 // RK_HEAD_DIM)

    head_sum = _head_sum

    def bd(x):
        z = jnp.zeros_like(x)
        return jnp.concatenate([jnp.where(head0, x, z), jnp.where(head0, z, x)], axis=0)

    keys = [(c, p) for c in range(nc) for p in range(npair)]
    rsl = {k: slice(k[0] * cl, (k[0] + 1) * cl) for k in keys}
    lsl = {k: slice(k[1] * pw_, (k[1] + 1) * pw_) for k in keys}
    blk = lambda arr, k: arr[rsl[k], lsl[k]]

    gram = {k: _dot_nt(jnp.concatenate([blk(a_t, k), blk(r_t, k)], axis=0),
                       jnp.concatenate([bd(blk(b_t, k)), bd(blk(k_t, k))], axis=0)) for k in keys}
    m_ab = {k: jnp.where(strict, gram[k][:cl, :pw_], 0.0) for k in keys}
    m_ak = {k: jnp.where(strict, gram[k][:cl, pw_:], 0.0).astype(BF16) for k in keys}
    m_rb = {k: jnp.where(incl, gram[k][cl:, :pw_], 0.0).astype(BF16) for k in keys}
    m_rk = {k: jnp.where(incl, gram[k][cl:, pw_:], 0.0).astype(BF16) for k in keys}
    vbd = {k: bd(blk(vv, k)) for k in keys}
    my = {k: _dot(jnp.concatenate([m_ak[k], m_rk[k]], axis=0), vbd[k]) for k in keys}
    v_t = {k: jnp.transpose(blk(vv, k).astype(F32)).astype(BF16) for k in keys}
    vk = {k: jnp.where(same_head, _dot(v_t[k], blk(k_h, k)), 0.0) for k in keys}
    inv = {k: eye + m_ab[k] for k in keys}
    pwb = {k: m_ab[k].astype(BF16) for k in keys}
    pw = {k: _dot(pwb[k], bd(pwb[k])) for k in keys}
    n = 4
    while n < cl:
        pwb = {k: pw[k].astype(BF16) for k in keys}
        res = {k: _dot(pwb[k], jnp.concatenate([bd(pwb[k]), bd(inv[k].astype(BF16))], axis=1)) for k in keys}
        pw = {k: res[k][:, :pw_] for k in keys}
        inv = {k: inv[k] + res[k][:, pw_:] for k in keys}
        n *= 2
    inv = {k: (inv[k] + _dot(pw[k].astype(BF16), bd(inv[k].astype(BF16)))).astype(BF16) for k in keys}
    au = {k: _dot(inv[k], jnp.concatenate([bd(blk(a_t, k)), bd(my[k][:cl].astype(BF16))], axis=1))
          for k in keys}
    a_hat = {k: au[k][:, :pw_].astype(BF16) for k in keys}
    u0_t = {k: jnp.transpose(au[k][:, pw_:]) for k in keys}
    ar = {k: jnp.concatenate([a_hat[k], blk(r_t, k)], axis=0) for k in keys}

    pairs = range(npair)
    for c in (range(nc - 1, -1, -1) if reverse else range(nc)):
        st = {p: s_ref[p] for p in pairs}
        stb = {p: st[p].astype(BF16) for p in pairs}
        res = {p: _dot_nt(ar[c, p], stb[p]) for p in pairs}
        res_t = {p: _dot_nt(stb[p], a_hat[c, p]) for p in pairs}
        ub = {p: (au[c, p][:, pw_:] + res[p][:cl]).astype(BF16) for p in pairs}
        ub_t = {p: (u0_t[c, p] + res_t[p]).astype(BF16) for p in pairs}
        y = {p: my[c, p][cl:] + res[p][cl:] + _dot(m_rb[c, p], bd(ub[p])) for p in pairs}
        for p in pairs:
            k = (c, p)
            decay = jnp.exp(tot_rows[c][:, lsl[k]])
            s_ref[p] = jnp.where(same_head, st[p] * decay + _dot(ub_t[p], blk(b_h, k)), 0.0) + vk[k]
        dy = {p: y[p] - head_sum(y[p]) * (1.0 / RK_HEAD_DIM) for p in pairs}
        var = {p: head_sum(dy[p] * dy[p]) * (1.0 / RK_HEAD_DIM) for p in pairs}
        for p in pairs:
            k = (c, p)
            yn = dy[p] * lax.rsqrt(var[p] + RK_GN_EPS) * lg_ref[:, lsl[k]] + lb_ref[:, lsl[k]]
            bonus = head_sum(blk(bonus_w, k)) * blk(vv, k).astype(F32)
            o_ref[rsl[k], lsl[k]] = (blk(gate_ref, k).astype(F32) * (yn + bonus)).astype(BF16)


def rwkv_mixer_layer(x, mod, p, ln_g, ln_b, tt=256):
    bsz, seq, d = x.shape
    tt = min(tt, seq)
    nt = seq // tt
    x_spec, prev_spec, next_spec = _tile_specs(seq, tt, d)
    tile = pl.BlockSpec((None, tt, d), lambda b, t: (b, t, 0))
    tile2 = pl.BlockSpec((2, None, tt, d), lambda b, t: (0, b, t, 0))
    rw, rg = p['w1'].shape[1] // 2, p['g1'].shape[1] // 2
    sh_bf = jax.ShapeDtypeStruct(x.shape, BF16)
    sh2_bf = jax.ShapeDtypeStruct((2,) + x.shape, BF16)
    sh2_f32 = jax.ShapeDtypeStruct((2,) + x.shape, F32)
    r, v, kkn, lw, km, bz, gate = pl.pallas_call(
        functools.partial(_rk_proj_kernel, nt=nt, tt=tt),
        grid=(bsz, nt),
        in_specs=[x_spec, prev_spec, next_spec, _mod_spec(d), _const_spec((6, d)),
                  _const_spec((3, d, d)),
                  _const_spec((d, 2 * rw)), _const_spec((2, rw, d)),
                  _const_spec((d, 2 * rw)), _const_spec((2, rw, d)),
                  _const_spec((d, 2 * rg)), _const_spec((2, rg, d)),
                  _const_spec((2, d)), _const_spec((2, d)), _const_spec((1, d)), _const_spec((1, d))],
        out_specs=[tile, tile, tile, tile2, tile2, tile2, tile2],
        out_shape=[sh_bf, sh_bf, sh_bf, sh2_f32, sh2_bf, sh2_bf, sh2_bf],
        compiler_params=_cparams("parallel", "arbitrary"),
        name="rwkv_proj",
    )(x, x, x, mod, p['mu'], p['w_rkv'], p['w1'], p['w2'], p['a1'], p['a2'], p['g1'], p['g2'],
      p['w0'], p['a0'], p['k_k'], p['k_a'])

    tb = min(RK_TOK_BLOCK, seq)
    hb = RK_LANE_BLOCK
    nb = seq // tb
    outs = []
    for z in range(2):
        rev = z == 1

        def tok(i, rev=rev):
            return nb - 1 - i if rev else i

        blk = pl.BlockSpec((None, tb, hb), lambda b, g, i: (b, tok(i), g))
        blk2 = pl.BlockSpec((None, None, tb, hb), lambda b, g, i, z=z: (z, b, tok(i), g))
        vec = pl.BlockSpec((1, hb), lambda b, g, i: (0, g))
        vec2 = pl.BlockSpec((None, 1, hb), lambda b, g, i, z=z: (z, 0, g))
        outs.append(pl.pallas_call(
            functools.partial(_rk_scan_kernel, reverse=rev),
            grid=(bsz, d // hb, nb),
            in_specs=[blk, blk, blk, blk2, blk2, blk2, blk2, vec, vec2, vec2],
            out_specs=blk,
            out_shape=sh_bf,
            scratch_shapes=[pltpu.VMEM((hb // (2 * RK_HEAD_DIM), 2 * RK_HEAD_DIM, 2 * RK_HEAD_DIM), F32)],
            compiler_params=_cparams("parallel", "parallel", "arbitrary"),
            name="rwkv_scan_rev" if rev else "rwkv_scan_fwd",
        )(r, v, kkn, lw, km, bz, gate, p['r_k'], p['lnx_g'], p['lnx_b']))
    return proj_residual_ln(outs, x, mod, p['w_o'], ln_g, ln_b)


def _prepare_params(w):
    d = D_MODEL
    bf = lambda a: a.astype(BF16)
    row = lambda a: a.reshape(1, d)
    lane = jnp.arange(d) // RK_HEAD_DIM
    p = {
        'w_ada': w['w_ada'], 'b_ada': w['b_ada'],
        'ln_g': w['ln_g'], 'ln_b': w['ln_b'],
        'conv_w_in': bf(w['conv_w_in']), 'conv_w': w['conv_w'], 'conv_w_out': bf(w['conv_w_out']),
        'na_w_qkv': bf(w['na_w_qkv']), 'na_w_o': bf(w['na_w_o']),
        'na_bias': [_na_bias_table(w['na_rpb'][j]) for j in range(w['na_rpb'].shape[0])],
        'ffn_w_gu': bf(w['ffn_w_gu']), 'ffn_w_down': bf(w['ffn_w_down']),
        'moe_router': [_pad_router(w['moe_w_router'][j], w['moe_b_router'][j])
                       for j in range(w['moe_w_router'].shape[0])],
        'moe_w_gu': bf(w['moe_w_gu']), 'moe_w_down': bf(w['moe_w_down']),
        'rk': [],
    }
    cat = lambda a: jnp.concatenate([a[0], a[1]], axis=1)
    for j in range(w['rk_mu'].shape[0]):
        p['rk'].append({
            'mu': w['rk_mu'][j], 'w_rkv': bf(w['rk_w_rkv'][j]),
            'w1': bf(cat(w['rk_w1'][j])), 'w2': bf(w['rk_w2'][j]),
            'a1': bf(cat(w['rk_a1'][j])), 'a2': bf(w['rk_a2'][j]),
            'g1': bf(cat(w['rk_g1'][j])), 'g2': bf(w['rk_g2'][j]),
            'w0': w['rk_w0'][j], 'a0': w['rk_a0'][j],
            'k_k': row(w['rk_k_k'][j]), 'k_a': row(w['rk_k_a'][j]),
            'r_k': w['rk_r_k'][j].reshape(1, d),
            'lnx_g': w['rk_lnx_g'][j].reshape(2, 1, d), 'lnx_b': w['rk_lnx_b'][j].reshape(2, 1, d),
            'w_o': bf(w['rk_w_o'][j]),
        })
    return p


def _trunk(x, c, p):
    d = D_MODEL
    mod_all = ada_modulation(c, p['w_ada'], p['b_ada'])
    for i in range(DEPTH):
        mod = mod_all[i]
        lng = lambda s: p['ln_g'][i, s].reshape(1, d)
        lnb = lambda s: p['ln_b'][i, s].reshape(1, d)
        kind, j = i % 3, i // 3
        if kind == 0:
            x = conv_mixer_layer(x, mod, p['conv_w_in'][j], p['conv_w'][j], p['conv_w_out'][j],
                                 lng(0), lnb(0))
        elif kind == 1:
            x = na_mixer_layer(x, mod, p['na_w_qkv'][j], p['na_bias'][j], p['na_w_o'][j], lng(0), lnb(0))
        else:
            x = rwkv_mixer_layer(x, mod, p['rk'][j], lng(0), lnb(0))
        if i % 2 == 0:
            x = dense_ffn_layer(x, mod, p['ffn_w_gu'][i // 2], p['ffn_w_down'][i // 2], lng(1), lnb(1))
        else:
            wr, br = p['moe_router'][i // 2]
            x = moe_ffn_layer(x, mod, wr, br, p['moe_w_gu'][i // 2], p['moe_w_down'][i // 2],
                              lng(1), lnb(1))
    return x


def kernel(x_prompt, x_sample, c_prompt, c_sample, w_ada, b_ada, ln_g, ln_b, conv_w_in, conv_w, conv_w_out, na_w_qkv, na_rpb, na_w_o, rk_mu, rk_w_rkv, rk_w0, rk_w1, rk_w2, rk_a0, rk_a1, rk_a2, rk_g1, rk_g2, rk_k_k, rk_k_a, rk_r_k, rk_lnx_g, rk_lnx_b, rk_w_o, ffn_w_gu, ffn_w_down, moe_w_router, moe_b_router, moe_w_gu, moe_w_down):
    p = _prepare_params(dict(
        w_ada=w_ada, b_ada=b_ada, ln_g=ln_g, ln_b=ln_b,
        conv_w_in=conv_w_in, conv_w=conv_w, conv_w_out=conv_w_out,
        na_w_qkv=na_w_qkv, na_rpb=na_rpb, na_w_o=na_w_o,
        rk_mu=rk_mu, rk_w_rkv=rk_w_rkv, rk_w0=rk_w0, rk_w1=rk_w1, rk_w2=rk_w2,
        rk_a0=rk_a0, rk_a1=rk_a1, rk_a2=rk_a2, rk_g1=rk_g1, rk_g2=rk_g2,
        rk_k_k=rk_k_k, rk_k_a=rk_k_a, rk_r_k=rk_r_k, rk_lnx_g=rk_lnx_g, rk_lnx_b=rk_lnx_b,
        rk_w_o=rk_w_o, ffn_w_gu=ffn_w_gu, ffn_w_down=ffn_w_down,
        moe_w_router=moe_w_router, moe_b_router=moe_b_router,
        moe_w_gu=moe_w_gu, moe_w_down=moe_w_down))
    return (_trunk(x_prompt, c_prompt, p), _trunk(x_sample, c_sample, p))
```

```python
import functools

import jax
import jax.numpy as jnp
from jax import lax
from jax.experimental import pallas as pl
from jax.experimental.pallas import tpu as pltpu

F32 = jnp.float32
BF16 = jnp.bfloat16

D_MODEL = 1024
DEPTH = 4
ALPHA = (2 * DEPTH) ** 0.25
LN_EPS = 1e-5

GRID_W = 64
NA_HEADS = 16
NA_HEAD_DIM = D_MODEL // NA_HEADS
NA_WIN_ROWS = 8
NA_WIN_COLS = 16
NEG_INF = -1e30

RK_HEAD_DIM = 64
RK_GN_EPS = 64e-5
RK_CHUNK = 64
RK_TOK_BLOCK = 128
RK_LANE_BLOCK = 1024

N_EXPERTS = 8
ROUTER_LANES = 128

VMEM_LIMIT = 52 * 1024 * 1024
MOE_VMEM_LIMIT = 58 * 1024 * 1024


def _cparams(*sem):
    return pltpu.CompilerParams(dimension_semantics=sem, vmem_limit_bytes=VMEM_LIMIT)


def _const_spec(shape):
    nd = len(shape)
    return pl.BlockSpec(shape, lambda *_: (0,) * nd, pipeline_mode=pl.Buffered(1))


def _dot(a, b):
    return jnp.dot(a, b, preferred_element_type=F32)


def _dot_nt(a, b):
    return lax.dot_general(a, b, (((1,), (1,)), ((), ())), preferred_element_type=F32)


def _dot_tn(a, b):
    return lax.dot_general(a, b, (((0,), (0,)), ((), ())), preferred_element_type=F32)


def _sigmoid(x):
    return 1.0 / (1.0 + jnp.exp(-x))


def _layer_norm(y, g, b):
    mu = jnp.mean(y, axis=-1, keepdims=True)
    d = y - mu
    var = jnp.mean(d * d, axis=-1, keepdims=True)
    return d * lax.rsqrt(var + LN_EPS) * g + b


def _ada_kernel(c_ref, w_ref, b_ref, o_ref):
    c = c_ref[...]
    s = (c * _sigmoid(c)).astype(BF16)
    o_ref[...] = _dot(s, w_ref[...].astype(BF16)) + b_ref[...]


def ada_modulation(c, w_ada, b_ada):
    bsz, d = c.shape
    depth = w_ada.shape[0]
    out = pl.pallas_call(
        _ada_kernel,
        grid=(depth, 6),
        in_specs=[
            pl.BlockSpec((bsz, d), lambda i, j: (0, 0)),
            pl.BlockSpec((None, d, d), lambda i, j: (i, 0, j)),
            pl.BlockSpec((None, None, 1, d), lambda i, j: (i, j, 0, 0)),
        ],
        out_specs=pl.BlockSpec((None, None, bsz, d), lambda i, j: (i, j, 0, 0)),
        out_shape=jax.ShapeDtypeStruct((depth, 6, bsz, d), F32),
        compiler_params=_cparams("arbitrary", "arbitrary"),
        name="ada_modulation",
    )(c, w_ada, b_ada.reshape(depth, 6, 1, d))
    return jnp.transpose(out, (0, 2, 1, 3))


def _tile_specs(seq, tt, d):
    nb8 = seq // 8
    per = tt // 8
    x_spec = pl.BlockSpec((None, tt, d), lambda b, t: (b, t, 0))
    prev_spec = pl.BlockSpec((None, 8, d), lambda b, t: (b, jnp.maximum(t * per - 1, 0), 0))
    next_spec = pl.BlockSpec((None, 8, d), lambda b, t: (b, jnp.minimum((t + 1) * per, nb8 - 1), 0))
    return x_spec, prev_spec, next_spec


def _mod_spec(d):
    return pl.BlockSpec((None, 6, d), lambda b, t: (b, 0, 0))


def _shifted(cur, prev_row, next_row, tt):
    row = lax.broadcasted_iota(jnp.int32, (tt, 1), 0)
    m1 = jnp.where(row == 0, prev_row, pltpu.roll(cur, 1, 0))
    p1 = jnp.where(row == tt - 1, next_row, pltpu.roll(cur, tt - 1, 0))
    return m1, p1


def _conv_kernel(x_ref, xp_ref, xn_ref, mod_ref, win_ref, cw_ref, wout_ref, lng_ref, lnb_ref,
                 o_ref, *, nt, tt, d):
    t = pl.program_id(1)
    sh, sc, gate = mod_ref[0:1, :], mod_ref[1:2, :], mod_ref[2:3, :]
    x = x_ref[...]
    h = (x * (1.0 + sc) + sh).astype(BF16)
    p = _dot(h, win_ref[...])
    z = p[:, d:2 * d] * p[:, 2 * d:]
    halo = jnp.concatenate([xp_ref[...], xn_ref[...]], axis=0)
    hh = (halo * (1.0 + sc) + sh).astype(BF16)
    ph = _dot(hh, win_ref[:, d:])
    zh = ph[:, :d] * ph[:, d:]
    z_prev = jnp.where(t > 0, zh[7:8, :], 0.0)
    z_next = jnp.where(t < nt - 1, zh[8:9, :], 0.0)
    z_m1, z_p1 = _shifted(z, z_prev, z_next, tt)
    conv = z_m1 * cw_ref[0:1, :] + z * cw_ref[1:2, :] + z_p1 * cw_ref[2:3, :]
    y = (p[:, :d] * conv).astype(BF16)
    mix = _dot(y, wout_ref[...])
    o_ref[...] = _layer_norm(ALPHA * x + gate * mix, lng_ref[...], lnb_ref[...])


def conv_mixer_layer(x, mod, w_in, conv_w, w_out, ln_g, ln_b, tt=512):
    bsz, seq, d = x.shape
    tt = min(tt, seq)
    nt = seq // tt
    x_spec, prev_spec, next_spec = _tile_specs(seq, tt, d)
    return pl.pallas_call(
        functools.partial(_conv_kernel, nt=nt, tt=tt, d=d),
        grid=(bsz, nt),
        in_specs=[x_spec, prev_spec, next_spec, _mod_spec(d),
                  _const_spec((d, 3 * d)), _const_spec((3, d)), _const_spec((d, d)),
                  _const_spec((1, d)), _const_spec((1, d))],
        out_specs=pl.BlockSpec((None, tt, d), lambda b, t: (b, t, 0)),
        out_shape=jax.ShapeDtypeStruct(x.shape, F32),
        compiler_params=_cparams("parallel", "arbitrary"),
        name="conv_mixer",
    )(x, x, x, mod, w_in, conv_w, w_out, ln_g, ln_b)


def _ffn_kernel(x_ref, mod_ref, wgu_ref, wd_ref, lng_ref, lnb_ref, o_ref, *, ff, fc):
    sh, sc, gate = mod_ref[3:4, :], mod_ref[4:5, :], mod_ref[5:6, :]
    x = x_ref[...]
    h = (x * (1.0 + sc) + sh).astype(BF16)
    acc = jnp.zeros(x.shape, F32)
    for c in range(ff // fc):
        g = _dot(h, wgu_ref[:, c * fc:(c + 1) * fc])
        u = _dot(h, wgu_ref[:, ff + c * fc:ff + (c + 1) * fc])
        act = (g * _sigmoid(g) * u).astype(BF16)
        acc = acc + _dot(act, wd_ref[c * fc:(c + 1) * fc, :])
    o_ref[...] = _layer_norm(ALPHA * x + gate * acc, lng_ref[...], lnb_ref[...])


def dense_ffn_layer(x, mod, w_gu, w_down, ln_g, ln_b, tt=512):
    bsz, seq, d = x.shape
    tt = min(tt, seq)
    ff = w_down.shape[0]
    fc = ff // 2
    return pl.pallas_call(
        functools.partial(_ffn_kernel, ff=ff, fc=fc),
        grid=(bsz, seq // tt),
        in_specs=[pl.BlockSpec((None, tt, d), lambda b, t: (b, t, 0)), _mod_spec(d),
                  _const_spec((d, 2 * ff)), _const_spec((ff, d)),
                  _const_spec((1, d)), _const_spec((1, d))],
        out_specs=pl.BlockSpec((None, tt, d), lambda b, t: (b, t, 0)),
        out_shape=jax.ShapeDtypeStruct(x.shape, F32),
        compiler_params=_cparams("parallel", "arbitrary"),
        name="dense_ffn",
    )(x, mod, w_gu, w_down, ln_g, ln_b)


MOE_ROW_CHUNK = 256
MOE_ROW_BLOCKS = (128, 256)


def _moe_kernel(x_ref, mod_ref, wr_ref, br_ref, wgu_ref, wd_ref, lng_ref, lnb_ref, o_ref,
                h_sc, comb_sc, acc_sc, rankc_sc, rankr_sc, maskr_sc, y_sc, *, fe, blocks):
    e = pl.program_id(2)
    tt = comb_sc.shape[0]
    lane = lax.broadcasted_iota(jnp.int32, comb_sc.shape, 1)

    @pl.when(e == 0)
    def _():
        sh, sc = mod_ref[3:4, :], mod_ref[4:5, :]
        h = x_ref[...] * (1.0 + sc) + sh
        h_hi = h.astype(BF16)
        h_sc[...] = h_hi
        h_lo = (h - h_hi.astype(F32)).astype(BF16)
        wr = wr_ref[...]
        w_hi = wr.astype(BF16)
        w_lo = (wr - w_hi.astype(F32)).astype(BF16)
        logits = _dot(h_hi, w_hi) + _dot(h_lo, w_hi) + _dot(h_hi, w_lo) + br_ref[...]
        m = jnp.max(logits, axis=-1, keepdims=True)
        ex = jnp.exp(logits - m)
        probs = ex / jnp.sum(ex, axis=-1, keepdims=True)
        valid = lane < N_EXPERTS
        p = jnp.where(valid, probs, -1.0)
        p1 = jnp.max(p, axis=-1, keepdims=True)
        i1 = jnp.min(jnp.where(p == p1, lane, ROUTER_LANES), axis=-1, keepdims=True)
        pr = jnp.where(lane == i1, -1.0, p)
        p2 = jnp.max(pr, axis=-1, keepdims=True)
        i2 = jnp.min(jnp.where(pr == p2, lane, ROUTER_LANES), axis=-1, keepdims=True)
        tot = p1 + p2
        comb = jnp.where(lane == i1, p1 / tot, 0.0) + jnp.where(lane == i2, p2 / tot, 0.0)
        comb_sc[...] = comb
        acc_sc[...] = jnp.zeros(acc_sc.shape, F32)
        y_sc[...] = jnp.zeros(y_sc.shape, BF16)
        routed = jnp.where(comb > 0.0, 1.0, 0.0).astype(BF16)
        q0 = lax.broadcasted_iota(jnp.int32, (ROUTER_LANES, ROUTER_LANES), 0)
        q1 = lax.broadcasted_iota(jnp.int32, (ROUTER_LANES, ROUTER_LANES), 1)
        routed_t = _dot_nt(jnp.where(q0 == q1, 1.0, 0.0).astype(BF16), routed)
        maskr_sc[...] = routed_t
        routed_tb = routed_t.astype(BF16)
        rc = min(MOE_ROW_CHUNK, tt)
        for c in range(tt // rc):
            r_i = lax.broadcasted_iota(jnp.int32, (rc, tt), 0) + c * rc
            c_i = lax.broadcasted_iota(jnp.int32, (rc, tt), 1)
            rankc_sc[c * rc:(c + 1) * rc, :] = _dot(jnp.where(c_i < r_i, 1.0, 0.0).astype(BF16), routed)
            r_j = lax.broadcasted_iota(jnp.int32, (tt, rc), 0)
            c_j = lax.broadcasted_iota(jnp.int32, (tt, rc), 1) + c * rc
            rankr_sc[:, c * rc:(c + 1) * rc] = _dot(routed_tb, jnp.where(r_j < c_j, 1.0, 0.0).astype(BF16))

    def expert(hb):
        g = _dot(hb, wgu_ref[:, :fe])
        u = _dot(hb, wgu_ref[:, fe:])
        act = (g * _sigmoid(g) * u).astype(BF16)
        return _dot(act, wd_ref[...])

    w_col = jnp.sum(jnp.where(lane == e, comb_sc[...], 0.0), axis=-1, keepdims=True)
    rk_col = jnp.sum(jnp.where(lane == e, rankc_sc[...], 0.0), axis=-1, keepdims=True).astype(jnp.int32)
    rk_row = rankr_sc[pl.ds(e, 1), :].astype(jnp.int32)
    mk_row = maskr_sc[pl.ds(e, 1), :]
    n_routed = jnp.sum(mk_row).astype(jnp.int32)
    half = tt // 2
    gb, sb = blocks

    def gather_block(j, carry):
        base = pl.multiple_of(j * gb, gb)
        sub = lax.broadcasted_iota(jnp.int32, (gb, tt), 0) + base
        pick = jnp.where((rk_row == sub) & (mk_row > 0.0), 1.0, 0.0).astype(BF16)
        y_sc[pl.ds(base, gb), :] = expert(_dot(pick, h_sc[...]).astype(BF16)).astype(BF16)
        return carry

    lax.fori_loop(0, (n_routed + gb - 1) // gb, gather_block, 0)

    for c in range(tt // sb):
        @pl.when(n_routed > c * sb)
        def _(c=c):
            for r0 in (0, half):
                rows = slice(r0, r0 + half)
                ln_i = lax.broadcasted_iota(jnp.int32, (half, sb), 1) + c * sb
                put = jnp.where((rk_col[rows] == ln_i) & (w_col[rows] > 0.0), 1.0, 0.0).astype(BF16)
                acc_sc[rows, :] += w_col[rows] * _dot(put, y_sc[c * sb:(c + 1) * sb, :])

    @pl.when(e == N_EXPERTS - 1)
    def _():
        gate = mod_ref[5:6, :]
        o_ref[...] = _layer_norm(ALPHA * x_ref[...] + gate * acc_sc[...], lng_ref[...], lnb_ref[...])


def moe_ffn_layer(x, mod, w_router, b_router, w_gu, w_down, ln_g, ln_b, tt=1024, blocks=MOE_ROW_BLOCKS):
    bsz, seq, d = x.shape
    tt = min(tt, seq)
    n_e, fe = w_down.shape[0], w_down.shape[1]
    return pl.pallas_call(
        functools.partial(_moe_kernel, fe=fe, blocks=tuple(min(b, tt) for b in blocks)),
        grid=(bsz, seq // tt, n_e),
        in_specs=[pl.BlockSpec((None, tt, d), lambda b, t, e: (b, t, 0), pipeline_mode=pl.Buffered(1)),
                  pl.BlockSpec((None, 6, d), lambda b, t, e: (b, 0, 0)),
                  _const_spec((d, ROUTER_LANES)), _const_spec((1, ROUTER_LANES)),
                  pl.BlockSpec((None, d, 2 * fe), lambda b, t, e: (e, 0, 0)),
                  pl.BlockSpec((None, fe, d), lambda b, t, e: (e, 0, 0)),
                  _const_spec((1, d)), _const_spec((1, d))],
        out_specs=pl.BlockSpec((None, tt, d), lambda b, t, e: (b, t, 0)),
        out_shape=jax.ShapeDtypeStruct(x.shape, F32),
        scratch_shapes=[pltpu.VMEM((tt, d), BF16), pltpu.VMEM((tt, ROUTER_LANES), F32),
                        pltpu.VMEM((tt, d), F32), pltpu.VMEM((tt, ROUTER_LANES), F32),
                        pltpu.VMEM((ROUTER_LANES, tt), F32), pltpu.VMEM((ROUTER_LANES, tt), F32),
                        pltpu.VMEM((tt, d), BF16)],
        compiler_params=pltpu.CompilerParams(
            dimension_semantics=("parallel", "arbitrary", "arbitrary"), vmem_limit_bytes=MOE_VMEM_LIMIT),
        name="moe_ffn",
    )(x, mod, w_router, b_router, w_gu, w_down, ln_g, ln_b)


def _pad_router(w_router, b_router):
    d, n_e = w_router.shape
    w = jnp.zeros((d, ROUTER_LANES), F32).at[:, :n_e].set(w_router)
    b = jnp.full((1, ROUTER_LANES), NEG_INF, F32).at[0, :n_e].set(b_router)
    return w, b


def _qkv_kernel(x_ref, mod_ref, w_ref, q_ref, k_ref, v_ref, *, d):
    sh, sc = mod_ref[0:1, :], mod_ref[1:2, :]
    h = (x_ref[...] * (1.0 + sc) + sh).astype(BF16)
    p = _dot(h, w_ref[...])
    q_ref[...] = (p[:, :d] * (NA_HEAD_DIM ** -0.5)).astype(BF16)
    k_ref[...] = p[:, d:2 * d].astype(BF16)
    v_ref[...] = p[:, 2 * d:].astype(BF16)


def _na_kernel(q_ref, k_ref, v_ref, bias_ref, o_ref, *, rows):
    r = pl.program_id(1)
    r0 = jnp.clip(r - NA_WIN_ROWS // 2, 0, rows - NA_WIN_ROWS)
    start = pl.multiple_of(r0 * GRID_W, GRID_W)
    nk = NA_WIN_ROWS * GRID_W
    pw_ = 2 * NA_HEAD_DIM
    pairs = range(NA_HEADS // 2)
    lanes = [slice(p * pw_, (p + 1) * pw_) for p in pairs]
    head0 = lax.broadcasted_iota(jnp.int32, (GRID_W, pw_), 1) < NA_HEAD_DIM
    q2 = {}
    for p in pairs:
        q = q_ref[:, lanes[p]]
        z = jnp.zeros_like(q)
        q2[p] = jnp.concatenate([jnp.where(head0, q, z), jnp.where(head0, z, q)], axis=0)
    s = {p: _dot_nt(q2[p], k_ref[pl.ds(start, nk), lanes[p]])
         + bias_ref[2 * p:2 * p + 2].reshape(2 * GRID_W, nk) for p in pairs}
    m = {p: jnp.max(s[p], axis=-1, keepdims=True) for p in pairs}
    e = {p: jnp.exp(s[p] - m[p]) for p in pairs}
    l = {p: jnp.sum(e[p], axis=-1, keepdims=True) for p in pairs}
    o = {p: _dot(e[p].astype(BF16), v_ref[pl.ds(start, nk), lanes[p]]) / l[p] for p in pairs}
    for p in pairs:
        o_ref[:, lanes[p]] = jnp.where(head0, o[p][:GRID_W], o[p][GRID_W:]).astype(BF16)


def _na_bias_table(rpb):
    nh, nr, nc = rpb.shape
    w = GRID_W
    qc = jnp.arange(w)[:, None]
    kc = jnp.arange(w)[None, :]
    cs = jnp.clip(qc - NA_WIN_COLS // 2, 0, w - NA_WIN_COLS)
    col_ok = (kc >= cs) & (kc < cs + NA_WIN_COLS)
    lo = w - NA_WIN_COLS
    e = jnp.pad(rpb, ((0, 0), (0, 0), (lo, 2 * w - lo - nc)))
    flat = jnp.tile(e, (1, 1, w))
    toep = flat[:, :, w - 1:w - 1 + w * (2 * w - 1)].reshape(nh, nr, w, 2 * w - 1)[..., :w]
    toep = jnp.where(col_ok[None, None], toep, NEG_INF)
    tab = jnp.stack([toep[:, dl:dl + NA_WIN_ROWS] for dl in range(NA_WIN_ROWS)], axis=1)
    tab = jnp.transpose(tab, (0, 1, 3, 2, 4))
    return tab.reshape(nh, NA_WIN_ROWS, w, NA_WIN_ROWS * w)


def _proj_ln_kernel(*refs, n_in):
    a_refs = refs[:n_in]
    x_ref, mod_ref, w_ref, lng_ref, lnb_ref, o_ref = refs[n_in:]
    a = a_refs[0][...]
    if n_in == 2:
        a = (a.astype(F32) + a_refs[1][...].astype(F32)).astype(BF16)
    mix = _dot(a, w_ref[...])
    gate = mod_ref[2:3, :]
    o_ref[...] = _layer_norm(ALPHA * x_ref[...] + gate * mix, lng_ref[...], lnb_ref[...])


def proj_residual_ln(acts, x, mod, w_o, ln_g, ln_b, tt=512):
    bsz, seq, d = x.shape
    tt = min(tt, seq)
    tile = pl.BlockSpec((None, tt, d), lambda b, t: (b, t, 0))
    return pl.pallas_call(
        functools.partial(_proj_ln_kernel, n_in=len(acts)),
        grid=(bsz, seq // tt),
        in_specs=[tile] * len(acts) + [tile, _mod_spec(d), _const_spec((d, d)),
                                       _const_spec((1, d)), _const_spec((1, d))],
        out_specs=tile,
        out_shape=jax.ShapeDtypeStruct(x.shape, F32),
        compiler_params=_cparams("parallel", "arbitrary"),
        name="proj_residual_ln",
    )(*acts, x, mod, w_o, ln_g, ln_b)


def na_mixer_layer(x, mod, w_qkv, bias_tab, w_o, ln_g, ln_b, tt=512):
    bsz, seq, d = x.shape
    tt = min(tt, seq)
    rows = seq // GRID_W
    assert rows >= NA_WIN_ROWS and seq % GRID_W == 0
    tile = pl.BlockSpec((None, tt, d), lambda b, t: (b, t, 0))
    q, k, v = pl.pallas_call(
        functools.partial(_qkv_kernel, d=d),
        grid=(bsz, seq // tt),
        in_specs=[tile, _mod_spec(d), _const_spec((d, 3 * d))],
        out_specs=[tile] * 3,
        out_shape=[jax.ShapeDtypeStruct(x.shape, BF16)] * 3,
        compiler_params=_cparams("parallel", "arbitrary"),
        name="na_qkv",
    )(x, mod, w_qkv)

    def delta(r):
        return jnp.clip(r - NA_WIN_ROWS // 2, 0, rows - NA_WIN_ROWS) - r + NA_WIN_ROWS - 1

    row_spec = pl.BlockSpec((None, GRID_W, d), lambda b, r: (b, r, 0))
    seq_spec = pl.BlockSpec((None, seq, d), lambda b, r: (b, 0, 0))
    att = pl.pallas_call(
        functools.partial(_na_kernel, rows=rows),
        grid=(bsz, rows),
        in_specs=[row_spec, seq_spec, seq_spec,
                  pl.BlockSpec((NA_HEADS, None, GRID_W, NA_WIN_ROWS * GRID_W),
                               lambda b, r: (0, delta(r), 0, 0))],
        out_specs=row_spec,
        out_shape=jax.ShapeDtypeStruct(x.shape, BF16),
        compiler_params=_cparams("parallel", "arbitrary"),
        name="na_attention",
    )(q, k, v, bias_tab)
    return proj_residual_ln([att], x, mod, w_o, ln_g, ln_b, tt)


def _softplus(y):
    return jnp.maximum(y, 0.0) + jnp.log(1.0 + jnp.exp(-jnp.abs(y)))


def _rk_proj_kernel(x_ref, xp_ref, xn_ref, mod_ref, mu_ref, wrkv_ref, w1_ref, w2_ref, a1_ref, a2_ref,
                    g1_ref, g2_ref, w0_ref, a0_ref, kk_ref, ka_ref,
                    r_o, v_o, kk_o, lw_o, km_o, b_o, gate_o, *, nt, tt):
    t = pl.program_id(1)
    sh, sc = mod_ref[0:1, :], mod_ref[1:2, :]
    h = x_ref[...] * (1.0 + sc) + sh
    h_prev = jnp.where(t > 0, xp_ref[7:8, :] * (1.0 + sc) + sh, 0.0)
    h_next = jnp.where(t < nt - 1, xn_ref[0:1, :] * (1.0 + sc) + sh, 0.0)
    h_m1, h_p1 = _shifted(h, h_prev, h_next, tt)
    xx = 0.5 * (h_m1 + h_p1) - h

    def mixed(p):
        return (h + xx * mu_ref[p:p + 1, :]).astype(BF16)

    r = _dot(mixed(0), wrkv_ref[0])
    k = _dot(mixed(1), wrkv_ref[1])
    v = _dot(mixed(2), wrkv_ref[2])
    r_o[...] = r.astype(BF16)
    v_o[...] = v.astype(BF16)
    lw_in = jnp.tanh(_dot(mixed(3), w1_ref[...])).astype(BF16)
    a_in = _dot(mixed(4), a1_ref[...]).astype(BF16)
    g_in = _sigmoid(_dot(mixed(5), g1_ref[...])).astype(BF16)

    kk = k * kk_ref[...]
    kk_o[...] = kk.astype(BF16)
    rw = w1_ref.shape[1] // 2
    rg = g1_ref.shape[1] // 2
    for z in range(2):
        wl = w0_ref[z:z + 1, :] + _dot(lw_in[:, z * rw:(z + 1) * rw], w2_ref[z])
        w_log = -_softplus(-wl) - 0.5
        lw_o[z] = -jnp.exp(w_log)
        a_lr = _sigmoid(a0_ref[z:z + 1, :] + _dot(a_in[:, z * rw:(z + 1) * rw], a2_ref[z]))
        km_o[z] = (k * (1.0 + (a_lr - 1.0) * ka_ref[...])).astype(BF16)
        b_o[z] = (kk * a_lr).astype(BF16)
        gate_o[z] = _dot(g_in[:, z * rg:(z + 1) * rg], g2_ref[z]).astype(BF16)


def _head_sum(x):
    head0 = lax.broadcasted_iota(jnp.int32, x.shape, 1) < RK_HEAD_DIM
    s0 = jnp.sum(jnp.where(head0, x, 0.0), axis=-1, keepdims=True)
    s1 = jnp.sum(jnp.where(head0, 0.0, x), axis=-1, keepdims=True)
    return jnp.where(head0, s0, s1)


def _rk_scan_kernel(r_ref, v_ref, kk_ref, lw_ref, km_ref, b_ref, gate_ref, rk_ref, lg_ref, lb_ref,
                    o_ref, s_ref, *, reverse):
    tb, hb = r_ref.shape
    cl = RK_CHUNK
    nc, nh = tb // cl, hb // RK_HEAD_DIM

    @pl.when(pl.program_id(2) == 0)
    def _():
        s_ref[...] = jnp.zeros(s_ref.shape, F32)

    ri = lax.broadcasted_iota(jnp.int32, (tb, tb), 0)
    ci = lax.broadcasted_iota(jnp.int32, (tb, tb), 1)
    ordered = (ci >= ri) if reverse else (ci <= ri)
    tri = jnp.where((ri // cl == ci // cl) & ordered, 1.0, 0.0).astype(BF16)
    lw = lw_ref[...]
    lw_hi = lw.astype(BF16)
    lw_lo = (lw - lw_hi.astype(F32)).astype(BF16)
    cum = _dot(tri, lw_hi) + _dot(tri, lw_lo)
    last = 0 if reverse else cl - 1
    tot_rows = [cum[c * cl + last:c * cl + last + 1, :] for c in range(nc)]
    tot = jnp.concatenate([jnp.broadcast_to(tr, (cl, hb)) for tr in tot_rows], axis=0)

    rr = r_ref[...].astype(F32)
    vv = v_ref[...]
    kk = kk_ref[...].astype(F32)
    pair = 2 * RK_HEAD_DIM
    rnorm = jnp.concatenate(
        [1.0 / jnp.maximum(jnp.sqrt(_head_sum(jnp.square(kk[:, i:i + pair]))), 1e-12) for i in range(0, hb, pair)],
        axis=1)
    kkn = kk * rnorm
    km = km_ref[...].astype(F32)
    bb = b_ref[...].astype(F32) * rnorm
    e_neg = jnp.exp(-cum)
    e_rem = jnp.exp(tot - cum)
    a_t = (-kkn * jnp.exp(cum - lw)).astype(BF16)
    r_t = (rr * jnp.exp(cum)).astype(BF16)
    b_t = (bb * e_neg).astype(BF16)
    k_t = (km * e_neg).astype(BF16)
    b_h = (bb * e_rem).astype(BF16)
    k_h = (km * e_rem).astype(BF16)
    bonus_w = rr * km * rk_ref[...]

    pw_ = 2 * RK_HEAD_DIM
    npair = hb // pw_
    t_i = lax.broadcasted_iota(jnp.int32, (cl, pw_), 0)
    s_i = lax.broadcasted_iota(jnp.int32, (cl, pw_), 1) % cl
    strict = (t_i < s_i) if reverse else (t_i > s_i)
    incl = (t_i <= s_i) if reverse else (t_i >= s_i)
    eye = jnp.where(t_i == s_i, 1.0, 0.0)
    head0 = lax.broadcasted_iota(jnp.int32, (cl, pw_), 1) < RK_HEAD_DIM
    q0 = lax.broadcasted_iota(jnp.int32, (pw_, pw_), 0)
    q1 = lax.broadcasted_iota(jnp.int32, (pw_, pw_), 1)
    same_head = (q0 // RK_HEAD_DIM) == (q1 // RK_HEAD_DIM)

    head_sum = _head_sum

    def bd(x):
        z = jnp.zeros_like(x)
        return jnp.concatenate([jnp.where(head0, x, z), jnp.where(head0, z, x)], axis=0)

    keys = [(c, p) for c in range(nc) for p in range(npair)]
    rsl = {k: slice(k[0] * cl, (k[0] + 1) * cl) for k in keys}
    lsl = {k: slice(k[1] * pw_, (k[1] + 1) * pw_) for k in keys}
    blk = lambda arr, k: arr[rsl[k], lsl[k]]

    gram = {k: _dot_nt(jnp.concatenate([blk(a_t, k), blk(r_t, k)], axis=0),
                       jnp.concatenate([bd(blk(b_t, k)), bd(blk(k_t, k))], axis=0)) for k in keys}
    m_ab = {k: jnp.where(strict, gram[k][:cl, :pw_], 0.0) for k in keys}
    m_ak = {k: jnp.where(strict, gram[k][:cl, pw_:], 0.0).astype(BF16) for k in keys}
    m_rb = {k: jnp.where(incl, gram[k][cl:, :pw_], 0.0).astype(BF16) for k in keys}
    m_rk = {k: jnp.where(incl, gram[k][cl:, pw_:], 0.0).astype(BF16) for k in keys}
    vbd = {k: bd(blk(vv, k)) for k in keys}
    my = {k: _dot(jnp.concatenate([m_ak[k], m_rk[k]], axis=0), vbd[k]) for k in keys}
    v_t = {k: jnp.transpose(blk(vv, k).astype(F32)).astype(BF16) for k in keys}
    vk = {k: jnp.where(same_head, _dot(v_t[k], blk(k_h, k)), 0.0) for k in keys}
    inv = {k: eye + m_ab[k] for k in keys}
    pwb = {k: m_ab[k].astype(BF16) for k in keys}
    pw = {k: _dot(pwb[k], bd(pwb[k])) for k in keys}
    n = 4
    while n < cl:
        pwb = {k: pw[k].astype(BF16) for k in keys}
        res = {k: _dot(pwb[k], jnp.concatenate([bd(pwb[k]), bd(inv[k].astype(BF16))], axis=1)) for k in keys}
        pw = {k: res[k][:, :pw_] for k in keys}
        inv = {k: inv[k] + res[k][:, pw_:] for k in keys}
        n *= 2
    inv = {k: (inv[k] + _dot(pw[k].astype(BF16), bd(inv[k].astype(BF16)))).astype(BF16) for k in keys}
    au = {k: _dot(inv[k], jnp.concatenate([bd(blk(a_t, k)), bd(my[k][:cl].astype(BF16))], axis=1))
          for k in keys}
    a_hat = {k: au[k][:, :pw_].astype(BF16) for k in keys}
    u0_t = {k: jnp.transpose(au[k][:, pw_:]) for k in keys}
    ar = {k: jnp.concatenate([a_hat[k], blk(r_t, k)], axis=0) for k in keys}

    pairs = range(npair)
    for c in (range(nc - 1, -1, -1) if reverse else range(nc)):
        st = {p: s_ref[p] for p in pairs}
        stb = {p: st[p].astype(BF16) for p in pairs}
        res = {p: _dot_nt(ar[c, p], stb[p]) for p in pairs}
        res_t = {p: _dot_nt(stb[p], a_hat[c, p]) for p in pairs}
        ub = {p: (au[c, p][:, pw_:] + res[p][:cl]).astype(BF16) for p in pairs}
        ub_t = {p: (u0_t[c, p] + res_t[p]).astype(BF16) for p in pairs}
        y = {p: my[c, p][cl:] + res[p][cl:] + _dot(m_rb[c, p], bd(ub[p])) for p in pairs}
        for p in pairs:
            k = (c, p)
            decay = jnp.exp(tot_rows[c][:, lsl[k]])
            s_ref[p] = jnp.where(same_head, st[p] * decay + _dot(ub_t[p], blk(b_h, k)), 0.0) + vk[k]
        dy = {p: y[p] - head_sum(y[p]) * (1.0 / RK_HEAD_DIM) for p in pairs}
        var = {p: head_sum(dy[p] * dy[p]) * (1.0 / RK_HEAD_DIM) for p in pairs}
        for p in pairs:
            k = (c, p)
            yn = dy[p] * lax.rsqrt(var[p] + RK_GN_EPS) * lg_ref[:, lsl[k]] + lb_ref[:, lsl[k]]
            bonus = head_sum(blk(bonus_w, k)) * blk(vv, k).astype(F32)
            o_ref[rsl[k], lsl[k]] = (blk(gate_ref, k).astype(F32) * (yn + bonus)).astype(BF16)


def rwkv_mixer_layer(x, mod, p, ln_g, ln_b, tt=256):
    bsz, seq, d = x.shape
    tt = min(tt, seq)
    nt = seq // tt
    x_spec, prev_spec, next_spec = _tile_specs(seq, tt, d)
    tile = pl.BlockSpec((None, tt, d), lambda b, t: (b, t, 0))
    tile2 = pl.BlockSpec((2, None, tt, d), lambda b, t: (0, b, t, 0))
    rw, rg = p['w1'].shape[1] // 2, p['g1'].shape[1] // 2
    sh_bf = jax.ShapeDtypeStruct(x.shape, BF16)
    sh2_bf = jax.ShapeDtypeStruct((2,) + x.shape, BF16)
    sh2_f32 = jax.ShapeDtypeStruct((2,) + x.shape, F32)
    r, v, kkn, lw, km, bz, gate = pl.pallas_call(
        functools.partial(_rk_proj_kernel, nt=nt, tt=tt),
        grid=(bsz, nt),
        in_specs=[x_spec, prev_spec, next_spec, _mod_spec(d), _const_spec((6, d)),
                  _const_spec((3, d, d)),
                  _const_spec((d, 2 * rw)), _const_spec((2, rw, d)),
                  _const_spec((d, 2 * rw)), _const_spec((2, rw, d)),
                  _const_spec((d, 2 * rg)), _const_spec((2, rg, d)),
                  _const_spec((2, d)), _const_spec((2, d)), _const_spec((1, d)), _const_spec((1, d))],
        out_specs=[tile, tile, tile, tile2, tile2, tile2, tile2],
        out_shape=[sh_bf, sh_bf, sh_bf, sh2_f32, sh2_bf, sh2_bf, sh2_bf],
        compiler_params=_cparams("parallel", "arbitrary"),
        name="rwkv_proj",
    )(x, x, x, mod, p['mu'], p['w_rkv'], p['w1'], p['w2'], p['a1'], p['a2'], p['g1'], p['g2'],
      p['w0'], p['a0'], p['k_k'], p['k_a'])

    tb = min(RK_TOK_BLOCK, seq)
    hb = RK_LANE_BLOCK
    nb = seq // tb
    outs = []
    for z in range(2):
        rev = z == 1

        def tok(i, rev=rev):
            return nb - 1 - i if rev else i

        blk = pl.BlockSpec((None, tb, hb), lambda b, g, i: (b, tok(i), g))
        blk2 = pl.BlockSpec((None, None, tb, hb), lambda b, g, i, z=z: (z, b, tok(i), g))
        vec = pl.BlockSpec((1, hb), lambda b, g, i: (0, g))
        vec2 = pl.BlockSpec((None, 1, hb), lambda b, g, i, z=z: (z, 0, g))
        outs.append(pl.pallas_call(
            functools.partial(_rk_scan_kernel, reverse=rev),
            grid=(bsz, d // hb, nb),
            in_specs=[blk, blk, blk, blk2, blk2, blk2, blk2, vec, vec2, vec2],
            out_specs=blk,
            out_shape=sh_bf,
            scratch_shapes=[pltpu.VMEM((hb // (2 * RK_HEAD_DIM), 2 * RK_HEAD_DIM, 2 * RK_HEAD_DIM), F32)],
            compiler_params=_cparams("parallel", "parallel", "arbitrary"),
            name="rwkv_scan_rev" if rev else "rwkv_scan_fwd",
        )(r, v, kkn, lw, km, bz, gate, p['r_k'], p['lnx_g'], p['lnx_b']))
    return proj_residual_ln(outs, x, mod, p['w_o'], ln_g, ln_b)


def _prepare_params(w):
    d = D_MODEL
    bf = lambda a: a.astype(BF16)
    row = lambda a: a.reshape(1, d)
    lane = jnp.arange(d) // RK_HEAD_DIM
    p = {
        'w_ada': w['w_ada'], 'b_ada': w['b_ada'],
        'ln_g': w['ln_g'], 'ln_b': w['ln_b'],
        'conv_w_in': bf(w['conv_w_in']), 'conv_w': w['conv_w'], 'conv_w_out': bf(w['conv_w_out']),
        'na_w_qkv': bf(w['na_w_qkv']), 'na_w_o': bf(w['na_w_o']),
        'na_bias': [_na_bias_table(w['na_rpb'][j]) for j in range(w['na_rpb'].shape[0])],
        'ffn_w_gu': bf(w['ffn_w_gu']), 'ffn_w_down': bf(w['ffn_w_down']),
        'moe_router': [_pad_router(w['moe_w_router'][j], w['moe_b_router'][j])
                       for j in range(w['moe_w_router'].shape[0])],
        'moe_w_gu': bf(w['moe_w_gu']), 'moe_w_down': bf(w['moe_w_down']),
        'rk': [],
    }
    cat = lambda a: jnp.concatenate([a[0], a[1]], axis=1)
    for j in range(w['rk_mu'].shape[0]):
        p['rk'].append({
            'mu': w['rk_mu'][j], 'w_rkv': bf(w['rk_w_rkv'][j]),
            'w1': bf(cat(w['rk_w1'][j])), 'w2': bf(w['rk_w2'][j]),
            'a1': bf(cat(w['rk_a1'][j])), 'a2': bf(w['rk_a2'][j]),
            'g1': bf(cat(w['rk_g1'][j])), 'g2': bf(w['rk_g2'][j]),
            'w0': w['rk_w0'][j], 'a0': w['rk_a0'][j],
            'k_k': row(w['rk_k_k'][j]), 'k_a': row(w['rk_k_a'][j]),
            'r_k': w['rk_r_k'][j].reshape(1, d),
            'lnx_g': w['rk_lnx_g'][j].reshape(2, 1, d), 'lnx_b': w['rk_lnx_b'][j].reshape(2, 1, d),
            'w_o': bf(w['rk_w_o'][j]),
        })
    return p


def _trunk(x, c, p):
    d = D_MODEL
    mod_all = ada_modulation(c, p['w_ada'], p['b_ada'])
    for i in range(DEPTH):
        mod = mod_all[i]
        lng = lambda s: p['ln_g'][i, s].reshape(1, d)
        lnb = lambda s: p['ln_b'][i, s].reshape(1, d)
        kind, j = i % 3, i // 3
        if kind == 0:
            x = conv_mixer_layer(x, mod, p['conv_w_in'][j], p['conv_w'][j], p['conv_w_out'][j],
                                 lng(0), lnb(0))
        elif kind == 1:
            x = na_mixer_layer(x, mod, p['na_w_qkv'][j], p['na_bias'][j], p['na_w_o'][j], lng(0), lnb(0))
        else:
            x = rwkv_mixer_layer(x, mod, p['rk'][j], lng(0), lnb(0))
        if i % 2 == 0:
            x = dense_ffn_layer(x, mod, p['ffn_w_gu'][i // 2], p['ffn_w_down'][i // 2], lng(1), lnb(1))
        else:
            wr, br = p['moe_router'][i // 2]
            x = moe_ffn_layer(x, mod, wr, br, p['moe_w_gu'][i // 2], p['moe_w_down'][i // 2],
                              lng(1), lnb(1))
    return x


def kernel(x_prompt, x_sample, c_prompt, c_sample, w_ada, b_ada, ln_g, ln_b, conv_w_in, conv_w, conv_w_out, na_w_qkv, na_rpb, na_w_o, rk_mu, rk_w_rkv, rk_w0, rk_w1, rk_w2, rk_a0, rk_a1, rk_a2, rk_g1, rk_g2, rk_k_k, rk_k_a, rk_r_k, rk_lnx_g, rk_lnx_b, rk_w_o, ffn_w_gu, ffn_w_down, moe_w_router, moe_b_router, moe_w_gu, moe_w_down):
    p = _prepare_params(dict(
        w_ada=w_ada, b_ada=b_ada, ln_g=ln_g, ln_b=ln_b,
        conv_w_in=conv_w_in, conv_w=conv_w, conv_w_out=conv_w_out,
        na_w_qkv=na_w_qkv, na_rpb=na_rpb, na_w_o=na_w_o,
        rk_mu=rk_mu, rk_w_rkv=rk_w_rkv, rk_w0=rk_w0, rk_w1=rk_w1, rk_w2=rk_w2,
        rk_a0=rk_a0, rk_a1=rk_a1, rk_a2=rk_a2, rk_g1=rk_g1, rk_g2=rk_g2,
        rk_k_k=rk_k_k, rk_k_a=rk_k_a, rk_r_k=rk_r_k, rk_lnx_g=rk_lnx_g, rk_lnx_b=rk_lnx_b,
        rk_w_o=rk_w_o, ffn_w_gu=ffn_w_gu, ffn_w_down=ffn_w_down,
        moe_w_router=moe_w_router, moe_b_router=moe_b_router,
        moe_w_gu=moe_w_gu, moe_w_down=moe_w_down))
    return (_trunk(x_prompt, c_prompt, p), _trunk(x_sample, c_sample, p))
```

```python
import functools

import jax
import jax.numpy as jnp
from jax import lax
from jax.experimental import pallas as pl
from jax.experimental.pallas import tpu as pltpu

F32 = jnp.float32
BF16 = jnp.bfloat16

D_MODEL = 1024
DEPTH = 4
ALPHA = (2 * DEPTH) ** 0.25
LN_EPS = 1e-5

GRID_W = 64
NA_HEADS = 16
NA_HEAD_DIM = D_MODEL // NA_HEADS
NA_WIN_ROWS = 8
NA_WIN_COLS = 16
NEG_INF = -1e30

RK_HEAD_DIM = 64
RK_GN_EPS = 64e-5
RK_DECAY_SCALE = 0.6065306597126334
RK_CHUNK = 64
RK_TOK_BLOCK = 256
RK_LANE_BLOCK = 1024

N_EXPERTS = 8
ROUTER_LANES = 128

VMEM_LIMIT = 52 * 1024 * 1024
MOE_VMEM_LIMIT = 58 * 1024 * 1024


def _cparams(*sem):
    return pltpu.CompilerParams(dimension_semantics=sem, vmem_limit_bytes=VMEM_LIMIT)


def _const_spec(shape):
    nd = len(shape)
    return pl.BlockSpec(shape, lambda *_: (0,) * nd, pipeline_mode=pl.Buffered(1))


def _dot(a, b):
    return jnp.dot(a, b, preferred_element_type=F32)


def _dot_nt(a, b):
    return lax.dot_general(a, b, (((1,), (1,)), ((), ())), preferred_element_type=F32)


def _dot_tn(a, b):
    return lax.dot_general(a, b, (((0,), (0,)), ((), ())), preferred_element_type=F32)


def _sigmoid(x):
    return 1.0 / (1.0 + jnp.exp(-x))


def _layer_norm(y, g, b):
    mu = jnp.mean(y, axis=-1, keepdims=True)
    d = y - mu
    var = jnp.mean(d * d, axis=-1, keepdims=True)
    return d * lax.rsqrt(var + LN_EPS) * g + b


def _ada_kernel(c_ref, w_ref, b_ref, o_ref):
    c = c_ref[...]
    s = (c * _sigmoid(c)).astype(BF16)
    o_ref[...] = _dot(s, w_ref[...].astype(BF16)) + b_ref[...]


def ada_modulation(c, w_ada, b_ada):
    bsz, d = c.shape
    depth = w_ada.shape[0]
    out = pl.pallas_call(
        _ada_kernel,
        grid=(depth, 6),
        in_specs=[
            pl.BlockSpec((bsz, d), lambda i, j: (0, 0)),
            pl.BlockSpec((None, d, d), lambda i, j: (i, 0, j)),
            pl.BlockSpec((None, None, 1, d), lambda i, j: (i, j, 0, 0)),
        ],
        out_specs=pl.BlockSpec((None, None, bsz, d), lambda i, j: (i, j, 0, 0)),
        out_shape=jax.ShapeDtypeStruct((depth, 6, bsz, d), F32),
        compiler_params=_cparams("arbitrary", "arbitrary"),
        name="ada_modulation",
    )(c, w_ada, b_ada.reshape(depth, 6, 1, d))
    return jnp.transpose(out, (0, 2, 1, 3))


def _tile_specs(seq, tt, d):
    nb8 = seq // 8
    per = tt // 8
    x_spec = pl.BlockSpec((None, tt, d), lambda b, t: (b, t, 0))
    prev_spec = pl.BlockSpec((None, 8, d), lambda b, t: (b, jnp.maximum(t * per - 1, 0), 0))
    next_spec = pl.BlockSpec((None, 8, d), lambda b, t: (b, jnp.minimum((t + 1) * per, nb8 - 1), 0))
    return x_spec, prev_spec, next_spec


def _mod_spec(d):
    return pl.BlockSpec((None, 6, d), lambda b, t: (b, 0, 0))


def _shifted(cur, prev_row, next_row, tt):
    row = lax.broadcasted_iota(jnp.int32, (tt, 1), 0)
    m1 = jnp.where(row == 0, prev_row, pltpu.roll(cur, 1, 0))
    p1 = jnp.where(row == tt - 1, next_row, pltpu.roll(cur, tt - 1, 0))
    return m1, p1


def _conv_kernel(x_ref, xp_ref, xn_ref, mod_ref, win_ref, cw_ref, wout_ref, lng_ref, lnb_ref,
                 o_ref, *, nt, tt, d):
    t = pl.program_id(1)
    sh, sc, gate = mod_ref[0:1, :], mod_ref[1:2, :], mod_ref[2:3, :]
    parts = [slice(0, tt // 2), slice(tt // 2, tt)]
    xs = [x_ref[r, :] for r in parts]
    ps = [_dot((x * (1.0 + sc) + sh).astype(BF16), win_ref[...]) for x in xs]
    z = jnp.concatenate([p[:, d:2 * d] * p[:, 2 * d:] for p in ps], axis=0)
    halo = jnp.concatenate([xp_ref[...], xn_ref[...]], axis=0)
    hh = (halo * (1.0 + sc) + sh).astype(BF16)
    ph = _dot(hh, win_ref[:, d:])
    zh = ph[:, :d] * ph[:, d:]
    z_prev = jnp.where(t > 0, zh[7:8, :], 0.0)
    z_next = jnp.where(t < nt - 1, zh[8:9, :], 0.0)
    z_m1, z_p1 = _shifted(z, z_prev, z_next, tt)
    conv = z_m1 * cw_ref[0:1, :] + z * cw_ref[1:2, :] + z_p1 * cw_ref[2:3, :]
    mixes = [_dot((p[:, :d] * conv[r]).astype(BF16), wout_ref[...]) for p, r in zip(ps, parts)]
    for x, r, mix in zip(xs, parts, mixes):
        o_ref[r, :] = _layer_norm(ALPHA * x + gate * mix, lng_ref[...], lnb_ref[...])


def conv_mixer_layer(x, mod, w_in, conv_w, w_out, ln_g, ln_b, tt=512):
    bsz, seq, d = x.shape
    tt = min(tt, seq)
    nt = seq // tt
    x_spec, prev_spec, next_spec = _tile_specs(seq, tt, d)
    return pl.pallas_call(
        functools.partial(_conv_kernel, nt=nt, tt=tt, d=d),
        grid=(bsz, nt),
        in_specs=[x_spec, prev_spec, next_spec, _mod_spec(d),
                  _const_spec((d, 3 * d)), _const_spec((3, d)), _const_spec((d, d)),
                  _const_spec((1, d)), _const_spec((1, d))],
        out_specs=pl.BlockSpec((None, tt, d), lambda b, t: (b, t, 0)),
        out_shape=jax.ShapeDtypeStruct(x.shape, F32),
        compiler_params=_cparams("parallel", "arbitrary"),
        name="conv_mixer",
    )(x, x, x, mod, w_in, conv_w, w_out, ln_g, ln_b)


def _ffn_kernel(x_ref, mod_ref, wgu_ref, wd_ref, lng_ref, lnb_ref, o_ref, *, ff, fc):
    sh, sc, gate = mod_ref[3:4, :], mod_ref[4:5, :], mod_ref[5:6, :]
    tt = x_ref.shape[0]
    parts = [slice(0, tt // 2), slice(tt // 2, tt)]
    chunks = range(ff // fc)
    xs = [x_ref[r, :] for r in parts]
    hs = [(x * (1.0 + sc) + sh).astype(BF16) for x in xs]
    gs = [[_dot(h, wgu_ref[:, c * fc:(c + 1) * fc]) for c in chunks] for h in hs]
    us = [[_dot(h, wgu_ref[:, ff + c * fc:ff + (c + 1) * fc]) for c in chunks] for h in hs]
    acts = [[(g * _sigmoid(g) * u).astype(BF16) for g, u in zip(gr, ur)] for gr, ur in zip(gs, us)]
    for x, r, ar in zip(xs, parts, acts):
        acc = _dot(ar[0], wd_ref[0:fc, :])
        for c in chunks[1:]:
            acc = acc + _dot(ar[c], wd_ref[c * fc:(c + 1) * fc, :])
        o_ref[r, :] = _layer_norm(ALPHA * x + gate * acc, lng_ref[...], lnb_ref[...])


def dense_ffn_layer(x, mod, w_gu, w_down, ln_g, ln_b, tt=512):
    bsz, seq, d = x.shape
    tt = min(tt, seq)
    ff = w_down.shape[0]
    fc = ff // 2
    return pl.pallas_call(
        functools.partial(_ffn_kernel, ff=ff, fc=fc),
        grid=(bsz, seq // tt),
        in_specs=[pl.BlockSpec((None, tt, d), lambda b, t: (b, t, 0)), _mod_spec(d),
                  _const_spec((d, 2 * ff)), _const_spec((ff, d)),
                  _const_spec((1, d)), _const_spec((1, d))],
        out_specs=pl.BlockSpec((None, tt, d), lambda b, t: (b, t, 0)),
        out_shape=jax.ShapeDtypeStruct(x.shape, F32),
        compiler_params=_cparams("parallel", "arbitrary"),
        name="dense_ffn",
    )(x, mod, w_gu, w_down, ln_g, ln_b)


MOE_ROW_CHUNK = 256
MOE_ROW_BLOCKS = (128, 256)


def _moe_kernel(x_ref, mod_ref, wr_ref, br_ref, wgu_ref, wd_ref, lng_ref, lnb_ref, o_ref,
                h_sc, comb_sc, acc_sc, rankc_sc, rankr_sc, maskr_sc, y_sc, *, fe, blocks):
    e = pl.program_id(2)
    tt = comb_sc.shape[0]
    lane = lax.broadcasted_iota(jnp.int32, comb_sc.shape, 1)

    @pl.when(e == 0)
    def _():
        sh, sc = mod_ref[3:4, :], mod_ref[4:5, :]
        h = x_ref[...] * (1.0 + sc) + sh
        h_hi = h.astype(BF16)
        h_sc[...] = h_hi
        h_lo = (h - h_hi.astype(F32)).astype(BF16)
        wr = wr_ref[...]
        w_hi = wr.astype(BF16)
        w_lo = (wr - w_hi.astype(F32)).astype(BF16)
        logits = _dot(h_hi, w_hi) + _dot(h_lo, w_hi) + _dot(h_hi, w_lo) + br_ref[...]
        m = jnp.max(logits, axis=-1, keepdims=True)
        ex = jnp.exp(logits - m)
        probs = ex / jnp.sum(ex, axis=-1, keepdims=True)
        valid = lane < N_EXPERTS
        p = jnp.where(valid, probs, -1.0)
        p1 = jnp.max(p, axis=-1, keepdims=True)
        i1 = jnp.min(jnp.where(p == p1, lane, ROUTER_LANES), axis=-1, keepdims=True)
        pr = jnp.where(lane == i1, -1.0, p)
        p2 = jnp.max(pr, axis=-1, keepdims=True)
        i2 = jnp.min(jnp.where(pr == p2, lane, ROUTER_LANES), axis=-1, keepdims=True)
        tot = p1 + p2
        comb = jnp.where(lane == i1, p1 / tot, 0.0) + jnp.where(lane == i2, p2 / tot, 0.0)
        comb_sc[...] = comb
        acc_sc[...] = jnp.zeros(acc_sc.shape, F32)
        y_sc[...] = jnp.zeros(y_sc.shape, BF16)
        routed = jnp.where(comb > 0.0, 1.0, 0.0).astype(BF16)
        q0 = lax.broadcasted_iota(jnp.int32, (ROUTER_LANES, ROUTER_LANES), 0)
        q1 = lax.broadcasted_iota(jnp.int32, (ROUTER_LANES, ROUTER_LANES), 1)
        routed_t = _dot_nt(jnp.where(q0 == q1, 1.0, 0.0).astype(BF16), routed)
        maskr_sc[...] = routed_t
        routed_tb = routed_t.astype(BF16)
        rc = min(MOE_ROW_CHUNK, tt)
        for c in range(tt // rc):
            r_i = lax.broadcasted_iota(jnp.int32, (rc, tt), 0) + c * rc
            c_i = lax.broadcasted_iota(jnp.int32, (rc, tt), 1)
            rankc_sc[c * rc:(c + 1) * rc, :] = _dot(jnp.where(c_i < r_i, 1.0, 0.0).astype(BF16), routed)
            r_j = lax.broadcasted_iota(jnp.int32, (tt, rc), 0)
            c_j = lax.broadcasted_iota(jnp.int32, (tt, rc), 1) + c * rc
            rankr_sc[:, c * rc:(c + 1) * rc] = _dot(routed_tb, jnp.where(r_j < c_j, 1.0, 0.0).astype(BF16))

    def expert(hb):
        g = _dot(hb, wgu_ref[:, :fe])
        u = _dot(hb, wgu_ref[:, fe:])
        act = (g * _sigmoid(g) * u).astype(BF16)
        return _dot(act, wd_ref[...])

    w_col = jnp.sum(jnp.where(lane == e, comb_sc[...], 0.0), axis=-1, keepdims=True)
    rk_col = jnp.sum(jnp.where(lane == e, rankc_sc[...], 0.0), axis=-1, keepdims=True).astype(jnp.int32)
    rk_row = rankr_sc[pl.ds(e, 1), :].astype(jnp.int32)
    mk_row = maskr_sc[pl.ds(e, 1), :]
    n_routed = jnp.sum(mk_row).astype(jnp.int32)
    half = tt // 2
    gb, sb = blocks

    def gather_block(j, carry):
        base = pl.multiple_of(j * gb, gb)
        sub = lax.broadcasted_iota(jnp.int32, (gb, tt), 0) + base
        pick = jnp.where((rk_row == sub) & (mk_row > 0.0), 1.0, 0.0).astype(BF16)
        y_sc[pl.ds(base, gb), :] = expert(_dot(pick, h_sc[...]).astype(BF16)).astype(BF16)
        return carry

    lax.fori_loop(0, (n_routed + gb - 1) // gb, gather_block, 0)

    for c in range(tt // sb):
        @pl.when(n_routed > c * sb)
        def _(c=c):
            for r0 in (0, half):
                rows = slice(r0, r0 + half)
                ln_i = lax.broadcasted_iota(jnp.int32, (half, sb), 1) + c * sb
                put = jnp.where((rk_col[rows] == ln_i) & (w_col[rows] > 0.0), 1.0, 0.0).astype(BF16)
                acc_sc[rows, :] += w_col[rows] * _dot(put, y_sc[c * sb:(c + 1) * sb, :])

    @pl.when(e == N_EXPERTS - 1)
    def _():
        gate = mod_ref[5:6, :]
        o_ref[...] = _layer_norm(ALPHA * x_ref[...] + gate * acc_sc[...], lng_ref[...], lnb_ref[...])


def moe_ffn_layer(x, mod, w_router, b_router, w_gu, w_down, ln_g, ln_b, tt=1024, blocks=MOE_ROW_BLOCKS):
    bsz, seq, d = x.shape
    tt = min(tt, seq)
    n_e, fe = w_down.shape[0], w_down.shape[1]
    return pl.pallas_call(
        functools.partial(_moe_kernel, fe=fe, blocks=tuple(min(b, tt) for b in blocks)),
        grid=(bsz, seq // tt, n_e),
        in_specs=[pl.BlockSpec((None, tt, d), lambda b, t, e: (b, t, 0), pipeline_mode=pl.Buffered(1)),
                  pl.BlockSpec((None, 6, d), lambda b, t, e: (b, 0, 0)),
                  _const_spec((d, ROUTER_LANES)), _const_spec((1, ROUTER_LANES)),
                  pl.BlockSpec((None, d, 2 * fe), lambda b, t, e: (e, 0, 0)),
                  pl.BlockSpec((None, fe, d), lambda b, t, e: (e, 0, 0)),
                  _const_spec((1, d)), _const_spec((1, d))],
        out_specs=pl.BlockSpec((None, tt, d), lambda b, t, e: (b, t, 0)),
        out_shape=jax.ShapeDtypeStruct(x.shape, F32),
        scratch_shapes=[pltpu.VMEM((tt, d), BF16), pltpu.VMEM((tt, ROUTER_LANES), F32),
                        pltpu.VMEM((tt, d), F32), pltpu.VMEM((tt, ROUTER_LANES), F32),
                        pltpu.VMEM((ROUTER_LANES, tt), F32), pltpu.VMEM((ROUTER_LANES, tt), F32),
                        pltpu.VMEM((tt, d), BF16)],
        compiler_params=pltpu.CompilerParams(
            dimension_semantics=("parallel", "arbitrary", "arbitrary"), vmem_limit_bytes=MOE_VMEM_LIMIT),
        name="moe_ffn",
    )(x, mod, w_router, b_router, w_gu, w_down, ln_g, ln_b)


def _pad_router(w_router, b_router):
    d, n_e = w_router.shape
    w = jnp.zeros((d, ROUTER_LANES), F32).at[:, :n_e].set(w_router)
    b = jnp.full((1, ROUTER_LANES), NEG_INF, F32).at[0, :n_e].set(b_router)
    return w, b


def _qkv_kernel(x_ref, mod_ref, w_ref, q_ref, k_ref, v_ref, *, d):
    sh, sc = mod_ref[0:1, :], mod_ref[1:2, :]
    h = (x_ref[...] * (1.0 + sc) + sh).astype(BF16)
    p = _dot(h, w_ref[...])
    q_ref[...] = (p[:, :d] * (NA_HEAD_DIM ** -0.5)).astype(BF16)
    k_ref[...] = p[:, d:2 * d].astype(BF16)
    v_ref[...] = p[:, 2 * d:].astype(BF16)


def _na_kernel(q_ref, k_ref, v_ref, bias_ref, o_ref, *, rows):
    r = pl.program_id(1)
    r0 = jnp.clip(r - NA_WIN_ROWS // 2, 0, rows - NA_WIN_ROWS)
    start = pl.multiple_of(r0 * GRID_W, GRID_W)
    nk = NA_WIN_ROWS * GRID_W
    pw_ = 2 * NA_HEAD_DIM
    pairs = range(NA_HEADS // 2)
    lanes = [slice(p * pw_, (p + 1) * pw_) for p in pairs]
    head0 = lax.broadcasted_iota(jnp.int32, (GRID_W, pw_), 1) < NA_HEAD_DIM
    q2 = {}
    for p in pairs:
        q = q_ref[:, lanes[p]]
        z = jnp.zeros_like(q)
        q2[p] = jnp.concatenate([jnp.where(head0, q, z), jnp.where(head0, z, q)], axis=0)
    s = {p: _dot_nt(q2[p], k_ref[pl.ds(start, nk), lanes[p]])
         + bias_ref[2 * p:2 * p + 2].reshape(2 * GRID_W, nk) for p in pairs}
    m = {p: jnp.max(s[p], axis=-1, keepdims=True) for p in pairs}
    e = {p: jnp.exp(s[p] - m[p]) for p in pairs}
    l = {p: jnp.sum(e[p], axis=-1, keepdims=True) for p in pairs}
    o = {p: _dot(e[p].astype(BF16), v_ref[pl.ds(start, nk), lanes[p]]) / l[p] for p in pairs}
    for p in pairs:
        o_ref[:, lanes[p]] = jnp.where(head0, o[p][:GRID_W], o[p][GRID_W:]).astype(BF16)


def _na_bias_table(rpb):
    nh, nr, nc = rpb.shape
    w = GRID_W
    qc = jnp.arange(w)[:, None]
    kc = jnp.arange(w)[None, :]
    cs = jnp.clip(qc - NA_WIN_COLS // 2, 0, w - NA_WIN_COLS)
    col_ok = (kc >= cs) & (kc < cs + NA_WIN_COLS)
    lo = w - NA_WIN_COLS
    e = jnp.pad(rpb, ((0, 0), (0, 0), (lo, 2 * w - lo - nc)))
    flat = jnp.tile(e, (1, 1, w))
    toep = flat[:, :, w - 1:w - 1 + w * (2 * w - 1)].reshape(nh, nr, w, 2 * w - 1)[..., :w]
    toep = jnp.where(col_ok[None, None], toep, NEG_INF)
    tab = jnp.stack([toep[:, dl:dl + NA_WIN_ROWS] for dl in range(NA_WIN_ROWS)], axis=1)
    tab = jnp.transpose(tab, (0, 1, 3, 2, 4))
    return tab.reshape(nh, NA_WIN_ROWS, w, NA_WIN_ROWS * w)


def _proj_ln_kernel(*refs, n_in):
    a_refs = refs[:n_in]
    x_ref, mod_ref, w_ref, lng_ref, lnb_ref, o_ref = refs[n_in:]
    a = a_refs[0][...]
    if n_in == 2:
        a = (a.astype(F32) + a_refs[1][...].astype(F32)).astype(BF16)
    mix = _dot(a, w_ref[...])
    gate = mod_ref[2:3, :]
    o_ref[...] = _layer_norm(ALPHA * x_ref[...] + gate * mix, lng_ref[...], lnb_ref[...])


def proj_residual_ln(acts, x, mod, w_o, ln_g, ln_b, tt=512):
    bsz, seq, d = x.shape
    tt = min(tt, seq)
    tile = pl.BlockSpec((None, tt, d), lambda b, t: (b, t, 0))
    return pl.pallas_call(
        functools.partial(_proj_ln_kernel, n_in=len(acts)),
        grid=(bsz, seq // tt),
        in_specs=[tile] * len(acts) + [tile, _mod_spec(d), _const_spec((d, d)),
                                       _const_spec((1, d)), _const_spec((1, d))],
        out_specs=tile,
        out_shape=jax.ShapeDtypeStruct(x.shape, F32),
        compiler_params=_cparams("parallel", "arbitrary"),
        name="proj_residual_ln",
    )(*acts, x, mod, w_o, ln_g, ln_b)


def na_mixer_layer(x, mod, w_qkv, bias_tab, w_o, ln_g, ln_b, tt=512):
    bsz, seq, d = x.shape
    tt = min(tt, seq)
    rows = seq // GRID_W
    assert rows >= NA_WIN_ROWS and seq % GRID_W == 0
    tile = pl.BlockSpec((None, tt, d), lambda b, t: (b, t, 0))
    q, k, v = pl.pallas_call(
        functools.partial(_qkv_kernel, d=d),
        grid=(bsz, seq // tt),
        in_specs=[tile, _mod_spec(d), _const_spec((d, 3 * d))],
        out_specs=[tile] * 3,
        out_shape=[jax.ShapeDtypeStruct(x.shape, BF16)] * 3,
        compiler_params=_cparams("parallel", "arbitrary"),
        name="na_qkv",
    )(x, mod, w_qkv)

    def delta(r):
        return jnp.clip(r - NA_WIN_ROWS // 2, 0, rows - NA_WIN_ROWS) - r + NA_WIN_ROWS - 1

    row_spec = pl.BlockSpec((None, GRID_W, d), lambda b, r: (b, r, 0))
    seq_spec = pl.BlockSpec((None, seq, d), lambda b, r: (b, 0, 0))
    att = pl.pallas_call(
        functools.partial(_na_kernel, rows=rows),
        grid=(bsz, rows),
        in_specs=[row_spec, seq_spec, seq_spec,
                  pl.BlockSpec((NA_HEADS, None, GRID_W, NA_WIN_ROWS * GRID_W),
                               lambda b, r: (0, delta(r), 0, 0))],
        out_specs=row_spec,
        out_shape=jax.ShapeDtypeStruct(x.shape, BF16),
        compiler_params=_cparams("parallel", "arbitrary"),
        name="na_attention",
    )(q, k, v, bias_tab)
    return proj_residual_ln([att], x, mod, w_o, ln_g, ln_b, tt)


def _rk_proj_kernel(x_ref, xp_ref, xn_ref, mod_ref, mu_ref, wrkv_ref, w1_ref, w2_ref, a1_ref, a2_ref,
                    g1_ref, g2_ref, w0_ref, a0_ref, kk_ref, ka_ref,
                    r_o, v_o, kk_o, lw_o, km_o, b_o, gate_o, *, nt, tt):
    t = pl.program_id(1)
    sh, sc = mod_ref[0:1, :], mod_ref[1:2, :]
    h = x_ref[...] * (1.0 + sc) + sh
    h_prev = jnp.where(t > 0, xp_ref[7:8, :] * (1.0 + sc) + sh, 0.0)
    h_next = jnp.where(t < nt - 1, xn_ref[0:1, :] * (1.0 + sc) + sh, 0.0)
    h_m1, h_p1 = _shifted(h, h_prev, h_next, tt)
    xx = 0.5 * (h_m1 + h_p1) - h

    h_b, xx_b, mu_b = h.astype(BF16), xx.astype(BF16), mu_ref[...].astype(BF16)

    def mixed(p):
        return h_b + xx_b * mu_b[p:p + 1, :]

    r = _dot(mixed(0), wrkv_ref[0])
    k = _dot(mixed(1), wrkv_ref[1])
    v = _dot(mixed(2), wrkv_ref[2])
    r_o[...] = r.astype(BF16)
    v_o[...] = v.astype(BF16)
    lw_in = jnp.tanh(_dot(mixed(3), w1_ref[...])).astype(BF16)
    a_in = _dot(mixed(4), a1_ref[...]).astype(BF16)
    g_in = _sigmoid(_dot(mixed(5), g1_ref[...])).astype(BF16)

    kk = k * kk_ref[...]
    kk_o[...] = kk.astype(BF16)
    rw = w1_ref.shape[1] // 2
    rg = g1_ref.shape[1] // 2
    for z in range(2):
        wl = w0_ref[z:z + 1, :] + _dot(lw_in[:, z * rw:(z + 1) * rw], w2_ref[z])
        lw_o[z] = (-RK_DECAY_SCALE) * _sigmoid(wl)
        a_lr = _sigmoid(a0_ref[z:z + 1, :] + _dot(a_in[:, z * rw:(z + 1) * rw], a2_ref[z]))
        km_o[z] = (k * (1.0 + (a_lr - 1.0) * ka_ref[...])).astype(BF16)
        b_o[z] = (kk * a_lr).astype(BF16)
        gate_o[z] = _dot(g_in[:, z * rg:(z + 1) * rg], g2_ref[z]).astype(BF16)


def _head_sum(x):
    head0 = lax.broadcasted_iota(jnp.int32, x.shape, 1) < RK_HEAD_DIM
    s0 = jnp.sum(jnp.where(head0, x, 0.0), axis=-1, keepdims=True)
    s1 = jnp.sum(jnp.where(head0, 0.0, x), axis=-1, keepdims=True)
    return jnp.where(head0, s0, s1)


def _rk_scan_kernel(r_ref, v_ref, kk_ref, lw_ref, km_ref, b_ref, gate_ref, rk_ref, lg_ref, lb_ref,
                    o_ref, s_ref, *, reverse):
    tb, hb = r_ref.shape
    cl = RK_CHUNK
    nc, nh = tb // cl, hb // RK_HEAD_DIM

    @pl.when(pl.program_id(2) == 0)
    def _():
        s_ref[...] = jnp.zeros(s_ref.shape, F32)

    ri = lax.broadcasted_iota(jnp.int32, (tb, tb), 0)
    ci = lax.broadcasted_iota(jnp.int32, (tb, tb), 1)
    ordered = (ci >= ri) if reverse else (ci <= ri)
    tri = jnp.where((ri // cl == ci // cl) & ordered, 1.0, 0.0).astype(BF16)
    lw = lw_ref[...]
    lw_hi = lw.astype(BF16)
    lw_lo = (lw - lw_hi.astype(F32)).astype(BF16)
    cum = _dot(tri, lw_hi) + _dot(tri, lw_lo)
    last = 0 if reverse else cl - 1
    tot_rows = [cum[c * cl + last:c * cl + last + 1, :] for c in range(nc)]
    tot = jnp.concatenate([jnp.broadcast_to(tr, (cl, hb)) for tr in tot_rows], axis=0)

    rr = r_ref[...].astype(F32)
    vv = v_ref[...]
    kk = kk_ref[...].astype(F32)
    pair = 2 * RK_HEAD_DIM
    rnorm = jnp.concatenate(
        [1.0 / jnp.maximum(jnp.sqrt(_head_sum(jnp.square(kk[:, i:i + pair]))), 1e-12) for i in range(0, hb, pair)],
        axis=1)
    kkn = kk * rnorm
    km = km_ref[...].astype(F32)
    bb = b_ref[...].astype(F32) * rnorm
    e_neg = jnp.exp(-cum)
    e_rem = jnp.exp(tot - cum)
    a_t = (-kkn * jnp.exp(cum - lw)).astype(BF16)
    r_t = (rr * jnp.exp(cum)).astype(BF16)
    b_t = (bb * e_neg).astype(BF16)
    k_t = (km * e_neg).astype(BF16)
    b_h = (bb * e_rem).astype(BF16)
    k_h = (km * e_rem).astype(BF16)
    bonus_w = rr * km * rk_ref[...]

    pw_ = 2 * RK_HEAD_DIM
    npair = hb // pw_
    t_i = lax.broadcasted_iota(jnp.int32, (cl, pw_), 0)
    s_i = lax.broadcasted_iota(jnp.int32, (cl, pw_), 1) % cl
    strict = (t_i < s_i) if reverse else (t_i > s_i)
    incl = (t_i <= s_i) if reverse else (t_i >= s_i)
    eye = jnp.where(t_i == s_i, 1.0, 0.0)
    head0 = lax.broadcasted_iota(jnp.int32, (cl, pw_), 1) < RK_HEAD_DIM
    q0 = lax.broadcasted_iota(jnp.int32, (pw_, pw_), 0)
    q1 = lax.broadcasted_iota(jnp.int32, (pw_, pw_), 1)
    same_head = (q0 // RK_HEAD_DIM) == (q1 // RK_HEAD_DIM)

    head_sum = _head_sum

    def bd(x):
        z = jnp.zeros_like(x)
        return jnp.concatenate([jnp.where(head0, x, z), jnp.where(head0, z, x)], axis=0)

    keys = [(c, p) for c in range(nc) for p in range(npair)]
    rsl = {k: slice(k[0] * cl, (k[0] + 1) * cl) for k in keys}
    lsl = {k: slice(k[1] * pw_, (k[1] + 1) * pw_) for k in keys}
    blk = lambda arr, k: arr[rsl[k], lsl[k]]

    gram = {k: _dot_nt(jnp.concatenate([blk(a_t, k), blk(r_t, k)], axis=0),
                       jnp.concatenate([bd(blk(b_t, k)), bd(blk(k_t, k))], axis=0)) for k in keys}
    m_ab = {k: jnp.where(strict, gram[k][:cl, :pw_], 0.0) for k in keys}
    m_ak = {k: jnp.where(strict, gram[k][:cl, pw_:], 0.0).astype(BF16) for k in keys}
    m_rb = {k: jnp.where(incl, gram[k][cl:, :pw_], 0.0).astype(BF16) for k in keys}
    m_rk = {k: jnp.where(incl, gram[k][cl:, pw_:], 0.0).astype(BF16) for k in keys}
    vbd = {k: bd(blk(vv, k)) for k in keys}
    my = {k: _dot(jnp.concatenate([m_ak[k], m_rk[k]], axis=0), vbd[k]) for k in keys}
    v_t = {k: jnp.transpose(blk(vv, k).astype(F32)).astype(BF16) for k in keys}
    vk = {k: jnp.where(same_head, _dot(v_t[k], blk(k_h, k)), 0.0) for k in keys}
    inv = {k: eye + m_ab[k] for k in keys}
    pwb = {k: m_ab[k].astype(BF16) for k in keys}
    pw = {k: _dot(pwb[k], bd(pwb[k])) for k in keys}
    n = 4
    while n < cl:
        pwb = {k: pw[k].astype(BF16) for k in keys}
        res = {k: _dot(pwb[k], jnp.concatenate([bd(pwb[k]), bd(inv[k].astype(BF16))], axis=1)) for k in keys}
        pw = {k: res[k][:, :pw_] for k in keys}
        inv = {k: inv[k] + res[k][:, pw_:] for k in keys}
        n *= 2
    inv = {k: (inv[k] + _dot(pw[k].astype(BF16), bd(inv[k].astype(BF16)))).astype(BF16) for k in keys}
    au = {k: _dot(inv[k], jnp.concatenate([bd(blk(a_t, k)), bd(my[k][:cl].astype(BF16))], axis=1))
          for k in keys}
    a_hat = {k: au[k][:, :pw_].astype(BF16) for k in keys}
    u0_t = {k: jnp.transpose(au[k][:, pw_:]) for k in keys}
    ar = {k: jnp.concatenate([a_hat[k], blk(r_t, k)], axis=0) for k in keys}

    pairs = range(npair)
    for c in (range(nc - 1, -1, -1) if reverse else range(nc)):
        st = {p: s_ref[p] for p in pairs}
        stb = {p: st[p].astype(BF16) for p in pairs}
        res = {p: _dot_nt(ar[c, p], stb[p]) for p in pairs}
        res_t = {p: _dot_nt(stb[p], a_hat[c, p]) for p in pairs}
        ub = {p: (au[c, p][:, pw_:] + res[p][:cl]).astype(BF16) for p in pairs}
        ub_t = {p: (u0_t[c, p] + res_t[p]).astype(BF16) for p in pairs}
        y = {p: my[c, p][cl:] + res[p][cl:] + _dot(m_rb[c, p], bd(ub[p])) for p in pairs}
        for p in pairs:
            k = (c, p)
            decay = jnp.exp(tot_rows[c][:, lsl[k]])
            s_ref[p] = jnp.where(same_head, st[p] * decay + _dot(ub_t[p], blk(b_h, k)), 0.0) + vk[k]
        dy = {p: y[p] - head_sum(y[p]) * (1.0 / RK_HEAD_DIM) for p in pairs}
        var = {p: head_sum(dy[p] * dy[p]) * (1.0 / RK_HEAD_DIM) for p in pairs}
        for p in pairs:
            k = (c, p)
            yn = dy[p] * lax.rsqrt(var[p] + RK_GN_EPS) * lg_ref[:, lsl[k]] + lb_ref[:, lsl[k]]
            bonus = head_sum(blk(bonus_w, k)) * blk(vv, k).astype(F32)
            o_ref[rsl[k], lsl[k]] = (blk(gate_ref, k).astype(F32) * (yn + bonus)).astype(BF16)


def rwkv_mixer_layer(x, mod, p, ln_g, ln_b, tt=256):
    bsz, seq, d = x.shape
    tt = min(tt, seq)
    nt = seq // tt
    x_spec, prev_spec, next_spec = _tile_specs(seq, tt, d)
    tile = pl.BlockSpec((None, tt, d), lambda b, t: (b, t, 0))
    tile2 = pl.BlockSpec((2, None, tt, d), lambda b, t: (0, b, t, 0))
    rw, rg = p['w1'].shape[1] // 2, p['g1'].shape[1] // 2
    sh_bf = jax.ShapeDtypeStruct(x.shape, BF16)
    sh2_bf = jax.ShapeDtypeStruct((2,) + x.shape, BF16)
    sh2_f32 = jax.ShapeDtypeStruct((2,) + x.shape, F32)
    r, v, kkn, lw, km, bz, gate = pl.pallas_call(
        functools.partial(_rk_proj_kernel, nt=nt, tt=tt),
        grid=(bsz, nt),
        in_specs=[x_spec, prev_spec, next_spec, _mod_spec(d), _const_spec((6, d)),
                  _const_spec((3, d, d)),
                  _const_spec((d, 2 * rw)), _const_spec((2, rw, d)),
                  _const_spec((d, 2 * rw)), _const_spec((2, rw, d)),
                  _const_spec((d, 2 * rg)), _const_spec((2, rg, d)),
                  _const_spec((2, d)), _const_spec((2, d)), _const_spec((1, d)), _const_spec((1, d))],
        out_specs=[tile, tile, tile, tile2, tile2, tile2, tile2],
        out_shape=[sh_bf, sh_bf, sh_bf, sh2_f32, sh2_bf, sh2_bf, sh2_bf],
        compiler_params=_cparams("parallel", "arbitrary"),
        name="rwkv_proj",
    )(x, x, x, mod, p['mu'], p['w_rkv'], p['w1'], p['w2'], p['a1'], p['a2'], p['g1'], p['g2'],
      p['w0'], p['a0'], p['k_k'], p['k_a'])

    tb = min(RK_TOK_BLOCK, seq)
    hb = RK_LANE_BLOCK
    nb = seq // tb
    outs = []
    for z in range(2):
        rev = z == 1

        def tok(i, rev=rev):
            return nb - 1 - i if rev else i

        blk = pl.BlockSpec((None, tb, hb), lambda b, g, i: (b, tok(i), g))
        blk2 = pl.BlockSpec((None, None, tb, hb), lambda b, g, i, z=z: (z, b, tok(i), g))
        vec = pl.BlockSpec((1, hb), lambda b, g, i: (0, g))
        vec2 = pl.BlockSpec((None, 1, hb), lambda b, g, i, z=z: (z, 0, g))
        outs.append(pl.pallas_call(
            functools.partial(_rk_scan_kernel, reverse=rev),
            grid=(bsz, d // hb, nb),
            in_specs=[blk, blk, blk, blk2, blk2, blk2, blk2, vec, vec2, vec2],
            out_specs=blk,
            out_shape=sh_bf,
            scratch_shapes=[pltpu.VMEM((hb // (2 * RK_HEAD_DIM), 2 * RK_HEAD_DIM, 2 * RK_HEAD_DIM), F32)],
            compiler_params=_cparams("parallel", "parallel", "arbitrary"),
            name="rwkv_scan_rev" if rev else "rwkv_scan_fwd",
        )(r, v, kkn, lw, km, bz, gate, p['r_k'], p['lnx_g'], p['lnx_b']))
    return proj_residual_ln(outs, x, mod, p['w_o'], ln_g, ln_b)


def _prepare_params(w):
    d = D_MODEL
    bf = lambda a: a.astype(BF16)
    row = lambda a: a.reshape(1, d)
    lane = jnp.arange(d) // RK_HEAD_DIM
    p = {
        'w_ada': w['w_ada'], 'b_ada': w['b_ada'],
        'ln_g': w['ln_g'], 'ln_b': w['ln_b'],
        'conv_w_in': bf(w['conv_w_in']), 'conv_w': w['conv_w'], 'conv_w_out': bf(w['conv_w_out']),
        'na_w_qkv': bf(w['na_w_qkv']), 'na_w_o': bf(w['na_w_o']),
        'na_bias': [_na_bias_table(w['na_rpb'][j]) for j in range(w['na_rpb'].shape[0])],
        'ffn_w_gu': bf(w['ffn_w_gu']), 'ffn_w_down': bf(w['ffn_w_down']),
        'moe_router': [_pad_router(w['moe_w_router'][j], w['moe_b_router'][j])
                       for j in range(w['moe_w_router'].shape[0])],
        'moe_w_gu': bf(w['moe_w_gu']), 'moe_w_down': bf(w['moe_w_down']),
        'rk': [],
    }
    cat = lambda a: jnp.concatenate([a[0], a[1]], axis=1)
    for j in range(w['rk_mu'].shape[0]):
        p['rk'].append({
            'mu': w['rk_mu'][j], 'w_rkv': bf(w['rk_w_rkv'][j]),
            'w1': bf(cat(w['rk_w1'][j])), 'w2': bf(w['rk_w2'][j]),
            'a1': bf(cat(w['rk_a1'][j])), 'a2': bf(w['rk_a2'][j]),
            'g1': bf(cat(w['rk_g1'][j])), 'g2': bf(w['rk_g2'][j]),
            'w0': w['rk_w0'][j], 'a0': w['rk_a0'][j],
            'k_k': row(w['rk_k_k'][j]), 'k_a': row(w['rk_k_a'][j]),
            'r_k': w['rk_r_k'][j].reshape(1, d),
            'lnx_g': w['rk_lnx_g'][j].reshape(2, 1, d), 'lnx_b': w['rk_lnx_b'][j].reshape(2, 1, d),
            'w_o': bf(w['rk_w_o'][j]),
        })
    return p


def _trunk(x, mod_all, p):
    d = D_MODEL
    for i in range(DEPTH):
        mod = mod_all[i]
        lng = lambda s: p['ln_g'][i, s].reshape(1, d)
        lnb = lambda s: p['ln_b'][i, s].reshape(1, d)
        kind, j = i % 3, i // 3
        if kind == 0:
            x = conv_mixer_layer(x, mod, p['conv_w_in'][j], p['conv_w'][j], p['conv_w_out'][j],
                                 lng(0), lnb(0))
        elif kind == 1:
            x = na_mixer_layer(x, mod, p['na_w_qkv'][j], p['na_bias'][j], p['na_w_o'][j], lng(0), lnb(0))
        else:
            x = rwkv_mixer_layer(x, mod, p['rk'][j], lng(0), lnb(0))
        if i % 2 == 0:
            x = dense_ffn_layer(x, mod, p['ffn_w_gu'][i // 2], p['ffn_w_down'][i // 2], lng(1), lnb(1))
        else:
            wr, br = p['moe_router'][i // 2]
            x = moe_ffn_layer(x, mod, wr, br, p['moe_w_gu'][i // 2], p['moe_w_down'][i // 2],
                              lng(1), lnb(1))
    return x


def kernel(x_prompt, x_sample, c_prompt, c_sample, w_ada, b_ada, ln_g, ln_b, conv_w_in, conv_w, conv_w_out, na_w_qkv, na_rpb, na_w_o, rk_mu, rk_w_rkv, rk_w0, rk_w1, rk_w2, rk_a0, rk_a1, rk_a2, rk_g1, rk_g2, rk_k_k, rk_k_a, rk_r_k, rk_lnx_g, rk_lnx_b, rk_w_o, ffn_w_gu, ffn_w_down, moe_w_router, moe_b_router, moe_w_gu, moe_w_down):
    p = _prepare_params(dict(
        w_ada=w_ada, b_ada=b_ada, ln_g=ln_g, ln_b=ln_b,
        conv_w_in=conv_w_in, conv_w=conv_w, conv_w_out=conv_w_out,
        na_w_qkv=na_w_qkv, na_rpb=na_rpb, na_w_o=na_w_o,
        rk_mu=rk_mu, rk_w_rkv=rk_w_rkv, rk_w0=rk_w0, rk_w1=rk_w1, rk_w2=rk_w2,
        rk_a0=rk_a0, rk_a1=rk_a1, rk_a2=rk_a2, rk_g1=rk_g1, rk_g2=rk_g2,
        rk_k_k=rk_k_k, rk_k_a=rk_k_a, rk_r_k=rk_r_k, rk_lnx_g=rk_lnx_g, rk_lnx_b=rk_lnx_b,
        rk_w_o=rk_w_o, ffn_w_gu=ffn_w_gu, ffn_w_down=ffn_w_down,
        moe_w_router=moe_w_router, moe_b_router=moe_b_router,
        moe_w_gu=moe_w_gu, moe_w_down=moe_w_down))
    n_p = c_prompt.shape[0]
    mod_all = ada_modulation(jnp.concatenate([c_prompt, c_sample], axis=0), p['w_ada'], p['b_ada'])
    return (_trunk(x_prompt, mod_all[:, :n_p], p), _trunk(x_sample, mod_all[:, n_p:], p))
```

```python
import functools

import jax
import jax.numpy as jnp
from jax import lax
from jax.experimental import pallas as pl
from jax.experimental.pallas import tpu as pltpu

F32 = jnp.float32
BF16 = jnp.bfloat16

D_MODEL = 1024
DEPTH = 4
ALPHA = (2 * DEPTH) ** 0.25
LN_EPS = 1e-5

GRID_W = 64
NA_HEADS = 16
NA_HEAD_DIM = D_MODEL // NA_HEADS
NA_WIN_ROWS = 8
NA_WIN_COLS = 16
NEG_INF = -1e30

RK_HEAD_DIM = 64
RK_GN_EPS = 64e-5
RK_DECAY_SCALE = 0.6065306597126334
RK_CHUNK = 64
RK_TOK_BLOCK = 256
RK_LANE_BLOCK = 1024

N_EXPERTS = 8
ROUTER_LANES = 128

SUBLANES = 8
VMEM_LIMIT = 52 * 1024 * 1024
MOE_VMEM_LIMIT = 58 * 1024 * 1024


def _cparams(*sem):
    return pltpu.CompilerParams(dimension_semantics=sem, vmem_limit_bytes=VMEM_LIMIT)


def _const_spec(shape):
    nd = len(shape)
    return pl.BlockSpec(shape, lambda *_: (0,) * nd, pipeline_mode=pl.Buffered(1))


def _dot(a, b):
    return jnp.dot(a, b, preferred_element_type=F32)


def _dot_nt(a, b):
    return lax.dot_general(a, b, (((1,), (1,)), ((), ())), preferred_element_type=F32)


def _sigmoid(x):
    return 1.0 / (1.0 + jnp.exp(-x))


def _layer_norm(y, g, b):
    mu = jnp.mean(y, axis=-1, keepdims=True)
    d = y - mu
    var = jnp.mean(d * d, axis=-1, keepdims=True)
    return d * lax.rsqrt(var + LN_EPS) * g + b


def _ada_kernel(c_ref, w_ref, b_ref, o_ref):
    c = c_ref[...]
    s = (c * _sigmoid(c)).astype(BF16)
    o_ref[...] = _dot(s, w_ref[...].astype(BF16)) + b_ref[...]


def ada_modulation(c, w_ada, b_ada):
    bsz, d = c.shape
    depth = w_ada.shape[0]
    out = pl.pallas_call(
        _ada_kernel,
        grid=(depth, 6),
        in_specs=[
            pl.BlockSpec((bsz, d), lambda i, j: (0, 0)),
            pl.BlockSpec((None, d, d), lambda i, j: (i, 0, j)),
            pl.BlockSpec((None, None, 1, d), lambda i, j: (i, j, 0, 0)),
        ],
        out_specs=pl.BlockSpec((None, None, bsz, d), lambda i, j: (i, j, 0, 0)),
        out_shape=jax.ShapeDtypeStruct((depth, 6, bsz, d), F32),
        compiler_params=_cparams("arbitrary", "arbitrary"),
        name="ada_modulation",
    )(c, w_ada, b_ada.reshape(depth, 6, 1, d))
    return jnp.transpose(out, (0, 2, 1, 3))


def _tile_specs(seq, tt, d):
    nb8 = seq // SUBLANES
    per = tt // SUBLANES
    x_spec = pl.BlockSpec((None, tt, d), lambda b, t: (b, t, 0))
    prev_spec = pl.BlockSpec((None, SUBLANES, d), lambda b, t: (b, jnp.maximum(t * per - 1, 0), 0))
    next_spec = pl.BlockSpec((None, SUBLANES, d), lambda b, t: (b, jnp.minimum((t + 1) * per, nb8 - 1), 0))
    return x_spec, prev_spec, next_spec


def _mod_spec(d):
    return pl.BlockSpec((None, 6, d), lambda b, t: (b, 0, 0))


def _shifted(cur, prev_row, next_row, tt):
    row = lax.broadcasted_iota(jnp.int32, (tt, 1), 0)
    m1 = jnp.where(row == 0, prev_row, pltpu.roll(cur, 1, 0))
    p1 = jnp.where(row == tt - 1, next_row, pltpu.roll(cur, tt - 1, 0))
    return m1, p1


def _conv_kernel(x_ref, xp_ref, xn_ref, mod_ref, win_ref, cw_ref, wout_ref, lng_ref, lnb_ref,
                 o_ref, *, nt, tt, d):
    t = pl.program_id(1)
    sh, sc, gate = mod_ref[0:1, :], mod_ref[1:2, :], mod_ref[2:3, :]
    parts = [slice(0, tt // 2), slice(tt // 2, tt)]
    xs = [x_ref[r, :] for r in parts]
    ps = [_dot((x * (1.0 + sc) + sh).astype(BF16), win_ref[...]) for x in xs]
    z = jnp.concatenate([p[:, d:2 * d] * p[:, 2 * d:] for p in ps], axis=0)
    halo = jnp.concatenate([xp_ref[...], xn_ref[...]], axis=0)
    hh = (halo * (1.0 + sc) + sh).astype(BF16)
    ph = _dot(hh, win_ref[:, d:])
    zh = ph[:, :d] * ph[:, d:]
    z_prev = jnp.where(t > 0, zh[SUBLANES - 1:SUBLANES, :], 0.0)
    z_next = jnp.where(t < nt - 1, zh[SUBLANES:SUBLANES + 1, :], 0.0)
    z_m1, z_p1 = _shifted(z, z_prev, z_next, tt)
    conv = z_m1 * cw_ref[0:1, :] + z * cw_ref[1:2, :] + z_p1 * cw_ref[2:3, :]
    mixes = [_dot((p[:, :d] * conv[r]).astype(BF16), wout_ref[...]) for p, r in zip(ps, parts)]
    for x, r, mix in zip(xs, parts, mixes):
        o_ref[r, :] = _layer_norm(ALPHA * x + gate * mix, lng_ref[...], lnb_ref[...])


def conv_mixer_layer(x, mod, w_in, conv_w, w_out, ln_g, ln_b, tt=512):
    bsz, seq, d = x.shape
    tt = min(tt, seq)
    nt = seq // tt
    x_spec, prev_spec, next_spec = _tile_specs(seq, tt, d)
    return pl.pallas_call(
        functools.partial(_conv_kernel, nt=nt, tt=tt, d=d),
        grid=(bsz, nt),
        in_specs=[x_spec, prev_spec, next_spec, _mod_spec(d),
                  _const_spec((d, 3 * d)), _const_spec((3, d)), _const_spec((d, d)),
                  _const_spec((1, d)), _const_spec((1, d))],
        out_specs=pl.BlockSpec((None, tt, d), lambda b, t: (b, t, 0)),
        out_shape=jax.ShapeDtypeStruct(x.shape, F32),
        compiler_params=_cparams("parallel", "arbitrary"),
        name="conv_mixer",
    )(x, x, x, mod, w_in, conv_w, w_out, ln_g, ln_b)


def _ffn_kernel(x_ref, mod_ref, wgu_ref, wd_ref, lng_ref, lnb_ref, o_ref, *, ff, fc):
    sh, sc, gate = mod_ref[3:4, :], mod_ref[4:5, :], mod_ref[5:6, :]
    tt = x_ref.shape[0]
    parts = [slice(0, tt // 2), slice(tt // 2, tt)]
    chunks = range(ff // fc)
    xs = [x_ref[r, :] for r in parts]
    hs = [(x * (1.0 + sc) + sh).astype(BF16) for x in xs]
    gs = [[_dot(h, wgu_ref[:, c * fc:(c + 1) * fc]) for c in chunks] for h in hs]
    us = [[_dot(h, wgu_ref[:, ff + c * fc:ff + (c + 1) * fc]) for c in chunks] for h in hs]
    acts = [[(g * _sigmoid(g) * u).astype(BF16) for g, u in zip(gr, ur)] for gr, ur in zip(gs, us)]
    for x, r, ar in zip(xs, parts, acts):
        acc = _dot(ar[0], wd_ref[0:fc, :])
        for c in chunks[1:]:
            acc = acc + _dot(ar[c], wd_ref[c * fc:(c + 1) * fc, :])
        o_ref[r, :] = _layer_norm(ALPHA * x + gate * acc, lng_ref[...], lnb_ref[...])


def dense_ffn_layer(x, mod, w_gu, w_down, ln_g, ln_b, tt=512):
    bsz, seq, d = x.shape
    tt = min(tt, seq)
    ff = w_down.shape[0]
    fc = ff // 2
    return pl.pallas_call(
        functools.partial(_ffn_kernel, ff=ff, fc=fc),
        grid=(bsz, seq // tt),
        in_specs=[pl.BlockSpec((None, tt, d), lambda b, t: (b, t, 0)), _mod_spec(d),
                  _const_spec((d, 2 * ff)), _const_spec((ff, d)),
                  _const_spec((1, d)), _const_spec((1, d))],
        out_specs=pl.BlockSpec((None, tt, d), lambda b, t: (b, t, 0)),
        out_shape=jax.ShapeDtypeStruct(x.shape, F32),
        compiler_params=_cparams("parallel", "arbitrary"),
        name="dense_ffn",
    )(x, mod, w_gu, w_down, ln_g, ln_b)


MOE_ROW_CHUNK = 256
MOE_ROW_BLOCKS = (128, 256)


def _moe_kernel(x_ref, mod_ref, wr_ref, br_ref, wgu_ref, wd_ref, lng_ref, lnb_ref, o_ref,
                h_sc, comb_sc, acc_sc, rankc_sc, rankr_sc, maskr_sc, y_sc, *, fe, blocks):
    e = pl.program_id(2)
    tt = comb_sc.shape[0]
    lane = lax.broadcasted_iota(jnp.int32, comb_sc.shape, 1)

    @pl.when(e == 0)
    def _():
        sh, sc = mod_ref[3:4, :], mod_ref[4:5, :]
        h = x_ref[...] * (1.0 + sc) + sh
        h_hi = h.astype(BF16)
        h_sc[...] = h_hi
        h_lo = (h - h_hi.astype(F32)).astype(BF16)
        wr = wr_ref[...]
        w_hi = wr.astype(BF16)
        w_lo = (wr - w_hi.astype(F32)).astype(BF16)
        logits = _dot(h_hi, w_hi) + _dot(h_lo, w_hi) + _dot(h_hi, w_lo) + br_ref[...]
        m = jnp.max(logits, axis=-1, keepdims=True)
        ex = jnp.exp(logits - m)
        probs = ex / jnp.sum(ex, axis=-1, keepdims=True)
        valid = lane < N_EXPERTS
        p = jnp.where(valid, probs, -1.0)
        p1 = jnp.max(p, axis=-1, keepdims=True)
        i1 = jnp.min(jnp.where(p == p1, lane, ROUTER_LANES), axis=-1, keepdims=True)
        pr = jnp.where(lane == i1, -1.0, p)
        p2 = jnp.max(pr, axis=-1, keepdims=True)
        i2 = jnp.min(jnp.where(pr == p2, lane, ROUTER_LANES), axis=-1, keepdims=True)
        tot = p1 + p2
        comb = jnp.where(lane == i1, p1 / tot, 0.0) + jnp.where(lane == i2, p2 / tot, 0.0)
        comb_sc[...] = comb
        acc_sc[...] = jnp.zeros(acc_sc.shape, F32)
        y_sc[...] = jnp.zeros(y_sc.shape, BF16)
        routed = jnp.where(comb > 0.0, 1.0, 0.0).astype(BF16)
        q0 = lax.broadcasted_iota(jnp.int32, (ROUTER_LANES, ROUTER_LANES), 0)
        q1 = lax.broadcasted_iota(jnp.int32, (ROUTER_LANES, ROUTER_LANES), 1)
        routed_t = _dot_nt(jnp.where(q0 == q1, 1.0, 0.0).astype(BF16), routed)
        maskr_sc[...] = routed_t
        routed_tb = routed_t.astype(BF16)
        rc = min(MOE_ROW_CHUNK, tt)
        for c in range(tt // rc):
            r_i = lax.broadcasted_iota(jnp.int32, (rc, tt), 0) + c * rc
            c_i = lax.broadcasted_iota(jnp.int32, (rc, tt), 1)
            rankc_sc[c * rc:(c + 1) * rc, :] = _dot(jnp.where(c_i < r_i, 1.0, 0.0).astype(BF16), routed)
            r_j = lax.broadcasted_iota(jnp.int32, (tt, rc), 0)
            c_j = lax.broadcasted_iota(jnp.int32, (tt, rc), 1) + c * rc
            rankr_sc[:, c * rc:(c + 1) * rc] = _dot(routed_tb, jnp.where(r_j < c_j, 1.0, 0.0).astype(BF16))

    def expert(hb):
        g = _dot(hb, wgu_ref[:, :fe])
        u = _dot(hb, wgu_ref[:, fe:])
        act = (g * _sigmoid(g) * u).astype(BF16)
        return _dot(act, wd_ref[...])

    w_col = jnp.sum(jnp.where(lane == e, comb_sc[...], 0.0), axis=-1, keepdims=True)
    rk_col = jnp.sum(jnp.where(lane == e, rankc_sc[...], 0.0), axis=-1, keepdims=True).astype(jnp.int32)
    rk_row = rankr_sc[pl.ds(e, 1), :].astype(jnp.int32)
    mk_row = maskr_sc[pl.ds(e, 1), :]
    n_routed = jnp.sum(mk_row).astype(jnp.int32)
    half = tt // 2
    gb, sb = blocks

    def gather_block(j, carry):
        base = pl.multiple_of(j * gb, gb)
        sub = lax.broadcasted_iota(jnp.int32, (gb, tt), 0) + base
        pick = jnp.where((rk_row == sub) & (mk_row > 0.0), 1.0, 0.0).astype(BF16)
        y_sc[pl.ds(base, gb), :] = expert(_dot(pick, h_sc[...]).astype(BF16)).astype(BF16)
        return carry

    lax.fori_loop(0, (n_routed + gb - 1) // gb, gather_block, 0)

    for c in range(tt // sb):
        @pl.when(n_routed > c * sb)
        def _(c=c):
            for r0 in (0, half):
                rows = slice(r0, r0 + half)
                ln_i = lax.broadcasted_iota(jnp.int32, (half, sb), 1) + c * sb
                put = jnp.where((rk_col[rows] == ln_i) & (w_col[rows] > 0.0), 1.0, 0.0).astype(BF16)
                acc_sc[rows, :] += w_col[rows] * _dot(put, y_sc[c * sb:(c + 1) * sb, :])

    @pl.when(e == N_EXPERTS - 1)
    def _():
        gate = mod_ref[5:6, :]
        o_ref[...] = _layer_norm(ALPHA * x_ref[...] + gate * acc_sc[...], lng_ref[...], lnb_ref[...])


def moe_ffn_layer(x, mod, w_router, b_router, w_gu, w_down, ln_g, ln_b, tt=1024, blocks=MOE_ROW_BLOCKS):
    bsz, seq, d = x.shape
    tt = min(tt, seq)
    n_e, fe = w_down.shape[0], w_down.shape[1]
    return pl.pallas_call(
        functools.partial(_moe_kernel, fe=fe, blocks=tuple(min(b, tt) for b in blocks)),
        grid=(bsz, seq // tt, n_e),
        in_specs=[pl.BlockSpec((None, tt, d), lambda b, t, e: (b, t, 0), pipeline_mode=pl.Buffered(1)),
                  pl.BlockSpec((None, 6, d), lambda b, t, e: (b, 0, 0)),
                  _const_spec((d, ROUTER_LANES)), _const_spec((1, ROUTER_LANES)),
                  pl.BlockSpec((None, d, 2 * fe), lambda b, t, e: (e, 0, 0)),
                  pl.BlockSpec((None, fe, d), lambda b, t, e: (e, 0, 0)),
                  _const_spec((1, d)), _const_spec((1, d))],
        out_specs=pl.BlockSpec((None, tt, d), lambda b, t, e: (b, t, 0)),
        out_shape=jax.ShapeDtypeStruct(x.shape, F32),
        scratch_shapes=[pltpu.VMEM((tt, d), BF16), pltpu.VMEM((tt, ROUTER_LANES), F32),
                        pltpu.VMEM((tt, d), F32), pltpu.VMEM((tt, ROUTER_LANES), F32),
                        pltpu.VMEM((ROUTER_LANES, tt), F32), pltpu.VMEM((ROUTER_LANES, tt), F32),
                        pltpu.VMEM((tt, d), BF16)],
        compiler_params=pltpu.CompilerParams(
            dimension_semantics=("parallel", "arbitrary", "arbitrary"), vmem_limit_bytes=MOE_VMEM_LIMIT),
        name="moe_ffn",
    )(x, mod, w_router, b_router, w_gu, w_down, ln_g, ln_b)


def _pad_router(w_router, b_router):
    d, n_e = w_router.shape
    w = jnp.zeros((d, ROUTER_LANES), F32).at[:, :n_e].set(w_router)
    b = jnp.full((1, ROUTER_LANES), NEG_INF, F32).at[0, :n_e].set(b_router)
    return w, b


def _qkv_kernel(x_ref, mod_ref, w_ref, q_ref, k_ref, v_ref, *, d):
    sh, sc = mod_ref[0:1, :], mod_ref[1:2, :]
    h = (x_ref[...] * (1.0 + sc) + sh).astype(BF16)
    p = _dot(h, w_ref[...])
    q_ref[...] = (p[:, :d] * (NA_HEAD_DIM ** -0.5)).astype(BF16)
    k_ref[...] = p[:, d:2 * d].astype(BF16)
    v_ref[...] = p[:, 2 * d:].astype(BF16)


def _na_kernel(q_ref, k_ref, v_ref, bias_ref, o_ref, *, rows):
    r = pl.program_id(1)
    r0 = jnp.clip(r - NA_WIN_ROWS // 2, 0, rows - NA_WIN_ROWS)
    start = pl.multiple_of(r0 * GRID_W, GRID_W)
    nk = NA_WIN_ROWS * GRID_W
    pw_ = 2 * NA_HEAD_DIM
    pairs = range(NA_HEADS // 2)
    lanes = [slice(p * pw_, (p + 1) * pw_) for p in pairs]
    head0 = lax.broadcasted_iota(jnp.int32, (GRID_W, pw_), 1) < NA_HEAD_DIM
    q2 = {}
    for p in pairs:
        q = q_ref[:, lanes[p]]
        z = jnp.zeros_like(q)
        q2[p] = jnp.concatenate([jnp.where(head0, q, z), jnp.where(head0, z, q)], axis=0)
    s = {p: _dot_nt(q2[p], k_ref[pl.ds(start, nk), lanes[p]])
         + bias_ref[2 * p:2 * p + 2].reshape(2 * GRID_W, nk) for p in pairs}
    m = {p: jnp.max(s[p], axis=-1, keepdims=True) for p in pairs}
    e = {p: jnp.exp(s[p] - m[p]) for p in pairs}
    l = {p: jnp.sum(e[p], axis=-1, keepdims=True) for p in pairs}
    o = {p: _dot(e[p].astype(BF16), v_ref[pl.ds(start, nk), lanes[p]]) / l[p] for p in pairs}
    for p in pairs:
        o_ref[:, lanes[p]] = jnp.where(head0, o[p][:GRID_W], o[p][GRID_W:]).astype(BF16)


def _na_bias_table(rpb):
    nh, nr, nc = rpb.shape
    w = GRID_W
    qc = jnp.arange(w)[:, None]
    kc = jnp.arange(w)[None, :]
    cs = jnp.clip(qc - NA_WIN_COLS // 2, 0, w - NA_WIN_COLS)
    col_ok = (kc >= cs) & (kc < cs + NA_WIN_COLS)
    lo = w - NA_WIN_COLS
    e = jnp.pad(rpb, ((0, 0), (0, 0), (lo, 2 * w - lo - nc)))
    flat = jnp.tile(e, (1, 1, w))
    toep = flat[:, :, w - 1:w - 1 + w * (2 * w - 1)].reshape(nh, nr, w, 2 * w - 1)[..., :w]
    toep = jnp.where(col_ok[None, None], toep, NEG_INF)
    tab = jnp.stack([toep[:, dl:dl + NA_WIN_ROWS] for dl in range(NA_WIN_ROWS)], axis=1)
    tab = jnp.transpose(tab, (0, 1, 3, 2, 4))
    return tab.reshape(nh, NA_WIN_ROWS, w, NA_WIN_ROWS * w)


def _proj_ln_kernel(*refs, n_in):
    a_refs = refs[:n_in]
    x_ref, mod_ref, w_ref, lng_ref, lnb_ref, o_ref = refs[n_in:]
    a = a_refs[0][...]
    if n_in == 2:
        a = (a.astype(F32) + a_refs[1][...].astype(F32)).astype(BF16)
    mix = _dot(a, w_ref[...])
    gate = mod_ref[2:3, :]
    o_ref[...] = _layer_norm(ALPHA * x_ref[...] + gate * mix, lng_ref[...], lnb_ref[...])


def proj_residual_ln(acts, x, mod, w_o, ln_g, ln_b, tt=512):
    bsz, seq, d = x.shape
    tt = min(tt, seq)
    tile = pl.BlockSpec((None, tt, d), lambda b, t: (b, t, 0))
    return pl.pallas_call(
        functools.partial(_proj_ln_kernel, n_in=len(acts)),
        grid=(bsz, seq // tt),
        in_specs=[tile] * len(acts) + [tile, _mod_spec(d), _const_spec((d, d)),
                                       _const_spec((1, d)), _const_spec((1, d))],
        out_specs=tile,
        out_shape=jax.ShapeDtypeStruct(x.shape, F32),
        compiler_params=_cparams("parallel", "arbitrary"),
        name="proj_residual_ln",
    )(*acts, x, mod, w_o, ln_g, ln_b)


def na_mixer_layer(x, mod, w_qkv, bias_tab, w_o, ln_g, ln_b, tt=512):
    bsz, seq, d = x.shape
    tt = min(tt, seq)
    rows = seq // GRID_W
    assert rows >= NA_WIN_ROWS and seq % GRID_W == 0
    tile = pl.BlockSpec((None, tt, d), lambda b, t: (b, t, 0))
    q, k, v = pl.pallas_call(
        functools.partial(_qkv_kernel, d=d),
        grid=(bsz, seq // tt),
        in_specs=[tile, _mod_spec(d), _const_spec((d, 3 * d))],
        out_specs=[tile] * 3,
        out_shape=[jax.ShapeDtypeStruct(x.shape, BF16)] * 3,
        compiler_params=_cparams("parallel", "arbitrary"),
        name="na_qkv",
    )(x, mod, w_qkv)

    def delta(r):
        return jnp.clip(r - NA_WIN_ROWS // 2, 0, rows - NA_WIN_ROWS) - r + NA_WIN_ROWS - 1

    row_spec = pl.BlockSpec((None, GRID_W, d), lambda b, r: (b, r, 0))
    seq_spec = pl.BlockSpec((None, seq, d), lambda b, r: (b, 0, 0))
    att = pl.pallas_call(
        functools.partial(_na_kernel, rows=rows),
        grid=(bsz, rows),
        in_specs=[row_spec, seq_spec, seq_spec,
                  pl.BlockSpec((NA_HEADS, None, GRID_W, NA_WIN_ROWS * GRID_W),
                               lambda b, r: (0, delta(r), 0, 0))],
        out_specs=row_spec,
        out_shape=jax.ShapeDtypeStruct(x.shape, BF16),
        compiler_params=_cparams("parallel", "arbitrary"),
        name="na_attention",
    )(q, k, v, bias_tab)
    return proj_residual_ln([att], x, mod, w_o, ln_g, ln_b, tt)


def _rk_proj_kernel(x_ref, xp_ref, xn_ref, mod_ref, mu_ref, wrkv_ref, w1_ref, w2_ref, a1_ref, a2_ref,
                    g1_ref, g2_ref, w0_ref, a0_ref, kk_ref, ka_ref,
                    r_o, v_o, kk_o, lw_o, km_o, b_o, gate_o, *, nt, tt):
    t = pl.program_id(1)
    sh, sc = mod_ref[0:1, :], mod_ref[1:2, :]
    h = x_ref[...] * (1.0 + sc) + sh
    h_prev = jnp.where(t > 0, xp_ref[SUBLANES - 1:SUBLANES, :] * (1.0 + sc) + sh, 0.0)
    h_next = jnp.where(t < nt - 1, xn_ref[0:1, :] * (1.0 + sc) + sh, 0.0)
    h_m1, h_p1 = _shifted(h, h_prev, h_next, tt)
    xx = 0.5 * (h_m1 + h_p1) - h

    h_b, xx_b, mu_b = h.astype(BF16), xx.astype(BF16), mu_ref[...].astype(BF16)

    def mixed(p):
        return h_b + xx_b * mu_b[p:p + 1, :]

    r = _dot(mixed(0), wrkv_ref[0])
    k = _dot(mixed(1), wrkv_ref[1])
    v = _dot(mixed(2), wrkv_ref[2])
    r_o[...] = r.astype(BF16)
    v_o[...] = v.astype(BF16)
    lw_in = jnp.tanh(_dot(mixed(3), w1_ref[...])).astype(BF16)
    a_in = _dot(mixed(4), a1_ref[...]).astype(BF16)
    g_in = _sigmoid(_dot(mixed(5), g1_ref[...])).astype(BF16)

    kk = k * kk_ref[...]
    kk_o[...] = kk.astype(BF16)
    rw = w1_ref.shape[1] // 2
    rg = g1_ref.shape[1] // 2
    for z in range(2):
        wl = w0_ref[z:z + 1, :] + _dot(lw_in[:, z * rw:(z + 1) * rw], w2_ref[z])
        lw_o[z] = (-RK_DECAY_SCALE) * _sigmoid(wl)
        a_lr = _sigmoid(a0_ref[z:z + 1, :] + _dot(a_in[:, z * rw:(z + 1) * rw], a2_ref[z]))
        km_o[z] = (k * (1.0 + (a_lr - 1.0) * ka_ref[...])).astype(BF16)
        b_o[z] = (kk * a_lr).astype(BF16)
        gate_o[z] = _dot(g_in[:, z * rg:(z + 1) * rg], g2_ref[z]).astype(BF16)


def _head_sum(x):
    head0 = lax.broadcasted_iota(jnp.int32, x.shape, 1) < RK_HEAD_DIM
    s0 = jnp.sum(jnp.where(head0, x, 0.0), axis=-1, keepdims=True)
    s1 = jnp.sum(jnp.where(head0, 0.0, x), axis=-1, keepdims=True)
    return jnp.where(head0, s0, s1)


def _rk_scan_kernel(r_ref, v_ref, kk_ref, lw_ref, km_ref, b_ref, gate_ref, rk_ref, lg_ref, lb_ref,
                    o_ref, s_ref, *, reverse):
    tb, hb = r_ref.shape
    cl = RK_CHUNK
    nc, nh = tb // cl, hb // RK_HEAD_DIM

    @pl.when(pl.program_id(2) == 0)
    def _():
        s_ref[...] = jnp.zeros(s_ref.shape, F32)

    ri = lax.broadcasted_iota(jnp.int32, (tb, tb), 0)
    ci = lax.broadcasted_iota(jnp.int32, (tb, tb), 1)
    ordered = (ci >= ri) if reverse else (ci <= ri)
    tri = jnp.where((ri // cl == ci // cl) & ordered, 1.0, 0.0).astype(BF16)
    lw = lw_ref[...]
    lw_hi = lw.astype(BF16)
    lw_lo = (lw - lw_hi.astype(F32)).astype(BF16)
    cum = _dot(tri, lw_hi) + _dot(tri, lw_lo)
    last = 0 if reverse else cl - 1
    tot_rows = [cum[c * cl + last:c * cl + last + 1, :] for c in range(nc)]
    tot = jnp.concatenate([jnp.broadcast_to(tr, (cl, hb)) for tr in tot_rows], axis=0)

    rr = r_ref[...].astype(F32)
    vv = v_ref[...]
    kk = kk_ref[...].astype(F32)
    pair = 2 * RK_HEAD_DIM
    rnorm = jnp.concatenate(
        [1.0 / jnp.maximum(jnp.sqrt(_head_sum(jnp.square(kk[:, i:i + pair]))), 1e-12) for i in range(0, hb, pair)],
        axis=1)
    kkn = kk * rnorm
    km = km_ref[...].astype(F32)
    bb = b_ref[...].astype(F32) * rnorm
    e_neg = jnp.exp(-cum)
    e_rem = jnp.exp(tot - cum)
    a_t = (-kkn * jnp.exp(cum - lw)).astype(BF16)
    r_t = (rr * jnp.exp(cum)).astype(BF16)
    b_t = (bb * e_neg).astype(BF16)
    k_t = (km * e_neg).astype(BF16)
    b_h = (bb * e_rem).astype(BF16)
    k_h = (km * e_rem).astype(BF16)
    bonus_w = rr * km * rk_ref[...]

    pw_ = 2 * RK_HEAD_DIM
    npair = hb // pw_
    t_i = lax.broadcasted_iota(jnp.int32, (cl, pw_), 0)
    s_i = lax.broadcasted_iota(jnp.int32, (cl, pw_), 1) % cl
    strict = (t_i < s_i) if reverse else (t_i > s_i)
    incl = (t_i <= s_i) if reverse else (t_i >= s_i)
    eye = jnp.where(t_i == s_i, 1.0, 0.0)
    head0 = lax.broadcasted_iota(jnp.int32, (cl, pw_), 1) < RK_HEAD_DIM
    q0 = lax.broadcasted_iota(jnp.int32, (pw_, pw_), 0)
    q1 = lax.broadcasted_iota(jnp.int32, (pw_, pw_), 1)
    same_head = (q0 // RK_HEAD_DIM) == (q1 // RK_HEAD_DIM)

    head_sum = _head_sum

    def bd(x):
        z = jnp.zeros_like(x)
        return jnp.concatenate([jnp.where(head0, x, z), jnp.where(head0, z, x)], axis=0)

    keys = [(c, p) for c in range(nc) for p in range(npair)]
    rsl = {k: slice(k[0] * cl, (k[0] + 1) * cl) for k in keys}
    lsl = {k: slice(k[1] * pw_, (k[1] + 1) * pw_) for k in keys}
    blk = lambda arr, k: arr[rsl[k], lsl[k]]

    gram = {k: _dot_nt(jnp.concatenate([blk(a_t, k), blk(r_t, k)], axis=0),
                       jnp.concatenate([bd(blk(b_t, k)), bd(blk(k_t, k))], axis=0)) for k in keys}
    m_ab = {k: jnp.where(strict, gram[k][:cl, :pw_], 0.0) for k in keys}
    m_ak = {k: jnp.where(strict, gram[k][:cl, pw_:], 0.0).astype(BF16) for k in keys}
    m_rb = {k: jnp.where(incl, gram[k][cl:, :pw_], 0.0).astype(BF16) for k in keys}
    m_rk = {k: jnp.where(incl, gram[k][cl:, pw_:], 0.0).astype(BF16) for k in keys}
    vbd = {k: bd(blk(vv, k)) for k in keys}
    my = {k: _dot(jnp.concatenate([m_ak[k], m_rk[k]], axis=0), vbd[k]) for k in keys}
    v_t = {k: jnp.transpose(blk(vv, k).astype(F32)).astype(BF16) for k in keys}
    vk = {k: jnp.where(same_head, _dot(v_t[k], blk(k_h, k)), 0.0) for k in keys}
    inv = {k: eye + m_ab[k] for k in keys}
    pwb = {k: m_ab[k].astype(BF16) for k in keys}
    pw = {k: _dot(pwb[k], bd(pwb[k])) for k in keys}
    n = 4
    while n < cl:
        pwb = {k: pw[k].astype(BF16) for k in keys}
        res = {k: _dot(pwb[k], jnp.concatenate([bd(pwb[k]), bd(inv[k].astype(BF16))], axis=1)) for k in keys}
        pw = {k: res[k][:, :pw_] for k in keys}
        inv = {k: inv[k] + res[k][:, pw_:] for k in keys}
        n *= 2
    inv = {k: (inv[k] + _dot(pw[k].astype(BF16), bd(inv[k].astype(BF16)))).astype(BF16) for k in keys}
    au = {k: _dot(inv[k], jnp.concatenate([bd(blk(a_t, k)), bd(my[k][:cl].astype(BF16))], axis=1))
          for k in keys}
    a_hat = {k: au[k][:, :pw_].astype(BF16) for k in keys}
    u0_t = {k: jnp.transpose(au[k][:, pw_:]) for k in keys}
    ar = {k: jnp.concatenate([a_hat[k], blk(r_t, k)], axis=0) for k in keys}

    pairs = range(npair)
    for c in (range(nc - 1, -1, -1) if reverse else range(nc)):
        st = {p: s_ref[p] for p in pairs}
        stb = {p: st[p].astype(BF16) for p in pairs}
        res = {p: _dot_nt(ar[c, p], stb[p]) for p in pairs}
        res_t = {p: _dot_nt(stb[p], a_hat[c, p]) for p in pairs}
        ub = {p: (au[c, p][:, pw_:] + res[p][:cl]).astype(BF16) for p in pairs}
        ub_t = {p: (u0_t[c, p] + res_t[p]).astype(BF16) for p in pairs}
        y = {p: my[c, p][cl:] + res[p][cl:] + _dot(m_rb[c, p], bd(ub[p])) for p in pairs}
        for p in pairs:
            k = (c, p)
            decay = jnp.exp(tot_rows[c][:, lsl[k]])
            s_ref[p] = jnp.where(same_head, st[p] * decay + _dot(ub_t[p], blk(b_h, k)), 0.0) + vk[k]
        dy = {p: y[p] - head_sum(y[p]) * (1.0 / RK_HEAD_DIM) for p in pairs}
        var = {p: head_sum(dy[p] * dy[p]) * (1.0 / RK_HEAD_DIM) for p in pairs}
        for p in pairs:
            k = (c, p)
            yn = dy[p] * lax.rsqrt(var[p] + RK_GN_EPS) * lg_ref[:, lsl[k]] + lb_ref[:, lsl[k]]
            bonus = head_sum(blk(bonus_w, k)) * blk(vv, k).astype(F32)
            o_ref[rsl[k], lsl[k]] = (blk(gate_ref, k).astype(F32) * (yn + bonus)).astype(BF16)


def rwkv_mixer_layer(x, mod, p, ln_g, ln_b, tt=256):
    bsz, seq, d = x.shape
    tt = min(tt, seq)
    nt = seq // tt
    x_spec, prev_spec, next_spec = _tile_specs(seq, tt, d)
    tile = pl.BlockSpec((None, tt, d), lambda b, t: (b, t, 0))
    tile2 = pl.BlockSpec((2, None, tt, d), lambda b, t: (0, b, t, 0))
    rw, rg = p['w1'].shape[1] // 2, p['g1'].shape[1] // 2
    sh_bf = jax.ShapeDtypeStruct(x.shape, BF16)
    sh2_bf = jax.ShapeDtypeStruct((2,) + x.shape, BF16)
    sh2_f32 = jax.ShapeDtypeStruct((2,) + x.shape, F32)
    r, v, kkn, lw, km, bz, gate = pl.pallas_call(
        functools.partial(_rk_proj_kernel, nt=nt, tt=tt),
        grid=(bsz, nt),
        in_specs=[x_spec, prev_spec, next_spec, _mod_spec(d), _const_spec((6, d)),
                  _const_spec((3, d, d)),
                  _const_spec((d, 2 * rw)), _const_spec((2, rw, d)),
                  _const_spec((d, 2 * rw)), _const_spec((2, rw, d)),
                  _const_spec((d, 2 * rg)), _const_spec((2, rg, d)),
                  _const_spec((2, d)), _const_spec((2, d)), _const_spec((1, d)), _const_spec((1, d))],
        out_specs=[tile, tile, tile, tile2, tile2, tile2, tile2],
        out_shape=[sh_bf, sh_bf, sh_bf, sh2_f32, sh2_bf, sh2_bf, sh2_bf],
        compiler_params=_cparams("parallel", "arbitrary"),
        name="rwkv_proj",
    )(x, x, x, mod, p['mu'], p['w_rkv'], p['w1'], p['w2'], p['a1'], p['a2'], p['g1'], p['g2'],
      p['w0'], p['a0'], p['k_k'], p['k_a'])

    tb = min(RK_TOK_BLOCK, seq)
    hb = RK_LANE_BLOCK
    nb = seq // tb
    outs = []
    for z in range(2):
        rev = z == 1

        def tok(i, rev=rev):
            return nb - 1 - i if rev else i

        blk = pl.BlockSpec((None, tb, hb), lambda b, g, i: (b, tok(i), g))
        blk2 = pl.BlockSpec((None, None, tb, hb), lambda b, g, i, z=z: (z, b, tok(i), g))
        vec = pl.BlockSpec((1, hb), lambda b, g, i: (0, g))
        vec2 = pl.BlockSpec((None, 1, hb), lambda b, g, i, z=z: (z, 0, g))
        outs.append(pl.pallas_call(
            functools.partial(_rk_scan_kernel, reverse=rev),
            grid=(bsz, d // hb, nb),
            in_specs=[blk, blk, blk, blk2, blk2, blk2, blk2, vec, vec2, vec2],
            out_specs=blk,
            out_shape=sh_bf,
            scratch_shapes=[pltpu.VMEM((hb // (2 * RK_HEAD_DIM), 2 * RK_HEAD_DIM, 2 * RK_HEAD_DIM), F32)],
            compiler_params=_cparams("parallel", "parallel", "arbitrary"),
            name="rwkv_scan_rev" if rev else "rwkv_scan_fwd",
        )(r, v, kkn, lw, km, bz, gate, p['r_k'], p['lnx_g'], p['lnx_b']))
    return proj_residual_ln(outs, x, mod, p['w_o'], ln_g, ln_b)


def _prepare_params(w):
    d = D_MODEL
    bf = lambda a: a.astype(BF16)
    row = lambda a: a.reshape(1, d)
    p = {
        'w_ada': w['w_ada'], 'b_ada': w['b_ada'],
        'ln_g': w['ln_g'], 'ln_b': w['ln_b'],
        'conv_w_in': bf(w['conv_w_in']), 'conv_w': w['conv_w'], 'conv_w_out': bf(w['conv_w_out']),
        'na_w_qkv': bf(w['na_w_qkv']), 'na_w_o': bf(w['na_w_o']),
        'na_bias': [_na_bias_table(w['na_rpb'][j]) for j in range(w['na_rpb'].shape[0])],
        'ffn_w_gu': bf(w['ffn_w_gu']), 'ffn_w_down': bf(w['ffn_w_down']),
        'moe_router': [_pad_router(w['moe_w_router'][j], w['moe_b_router'][j])
                       for j in range(w['moe_w_router'].shape[0])],
        'moe_w_gu': bf(w['moe_w_gu']), 'moe_w_down': bf(w['moe_w_down']),
        'rk': [],
    }
    cat = lambda a: jnp.concatenate([a[0], a[1]], axis=1)
    for j in range(w['rk_mu'].shape[0]):
        p['rk'].append({
            'mu': w['rk_mu'][j], 'w_rkv': bf(w['rk_w_rkv'][j]),
            'w1': bf(cat(w['rk_w1'][j])), 'w2': bf(w['rk_w2'][j]),
            'a1': bf(cat(w['rk_a1'][j])), 'a2': bf(w['rk_a2'][j]),
            'g1': bf(cat(w['rk_g1'][j])), 'g2': bf(w['rk_g2'][j]),
            'w0': w['rk_w0'][j], 'a0': w['rk_a0'][j],
            'k_k': row(w['rk_k_k'][j]), 'k_a': row(w['rk_k_a'][j]),
            'r_k': w['rk_r_k'][j].reshape(1, d),
            'lnx_g': w['rk_lnx_g'][j].reshape(2, 1, d), 'lnx_b': w['rk_lnx_b'][j].reshape(2, 1, d),
            'w_o': bf(w['rk_w_o'][j]),
        })
    return p


def _trunk(x, mod_all, p):
    d = D_MODEL
    for i in range(DEPTH):
        mod = mod_all[i]
        lng = lambda s: p['ln_g'][i, s].reshape(1, d)
        lnb = lambda s: p['ln_b'][i, s].reshape(1, d)
        kind, j = i % 3, i // 3
        if kind == 0:
            x = conv_mixer_layer(x, mod, p['conv_w_in'][j], p['conv_w'][j], p['conv_w_out'][j],
                                 lng(0), lnb(0))
        elif kind == 1:
            x = na_mixer_layer(x, mod, p['na_w_qkv'][j], p['na_bias'][j], p['na_w_o'][j], lng(0), lnb(0))
        else:
            x = rwkv_mixer_layer(x, mod, p['rk'][j], lng(0), lnb(0))
        if i % 2 == 0:
            x = dense_ffn_layer(x, mod, p['ffn_w_gu'][i // 2], p['ffn_w_down'][i // 2], lng(1), lnb(1))
        else:
            wr, br = p['moe_router'][i // 2]
            x = moe_ffn_layer(x, mod, wr, br, p['moe_w_gu'][i // 2], p['moe_w_down'][i // 2],
                              lng(1), lnb(1))
    return x


def kernel(x_prompt, x_sample, c_prompt, c_sample, w_ada, b_ada, ln_g, ln_b, conv_w_in, conv_w, conv_w_out, na_w_qkv, na_rpb, na_w_o, rk_mu, rk_w_rkv, rk_w0, rk_w1, rk_w2, rk_a0, rk_a1, rk_a2, rk_g1, rk_g2, rk_k_k, rk_k_a, rk_r_k, rk_lnx_g, rk_lnx_b, rk_w_o, ffn_w_gu, ffn_w_down, moe_w_router, moe_b_router, moe_w_gu, moe_w_down):
    p = _prepare_params(dict(
        w_ada=w_ada, b_ada=b_ada, ln_g=ln_g, ln_b=ln_b,
        conv_w_in=conv_w_in, conv_w=conv_w, conv_w_out=conv_w_out,
        na_w_qkv=na_w_qkv, na_rpb=na_rpb, na_w_o=na_w_o,
        rk_mu=rk_mu, rk_w_rkv=rk_w_rkv, rk_w0=rk_w0, rk_w1=rk_w1, rk_w2=rk_w2,
        rk_a0=rk_a0, rk_a1=rk_a1, rk_a2=rk_a2, rk_g1=rk_g1, rk_g2=rk_g2,
        rk_k_k=rk_k_k, rk_k_a=rk_k_a, rk_r_k=rk_r_k, rk_lnx_g=rk_lnx_g, rk_lnx_b=rk_lnx_b,
        rk_w_o=rk_w_o, ffn_w_gu=ffn_w_gu, ffn_w_down=ffn_w_down,
        moe_w_router=moe_w_router, moe_b_router=moe_b_router,
        moe_w_gu=moe_w_gu, moe_w_down=moe_w_down))
    n_p = c_prompt.shape[0]
    mod_all = ada_modulation(jnp.concatenate([c_prompt, c_sample], axis=0), p['w_ada'], p['b_ada'])
    return (_trunk(x_prompt, mod_all[:, :n_p], p), _trunk(x_sample, mod_all[:, n_p:], p))
```

```python
import functools

import jax
import jax.numpy as jnp
from jax import lax
from jax.experimental import pallas as pl
from jax.experimental.pallas import tpu as pltpu

F32 = jnp.float32
BF16 = jnp.bfloat16

D_MODEL = 1024
DEPTH = 4
ALPHA = (2 * DEPTH) ** 0.25
LN_EPS = 1e-5

GRID_W = 64
NA_HEADS = 16
NA_HEAD_DIM = D_MODEL // NA_HEADS
NA_WIN_ROWS = 8
NA_WIN_COLS = 16
NEG_INF = -1e30

RK_HEAD_DIM = 64
RK_GN_EPS = 64e-5
RK_DECAY_SCALE = 0.6065306597126334
RK_CHUNK = 64
RK_TOK_BLOCK = 256
RK_LANE_BLOCK = 1024

N_EXPERTS = 8
ROUTER_LANES = 128

SUBLANES = 8
VMEM_LIMIT = 52 * 1024 * 1024
MOE_VMEM_LIMIT = 58 * 1024 * 1024


def _cparams(*sem):
    return pltpu.CompilerParams(dimension_semantics=sem, vmem_limit_bytes=VMEM_LIMIT)


def _const_spec(shape):
    nd = len(shape)
    return pl.BlockSpec(shape, lambda *_: (0,) * nd, pipeline_mode=pl.Buffered(1))


def _dot(a, b):
    return jnp.dot(a, b, preferred_element_type=F32)


def _dot_nt(a, b):
    return lax.dot_general(a, b, (((1,), (1,)), ((), ())), preferred_element_type=F32)


def _sigmoid(x):
    return 1.0 / (1.0 + jnp.exp(-x))


def _layer_norm(y, g, b):
    mu = jnp.mean(y, axis=-1, keepdims=True)
    d = y - mu
    var = jnp.mean(d * d, axis=-1, keepdims=True)
    return d * lax.rsqrt(var + LN_EPS) * g + b


def _ada_kernel(c_ref, w_ref, b_ref, o_ref):
    c = c_ref[...]
    s = (c * _sigmoid(c)).astype(BF16)
    o_ref[...] = _dot(s, w_ref[...].astype(BF16)) + b_ref[...]


def ada_modulation(c, w_ada, b_ada):
    bsz, d = c.shape
    depth = w_ada.shape[0]
    out = pl.pallas_call(
        _ada_kernel,
        grid=(depth, 6),
        in_specs=[
            pl.BlockSpec((bsz, d), lambda i, j: (0, 0)),
            pl.BlockSpec((None, d, d), lambda i, j: (i, 0, j)),
            pl.BlockSpec((None, None, 1, d), lambda i, j: (i, j, 0, 0)),
        ],
        out_specs=pl.BlockSpec((None, None, bsz, d), lambda i, j: (i, j, 0, 0)),
        out_shape=jax.ShapeDtypeStruct((depth, 6, bsz, d), F32),
        compiler_params=_cparams("arbitrary", "arbitrary"),
        name="ada_modulation",
    )(c, w_ada, b_ada.reshape(depth, 6, 1, d))
    return jnp.transpose(out, (0, 2, 1, 3))


def _tile_specs(seq, tt, d):
    nb8 = seq // SUBLANES
    per = tt // SUBLANES
    x_spec = pl.BlockSpec((None, tt, d), lambda b, t: (b, t, 0))
    prev_spec = pl.BlockSpec((None, SUBLANES, d), lambda b, t: (b, jnp.maximum(t * per - 1, 0), 0))
    next_spec = pl.BlockSpec((None, SUBLANES, d), lambda b, t: (b, jnp.minimum((t + 1) * per, nb8 - 1), 0))
    return x_spec, prev_spec, next_spec


def _mod_spec(d):
    return pl.BlockSpec((None, 6, d), lambda b, t: (b, 0, 0))


def _shifted(cur, prev_row, next_row, tt):
    row = lax.broadcasted_iota(jnp.int32, (tt, 1), 0)
    m1 = jnp.where(row == 0, prev_row, pltpu.roll(cur, 1, 0))
    p1 = jnp.where(row == tt - 1, next_row, pltpu.roll(cur, tt - 1, 0))
    return m1, p1


def _conv_kernel(x_ref, xp_ref, xn_ref, mod_ref, win_ref, cw_ref, wout_ref, lng_ref, lnb_ref,
                 o_ref, *, nt, tt, d):
    t = pl.program_id(1)
    sh, sc, gate = mod_ref[0:1, :], mod_ref[1:2, :], mod_ref[2:3, :]
    parts = [slice(i * tt // CONV_PARTS, (i + 1) * tt // CONV_PARTS) for i in range(CONV_PARTS)]
    xs = [x_ref[r, :] for r in parts]
    ps = [_dot((x * (1.0 + sc) + sh).astype(BF16), win_ref[...]) for x in xs]
    z = jnp.concatenate([p[:, d:2 * d] * p[:, 2 * d:] for p in ps], axis=0)
    halo = jnp.concatenate([xp_ref[...], xn_ref[...]], axis=0)
    hh = (halo * (1.0 + sc) + sh).astype(BF16)
    ph = _dot(hh, win_ref[:, d:])
    zh = ph[:, :d] * ph[:, d:]
    z_prev = jnp.where(t > 0, zh[SUBLANES - 1:SUBLANES, :], 0.0)
    z_next = jnp.where(t < nt - 1, zh[SUBLANES:SUBLANES + 1, :], 0.0)
    z_m1, z_p1 = _shifted(z, z_prev, z_next, tt)
    conv = z_m1 * cw_ref[0:1, :] + z * cw_ref[1:2, :] + z_p1 * cw_ref[2:3, :]
    mixes = [_dot((p[:, :d] * conv[r]).astype(BF16), wout_ref[...]) for p, r in zip(ps, parts)]
    for x, r, mix in zip(xs, parts, mixes):
        o_ref[r, :] = _layer_norm(ALPHA * x + gate * mix, lng_ref[...], lnb_ref[...])


CONV_PARTS = 4


def conv_mixer_layer(x, mod, w_in, conv_w, w_out, ln_g, ln_b, tt=1024):
    bsz, seq, d = x.shape
    tt = min(tt, seq)
    nt = seq // tt
    x_spec, prev_spec, next_spec = _tile_specs(seq, tt, d)
    return pl.pallas_call(
        functools.partial(_conv_kernel, nt=nt, tt=tt, d=d),
        grid=(bsz, nt),
        in_specs=[x_spec, prev_spec, next_spec, _mod_spec(d),
                  _const_spec((d, 3 * d)), _const_spec((3, d)), _const_spec((d, d)),
                  _const_spec((1, d)), _const_spec((1, d))],
        out_specs=pl.BlockSpec((None, tt, d), lambda b, t: (b, t, 0)),
        out_shape=jax.ShapeDtypeStruct(x.shape, F32),
        compiler_params=_cparams("parallel", "arbitrary"),
        name="conv_mixer",
    )(x, x, x, mod, w_in, conv_w, w_out, ln_g, ln_b)


def _ffn_kernel(x_ref, mod_ref, wgu_ref, wd_ref, lng_ref, lnb_ref, o_ref, *, ff, fc):
    sh, sc, gate = mod_ref[3:4, :], mod_ref[4:5, :], mod_ref[5:6, :]
    tt = x_ref.shape[0]
    parts = [slice(i * tt // FFN_PARTS, (i + 1) * tt // FFN_PARTS) for i in range(FFN_PARTS)]
    chunks = range(ff // fc)
    xs = [x_ref[r, :] for r in parts]
    hs = [(x * (1.0 + sc) + sh).astype(BF16) for x in xs]
    gs = [[_dot(h, wgu_ref[:, c * fc:(c + 1) * fc]) for c in chunks] for h in hs]
    us = [[_dot(h, wgu_ref[:, ff + c * fc:ff + (c + 1) * fc]) for c in chunks] for h in hs]
    acts = [[(g * _sigmoid(g) * u).astype(BF16) for g, u in zip(gr, ur)] for gr, ur in zip(gs, us)]
    for x, r, ar in zip(xs, parts, acts):
        acc = _dot(ar[0], wd_ref[0:fc, :])
        for c in chunks[1:]:
            acc = acc + _dot(ar[c], wd_ref[c * fc:(c + 1) * fc, :])
        o_ref[r, :] = _layer_norm(ALPHA * x + gate * acc, lng_ref[...], lnb_ref[...])


FFN_PARTS = 2


def dense_ffn_layer(x, mod, w_gu, w_down, ln_g, ln_b, tt=512):
    bsz, seq, d = x.shape
    tt = min(tt, seq)
    ff = w_down.shape[0]
    fc = ff
    return pl.pallas_call(
        functools.partial(_ffn_kernel, ff=ff, fc=fc),
        grid=(bsz, seq // tt),
        in_specs=[pl.BlockSpec((None, tt, d), lambda b, t: (b, t, 0)), _mod_spec(d),
                  _const_spec((d, 2 * ff)), _const_spec((ff, d)),
                  _const_spec((1, d)), _const_spec((1, d))],
        out_specs=pl.BlockSpec((None, tt, d), lambda b, t: (b, t, 0)),
        out_shape=jax.ShapeDtypeStruct(x.shape, F32),
        compiler_params=_cparams("parallel", "arbitrary"),
        name="dense_ffn",
    )(x, mod, w_gu, w_down, ln_g, ln_b)


MOE_ROW_CHUNK = 256
MOE_ROW_BLOCKS = (128, 256)


def _moe_kernel(x_ref, mod_ref, wr_ref, br_ref, wgu_ref, wd_ref, lng_ref, lnb_ref, o_ref,
                h_sc, comb_sc, acc_sc, rankc_sc, rankr_sc, maskr_sc, y_sc, *, fe, blocks):
    e = pl.program_id(2)
    tt = comb_sc.shape[0]
    lane = lax.broadcasted_iota(jnp.int32, comb_sc.shape, 1)

    @pl.when(e == 0)
    def _():
        sh, sc = mod_ref[3:4, :], mod_ref[4:5, :]
        h = x_ref[...] * (1.0 + sc) + sh
        h_hi = h.astype(BF16)
        h_sc[...] = h_hi
        h_lo = (h - h_hi.astype(F32)).astype(BF16)
        wr = wr_ref[...]
        w_hi = wr.astype(BF16)
        w_lo = (wr - w_hi.astype(F32)).astype(BF16)
        logits = _dot(h_hi, w_hi) + _dot(h_lo, w_hi) + _dot(h_hi, w_lo) + br_ref[...]
        m = jnp.max(logits, axis=-1, keepdims=True)
        ex = jnp.exp(logits - m)
        probs = ex / jnp.sum(ex, axis=-1, keepdims=True)
        valid = lane < N_EXPERTS
        p = jnp.where(valid, probs, -1.0)
        p1 = jnp.max(p, axis=-1, keepdims=True)
        i1 = jnp.min(jnp.where(p == p1, lane, ROUTER_LANES), axis=-1, keepdims=True)
        pr = jnp.where(lane == i1, -1.0, p)
        p2 = jnp.max(pr, axis=-1, keepdims=True)
        i2 = jnp.min(jnp.where(pr == p2, lane, ROUTER_LANES), axis=-1, keepdims=True)
        tot = p1 + p2
        comb = jnp.where(lane == i1, p1 / tot, 0.0) + jnp.where(lane == i2, p2 / tot, 0.0)
        comb_sc[...] = comb
        acc_sc[...] = jnp.zeros(acc_sc.shape, F32)
        y_sc[...] = jnp.zeros(y_sc.shape, BF16)
        routed = jnp.where(comb > 0.0, 1.0, 0.0).astype(BF16)
        q0 = lax.broadcasted_iota(jnp.int32, (ROUTER_LANES, ROUTER_LANES), 0)
        q1 = lax.broadcasted_iota(jnp.int32, (ROUTER_LANES, ROUTER_LANES), 1)
        routed_t = _dot_nt(jnp.where(q0 == q1, 1.0, 0.0).astype(BF16), routed)
        maskr_sc[...] = routed_t
        routed_tb = routed_t.astype(BF16)
        rc = min(MOE_ROW_CHUNK, tt)
        for c in range(tt // rc):
            r_i = lax.broadcasted_iota(jnp.int32, (rc, tt), 0) + c * rc
            c_i = lax.broadcasted_iota(jnp.int32, (rc, tt), 1)
            rankc_sc[c * rc:(c + 1) * rc, :] = _dot(jnp.where(c_i < r_i, 1.0, 0.0).astype(BF16), routed)
            r_j = lax.broadcasted_iota(jnp.int32, (tt, rc), 0)
            c_j = lax.broadcasted_iota(jnp.int32, (tt, rc), 1) + c * rc
            rankr_sc[:, c * rc:(c + 1) * rc] = _dot(routed_tb, jnp.where(r_j < c_j, 1.0, 0.0).astype(BF16))

    def expert(hb):
        g = _dot(hb, wgu_ref[:, :fe])
        u = _dot(hb, wgu_ref[:, fe:])
        act = (g * _sigmoid(g) * u).astype(BF16)
        return _dot(act, wd_ref[...])

    w_col = jnp.sum(jnp.where(lane == e, comb_sc[...], 0.0), axis=-1, keepdims=True)
    rk_col = jnp.sum(jnp.where(lane == e, rankc_sc[...], 0.0), axis=-1, keepdims=True).astype(jnp.int32)
    rk_row = rankr_sc[pl.ds(e, 1), :].astype(jnp.int32)
    mk_row = maskr_sc[pl.ds(e, 1), :]
    n_routed = jnp.sum(mk_row).astype(jnp.int32)
    half = tt // 2
    gb, sb = blocks

    def gather_block(j, carry):
        base = pl.multiple_of(j * gb, gb)
        sub = lax.broadcasted_iota(jnp.int32, (gb, tt), 0) + base
        pick = jnp.where((rk_row == sub) & (mk_row > 0.0), 1.0, 0.0).astype(BF16)
        y_sc[pl.ds(base, gb), :] = expert(_dot(pick, h_sc[...]).astype(BF16)).astype(BF16)
        return carry

    lax.fori_loop(0, (n_routed + gb - 1) // gb, gather_block, 0)

    for c in range(tt // sb):
        @pl.when(n_routed > c * sb)
        def _(c=c):
            for r0 in (0, half):
                rows = slice(r0, r0 + half)
                ln_i = lax.broadcasted_iota(jnp.int32, (half, sb), 1) + c * sb
                put = jnp.where((rk_col[rows] == ln_i) & (w_col[rows] > 0.0), 1.0, 0.0).astype(BF16)
                acc_sc[rows, :] += w_col[rows] * _dot(put, y_sc[c * sb:(c + 1) * sb, :])

    @pl.when(e == N_EXPERTS - 1)
    def _():
        gate = mod_ref[5:6, :]
        o_ref[...] = _layer_norm(ALPHA * x_ref[...] + gate * acc_sc[...], lng_ref[...], lnb_ref[...])


def moe_ffn_layer(x, mod, w_router, b_router, w_gu, w_down, ln_g, ln_b, tt=1024, blocks=MOE_ROW_BLOCKS):
    bsz, seq, d = x.shape
    tt = min(tt, seq)
    n_e, fe = w_down.shape[0], w_down.shape[1]
    return pl.pallas_call(
        functools.partial(_moe_kernel, fe=fe, blocks=tuple(min(b, tt) for b in blocks)),
        grid=(bsz, seq // tt, n_e),
        in_specs=[pl.BlockSpec((None, tt, d), lambda b, t, e: (b, t, 0), pipeline_mode=pl.Buffered(1)),
                  pl.BlockSpec((None, 6, d), lambda b, t, e: (b, 0, 0)),
                  _const_spec((d, ROUTER_LANES)), _const_spec((1, ROUTER_LANES)),
                  pl.BlockSpec((None, d, 2 * fe), lambda b, t, e: (e, 0, 0)),
                  pl.BlockSpec((None, fe, d), lambda b, t, e: (e, 0, 0)),
                  _const_spec((1, d)), _const_spec((1, d))],
        out_specs=pl.BlockSpec((None, tt, d), lambda b, t, e: (b, t, 0)),
        out_shape=jax.ShapeDtypeStruct(x.shape, F32),
        scratch_shapes=[pltpu.VMEM((tt, d), BF16), pltpu.VMEM((tt, ROUTER_LANES), F32),
                        pltpu.VMEM((tt, d), F32), pltpu.VMEM((tt, ROUTER_LANES), F32),
                        pltpu.VMEM((ROUTER_LANES, tt), F32), pltpu.VMEM((ROUTER_LANES, tt), F32),
                        pltpu.VMEM((tt, d), BF16)],
        compiler_params=pltpu.CompilerParams(
            dimension_semantics=("parallel", "arbitrary", "arbitrary"), vmem_limit_bytes=MOE_VMEM_LIMIT),
        name="moe_ffn",
    )(x, mod, w_router, b_router, w_gu, w_down, ln_g, ln_b)


def _pad_router(w_router, b_router):
    d, n_e = w_router.shape
    w = jnp.zeros((d, ROUTER_LANES), F32).at[:, :n_e].set(w_router)
    b = jnp.full((1, ROUTER_LANES), NEG_INF, F32).at[0, :n_e].set(b_router)
    return w, b


def _qkv_kernel(x_ref, mod_ref, w_ref, q_ref, k_ref, v_ref, *, d):
    sh, sc = mod_ref[0:1, :], mod_ref[1:2, :]
    h = (x_ref[...] * (1.0 + sc) + sh).astype(BF16)
    p = _dot(h, w_ref[...])
    q_ref[...] = (p[:, :d] * (NA_HEAD_DIM ** -0.5)).astype(BF16)
    k_ref[...] = p[:, d:2 * d].astype(BF16)
    v_ref[...] = p[:, 2 * d:].astype(BF16)


def _na_kernel(q_ref, k_ref, v_ref, bias_ref, o_ref, *, rows):
    r = pl.program_id(1)
    r0 = jnp.clip(r - NA_WIN_ROWS // 2, 0, rows - NA_WIN_ROWS)
    start = pl.multiple_of(r0 * GRID_W, GRID_W)
    nk = NA_WIN_ROWS * GRID_W
    pw_ = 2 * NA_HEAD_DIM
    pairs = range(NA_HEADS // 2)
    lanes = [slice(p * pw_, (p + 1) * pw_) for p in pairs]
    head0 = lax.broadcasted_iota(jnp.int32, (GRID_W, pw_), 1) < NA_HEAD_DIM
    q2 = {}
    for p in pairs:
        q = q_ref[:, lanes[p]]
        z = jnp.zeros_like(q)
        q2[p] = jnp.concatenate([jnp.where(head0, q, z), jnp.where(head0, z, q)], axis=0)
    s = {p: _dot_nt(q2[p], k_ref[pl.ds(start, nk), lanes[p]])
         + bias_ref[2 * p:2 * p + 2].reshape(2 * GRID_W, nk) for p in pairs}
    m = {p: jnp.max(s[p], axis=-1, keepdims=True) for p in pairs}
    e = {p: jnp.exp(s[p] - m[p]) for p in pairs}
    l = {p: jnp.sum(e[p], axis=-1, keepdims=True) for p in pairs}
    o = {p: _dot(e[p].astype(BF16), v_ref[pl.ds(start, nk), lanes[p]]) / l[p] for p in pairs}
    for p in pairs:
        o_ref[:, lanes[p]] = jnp.where(head0, o[p][:GRID_W], o[p][GRID_W:]).astype(BF16)


def _na_bias_table(rpb):
    nh, nr, nc = rpb.shape
    w = GRID_W
    qc = jnp.arange(w)[:, None]
    kc = jnp.arange(w)[None, :]
    cs = jnp.clip(qc - NA_WIN_COLS // 2, 0, w - NA_WIN_COLS)
    col_ok = (kc >= cs) & (kc < cs + NA_WIN_COLS)
    lo = w - NA_WIN_COLS
    e = jnp.pad(rpb, ((0, 0), (0, 0), (lo, 2 * w - lo - nc)))
    flat = jnp.tile(e, (1, 1, w))
    toep = flat[:, :, w - 1:w - 1 + w * (2 * w - 1)].reshape(nh, nr, w, 2 * w - 1)[..., :w]
    toep = jnp.where(col_ok[None, None], toep, NEG_INF)
    tab = jnp.stack([toep[:, dl:dl + NA_WIN_ROWS] for dl in range(NA_WIN_ROWS)], axis=1)
    tab = jnp.transpose(tab, (0, 1, 3, 2, 4))
    return tab.reshape(nh, NA_WIN_ROWS, w, NA_WIN_ROWS * w)


def _proj_ln_kernel(*refs, n_in):
    a_refs = refs[:n_in]
    x_ref, mod_ref, w_ref, lng_ref, lnb_ref, o_ref = refs[n_in:]
    a = a_refs[0][...]
    if n_in == 2:
        a = (a.astype(F32) + a_refs[1][...].astype(F32)).astype(BF16)
    mix = _dot(a, w_ref[...])
    gate = mod_ref[2:3, :]
    o_ref[...] = _layer_norm(ALPHA * x_ref[...] + gate * mix, lng_ref[...], lnb_ref[...])


def proj_residual_ln(acts, x, mod, w_o, ln_g, ln_b, tt=512):
    bsz, seq, d = x.shape
    tt = min(tt, seq)
    tile = pl.BlockSpec((None, tt, d), lambda b, t: (b, t, 0))
    return pl.pallas_call(
        functools.partial(_proj_ln_kernel, n_in=len(acts)),
        grid=(bsz, seq // tt),
        in_specs=[tile] * len(acts) + [tile, _mod_spec(d), _const_spec((d, d)),
                                       _const_spec((1, d)), _const_spec((1, d))],
        out_specs=tile,
        out_shape=jax.ShapeDtypeStruct(x.shape, F32),
        compiler_params=_cparams("parallel", "arbitrary"),
        name="proj_residual_ln",
    )(*acts, x, mod, w_o, ln_g, ln_b)


def na_mixer_layer(x, mod, w_qkv, bias_tab, w_o, ln_g, ln_b, tt=512):
    bsz, seq, d = x.shape
    tt = min(tt, seq)
    rows = seq // GRID_W
    assert rows >= NA_WIN_ROWS and seq % GRID_W == 0
    tile = pl.BlockSpec((None, tt, d), lambda b, t: (b, t, 0))
    q, k, v = pl.pallas_call(
        functools.partial(_qkv_kernel, d=d),
        grid=(bsz, seq // tt),
        in_specs=[tile, _mod_spec(d), _const_spec((d, 3 * d))],
        out_specs=[tile] * 3,
        out_shape=[jax.ShapeDtypeStruct(x.shape, BF16)] * 3,
        compiler_params=_cparams("parallel", "arbitrary"),
        name="na_qkv",
    )(x, mod, w_qkv)

    def delta(r):
        return jnp.clip(r - NA_WIN_ROWS // 2, 0, rows - NA_WIN_ROWS) - r + NA_WIN_ROWS - 1

    row_spec = pl.BlockSpec((None, GRID_W, d), lambda b, r: (b, r, 0))
    seq_spec = pl.BlockSpec((None, seq, d), lambda b, r: (b, 0, 0))
    att = pl.pallas_call(
        functools.partial(_na_kernel, rows=rows),
        grid=(bsz, rows),
        in_specs=[row_spec, seq_spec, seq_spec,
                  pl.BlockSpec((NA_HEADS, None, GRID_W, NA_WIN_ROWS * GRID_W),
                               lambda b, r: (0, delta(r), 0, 0))],
        out_specs=row_spec,
        out_shape=jax.ShapeDtypeStruct(x.shape, BF16),
        compiler_params=_cparams("parallel", "arbitrary"),
        name="na_attention",
    )(q, k, v, bias_tab)
    return proj_residual_ln([att], x, mod, w_o, ln_g, ln_b, tt)


def _rk_proj_kernel(x_ref, xp_ref, xn_ref, mod_ref, mu_ref, wrkv_ref, w1_ref, w2_ref, a1_ref, a2_ref,
                    g1_ref, g2_ref, w0_ref, a0_ref, kk_ref, ka_ref,
                    r_o, v_o, kk_o, lw_o, km_o, b_o, gate_o, *, nt, tt):
    t = pl.program_id(1)
    sh, sc = mod_ref[0:1, :], mod_ref[1:2, :]
    h = x_ref[...] * (1.0 + sc) + sh
    h_prev = jnp.where(t > 0, xp_ref[SUBLANES - 1:SUBLANES, :] * (1.0 + sc) + sh, 0.0)
    h_next = jnp.where(t < nt - 1, xn_ref[0:1, :] * (1.0 + sc) + sh, 0.0)
    h_m1, h_p1 = _shifted(h, h_prev, h_next, tt)
    xx = 0.5 * (h_m1 + h_p1) - h

    h_b, xx_b, mu_b = h.astype(BF16), xx.astype(BF16), mu_ref[...].astype(BF16)

    def mixed(p):
        return h_b + xx_b * mu_b[p:p + 1, :]

    r = _dot(mixed(0), wrkv_ref[0])
    k = _dot(mixed(1), wrkv_ref[1])
    v = _dot(mixed(2), wrkv_ref[2])
    r_o[...] = r.astype(BF16)
    v_o[...] = v.astype(BF16)
    lw_in = jnp.tanh(_dot(mixed(3), w1_ref[...])).astype(BF16)
    a_in = _dot(mixed(4), a1_ref[...]).astype(BF16)
    g_in = _sigmoid(_dot(mixed(5), g1_ref[...])).astype(BF16)

    kk = k * kk_ref[...]
    kk_o[...] = kk.astype(BF16)
    rw = w1_ref.shape[1] // 2
    rg = g1_ref.shape[1] // 2
    for z in range(2):
        wl = w0_ref[z:z + 1, :] + _dot(lw_in[:, z * rw:(z + 1) * rw], w2_ref[z])
        lw_o[z] = (-RK_DECAY_SCALE) * _sigmoid(wl)
        a_lr = _sigmoid(a0_ref[z:z + 1, :] + _dot(a_in[:, z * rw:(z + 1) * rw], a2_ref[z]))
        km_o[z] = (k * (1.0 + (a_lr - 1.0) * ka_ref[...])).astype(BF16)
        b_o[z] = (kk * a_lr).astype(BF16)
        gate_o[z] = _dot(g_in[:, z * rg:(z + 1) * rg], g2_ref[z]).astype(BF16)


def _head_sum(x):
    head0 = lax.broadcasted_iota(jnp.int32, x.shape, 1) < RK_HEAD_DIM
    s0 = jnp.sum(jnp.where(head0, x, 0.0), axis=-1, keepdims=True)
    s1 = jnp.sum(jnp.where(head0, 0.0, x), axis=-1, keepdims=True)
    return jnp.where(head0, s0, s1)


def _rk_scan_kernel(r_ref, v_ref, kk_ref, lw_ref, km_ref, b_ref, gate_ref, rk_ref, lg_ref, lb_ref,
                    o_ref, s_ref, *, reverse):
    tb, hb = r_ref.shape
    cl = RK_CHUNK
    nc, nh = tb // cl, hb // RK_HEAD_DIM

    @pl.when(pl.program_id(2) == 0)
    def _():
        s_ref[...] = jnp.zeros(s_ref.shape, F32)

    ri = lax.broadcasted_iota(jnp.int32, (tb, tb), 0)
    ci = lax.broadcasted_iota(jnp.int32, (tb, tb), 1)
    ordered = (ci >= ri) if reverse else (ci <= ri)
    tri = jnp.where((ri // cl == ci // cl) & ordered, 1.0, 0.0).astype(BF16)
    lw = lw_ref[...]
    lw_hi = lw.astype(BF16)
    lw_lo = (lw - lw_hi.astype(F32)).astype(BF16)
    cum = _dot(tri, lw_hi) + _dot(tri, lw_lo)
    last = 0 if reverse else cl - 1
    tot_rows = [cum[c * cl + last:c * cl + last + 1, :] for c in range(nc)]
    tot = jnp.concatenate([jnp.broadcast_to(tr, (cl, hb)) for tr in tot_rows], axis=0)

    rr = r_ref[...].astype(F32)
    vv = v_ref[...]
    kk = kk_ref[...].astype(F32)
    pair = 2 * RK_HEAD_DIM
    rnorm = jnp.concatenate(
        [1.0 / jnp.maximum(jnp.sqrt(_head_sum(jnp.square(kk[:, i:i + pair]))), 1e-12) for i in range(0, hb, pair)],
        axis=1)
    kkn = kk * rnorm
    km = km_ref[...].astype(F32)
    bb = b_ref[...].astype(F32) * rnorm
    e_neg = jnp.exp(-cum)
    e_rem = jnp.exp(tot - cum)
    a_t = (-kkn * jnp.exp(cum - lw)).astype(BF16)
    r_t = (rr * jnp.exp(cum)).astype(BF16)
    b_t = (bb * e_neg).astype(BF16)
    k_t = (km * e_neg).astype(BF16)
    b_h = (bb * e_rem).astype(BF16)
    k_h = (km * e_rem).astype(BF16)
    bonus_w = rr * km * rk_ref[...]

    pw_ = 2 * RK_HEAD_DIM
    npair = hb // pw_
    t_i = lax.broadcasted_iota(jnp.int32, (cl, pw_), 0)
    s_i = lax.broadcasted_iota(jnp.int32, (cl, pw_), 1) % cl
    strict = (t_i < s_i) if reverse else (t_i > s_i)
    incl = (t_i <= s_i) if reverse else (t_i >= s_i)
    eye = jnp.where(t_i == s_i, 1.0, 0.0)
    head0 = lax.broadcasted_iota(jnp.int32, (cl, pw_), 1) < RK_HEAD_DIM
    q0 = lax.broadcasted_iota(jnp.int32, (pw_, pw_), 0)
    q1 = lax.broadcasted_iota(jnp.int32, (pw_, pw_), 1)
    same_head = (q0 // RK_HEAD_DIM) == (q1 // RK_HEAD_DIM)

    head_sum = _head_sum

    def bd(x):
        z = jnp.zeros_like(x)
        return jnp.concatenate([jnp.where(head0, x, z), jnp.where(head0, z, x)], axis=0)

    keys = [(c, p) for c in range(nc) for p in range(npair)]
    rsl = {k: slice(k[0] * cl, (k[0] + 1) * cl) for k in keys}
    lsl = {k: slice(k[1] * pw_, (k[1] + 1) * pw_) for k in keys}
    blk = lambda arr, k: arr[rsl[k], lsl[k]]

    gram = {k: _dot_nt(jnp.concatenate([blk(a_t, k), blk(r_t, k)], axis=0),
                       jnp.concatenate([bd(blk(b_t, k)), bd(blk(k_t, k))], axis=0)) for k in keys}
    m_ab = {k: jnp.where(strict, gram[k][:cl, :pw_], 0.0) for k in keys}
    m_ak = {k: jnp.where(strict, gram[k][:cl, pw_:], 0.0).astype(BF16) for k in keys}
    m_rb = {k: jnp.where(incl, gram[k][cl:, :pw_], 0.0).astype(BF16) for k in keys}
    m_rk = {k: jnp.where(incl, gram[k][cl:, pw_:], 0.0).astype(BF16) for k in keys}
    vbd = {k: bd(blk(vv, k)) for k in keys}
    my = {k: _dot(jnp.concatenate([m_ak[k], m_rk[k]], axis=0), vbd[k]) for k in keys}
    v_t = {k: jnp.transpose(blk(vv, k).astype(F32)).astype(BF16) for k in keys}
    vk = {k: jnp.where(same_head, _dot(v_t[k], blk(k_h, k)), 0.0) for k in keys}
    inv = {k: eye + m_ab[k] for k in keys}
    pwb = {k: m_ab[k].astype(BF16) for k in keys}
    pw = {k: _dot(pwb[k], bd(pwb[k])) for k in keys}
    n = 4
    while n < cl:
        pwb = {k: pw[k].astype(BF16) for k in keys}
        res = {k: _dot(pwb[k], jnp.concatenate([bd(pwb[k]), bd(inv[k].astype(BF16))], axis=1)) for k in keys}
        pw = {k: res[k][:, :pw_] for k in keys}
        inv = {k: inv[k] + res[k][:, pw_:] for k in keys}
        n *= 2
    inv = {k: (inv[k] + _dot(pw[k].astype(BF16), bd(inv[k].astype(BF16)))).astype(BF16) for k in keys}
    au = {k: _dot(inv[k], jnp.concatenate([bd(blk(a_t, k)), bd(my[k][:cl].astype(BF16))], axis=1))
          for k in keys}
    a_hat = {k: au[k][:, :pw_].astype(BF16) for k in keys}
    u0_t = {k: jnp.transpose(au[k][:, pw_:]) for k in keys}
    ar = {k: jnp.concatenate([a_hat[k], blk(r_t, k)], axis=0) for k in keys}

    pairs = range(npair)
    for c in (range(nc - 1, -1, -1) if reverse else range(nc)):
        st = {p: s_ref[p] for p in pairs}
        stb = {p: st[p].astype(BF16) for p in pairs}
        res = {p: _dot_nt(ar[c, p], stb[p]) for p in pairs}
        res_t = {p: _dot_nt(stb[p], a_hat[c, p]) for p in pairs}
        ub = {p: (au[c, p][:, pw_:] + res[p][:cl]).astype(BF16) for p in pairs}
        ub_t = {p: (u0_t[c, p] + res_t[p]).astype(BF16) for p in pairs}
        y = {p: my[c, p][cl:] + res[p][cl:] + _dot(m_rb[c, p], bd(ub[p])) for p in pairs}
        for p in pairs:
            k = (c, p)
            decay = jnp.exp(tot_rows[c][:, lsl[k]])
            s_ref[p] = jnp.where(same_head, st[p] * decay + _dot(ub_t[p], blk(b_h, k)), 0.0) + vk[k]
        dy = {p: y[p] - head_sum(y[p]) * (1.0 / RK_HEAD_DIM) for p in pairs}
        var = {p: head_sum(dy[p] * dy[p]) * (1.0 / RK_HEAD_DIM) for p in pairs}
        for p in pairs:
            k = (c, p)
            yn = dy[p] * lax.rsqrt(var[p] + RK_GN_EPS) * lg_ref[:, lsl[k]] + lb_ref[:, lsl[k]]
            bonus = head_sum(blk(bonus_w, k)) * blk(vv, k).astype(F32)
            o_ref[rsl[k], lsl[k]] = (blk(gate_ref, k).astype(F32) * (yn + bonus)).astype(BF16)


def rwkv_mixer_layer(x, mod, p, ln_g, ln_b, tt=256):
    bsz, seq, d = x.shape
    tt = min(tt, seq)
    nt = seq // tt
    x_spec, prev_spec, next_spec = _tile_specs(seq, tt, d)
    tile = pl.BlockSpec((None, tt, d), lambda b, t: (b, t, 0))
    tile2 = pl.BlockSpec((2, None, tt, d), lambda b, t: (0, b, t, 0))
    rw, rg = p['w1'].shape[1] // 2, p['g1'].shape[1] // 2
    sh_bf = jax.ShapeDtypeStruct(x.shape, BF16)
    sh2_bf = jax.ShapeDtypeStruct((2,) + x.shape, BF16)
    sh2_f32 = jax.ShapeDtypeStruct((2,) + x.shape, F32)
    r, v, kkn, lw, km, bz, gate = pl.pallas_call(
        functools.partial(_rk_proj_kernel, nt=nt, tt=tt),
        grid=(bsz, nt),
        in_specs=[x_spec, prev_spec, next_spec, _mod_spec(d), _const_spec((6, d)),
                  _const_spec((3, d, d)),
                  _const_spec((d, 2 * rw)), _const_spec((2, rw, d)),
                  _const_spec((d, 2 * rw)), _const_spec((2, rw, d)),
                  _const_spec((d, 2 * rg)), _const_spec((2, rg, d)),
                  _const_spec((2, d)), _const_spec((2, d)), _const_spec((1, d)), _const_spec((1, d))],
        out_specs=[tile, tile, tile, tile2, tile2, tile2, tile2],
        out_shape=[sh_bf, sh_bf, sh_bf, sh2_f32, sh2_bf, sh2_bf, sh2_bf],
        compiler_params=_cparams("parallel", "arbitrary"),
        name="rwkv_proj",
    )(x, x, x, mod, p['mu'], p['w_rkv'], p['w1'], p['w2'], p['a1'], p['a2'], p['g1'], p['g2'],
      p['w0'], p['a0'], p['k_k'], p['k_a'])

    tb = min(RK_TOK_BLOCK, seq)
    hb = RK_LANE_BLOCK
    nb = seq // tb
    outs = []
    for z in range(2):
        rev = z == 1

        def tok(i, rev=rev):
            return nb - 1 - i if rev else i

        blk = pl.BlockSpec((None, tb, hb), lambda b, g, i: (b, tok(i), g))
        blk2 = pl.BlockSpec((None, None, tb, hb), lambda b, g, i, z=z: (z, b, tok(i), g))
        vec = pl.BlockSpec((1, hb), lambda b, g, i: (0, g))
        vec2 = pl.BlockSpec((None, 1, hb), lambda b, g, i, z=z: (z, 0, g))
        outs.append(pl.pallas_call(
            functools.partial(_rk_scan_kernel, reverse=rev),
            grid=(bsz, d // hb, nb),
            in_specs=[blk, blk, blk, blk2, blk2, blk2, blk2, vec, vec2, vec2],
            out_specs=blk,
            out_shape=sh_bf,
            scratch_shapes=[pltpu.VMEM((hb // (2 * RK_HEAD_DIM), 2 * RK_HEAD_DIM, 2 * RK_HEAD_DIM), F32)],
            compiler_params=_cparams("parallel", "parallel", "arbitrary"),
            name="rwkv_scan_rev" if rev else "rwkv_scan_fwd",
        )(r, v, kkn, lw, km, bz, gate, p['r_k'], p['lnx_g'], p['lnx_b']))
    return proj_residual_ln(outs, x, mod, p['w_o'], ln_g, ln_b)


def _prepare_params(w):
    d = D_MODEL
    bf = lambda a: a.astype(BF16)
    row = lambda a: a.reshape(1, d)
    p = {
        'w_ada': w['w_ada'], 'b_ada': w['b_ada'],
        'ln_g': w['ln_g'], 'ln_b': w['ln_b'],
        'conv_w_in': bf(w['conv_w_in']), 'conv_w': w['conv_w'], 'conv_w_out': bf(w['conv_w_out']),
        'na_w_qkv': bf(w['na_w_qkv']), 'na_w_o': bf(w['na_w_o']),
        'na_bias': [_na_bias_table(w['na_rpb'][j]) for j in range(w['na_rpb'].shape[0])],
        'ffn_w_gu': bf(w['ffn_w_gu']), 'ffn_w_down': bf(w['ffn_w_down']),
        'moe_router': [_pad_router(w['moe_w_router'][j], w['moe_b_router'][j])
                       for j in range(w['moe_w_router'].shape[0])],
        'moe_w_gu': bf(w['moe_w_gu']), 'moe_w_down': bf(w['moe_w_down']),
        'rk': [],
    }
    cat = lambda a: jnp.concatenate([a[0], a[1]], axis=1)
    for j in range(w['rk_mu'].shape[0]):
        p['rk'].append({
            'mu': w['rk_mu'][j], 'w_rkv': bf(w['rk_w_rkv'][j]),
            'w1': bf(cat(w['rk_w1'][j])), 'w2': bf(w['rk_w2'][j]),
            'a1': bf(cat(w['rk_a1'][j])), 'a2': bf(w['rk_a2'][j]),
            'g1': bf(cat(w['rk_g1'][j])), 'g2': bf(w['rk_g2'][j]),
            'w0': w['rk_w0'][j], 'a0': w['rk_a0'][j],
            'k_k': row(w['rk_k_k'][j]), 'k_a': row(w['rk_k_a'][j]),
            'r_k': w['rk_r_k'][j].reshape(1, d),
            'lnx_g': w['rk_lnx_g'][j].reshape(2, 1, d), 'lnx_b': w['rk_lnx_b'][j].reshape(2, 1, d),
            'w_o': bf(w['rk_w_o'][j]),
        })
    return p


def _trunk(x, mod_all, p):
    d = D_MODEL
    for i in range(DEPTH):
        mod = mod_all[i]
        lng = lambda s: p['ln_g'][i, s].reshape(1, d)
        lnb = lambda s: p['ln_b'][i, s].reshape(1, d)
        kind, j = i % 3, i // 3
        if kind == 0:
            x = conv_mixer_layer(x, mod, p['conv_w_in'][j], p['conv_w'][j], p['conv_w_out'][j],
                                 lng(0), lnb(0))
        elif kind == 1:
            x = na_mixer_layer(x, mod, p['na_w_qkv'][j], p['na_bias'][j], p['na_w_o'][j], lng(0), lnb(0))
        else:
            x = rwkv_mixer_layer(x, mod, p['rk'][j], lng(0), lnb(0))
        if i % 2 == 0:
            x = dense_ffn_layer(x, mod, p['ffn_w_gu'][i // 2], p['ffn_w_down'][i // 2], lng(1), lnb(1))
        else:
            wr, br = p['moe_router'][i // 2]
            x = moe_ffn_layer(x, mod, wr, br, p['moe_w_gu'][i // 2], p['moe_w_down'][i // 2],
                              lng(1), lnb(1))
    return x


def kernel(x_prompt, x_sample, c_prompt, c_sample, w_ada, b_ada, ln_g, ln_b, conv_w_in, conv_w, conv_w_out, na_w_qkv, na_rpb, na_w_o, rk_mu, rk_w_rkv, rk_w0, rk_w1, rk_w2, rk_a0, rk_a1, rk_a2, rk_g1, rk_g2, rk_k_k, rk_k_a, rk_r_k, rk_lnx_g, rk_lnx_b, rk_w_o, ffn_w_gu, ffn_w_down, moe_w_router, moe_b_router, moe_w_gu, moe_w_down):
    p = _prepare_params(dict(
        w_ada=w_ada, b_ada=b_ada, ln_g=ln_g, ln_b=ln_b,
        conv_w_in=conv_w_in, conv_w=conv_w, conv_w_out=conv_w_out,
        na_w_qkv=na_w_qkv, na_rpb=na_rpb, na_w_o=na_w_o,
        rk_mu=rk_mu, rk_w_rkv=rk_w_rkv, rk_w0=rk_w0, rk_w1=rk_w1, rk_w2=rk_w2,
        rk_a0=rk_a0, rk_a1=rk_a1, rk_a2=rk_a2, rk_g1=rk_g1, rk_g2=rk_g2,
        rk_k_k=rk_k_k, rk_k_a=rk_k_a, rk_r_k=rk_r_k, rk_lnx_g=rk_lnx_g, rk_lnx_b=rk_lnx_b,
        rk_w_o=rk_w_o, ffn_w_gu=ffn_w_gu, ffn_w_down=ffn_w_down,
        moe_w_router=moe_w_router, moe_b_router=moe_b_router,
        moe_w_gu=moe_w_gu, moe_w_down=moe_w_down))
    n_p = c_prompt.shape[0]
    mod_all = ada_modulation(jnp.concatenate([c_prompt, c_sample], axis=0), p['w_ada'], p['b_ada'])
    return (_trunk(x_prompt, mod_all[:, :n_p], p), _trunk(x_sample, mod_all[:, n_p:], p))
```

```python
import functools

import jax
import jax.numpy as jnp
from jax import lax
from jax.experimental import pallas as pl
from jax.experimental.pallas import tpu as pltpu

F32 = jnp.float32
BF16 = jnp.bfloat16

D_MODEL = 1024
DEPTH = 4
ALPHA = (2 * DEPTH) ** 0.25
LN_EPS = 1e-5

GRID_W = 64
NA_HEADS = 16
NA_HEAD_DIM = D_MODEL // NA_HEADS
NA_WIN_ROWS = 8
NA_WIN_COLS = 16
NEG_INF = -1e30

RK_HEAD_DIM = 64
RK_GN_EPS = 64e-5
RK_DECAY_SCALE = 0.6065306597126334
RK_CHUNK = 64
RK_TOK_BLOCK = 256
RK_LANE_BLOCK = 1024

N_EXPERTS = 8
ROUTER_LANES = 128

SUBLANES = 8
VMEM_LIMIT = 52 * 1024 * 1024
MOE_VMEM_LIMIT = 58 * 1024 * 1024


def _cparams(*sem):
    return pltpu.CompilerParams(dimension_semantics=sem, vmem_limit_bytes=VMEM_LIMIT)


def _const_spec(shape):
    nd = len(shape)
    return pl.BlockSpec(shape, lambda *_: (0,) * nd, pipeline_mode=pl.Buffered(1))


def _dot(a, b):
    return jnp.dot(a, b, preferred_element_type=F32)


def _dot_nt(a, b):
    return lax.dot_general(a, b, (((1,), (1,)), ((), ())), preferred_element_type=F32)


def _sigmoid(x):
    return 1.0 / (1.0 + jnp.exp(-x))


def _layer_norm(y, g, b):
    mu = jnp.mean(y, axis=-1, keepdims=True)
    d = y - mu
    var = jnp.mean(d * d, axis=-1, keepdims=True)
    return d * lax.rsqrt(var + LN_EPS) * g + b


def _ada_kernel(c_ref, w_ref, b_ref, o_ref):
    c = c_ref[...]
    s = (c * _sigmoid(c)).astype(BF16)
    o_ref[...] = _dot(s, w_ref[...].astype(BF16)) + b_ref[...]


def ada_modulation(c, w_ada, b_ada):
    bsz, d = c.shape
    depth = w_ada.shape[0]
    out = pl.pallas_call(
        _ada_kernel,
        grid=(depth, 6),
        in_specs=[
            pl.BlockSpec((bsz, d), lambda i, j: (0, 0)),
            pl.BlockSpec((None, d, d), lambda i, j: (i, 0, j)),
            pl.BlockSpec((None, None, 1, d), lambda i, j: (i, j, 0, 0)),
        ],
        out_specs=pl.BlockSpec((None, None, bsz, d), lambda i, j: (i, j, 0, 0)),
        out_shape=jax.ShapeDtypeStruct((depth, 6, bsz, d), F32),
        compiler_params=_cparams("arbitrary", "arbitrary"),
        name="ada_modulation",
    )(c, w_ada, b_ada.reshape(depth, 6, 1, d))
    return jnp.transpose(out, (0, 2, 1, 3))


def _tile_specs(seq, tt, d):
    nb8 = seq // SUBLANES
    per = tt // SUBLANES
    x_spec = pl.BlockSpec((None, tt, d), lambda b, t: (b, t, 0))
    prev_spec = pl.BlockSpec((None, SUBLANES, d), lambda b, t: (b, jnp.maximum(t * per - 1, 0), 0))
    next_spec = pl.BlockSpec((None, SUBLANES, d), lambda b, t: (b, jnp.minimum((t + 1) * per, nb8 - 1), 0))
    return x_spec, prev_spec, next_spec


def _mod_spec(d):
    return pl.BlockSpec((None, 6, d), lambda b, t: (b, 0, 0))


def _shifted(cur, prev_row, next_row, tt):
    row = lax.broadcasted_iota(jnp.int32, (tt, 1), 0)
    m1 = jnp.where(row == 0, prev_row, pltpu.roll(cur, 1, 0))
    p1 = jnp.where(row == tt - 1, next_row, pltpu.roll(cur, tt - 1, 0))
    return m1, p1


def _conv_kernel(x_ref, xp_ref, xn_ref, mod_ref, win_ref, cw_ref, wout_ref, lng_ref, lnb_ref,
                 o_ref, *, nt, tt, d):
    t = pl.program_id(1)
    sh, sc, gate = mod_ref[0:1, :], mod_ref[1:2, :], mod_ref[2:3, :]
    parts = [slice(i * tt // CONV_PARTS, (i + 1) * tt // CONV_PARTS) for i in range(CONV_PARTS)]
    xs = [x_ref[r, :] for r in parts]
    ps = [_dot((x * (1.0 + sc) + sh).astype(BF16), win_ref[...]) for x in xs]
    z = jnp.concatenate([p[:, d:2 * d] * p[:, 2 * d:] for p in ps], axis=0)
    halo = jnp.concatenate([xp_ref[...], xn_ref[...]], axis=0)
    hh = (halo * (1.0 + sc) + sh).astype(BF16)
    ph = _dot(hh, win_ref[:, d:])
    zh = ph[:, :d] * ph[:, d:]
    z_prev = jnp.where(t > 0, zh[SUBLANES - 1:SUBLANES, :], 0.0)
    z_next = jnp.where(t < nt - 1, zh[SUBLANES:SUBLANES + 1, :], 0.0)
    z_m1, z_p1 = _shifted(z, z_prev, z_next, tt)
    conv = z_m1 * cw_ref[0:1, :] + z * cw_ref[1:2, :] + z_p1 * cw_ref[2:3, :]
    mixes = [_dot((p[:, :d] * conv[r]).astype(BF16), wout_ref[...]) for p, r in zip(ps, parts)]
    for x, r, mix in zip(xs, parts, mixes):
        o_ref[r, :] = _layer_norm(ALPHA * x + gate * mix, lng_ref[...], lnb_ref[...])


CONV_PARTS = 4


def conv_mixer_layer(x, mod, w_in, conv_w, w_out, ln_g, ln_b, tt=1024):
    bsz, seq, d = x.shape
    tt = min(tt, seq)
    nt = seq // tt
    x_spec, prev_spec, next_spec = _tile_specs(seq, tt, d)
    return pl.pallas_call(
        functools.partial(_conv_kernel, nt=nt, tt=tt, d=d),
        grid=(bsz, nt),
        in_specs=[x_spec, prev_spec, next_spec, _mod_spec(d),
                  _const_spec((d, 3 * d)), _const_spec((3, d)), _const_spec((d, d)),
                  _const_spec((1, d)), _const_spec((1, d))],
        out_specs=pl.BlockSpec((None, tt, d), lambda b, t: (b, t, 0)),
        out_shape=jax.ShapeDtypeStruct(x.shape, F32),
        compiler_params=_cparams("parallel", "arbitrary"),
        name="conv_mixer",
    )(x, x, x, mod, w_in, conv_w, w_out, ln_g, ln_b)


def _ffn_kernel(x_ref, mod_ref, wgu_ref, wd_ref, lng_ref, lnb_ref, o_ref, *, ff, fc):
    sh, sc, gate = mod_ref[3:4, :], mod_ref[4:5, :], mod_ref[5:6, :]
    tt = x_ref.shape[0]
    parts = [slice(i * tt // FFN_PARTS, (i + 1) * tt // FFN_PARTS) for i in range(FFN_PARTS)]
    chunks = range(ff // fc)
    xs = [x_ref[r, :] for r in parts]
    hs = [(x * (1.0 + sc) + sh).astype(BF16) for x in xs]
    gs = [[_dot(h, wgu_ref[:, c * fc:(c + 1) * fc]) for c in chunks] for h in hs]
    us = [[_dot(h, wgu_ref[:, ff + c * fc:ff + (c + 1) * fc]) for c in chunks] for h in hs]
    acts = [[(g * _sigmoid(g) * u).astype(BF16) for g, u in zip(gr, ur)] for gr, ur in zip(gs, us)]
    for x, r, ar in zip(xs, parts, acts):
        acc = _dot(ar[0], wd_ref[0:fc, :])
        for c in chunks[1:]:
            acc = acc + _dot(ar[c], wd_ref[c * fc:(c + 1) * fc, :])
        o_ref[r, :] = _layer_norm(ALPHA * x + gate * acc, lng_ref[...], lnb_ref[...])


FFN_PARTS = 2


def dense_ffn_layer(x, mod, w_gu, w_down, ln_g, ln_b, tt=512):
    bsz, seq, d = x.shape
    tt = min(tt, seq)
    ff = w_down.shape[0]
    fc = ff
    return pl.pallas_call(
        functools.partial(_ffn_kernel, ff=ff, fc=fc),
        grid=(bsz, seq // tt),
        in_specs=[pl.BlockSpec((None, tt, d), lambda b, t: (b, t, 0)), _mod_spec(d),
                  _const_spec((d, 2 * ff)), _const_spec((ff, d)),
                  _const_spec((1, d)), _const_spec((1, d))],
        out_specs=pl.BlockSpec((None, tt, d), lambda b, t: (b, t, 0)),
        out_shape=jax.ShapeDtypeStruct(x.shape, F32),
        compiler_params=_cparams("parallel", "arbitrary"),
        name="dense_ffn",
    )(x, mod, w_gu, w_down, ln_g, ln_b)


MOE_ROW_CHUNK = 256
MOE_ROW_BLOCKS = (128, 256)


def _moe_kernel(x_ref, mod_ref, wr_ref, br_ref, wgu_ref, wd_ref, lng_ref, lnb_ref, o_ref,
                h_sc, comb_sc, acc_sc, rankc_sc, rankr_sc, maskr_sc, y_sc, *, fe, blocks):
    e = pl.program_id(2)
    tt = comb_sc.shape[0]
    lane = lax.broadcasted_iota(jnp.int32, comb_sc.shape, 1)

    @pl.when(e == 0)
    def _():
        sh, sc = mod_ref[3:4, :], mod_ref[4:5, :]
        h = x_ref[...] * (1.0 + sc) + sh
        h_hi = h.astype(BF16)
        h_sc[...] = h_hi
        h_lo = (h - h_hi.astype(F32)).astype(BF16)
        wr = wr_ref[...]
        w_hi = wr.astype(BF16)
        w_lo = (wr - w_hi.astype(F32)).astype(BF16)
        logits = _dot(h_hi, w_hi) + _dot(h_lo, w_hi) + _dot(h_hi, w_lo) + br_ref[...]
        m = jnp.max(logits, axis=-1, keepdims=True)
        ex = jnp.exp(logits - m)
        probs = ex / jnp.sum(ex, axis=-1, keepdims=True)
        valid = lane < N_EXPERTS
        p = jnp.where(valid, probs, -1.0)
        p1 = jnp.max(p, axis=-1, keepdims=True)
        i1 = jnp.min(jnp.where(p == p1, lane, ROUTER_LANES), axis=-1, keepdims=True)
        pr = jnp.where(lane == i1, -1.0, p)
        p2 = jnp.max(pr, axis=-1, keepdims=True)
        i2 = jnp.min(jnp.where(pr == p2, lane, ROUTER_LANES), axis=-1, keepdims=True)
        tot = p1 + p2
        comb = jnp.where(lane == i1, p1 / tot, 0.0) + jnp.where(lane == i2, p2 / tot, 0.0)
        comb_sc[...] = comb
        acc_sc[...] = jnp.zeros(acc_sc.shape, F32)
        y_sc[...] = jnp.zeros(y_sc.shape, BF16)
        routed = jnp.where(comb > 0.0, 1.0, 0.0).astype(BF16)
        q0 = lax.broadcasted_iota(jnp.int32, (ROUTER_LANES, ROUTER_LANES), 0)
        q1 = lax.broadcasted_iota(jnp.int32, (ROUTER_LANES, ROUTER_LANES), 1)
        routed_t = _dot_nt(jnp.where(q0 == q1, 1.0, 0.0).astype(BF16), routed)
        maskr_sc[...] = routed_t
        routed_tb = routed_t.astype(BF16)
        rc = min(MOE_ROW_CHUNK, tt)
        for c in range(tt // rc):
            r_i = lax.broadcasted_iota(jnp.int32, (rc, tt), 0) + c * rc
            c_i = lax.broadcasted_iota(jnp.int32, (rc, tt), 1)
            rankc_sc[c * rc:(c + 1) * rc, :] = _dot(jnp.where(c_i < r_i, 1.0, 0.0).astype(BF16), routed)
            r_j = lax.broadcasted_iota(jnp.int32, (tt, rc), 0)
            c_j = lax.broadcasted_iota(jnp.int32, (tt, rc), 1) + c * rc
            rankr_sc[:, c * rc:(c + 1) * rc] = _dot(routed_tb, jnp.where(r_j < c_j, 1.0, 0.0).astype(BF16))

    def expert(hb):
        g = _dot(hb, wgu_ref[:, :fe])
        u = _dot(hb, wgu_ref[:, fe:])
        act = (g * _sigmoid(g) * u).astype(BF16)
        return _dot(act, wd_ref[...])

    w_col = jnp.sum(jnp.where(lane == e, comb_sc[...], 0.0), axis=-1, keepdims=True)
    rk_col = jnp.sum(jnp.where(lane == e, rankc_sc[...], 0.0), axis=-1, keepdims=True).astype(jnp.int32)
    rk_row = rankr_sc[pl.ds(e, 1), :].astype(jnp.int32)
    mk_row = maskr_sc[pl.ds(e, 1), :]
    n_routed = jnp.sum(mk_row).astype(jnp.int32)
    half = tt // 2
    gb, sb = blocks

    def gather_block(j, carry):
        base = pl.multiple_of(j * gb, gb)
        sub = lax.broadcasted_iota(jnp.int32, (gb, tt), 0) + base
        pick = jnp.where((rk_row == sub) & (mk_row > 0.0), 1.0, 0.0).astype(BF16)
        y_sc[pl.ds(base, gb), :] = expert(_dot(pick, h_sc[...]).astype(BF16)).astype(BF16)
        return carry

    lax.fori_loop(0, (n_routed + gb - 1) // gb, gather_block, 0)

    for c in range(tt // sb):
        @pl.when(n_routed > c * sb)
        def _(c=c):
            for r0 in (0, half):
                rows = slice(r0, r0 + half)
                ln_i = lax.broadcasted_iota(jnp.int32, (half, sb), 1) + c * sb
                put = jnp.where((rk_col[rows] == ln_i) & (w_col[rows] > 0.0), 1.0, 0.0).astype(BF16)
                acc_sc[rows, :] += w_col[rows] * _dot(put, y_sc[c * sb:(c + 1) * sb, :])

    @pl.when(e == N_EXPERTS - 1)
    def _():
        gate = mod_ref[5:6, :]
        o_ref[...] = _layer_norm(ALPHA * x_ref[...] + gate * acc_sc[...], lng_ref[...], lnb_ref[...])


def moe_ffn_layer(x, mod, w_router, b_router, w_gu, w_down, ln_g, ln_b, tt=1024, blocks=MOE_ROW_BLOCKS):
    bsz, seq, d = x.shape
    tt = min(tt, seq)
    n_e, fe = w_down.shape[0], w_down.shape[1]
    return pl.pallas_call(
        functools.partial(_moe_kernel, fe=fe, blocks=tuple(min(b, tt) for b in blocks)),
        grid=(bsz, seq // tt, n_e),
        in_specs=[pl.BlockSpec((None, tt, d), lambda b, t, e: (b, t, 0), pipeline_mode=pl.Buffered(1)),
                  pl.BlockSpec((None, 6, d), lambda b, t, e: (b, 0, 0)),
                  _const_spec((d, ROUTER_LANES)), _const_spec((1, ROUTER_LANES)),
                  pl.BlockSpec((None, d, 2 * fe), lambda b, t, e: (e, 0, 0)),
                  pl.BlockSpec((None, fe, d), lambda b, t, e: (e, 0, 0)),
                  _const_spec((1, d)), _const_spec((1, d))],
        out_specs=pl.BlockSpec((None, tt, d), lambda b, t, e: (b, t, 0)),
        out_shape=jax.ShapeDtypeStruct(x.shape, F32),
        scratch_shapes=[pltpu.VMEM((tt, d), BF16), pltpu.VMEM((tt, ROUTER_LANES), F32),
                        pltpu.VMEM((tt, d), F32), pltpu.VMEM((tt, ROUTER_LANES), F32),
                        pltpu.VMEM((ROUTER_LANES, tt), F32), pltpu.VMEM((ROUTER_LANES, tt), F32),
                        pltpu.VMEM((tt, d), BF16)],
        compiler_params=pltpu.CompilerParams(
            dimension_semantics=("parallel", "arbitrary", "arbitrary"), vmem_limit_bytes=MOE_VMEM_LIMIT),
        name="moe_ffn",
    )(x, mod, w_router, b_router, w_gu, w_down, ln_g, ln_b)


def _pad_router(w_router, b_router):
    d, n_e = w_router.shape
    w = jnp.zeros((d, ROUTER_LANES), F32).at[:, :n_e].set(w_router)
    b = jnp.full((1, ROUTER_LANES), NEG_INF, F32).at[0, :n_e].set(b_router)
    return w, b


def _qkv_kernel(x_ref, mod_ref, w_ref, q_ref, k_ref, v_ref, *, d):
    sh, sc = mod_ref[0:1, :], mod_ref[1:2, :]
    h = (x_ref[...] * (1.0 + sc) + sh).astype(BF16)
    p = _dot(h, w_ref[...])
    q_ref[...] = (p[:, :d] * (NA_HEAD_DIM ** -0.5)).astype(BF16)
    k_ref[...] = p[:, d:2 * d].astype(BF16)
    v_ref[...] = p[:, 2 * d:].astype(BF16)


NA_ROWS_PER_STEP = 2


def _na_kernel(q_ref, k_ref, v_ref, *rest, rows):
    bias_refs, o_ref = rest[:-1], rest[-1]
    nk = NA_WIN_ROWS * GRID_W
    pw_ = 2 * NA_HEAD_DIM
    pairs = range(NA_HEADS // 2)
    lanes = [slice(p * pw_, (p + 1) * pw_) for p in pairs]
    head0 = lax.broadcasted_iota(jnp.int32, (GRID_W, pw_), 1) < NA_HEAD_DIM
    keys = [(i, p) for i in range(len(bias_refs)) for p in pairs]
    qrows = {i: slice(i * GRID_W, (i + 1) * GRID_W) for i in range(len(bias_refs))}
    start = {}
    for i in range(len(bias_refs)):
        r = pl.program_id(1) * len(bias_refs) + i
        r0 = jnp.clip(r - NA_WIN_ROWS // 2, 0, rows - NA_WIN_ROWS)
        start[i] = pl.multiple_of(r0 * GRID_W, GRID_W)
    q2 = {}
    for i, p in keys:
        q = q_ref[qrows[i], lanes[p]]
        z = jnp.zeros_like(q)
        q2[i, p] = jnp.concatenate([jnp.where(head0, q, z), jnp.where(head0, z, q)], axis=0)
    s = {(i, p): _dot_nt(q2[i, p], k_ref[pl.ds(start[i], nk), lanes[p]])
         + bias_refs[i][2 * p:2 * p + 2].reshape(2 * GRID_W, nk) for i, p in keys}
    m = {k: jnp.max(s[k], axis=-1, keepdims=True) for k in keys}
    e = {k: jnp.exp(s[k] - m[k]) for k in keys}
    l = {k: jnp.sum(e[k], axis=-1, keepdims=True) for k in keys}
    o = {(i, p): _dot(e[i, p].astype(BF16), v_ref[pl.ds(start[i], nk), lanes[p]]) / l[i, p] for i, p in keys}
    for i, p in keys:
        o_ref[qrows[i], lanes[p]] = jnp.where(head0, o[i, p][:GRID_W], o[i, p][GRID_W:]).astype(BF16)


def _na_bias_table(rpb):
    nh, nr, nc = rpb.shape
    w = GRID_W
    qc = jnp.arange(w)[:, None]
    kc = jnp.arange(w)[None, :]
    cs = jnp.clip(qc - NA_WIN_COLS // 2, 0, w - NA_WIN_COLS)
    col_ok = (kc >= cs) & (kc < cs + NA_WIN_COLS)
    lo = w - NA_WIN_COLS
    e = jnp.pad(rpb, ((0, 0), (0, 0), (lo, 2 * w - lo - nc)))
    flat = jnp.tile(e, (1, 1, w))
    toep = flat[:, :, w - 1:w - 1 + w * (2 * w - 1)].reshape(nh, nr, w, 2 * w - 1)[..., :w]
    toep = jnp.where(col_ok[None, None], toep, NEG_INF)
    tab = jnp.stack([toep[:, dl:dl + NA_WIN_ROWS] for dl in range(NA_WIN_ROWS)], axis=1)
    tab = jnp.transpose(tab, (0, 1, 3, 2, 4))
    return tab.reshape(nh, NA_WIN_ROWS, w, NA_WIN_ROWS * w)


def _proj_ln_kernel(*refs, n_in):
    a_refs = refs[:n_in]
    x_ref, mod_ref, w_ref, lng_ref, lnb_ref, o_ref = refs[n_in:]
    a = a_refs[0][...]
    if n_in == 2:
        a = (a.astype(F32) + a_refs[1][...].astype(F32)).astype(BF16)
    mix = _dot(a, w_ref[...])
    gate = mod_ref[2:3, :]
    o_ref[...] = _layer_norm(ALPHA * x_ref[...] + gate * mix, lng_ref[...], lnb_ref[...])


def proj_residual_ln(acts, x, mod, w_o, ln_g, ln_b, tt=1024):
    bsz, seq, d = x.shape
    tt = min(tt, seq)
    tile = pl.BlockSpec((None, tt, d), lambda b, t: (b, t, 0))
    return pl.pallas_call(
        functools.partial(_proj_ln_kernel, n_in=len(acts)),
        grid=(bsz, seq // tt),
        in_specs=[tile] * len(acts) + [tile, _mod_spec(d), _const_spec((d, d)),
                                       _const_spec((1, d)), _const_spec((1, d))],
        out_specs=tile,
        out_shape=jax.ShapeDtypeStruct(x.shape, F32),
        compiler_params=_cparams("parallel", "arbitrary"),
        name="proj_residual_ln",
    )(*acts, x, mod, w_o, ln_g, ln_b)


def na_mixer_layer(x, mod, w_qkv, bias_tab, w_o, ln_g, ln_b, tt=512):
    bsz, seq, d = x.shape
    tt = min(tt, seq)
    rows = seq // GRID_W
    assert rows >= NA_WIN_ROWS and seq % GRID_W == 0
    tile = pl.BlockSpec((None, tt, d), lambda b, t: (b, t, 0))
    q, k, v = pl.pallas_call(
        functools.partial(_qkv_kernel, d=d),
        grid=(bsz, seq // tt),
        in_specs=[tile, _mod_spec(d), _const_spec((d, 3 * d))],
        out_specs=[tile] * 3,
        out_shape=[jax.ShapeDtypeStruct(x.shape, BF16)] * 3,
        compiler_params=_cparams("parallel", "arbitrary"),
        name="na_qkv",
    )(x, mod, w_qkv)

    def delta(r):
        return jnp.clip(r - NA_WIN_ROWS // 2, 0, rows - NA_WIN_ROWS) - r + NA_WIN_ROWS - 1

    rps = NA_ROWS_PER_STEP
    assert rows % rps == 0
    row_spec = pl.BlockSpec((None, rps * GRID_W, d), lambda b, r: (b, r, 0))
    seq_spec = pl.BlockSpec((None, seq, d), lambda b, r: (b, 0, 0))
    bias_specs = [pl.BlockSpec((NA_HEADS, None, GRID_W, NA_WIN_ROWS * GRID_W),
                               lambda b, r, i=i: (0, delta(r * rps + i), 0, 0)) for i in range(rps)]
    att = pl.pallas_call(
        functools.partial(_na_kernel, rows=rows),
        grid=(bsz, rows // rps),
        in_specs=[row_spec, seq_spec, seq_spec] + bias_specs,
        out_specs=row_spec,
        out_shape=jax.ShapeDtypeStruct(x.shape, BF16),
        compiler_params=_cparams("parallel", "arbitrary"),
        name="na_attention",
    )(q, k, v, *([bias_tab] * rps))
    return proj_residual_ln([att], x, mod, w_o, ln_g, ln_b)


def _rk_proj_kernel(x_ref, xp_ref, xn_ref, mod_ref, mu_ref, wrkv_ref, w1_ref, w2_ref, a1_ref, a2_ref,
                    g1_ref, g2_ref, w0_ref, a0_ref, kk_ref, ka_ref,
                    r_o, v_o, kk_o, lw_o, km_o, b_o, gate_o, *, nt, tt):
    t = pl.program_id(1)
    sh, sc = mod_ref[0:1, :], mod_ref[1:2, :]
    h = x_ref[...] * (1.0 + sc) + sh
    h_prev = jnp.where(t > 0, xp_ref[SUBLANES - 1:SUBLANES, :] * (1.0 + sc) + sh, 0.0)
    h_next = jnp.where(t < nt - 1, xn_ref[0:1, :] * (1.0 + sc) + sh, 0.0)
    h_m1, h_p1 = _shifted(h, h_prev, h_next, tt)
    xx = 0.5 * (h_m1 + h_p1) - h

    h_b, xx_b, mu_b = h.astype(BF16), xx.astype(BF16), mu_ref[...].astype(BF16)

    def mixed(p):
        return h_b + xx_b * mu_b[p:p + 1, :]

    r = _dot(mixed(0), wrkv_ref[0])
    k = _dot(mixed(1), wrkv_ref[1])
    v = _dot(mixed(2), wrkv_ref[2])
    r_o[...] = r.astype(BF16)
    v_o[...] = v.astype(BF16)
    lw_in = jnp.tanh(_dot(mixed(3), w1_ref[...])).astype(BF16)
    a_in = _dot(mixed(4), a1_ref[...]).astype(BF16)
    g_in = _sigmoid(_dot(mixed(5), g1_ref[...])).astype(BF16)

    kk = k * kk_ref[...]
    kk_o[...] = kk.astype(BF16)
    rw = w1_ref.shape[1] // 2
    rg = g1_ref.shape[1] // 2
    for z in range(2):
        wl = w0_ref[z:z + 1, :] + _dot(lw_in[:, z * rw:(z + 1) * rw], w2_ref[z])
        lw_o[z] = (-RK_DECAY_SCALE) * _sigmoid(wl)
        a_lr = _sigmoid(a0_ref[z:z + 1, :] + _dot(a_in[:, z * rw:(z + 1) * rw], a2_ref[z]))
        km_o[z] = (k * (1.0 + (a_lr - 1.0) * ka_ref[...])).astype(BF16)
        b_o[z] = (kk * a_lr).astype(BF16)
        gate_o[z] = _dot(g_in[:, z * rg:(z + 1) * rg], g2_ref[z]).astype(BF16)


def _head_sum(x):
    head0 = lax.broadcasted_iota(jnp.int32, x.shape, 1) < RK_HEAD_DIM
    s0 = jnp.sum(jnp.where(head0, x, 0.0), axis=-1, keepdims=True)
    s1 = jnp.sum(jnp.where(head0, 0.0, x), axis=-1, keepdims=True)
    return jnp.where(head0, s0, s1)


def _rk_scan_kernel(r_ref, v_ref, kk_ref, lw_ref, km_ref, b_ref, gate_ref, rk_ref, lg_ref, lb_ref,
                    o_ref, s_ref, *, reverse):
    tb, hb = r_ref.shape
    cl = RK_CHUNK
    nc, nh = tb // cl, hb // RK_HEAD_DIM

    @pl.when(pl.program_id(2) == 0)
    def _():
        s_ref[...] = jnp.zeros(s_ref.shape, F32)

    ri = lax.broadcasted_iota(jnp.int32, (tb, tb), 0)
    ci = lax.broadcasted_iota(jnp.int32, (tb, tb), 1)
    ordered = (ci >= ri) if reverse else (ci <= ri)
    tri = jnp.where((ri // cl == ci // cl) & ordered, 1.0, 0.0).astype(BF16)
    lw = lw_ref[...]
    lw_hi = lw.astype(BF16)
    lw_lo = (lw - lw_hi.astype(F32)).astype(BF16)
    cum = _dot(tri, lw_hi) + _dot(tri, lw_lo)
    last = 0 if reverse else cl - 1
    tot_rows = [cum[c * cl + last:c * cl + last + 1, :] for c in range(nc)]
    tot = jnp.concatenate([jnp.broadcast_to(tr, (cl, hb)) for tr in tot_rows], axis=0)

    rr = r_ref[...].astype(F32)
    vv = v_ref[...]
    kk = kk_ref[...].astype(F32)
    pair = 2 * RK_HEAD_DIM
    rnorm = jnp.concatenate(
        [1.0 / jnp.maximum(jnp.sqrt(_head_sum(jnp.square(kk[:, i:i + pair]))), 1e-12) for i in range(0, hb, pair)],
        axis=1)
    kkn = kk * rnorm
    km = km_ref[...].astype(F32)
    bb = b_ref[...].astype(F32) * rnorm
    e_neg = jnp.exp(-cum)
    e_rem = jnp.exp(tot - cum)
    a_t = (-kkn * jnp.exp(cum - lw)).astype(BF16)
    r_t = (rr * jnp.exp(cum)).astype(BF16)
    b_t = (bb * e_neg).astype(BF16)
    k_t = (km * e_neg).astype(BF16)
    b_h = (bb * e_rem).astype(BF16)
    k_h = (km * e_rem).astype(BF16)
    bonus_w = rr * km * rk_ref[...]

    pw_ = 2 * RK_HEAD_DIM
    npair = hb // pw_
    t_i = lax.broadcasted_iota(jnp.int32, (cl, pw_), 0)
    s_i = lax.broadcasted_iota(jnp.int32, (cl, pw_), 1) % cl
    strict = (t_i < s_i) if reverse else (t_i > s_i)
    incl = (t_i <= s_i) if reverse else (t_i >= s_i)
    eye = jnp.where(t_i == s_i, 1.0, 0.0)
    head0 = lax.broadcasted_iota(jnp.int32, (cl, pw_), 1) < RK_HEAD_DIM
    q0 = lax.broadcasted_iota(jnp.int32, (pw_, pw_), 0)
    q1 = lax.broadcasted_iota(jnp.int32, (pw_, pw_), 1)
    same_head = (q0 // RK_HEAD_DIM) == (q1 // RK_HEAD_DIM)

    head_sum = _head_sum

    def bd(x):
        z = jnp.zeros_like(x)
        return jnp.concatenate([jnp.where(head0, x, z), jnp.where(head0, z, x)], axis=0)

    keys = [(c, p) for c in range(nc) for p in range(npair)]
    rsl = {k: slice(k[0] * cl, (k[0] + 1) * cl) for k in keys}
    lsl = {k: slice(k[1] * pw_, (k[1] + 1) * pw_) for k in keys}
    blk = lambda arr, k: arr[rsl[k], lsl[k]]

    gram = {k: _dot_nt(jnp.concatenate([blk(a_t, k), blk(r_t, k)], axis=0),
                       jnp.concatenate([bd(blk(b_t, k)), bd(blk(k_t, k))], axis=0)) for k in keys}
    m_ab = {k: jnp.where(strict, gram[k][:cl, :pw_], 0.0) for k in keys}
    m_ak = {k: jnp.where(strict, gram[k][:cl, pw_:], 0.0).astype(BF16) for k in keys}
    m_rb = {k: jnp.where(incl, gram[k][cl:, :pw_], 0.0).astype(BF16) for k in keys}
    m_rk = {k: jnp.where(incl, gram[k][cl:, pw_:], 0.0).astype(BF16) for k in keys}
    vbd = {k: bd(blk(vv, k)) for k in keys}
    my = {k: _dot(jnp.concatenate([m_ak[k], m_rk[k]], axis=0), vbd[k]) for k in keys}
    v_t = {k: jnp.transpose(blk(vv, k).astype(F32)).astype(BF16) for k in keys}
    vk = {k: jnp.where(same_head, _dot(v_t[k], blk(k_h, k)), 0.0) for k in keys}
    inv = {k: eye + m_ab[k] for k in keys}
    pwb = {k: m_ab[k].astype(BF16) for k in keys}
    pw = {k: _dot(pwb[k], bd(pwb[k])) for k in keys}
    n = 4
    while n < cl:
        pwb = {k: pw[k].astype(BF16) for k in keys}
        res = {k: _dot(pwb[k], jnp.concatenate([bd(pwb[k]), bd(inv[k].astype(BF16))], axis=1)) for k in keys}
        pw = {k: res[k][:, :pw_] for k in keys}
        inv = {k: inv[k] + res[k][:, pw_:] for k in keys}
        n *= 2
    inv = {k: (inv[k] + _dot(pw[k].astype(BF16), bd(inv[k].astype(BF16)))).astype(BF16) for k in keys}
    au = {k: _dot(inv[k], jnp.concatenate([bd(blk(a_t, k)), bd(my[k][:cl].astype(BF16))], axis=1))
          for k in keys}
    a_hat = {k: au[k][:, :pw_].astype(BF16) for k in keys}
    u0_t = {k: jnp.transpose(au[k][:, pw_:]) for k in keys}
    ar = {k: jnp.concatenate([a_hat[k], blk(r_t, k)], axis=0) for k in keys}

    pairs = range(npair)
    for c in (range(nc - 1, -1, -1) if reverse else range(nc)):
        st = {p: s_ref[p] for p in pairs}
        stb = {p: st[p].astype(BF16) for p in pairs}
        res = {p: _dot_nt(ar[c, p], stb[p]) for p in pairs}
        res_t = {p: _dot_nt(stb[p], a_hat[c, p]) for p in pairs}
        ub = {p: (au[c, p][:, pw_:] + res[p][:cl]).astype(BF16) for p in pairs}
        ub_t = {p: (u0_t[c, p] + res_t[p]).astype(BF16) for p in pairs}
        y = {p: my[c, p][cl:] + res[p][cl:] + _dot(m_rb[c, p], bd(ub[p])) for p in pairs}
        for p in pairs:
            k = (c, p)
            decay = jnp.exp(tot_rows[c][:, lsl[k]])
            s_ref[p] = jnp.where(same_head, st[p] * decay + _dot(ub_t[p], blk(b_h, k)), 0.0) + vk[k]
        dy = {p: y[p] - head_sum(y[p]) * (1.0 / RK_HEAD_DIM) for p in pairs}
        var = {p: head_sum(dy[p] * dy[p]) * (1.0 / RK_HEAD_DIM) for p in pairs}
        for p in pairs:
            k = (c, p)
            yn = dy[p] * lax.rsqrt(var[p] + RK_GN_EPS) * lg_ref[:, lsl[k]] + lb_ref[:, lsl[k]]
            bonus = head_sum(blk(bonus_w, k)) * blk(vv, k).astype(F32)
            o_ref[rsl[k], lsl[k]] = (blk(gate_ref, k).astype(F32) * (yn + bonus)).astype(BF16)


def rwkv_mixer_layer(x, mod, p, ln_g, ln_b, tt=256):
    bsz, seq, d = x.shape
    tt = min(tt, seq)
    nt = seq // tt
    x_spec, prev_spec, next_spec = _tile_specs(seq, tt, d)
    tile = pl.BlockSpec((None, tt, d), lambda b, t: (b, t, 0))
    tile2 = pl.BlockSpec((2, None, tt, d), lambda b, t: (0, b, t, 0))
    rw, rg = p['w1'].shape[1] // 2, p['g1'].shape[1] // 2
    sh_bf = jax.ShapeDtypeStruct(x.shape, BF16)
    sh2_bf = jax.ShapeDtypeStruct((2,) + x.shape, BF16)
    sh2_f32 = jax.ShapeDtypeStruct((2,) + x.shape, F32)
    r, v, kkn, lw, km, bz, gate = pl.pallas_call(
        functools.partial(_rk_proj_kernel, nt=nt, tt=tt),
        grid=(bsz, nt),
        in_specs=[x_spec, prev_spec, next_spec, _mod_spec(d), _const_spec((6, d)),
                  _const_spec((3, d, d)),
                  _const_spec((d, 2 * rw)), _const_spec((2, rw, d)),
                  _const_spec((d, 2 * rw)), _const_spec((2, rw, d)),
                  _const_spec((d, 2 * rg)), _const_spec((2, rg, d)),
                  _const_spec((2, d)), _const_spec((2, d)), _const_spec((1, d)), _const_spec((1, d))],
        out_specs=[tile, tile, tile, tile2, tile2, tile2, tile2],
        out_shape=[sh_bf, sh_bf, sh_bf, sh2_f32, sh2_bf, sh2_bf, sh2_bf],
        compiler_params=_cparams("parallel", "arbitrary"),
        name="rwkv_proj",
    )(x, x, x, mod, p['mu'], p['w_rkv'], p['w1'], p['w2'], p['a1'], p['a2'], p['g1'], p['g2'],
      p['w0'], p['a0'], p['k_k'], p['k_a'])

    tb = min(RK_TOK_BLOCK, seq)
    hb = RK_LANE_BLOCK
    nb = seq // tb
    outs = []
    for z in range(2):
        rev = z == 1

        def tok(i, rev=rev):
            return nb - 1 - i if rev else i

        blk = pl.BlockSpec((None, tb, hb), lambda b, g, i: (b, tok(i), g))
        blk2 = pl.BlockSpec((None, None, tb, hb), lambda b, g, i, z=z: (z, b, tok(i), g))
        vec = pl.BlockSpec((1, hb), lambda b, g, i: (0, g))
        vec2 = pl.BlockSpec((None, 1, hb), lambda b, g, i, z=z: (z, 0, g))
        outs.append(pl.pallas_call(
            functools.partial(_rk_scan_kernel, reverse=rev),
            grid=(bsz, d // hb, nb),
            in_specs=[blk, blk, blk, blk2, blk2, blk2, blk2, vec, vec2, vec2],
            out_specs=blk,
            out_shape=sh_bf,
            scratch_shapes=[pltpu.VMEM((hb // (2 * RK_HEAD_DIM), 2 * RK_HEAD_DIM, 2 * RK_HEAD_DIM), F32)],
            compiler_params=_cparams("parallel", "parallel", "arbitrary"),
            name="rwkv_scan_rev" if rev else "rwkv_scan_fwd",
        )(r, v, kkn, lw, km, bz, gate, p['r_k'], p['lnx_g'], p['lnx_b']))
    return proj_residual_ln(outs, x, mod, p['w_o'], ln_g, ln_b)


def _prepare_params(w):
    d = D_MODEL
    bf = lambda a: a.astype(BF16)
    row = lambda a: a.reshape(1, d)
    p = {
        'w_ada': w['w_ada'], 'b_ada': w['b_ada'],
        'ln_g': w['ln_g'], 'ln_b': w['ln_b'],
        'conv_w_in': bf(w['conv_w_in']), 'conv_w': w['conv_w'], 'conv_w_out': bf(w['conv_w_out']),
        'na_w_qkv': bf(w['na_w_qkv']), 'na_w_o': bf(w['na_w_o']),
        'na_bias': [_na_bias_table(w['na_rpb'][j]) for j in range(w['na_rpb'].shape[0])],
        'ffn_w_gu': bf(w['ffn_w_gu']), 'ffn_w_down': bf(w['ffn_w_down']),
        'moe_router': [_pad_router(w['moe_w_router'][j], w['moe_b_router'][j])
                       for j in range(w['moe_w_router'].shape[0])],
        'moe_w_gu': bf(w['moe_w_gu']), 'moe_w_down': bf(w['moe_w_down']),
        'rk': [],
    }
    cat = lambda a: jnp.concatenate([a[0], a[1]], axis=1)
    for j in range(w['rk_mu'].shape[0]):
        p['rk'].append({
            'mu': w['rk_mu'][j], 'w_rkv': bf(w['rk_w_rkv'][j]),
            'w1': bf(cat(w['rk_w1'][j])), 'w2': bf(w['rk_w2'][j]),
            'a1': bf(cat(w['rk_a1'][j])), 'a2': bf(w['rk_a2'][j]),
            'g1': bf(cat(w['rk_g1'][j])), 'g2': bf(w['rk_g2'][j]),
            'w0': w['rk_w0'][j], 'a0': w['rk_a0'][j],
            'k_k': row(w['rk_k_k'][j]), 'k_a': row(w['rk_k_a'][j]),
            'r_k': w['rk_r_k'][j].reshape(1, d),
            'lnx_g': w['rk_lnx_g'][j].reshape(2, 1, d), 'lnx_b': w['rk_lnx_b'][j].reshape(2, 1, d),
            'w_o': bf(w['rk_w_o'][j]),
        })
    return p


def _trunk(x, mod_all, p):
    d = D_MODEL
    for i in range(DEPTH):
        mod = mod_all[i]
        lng = lambda s: p['ln_g'][i, s].reshape(1, d)
        lnb = lambda s: p['ln_b'][i, s].reshape(1, d)
        kind, j = i % 3, i // 3
        if kind == 0:
            x = conv_mixer_layer(x, mod, p['conv_w_in'][j], p['conv_w'][j], p['conv_w_out'][j],
                                 lng(0), lnb(0))
        elif kind == 1:
            x = na_mixer_layer(x, mod, p['na_w_qkv'][j], p['na_bias'][j], p['na_w_o'][j], lng(0), lnb(0))
        else:
            x = rwkv_mixer_layer(x, mod, p['rk'][j], lng(0), lnb(0))
        if i % 2 == 0:
            x = dense_ffn_layer(x, mod, p['ffn_w_gu'][i // 2], p['ffn_w_down'][i // 2], lng(1), lnb(1))
        else:
            wr, br = p['moe_router'][i // 2]
            x = moe_ffn_layer(x, mod, wr, br, p['moe_w_gu'][i // 2], p['moe_w_down'][i // 2],
                              lng(1), lnb(1))
    return x


def kernel(x_prompt, x_sample, c_prompt, c_sample, w_ada, b_ada, ln_g, ln_b, conv_w_in, conv_w, conv_w_out, na_w_qkv, na_rpb, na_w_o, rk_mu, rk_w_rkv, rk_w0, rk_w1, rk_w2, rk_a0, rk_a1, rk_a2, rk_g1, rk_g2, rk_k_k, rk_k_a, rk_r_k, rk_lnx_g, rk_lnx_b, rk_w_o, ffn_w_gu, ffn_w_down, moe_w_router, moe_b_router, moe_w_gu, moe_w_down):
    p = _prepare_params(dict(
        w_ada=w_ada, b_ada=b_ada, ln_g=ln_g, ln_b=ln_b,
        conv_w_in=conv_w_in, conv_w=conv_w, conv_w_out=conv_w_out,
        na_w_qkv=na_w_qkv, na_rpb=na_rpb, na_w_o=na_w_o,
        rk_mu=rk_mu, rk_w_rkv=rk_w_rkv, rk_w0=rk_w0, rk_w1=rk_w1, rk_w2=rk_w2,
        rk_a0=rk_a0, rk_a1=rk_a1, rk_a2=rk_a2, rk_g1=rk_g1, rk_g2=rk_g2,
        rk_k_k=rk_k_k, rk_k_a=rk_k_a, rk_r_k=rk_r_k, rk_lnx_g=rk_lnx_g, rk_lnx_b=rk_lnx_b,
        rk_w_o=rk_w_o, ffn_w_gu=ffn_w_gu, ffn_w_down=ffn_w_down,
        moe_w_router=moe_w_router, moe_b_router=moe_b_router,
        moe_w_gu=moe_w_gu, moe_w_down=moe_w_down))
    n_p = c_prompt.shape[0]
    mod_all = ada_modulation(jnp.concatenate([c_prompt, c_sample], axis=0), p['w_ada'], p['b_ada'])
    return (_trunk(x_prompt, mod_all[:, :n_p], p), _trunk(x_sample, mod_all[:, n_p:], p))
```

```python
import functools

import jax
import jax.numpy as jnp
from jax import lax
from jax.experimental import pallas as pl
from jax.experimental.pallas import tpu as pltpu

F32 = jnp.float32
BF16 = jnp.bfloat16

D_MODEL = 1024
DEPTH = 4
ALPHA = (2 * DEPTH) ** 0.25
LN_EPS = 1e-5

GRID_W = 64
NA_HEADS = 16
NA_HEAD_DIM = D_MODEL // NA_HEADS
NA_WIN_ROWS = 8
NA_WIN_COLS = 16
NEG_INF = -1e30

RK_HEAD_DIM = 64
RK_GN_EPS = 64e-5
RK_DECAY_SCALE = 0.6065306597126334
RK_CHUNK = 64
RK_TOK_BLOCK = 256
RK_LANE_BLOCK = 1024

N_EXPERTS = 8
ROUTER_LANES = 128

SUBLANES = 8
VMEM_LIMIT = 52 * 1024 * 1024
MOE_VMEM_LIMIT = 58 * 1024 * 1024


def _cparams(*sem):
    return pltpu.CompilerParams(dimension_semantics=sem, vmem_limit_bytes=VMEM_LIMIT)


def _const_spec(shape):
    nd = len(shape)
    return pl.BlockSpec(shape, lambda *_: (0,) * nd, pipeline_mode=pl.Buffered(1))


def _dot(a, b):
    return jnp.dot(a, b, preferred_element_type=F32)


def _dot_nt(a, b):
    return lax.dot_general(a, b, (((1,), (1,)), ((), ())), preferred_element_type=F32)


def _sigmoid(x):
    return 1.0 / (1.0 + jnp.exp(-x))


def _layer_norm(y, g, b):
    mu = jnp.mean(y, axis=-1, keepdims=True)
    d = y - mu
    var = jnp.mean(d * d, axis=-1, keepdims=True)
    return d * lax.rsqrt(var + LN_EPS) * g + b


def _ada_kernel(c_ref, w_ref, b_ref, o_ref):
    c = c_ref[...]
    s = (c * _sigmoid(c)).astype(BF16)
    o_ref[...] = _dot(s, w_ref[...].astype(BF16)) + b_ref[...]


def ada_modulation(c, w_ada, b_ada):
    bsz, d = c.shape
    depth = w_ada.shape[0]
    out = pl.pallas_call(
        _ada_kernel,
        grid=(depth, 6),
        in_specs=[
            pl.BlockSpec((bsz, d), lambda i, j: (0, 0)),
            pl.BlockSpec((None, d, d), lambda i, j: (i, 0, j)),
            pl.BlockSpec((None, None, 1, d), lambda i, j: (i, j, 0, 0)),
        ],
        out_specs=pl.BlockSpec((None, None, bsz, d), lambda i, j: (i, j, 0, 0)),
        out_shape=jax.ShapeDtypeStruct((depth, 6, bsz, d), F32),
        compiler_params=_cparams("arbitrary", "arbitrary"),
        name="ada_modulation",
    )(c, w_ada, b_ada.reshape(depth, 6, 1, d))
    return jnp.transpose(out, (0, 2, 1, 3))


def _tile_specs(seq, tt, d):
    nb8 = seq // SUBLANES
    per = tt // SUBLANES
    x_spec = pl.BlockSpec((None, tt, d), lambda b, t: (b, t, 0))
    prev_spec = pl.BlockSpec((None, SUBLANES, d), lambda b, t: (b, jnp.maximum(t * per - 1, 0), 0))
    next_spec = pl.BlockSpec((None, SUBLANES, d), lambda b, t: (b, jnp.minimum((t + 1) * per, nb8 - 1), 0))
    return x_spec, prev_spec, next_spec


def _mod_spec(d):
    return pl.BlockSpec((None, 6, d), lambda b, t: (b, 0, 0))


def _shifted(cur, prev_row, next_row, tt):
    row = lax.broadcasted_iota(jnp.int32, (tt, 1), 0)
    m1 = jnp.where(row == 0, prev_row, pltpu.roll(cur, 1, 0))
    p1 = jnp.where(row == tt - 1, next_row, pltpu.roll(cur, tt - 1, 0))
    return m1, p1


def _conv_kernel(x_ref, xp_ref, xn_ref, mod_ref, win_ref, cw_ref, wout_ref, lng_ref, lnb_ref,
                 o_ref, *, nt, tt, d):
    t = pl.program_id(1)
    sh, sc, gate = mod_ref[0:1, :], mod_ref[1:2, :], mod_ref[2:3, :]
    parts = [slice(i * tt // CONV_PARTS, (i + 1) * tt // CONV_PARTS) for i in range(CONV_PARTS)]
    xs = [x_ref[r, :] for r in parts]
    ps = [_dot((x * (1.0 + sc) + sh).astype(BF16), win_ref[...]) for x in xs]
    z = jnp.concatenate([p[:, d:2 * d] * p[:, 2 * d:] for p in ps], axis=0)
    halo = jnp.concatenate([xp_ref[...], xn_ref[...]], axis=0)
    hh = (halo * (1.0 + sc) + sh).astype(BF16)
    ph = _dot(hh, win_ref[:, d:])
    zh = ph[:, :d] * ph[:, d:]
    z_prev = jnp.where(t > 0, zh[SUBLANES - 1:SUBLANES, :], 0.0)
    z_next = jnp.where(t < nt - 1, zh[SUBLANES:SUBLANES + 1, :], 0.0)
    z_m1, z_p1 = _shifted(z, z_prev, z_next, tt)
    conv = z_m1 * cw_ref[0:1, :] + z * cw_ref[1:2, :] + z_p1 * cw_ref[2:3, :]
    mixes = [_dot((p[:, :d] * conv[r]).astype(BF16), wout_ref[...]) for p, r in zip(ps, parts)]
    for x, r, mix in zip(xs, parts, mixes):
        o_ref[r, :] = _layer_norm(ALPHA * x + gate * mix, lng_ref[...], lnb_ref[...])


CONV_PARTS = 4


def conv_mixer_layer(x, mod, w_in, conv_w, w_out, ln_g, ln_b, tt=1024):
    bsz, seq, d = x.shape
    tt = min(tt, seq)
    nt = seq // tt
    x_spec, prev_spec, next_spec = _tile_specs(seq, tt, d)
    return pl.pallas_call(
        functools.partial(_conv_kernel, nt=nt, tt=tt, d=d),
        grid=(bsz, nt),
        in_specs=[x_spec, prev_spec, next_spec, _mod_spec(d),
                  _const_spec((d, 3 * d)), _const_spec((3, d)), _const_spec((d, d)),
                  _const_spec((1, d)), _const_spec((1, d))],
        out_specs=pl.BlockSpec((None, tt, d), lambda b, t: (b, t, 0)),
        out_shape=jax.ShapeDtypeStruct(x.shape, F32),
        compiler_params=_cparams("parallel", "arbitrary"),
        name="conv_mixer",
    )(x, x, x, mod, w_in, conv_w, w_out, ln_g, ln_b)


def _ffn_kernel(x_ref, mod_ref, wgu_ref, wd_ref, lng_ref, lnb_ref, o_ref, *, ff, fc):
    sh, sc, gate = mod_ref[3:4, :], mod_ref[4:5, :], mod_ref[5:6, :]
    tt = x_ref.shape[0]
    parts = [slice(i * tt // FFN_PARTS, (i + 1) * tt // FFN_PARTS) for i in range(FFN_PARTS)]
    chunks = range(ff // fc)
    xs = [x_ref[r, :] for r in parts]
    hs = [(x * (1.0 + sc) + sh).astype(BF16) for x in xs]
    gs = [[_dot(h, wgu_ref[:, c * fc:(c + 1) * fc]) for c in chunks] for h in hs]
    us = [[_dot(h, wgu_ref[:, ff + c * fc:ff + (c + 1) * fc]) for c in chunks] for h in hs]
    acts = [[(g * _sigmoid(g) * u).astype(BF16) for g, u in zip(gr, ur)] for gr, ur in zip(gs, us)]
    for x, r, ar in zip(xs, parts, acts):
        acc = _dot(ar[0], wd_ref[0:fc, :])
        for c in chunks[1:]:
            acc = acc + _dot(ar[c], wd_ref[c * fc:(c + 1) * fc, :])
        o_ref[r, :] = _layer_norm(ALPHA * x + gate * acc, lng_ref[...], lnb_ref[...])


FFN_PARTS = 2


def dense_ffn_layer(x, mod, w_gu, w_down, ln_g, ln_b, tt=512):
    bsz, seq, d = x.shape
    tt = min(tt, seq)
    ff = w_down.shape[0]
    fc = ff
    return pl.pallas_call(
        functools.partial(_ffn_kernel, ff=ff, fc=fc),
        grid=(bsz, seq // tt),
        in_specs=[pl.BlockSpec((None, tt, d), lambda b, t: (b, t, 0)), _mod_spec(d),
                  _const_spec((d, 2 * ff)), _const_spec((ff, d)),
                  _const_spec((1, d)), _const_spec((1, d))],
        out_specs=pl.BlockSpec((None, tt, d), lambda b, t: (b, t, 0)),
        out_shape=jax.ShapeDtypeStruct(x.shape, F32),
        compiler_params=_cparams("parallel", "arbitrary"),
        name="dense_ffn",
    )(x, mod, w_gu, w_down, ln_g, ln_b)


MOE_ROW_CHUNK = 256
MOE_ROW_BLOCKS = (128, 256)


def _moe_kernel(x_ref, mod_ref, wr_ref, br_ref, wgu_ref, wd_ref, lng_ref, lnb_ref, o_ref,
                h_sc, comb_sc, acc_sc, rankc_sc, rankr_sc, maskr_sc, y_sc, *, fe, blocks):
    e = pl.program_id(2)
    tt = comb_sc.shape[0]
    lane = lax.broadcasted_iota(jnp.int32, comb_sc.shape, 1)

    @pl.when(e == 0)
    def _():
        sh, sc = mod_ref[3:4, :], mod_ref[4:5, :]
        h = x_ref[...] * (1.0 + sc) + sh
        h_hi = h.astype(BF16)
        h_sc[...] = h_hi
        h_lo = (h - h_hi.astype(F32)).astype(BF16)
        wr = wr_ref[...]
        w_hi = wr.astype(BF16)
        w_lo = (wr - w_hi.astype(F32)).astype(BF16)
        logits = _dot(h_hi, w_hi) + _dot(h_lo, w_hi) + _dot(h_hi, w_lo) + br_ref[...]
        m = jnp.max(logits, axis=-1, keepdims=True)
        ex = jnp.exp(logits - m)
        probs = ex / jnp.sum(ex, axis=-1, keepdims=True)
        valid = lane < N_EXPERTS
        p = jnp.where(valid, probs, -1.0)
        p1 = jnp.max(p, axis=-1, keepdims=True)
        i1 = jnp.min(jnp.where(p == p1, lane, ROUTER_LANES), axis=-1, keepdims=True)
        pr = jnp.where(lane == i1, -1.0, p)
        p2 = jnp.max(pr, axis=-1, keepdims=True)
        i2 = jnp.min(jnp.where(pr == p2, lane, ROUTER_LANES), axis=-1, keepdims=True)
        tot = p1 + p2
        comb = jnp.where(lane == i1, p1 / tot, 0.0) + jnp.where(lane == i2, p2 / tot, 0.0)
        comb_sc[...] = comb
        acc_sc[...] = jnp.zeros(acc_sc.shape, F32)
        y_sc[...] = jnp.zeros(y_sc.shape, BF16)
        routed = jnp.where(comb > 0.0, 1.0, 0.0).astype(BF16)
        q0 = lax.broadcasted_iota(jnp.int32, (ROUTER_LANES, ROUTER_LANES), 0)
        q1 = lax.broadcasted_iota(jnp.int32, (ROUTER_LANES, ROUTER_LANES), 1)
        routed_t = _dot_nt(jnp.where(q0 == q1, 1.0, 0.0).astype(BF16), routed)
        maskr_sc[...] = routed_t
        routed_tb = routed_t.astype(BF16)
        rc = min(MOE_ROW_CHUNK, tt)
        for c in range(tt // rc):
            r_i = lax.broadcasted_iota(jnp.int32, (rc, tt), 0) + c * rc
            c_i = lax.broadcasted_iota(jnp.int32, (rc, tt), 1)
            rankc_sc[c * rc:(c + 1) * rc, :] = _dot(jnp.where(c_i < r_i, 1.0, 0.0).astype(BF16), routed)
            r_j = lax.broadcasted_iota(jnp.int32, (tt, rc), 0)
            c_j = lax.broadcasted_iota(jnp.int32, (tt, rc), 1) + c * rc
            rankr_sc[:, c * rc:(c + 1) * rc] = _dot(routed_tb, jnp.where(r_j < c_j, 1.0, 0.0).astype(BF16))

    def expert(hb):
        g = _dot(hb, wgu_ref[:, :fe])
        u = _dot(hb, wgu_ref[:, fe:])
        act = (g * _sigmoid(g) * u).astype(BF16)
        return _dot(act, wd_ref[...])

    w_col = jnp.sum(jnp.where(lane == e, comb_sc[...], 0.0), axis=-1, keepdims=True)
    rk_col = jnp.sum(jnp.where(lane == e, rankc_sc[...], 0.0), axis=-1, keepdims=True).astype(jnp.int32)
    rk_row = rankr_sc[pl.ds(e, 1), :].astype(jnp.int32)
    mk_row = maskr_sc[pl.ds(e, 1), :]
    n_routed = jnp.sum(mk_row).astype(jnp.int32)
    half = tt // 2
    gb, sb = blocks

    def gather_block(j, carry):
        base = pl.multiple_of(j * gb, gb)
        sub = lax.broadcasted_iota(jnp.int32, (gb, tt), 0) + base
        pick = jnp.where((rk_row == sub) & (mk_row > 0.0), 1.0, 0.0).astype(BF16)
        y_sc[pl.ds(base, gb), :] = expert(_dot(pick, h_sc[...]).astype(BF16)).astype(BF16)
        return carry

    lax.fori_loop(0, (n_routed + gb - 1) // gb, gather_block, 0)

    for c in range(tt // sb):
        @pl.when(n_routed > c * sb)
        def _(c=c):
            for r0 in (0, half):
                rows = slice(r0, r0 + half)
                ln_i = lax.broadcasted_iota(jnp.int32, (half, sb), 1) + c * sb
                put = jnp.where((rk_col[rows] == ln_i) & (w_col[rows] > 0.0), 1.0, 0.0).astype(BF16)
                acc_sc[rows, :] += w_col[rows] * _dot(put, y_sc[c * sb:(c + 1) * sb, :])

    @pl.when(e == N_EXPERTS - 1)
    def _():
        gate = mod_ref[5:6, :]
        o_ref[...] = _layer_norm(ALPHA * x_ref[...] + gate * acc_sc[...], lng_ref[...], lnb_ref[...])


def moe_ffn_layer(x, mod, w_router, b_router, w_gu, w_down, ln_g, ln_b, tt=1024, blocks=MOE_ROW_BLOCKS):
    bsz, seq, d = x.shape
    tt = min(tt, seq)
    n_e, fe = w_down.shape[0], w_down.shape[1]
    return pl.pallas_call(
        functools.partial(_moe_kernel, fe=fe, blocks=tuple(min(b, tt) for b in blocks)),
        grid=(bsz, seq // tt, n_e),
        in_specs=[pl.BlockSpec((None, tt, d), lambda b, t, e: (b, t, 0), pipeline_mode=pl.Buffered(1)),
                  pl.BlockSpec((None, 6, d), lambda b, t, e: (b, 0, 0)),
                  _const_spec((d, ROUTER_LANES)), _const_spec((1, ROUTER_LANES)),
                  pl.BlockSpec((None, d, 2 * fe), lambda b, t, e: (e, 0, 0)),
                  pl.BlockSpec((None, fe, d), lambda b, t, e: (e, 0, 0)),
                  _const_spec((1, d)), _const_spec((1, d))],
        out_specs=pl.BlockSpec((None, tt, d), lambda b, t, e: (b, t, 0)),
        out_shape=jax.ShapeDtypeStruct(x.shape, F32),
        scratch_shapes=[pltpu.VMEM((tt, d), BF16), pltpu.VMEM((tt, ROUTER_LANES), F32),
                        pltpu.VMEM((tt, d), F32), pltpu.VMEM((tt, ROUTER_LANES), F32),
                        pltpu.VMEM((ROUTER_LANES, tt), F32), pltpu.VMEM((ROUTER_LANES, tt), F32),
                        pltpu.VMEM((tt, d), BF16)],
        compiler_params=pltpu.CompilerParams(
            dimension_semantics=("parallel", "arbitrary", "arbitrary"), vmem_limit_bytes=MOE_VMEM_LIMIT),
        name="moe_ffn",
    )(x, mod, w_router, b_router, w_gu, w_down, ln_g, ln_b)


def _pad_router(w_router, b_router):
    d, n_e = w_router.shape
    w = jnp.zeros((d, ROUTER_LANES), F32).at[:, :n_e].set(w_router)
    b = jnp.full((1, ROUTER_LANES), NEG_INF, F32).at[0, :n_e].set(b_router)
    return w, b


def _qkv_kernel(x_ref, mod_ref, w_ref, q_ref, k_ref, v_ref, *, d):
    sh, sc = mod_ref[0:1, :], mod_ref[1:2, :]
    h = (x_ref[...] * (1.0 + sc) + sh).astype(BF16)
    p = _dot(h, w_ref[...])
    q_ref[...] = (p[:, :d] * (NA_HEAD_DIM ** -0.5)).astype(BF16)
    k_ref[...] = p[:, d:2 * d].astype(BF16)
    v_ref[...] = p[:, 2 * d:].astype(BF16)


NA_ROWS_PER_STEP = 4


def _na_kernel(q_ref, k_ref, v_ref, *rest, rows):
    bias_refs, o_ref = rest[:-1], rest[-1]
    nk = NA_WIN_ROWS * GRID_W
    pw_ = 2 * NA_HEAD_DIM
    pairs = range(NA_HEADS // 2)
    lanes = [slice(p * pw_, (p + 1) * pw_) for p in pairs]
    head0 = lax.broadcasted_iota(jnp.int32, (GRID_W, pw_), 1) < NA_HEAD_DIM
    keys = [(i, p) for i in range(len(bias_refs)) for p in pairs]
    qrows = {i: slice(i * GRID_W, (i + 1) * GRID_W) for i in range(len(bias_refs))}
    start = {}
    for i in range(len(bias_refs)):
        r = pl.program_id(1) * len(bias_refs) + i
        r0 = jnp.clip(r - NA_WIN_ROWS // 2, 0, rows - NA_WIN_ROWS)
        start[i] = pl.multiple_of(r0 * GRID_W, GRID_W)
    q2 = {}
    for i, p in keys:
        q = q_ref[qrows[i], lanes[p]]
        z = jnp.zeros_like(q)
        q2[i, p] = jnp.concatenate([jnp.where(head0, q, z), jnp.where(head0, z, q)], axis=0)
    s = {(i, p): _dot_nt(q2[i, p], k_ref[pl.ds(start[i], nk), lanes[p]])
         + bias_refs[i][2 * p:2 * p + 2].reshape(2 * GRID_W, nk) for i, p in keys}
    m = {k: jnp.max(s[k], axis=-1, keepdims=True) for k in keys}
    e = {k: jnp.exp(s[k] - m[k]) for k in keys}
    l = {k: jnp.sum(e[k], axis=-1, keepdims=True) for k in keys}
    o = {(i, p): _dot(e[i, p].astype(BF16), v_ref[pl.ds(start[i], nk), lanes[p]]) / l[i, p] for i, p in keys}
    for i, p in keys:
        o_ref[qrows[i], lanes[p]] = jnp.where(head0, o[i, p][:GRID_W], o[i, p][GRID_W:]).astype(BF16)


def _na_bias_table(rpb):
    nh, nr, nc = rpb.shape
    w = GRID_W
    qc = jnp.arange(w)[:, None]
    kc = jnp.arange(w)[None, :]
    cs = jnp.clip(qc - NA_WIN_COLS // 2, 0, w - NA_WIN_COLS)
    col_ok = (kc >= cs) & (kc < cs + NA_WIN_COLS)
    lo = w - NA_WIN_COLS
    e = jnp.pad(rpb, ((0, 0), (0, 0), (lo, 2 * w - lo - nc)))
    flat = jnp.tile(e, (1, 1, w))
    toep = flat[:, :, w - 1:w - 1 + w * (2 * w - 1)].reshape(nh, nr, w, 2 * w - 1)[..., :w]
    toep = jnp.where(col_ok[None, None], toep, NEG_INF)
    tab = jnp.stack([toep[:, dl:dl + NA_WIN_ROWS] for dl in range(NA_WIN_ROWS)], axis=1)
    tab = jnp.transpose(tab, (0, 1, 3, 2, 4))
    return tab.reshape(nh, NA_WIN_ROWS, w, NA_WIN_ROWS * w)


def _proj_ln_kernel(*refs, n_in):
    a_refs = refs[:n_in]
    x_ref, mod_ref, w_ref, lng_ref, lnb_ref, o_ref = refs[n_in:]
    a = a_refs[0][...]
    if n_in == 2:
        a = (a.astype(F32) + a_refs[1][...].astype(F32)).astype(BF16)
    mix = _dot(a, w_ref[...])
    gate = mod_ref[2:3, :]
    o_ref[...] = _layer_norm(ALPHA * x_ref[...] + gate * mix, lng_ref[...], lnb_ref[...])


def proj_residual_ln(acts, x, mod, w_o, ln_g, ln_b, tt=1024):
    bsz, seq, d = x.shape
    tt = min(tt, seq)
    tile = pl.BlockSpec((None, tt, d), lambda b, t: (b, t, 0))
    return pl.pallas_call(
        functools.partial(_proj_ln_kernel, n_in=len(acts)),
        grid=(bsz, seq // tt),
        in_specs=[tile] * len(acts) + [tile, _mod_spec(d), _const_spec((d, d)),
                                       _const_spec((1, d)), _const_spec((1, d))],
        out_specs=tile,
        out_shape=jax.ShapeDtypeStruct(x.shape, F32),
        compiler_params=_cparams("parallel", "arbitrary"),
        name="proj_residual_ln",
    )(*acts, x, mod, w_o, ln_g, ln_b)


def na_mixer_layer(x, mod, w_qkv, bias_tab, w_o, ln_g, ln_b, tt=512):
    bsz, seq, d = x.shape
    tt = min(tt, seq)
    rows = seq // GRID_W
    assert rows >= NA_WIN_ROWS and seq % GRID_W == 0
    tile = pl.BlockSpec((None, tt, d), lambda b, t: (b, t, 0))
    q, k, v = pl.pallas_call(
        functools.partial(_qkv_kernel, d=d),
        grid=(bsz, seq // tt),
        in_specs=[tile, _mod_spec(d), _const_spec((d, 3 * d))],
        out_specs=[tile] * 3,
        out_shape=[jax.ShapeDtypeStruct(x.shape, BF16)] * 3,
        compiler_params=_cparams("parallel", "arbitrary"),
        name="na_qkv",
    )(x, mod, w_qkv)

    def delta(r):
        return jnp.clip(r - NA_WIN_ROWS // 2, 0, rows - NA_WIN_ROWS) - r + NA_WIN_ROWS - 1

    rps = NA_ROWS_PER_STEP
    assert rows % rps == 0
    row_spec = pl.BlockSpec((None, rps * GRID_W, d), lambda b, r: (b, r, 0))
    seq_spec = pl.BlockSpec((None, seq, d), lambda b, r: (b, 0, 0))
    bias_specs = [pl.BlockSpec((NA_HEADS, None, GRID_W, NA_WIN_ROWS * GRID_W),
                               lambda b, r, i=i: (0, delta(r * rps + i), 0, 0)) for i in range(rps)]
    att = pl.pallas_call(
        functools.partial(_na_kernel, rows=rows),
        grid=(bsz, rows // rps),
        in_specs=[row_spec, seq_spec, seq_spec] + bias_specs,
        out_specs=row_spec,
        out_shape=jax.ShapeDtypeStruct(x.shape, BF16),
        compiler_params=_cparams("parallel", "arbitrary"),
        name="na_attention",
    )(q, k, v, *([bias_tab] * rps))
    return proj_residual_ln([att], x, mod, w_o, ln_g, ln_b)


def _rk_proj_kernel(x_ref, xp_ref, xn_ref, mod_ref, mu_ref, wrkv_ref, w1_ref, w2_ref, a1_ref, a2_ref,
                    g1_ref, g2_ref, w0_ref, a0_ref, kk_ref, ka_ref,
                    r_o, v_o, kk_o, lw_o, km_o, b_o, gate_o, *, nt, tt):
    t = pl.program_id(1)
    sh, sc = mod_ref[0:1, :], mod_ref[1:2, :]
    h = x_ref[...] * (1.0 + sc) + sh
    h_prev = jnp.where(t > 0, xp_ref[SUBLANES - 1:SUBLANES, :] * (1.0 + sc) + sh, 0.0)
    h_next = jnp.where(t < nt - 1, xn_ref[0:1, :] * (1.0 + sc) + sh, 0.0)
    h_m1, h_p1 = _shifted(h, h_prev, h_next, tt)
    xx = 0.5 * (h_m1 + h_p1) - h

    h_b, xx_b, mu_b = h.astype(BF16), xx.astype(BF16), mu_ref[...].astype(BF16)

    def mixed(p):
        return h_b + xx_b * mu_b[p:p + 1, :]

    r = _dot(mixed(0), wrkv_ref[0])
    k = _dot(mixed(1), wrkv_ref[1])
    v = _dot(mixed(2), wrkv_ref[2])
    r_o[...] = r.astype(BF16)
    v_o[...] = v.astype(BF16)
    lw_in = jnp.tanh(_dot(mixed(3), w1_ref[...])).astype(BF16)
    a_in = _dot(mixed(4), a1_ref[...]).astype(BF16)
    g_in = _sigmoid(_dot(mixed(5), g1_ref[...])).astype(BF16)

    kk = k * kk_ref[...]
    kk_o[...] = kk.astype(BF16)
    rw = w1_ref.shape[1] // 2
    rg = g1_ref.shape[1] // 2
    for z in range(2):
        wl = w0_ref[z:z + 1, :] + _dot(lw_in[:, z * rw:(z + 1) * rw], w2_ref[z])
        lw_o[z] = (-RK_DECAY_SCALE) * _sigmoid(wl)
        a_lr = _sigmoid(a0_ref[z:z + 1, :] + _dot(a_in[:, z * rw:(z + 1) * rw], a2_ref[z]))
        km_o[z] = (k * (1.0 + (a_lr - 1.0) * ka_ref[...])).astype(BF16)
        b_o[z] = (kk * a_lr).astype(BF16)
        gate_o[z] = _dot(g_in[:, z * rg:(z + 1) * rg], g2_ref[z]).astype(BF16)


def _head_sum(x):
    head0 = lax.broadcasted_iota(jnp.int32, x.shape, 1) < RK_HEAD_DIM
    s0 = jnp.sum(jnp.where(head0, x, 0.0), axis=-1, keepdims=True)
    s1 = jnp.sum(jnp.where(head0, 0.0, x), axis=-1, keepdims=True)
    return jnp.where(head0, s0, s1)


def _rk_scan_kernel(r_ref, v_ref, kk_ref, lw_ref, km_ref, b_ref, gate_ref, rk_ref, lg_ref, lb_ref,
                    o_ref, s_ref, *, reverse):
    tb, hb = r_ref.shape
    cl = RK_CHUNK
    nc, nh = tb // cl, hb // RK_HEAD_DIM

    @pl.when(pl.program_id(2) == 0)
    def _():
        s_ref[...] = jnp.zeros(s_ref.shape, F32)

    ri = lax.broadcasted_iota(jnp.int32, (tb, tb), 0)
    ci = lax.broadcasted_iota(jnp.int32, (tb, tb), 1)
    ordered = (ci >= ri) if reverse else (ci <= ri)
    tri = jnp.where((ri // cl == ci // cl) & ordered, 1.0, 0.0).astype(BF16)
    lw = lw_ref[...]
    lw_hi = lw.astype(BF16)
    lw_lo = (lw - lw_hi.astype(F32)).astype(BF16)
    cum = _dot(tri, lw_hi) + _dot(tri, lw_lo)
    last = 0 if reverse else cl - 1
    tot_rows = [cum[c * cl + last:c * cl + last + 1, :] for c in range(nc)]
    tot = jnp.concatenate([jnp.broadcast_to(tr, (cl, hb)) for tr in tot_rows], axis=0)

    rr = r_ref[...].astype(F32)
    vv = v_ref[...]
    kk = kk_ref[...].astype(F32)
    pair = 2 * RK_HEAD_DIM
    rnorm = jnp.concatenate(
        [1.0 / jnp.maximum(jnp.sqrt(_head_sum(jnp.square(kk[:, i:i + pair]))), 1e-12) for i in range(0, hb, pair)],
        axis=1)
    kkn = kk * rnorm
    km = km_ref[...].astype(F32)
    bb = b_ref[...].astype(F32) * rnorm
    e_neg = jnp.exp(-cum)
    e_rem = jnp.exp(tot - cum)
    a_t = (-kkn * jnp.exp(cum - lw)).astype(BF16)
    r_t = (rr * jnp.exp(cum)).astype(BF16)
    b_t = (bb * e_neg).astype(BF16)
    k_t = (km * e_neg).astype(BF16)
    b_h = (bb * e_rem).astype(BF16)
    k_h = (km * e_rem).astype(BF16)
    bonus_w = rr * km * rk_ref[...]

    pw_ = 2 * RK_HEAD_DIM
    npair = hb // pw_
    t_i = lax.broadcasted_iota(jnp.int32, (cl, pw_), 0)
    s_i = lax.broadcasted_iota(jnp.int32, (cl, pw_), 1) % cl
    strict = (t_i < s_i) if reverse else (t_i > s_i)
    incl = (t_i <= s_i) if reverse else (t_i >= s_i)
    eye = jnp.where(t_i == s_i, 1.0, 0.0)
    head0 = lax.broadcasted_iota(jnp.int32, (cl, pw_), 1) < RK_HEAD_DIM
    q0 = lax.broadcasted_iota(jnp.int32, (pw_, pw_), 0)
    q1 = lax.broadcasted_iota(jnp.int32, (pw_, pw_), 1)
    same_head = (q0 // RK_HEAD_DIM) == (q1 // RK_HEAD_DIM)

    head_sum = _head_sum

    def bd(x):
        z = jnp.zeros_like(x)
        return jnp.concatenate([jnp.where(head0, x, z), jnp.where(head0, z, x)], axis=0)

    keys = [(c, p) for c in range(nc) for p in range(npair)]
    rsl = {k: slice(k[0] * cl, (k[0] + 1) * cl) for k in keys}
    lsl = {k: slice(k[1] * pw_, (k[1] + 1) * pw_) for k in keys}
    blk = lambda arr, k: arr[rsl[k], lsl[k]]

    gram = {k: _dot_nt(jnp.concatenate([blk(a_t, k), blk(r_t, k)], axis=0),
                       jnp.concatenate([bd(blk(b_t, k)), bd(blk(k_t, k))], axis=0)) for k in keys}
    m_ab = {k: jnp.where(strict, gram[k][:cl, :pw_], 0.0) for k in keys}
    m_ak = {k: jnp.where(strict, gram[k][:cl, pw_:], 0.0).astype(BF16) for k in keys}
    m_rb = {k: jnp.where(incl, gram[k][cl:, :pw_], 0.0).astype(BF16) for k in keys}
    m_rk = {k: jnp.where(incl, gram[k][cl:, pw_:], 0.0).astype(BF16) for k in keys}
    vbd = {k: bd(blk(vv, k)) for k in keys}
    my = {k: _dot(jnp.concatenate([m_ak[k], m_rk[k]], axis=0), vbd[k]) for k in keys}
    v_t = {k: jnp.transpose(blk(vv, k).astype(F32)).astype(BF16) for k in keys}
    vk = {k: jnp.where(same_head, _dot(v_t[k], blk(k_h, k)), 0.0) for k in keys}
    inv = {k: eye + m_ab[k] for k in keys}
    pwb = {k: m_ab[k].astype(BF16) for k in keys}
    pw = {k: _dot(pwb[k], bd(pwb[k])) for k in keys}
    n = 4
    while n < cl:
        pwb = {k: pw[k].astype(BF16) for k in keys}
        res = {k: _dot(pwb[k], jnp.concatenate([bd(pwb[k]), bd(inv[k].astype(BF16))], axis=1)) for k in keys}
        pw = {k: res[k][:, :pw_] for k in keys}
        inv = {k: inv[k] + res[k][:, pw_:] for k in keys}
        n *= 2
    inv = {k: (inv[k] + _dot(pw[k].astype(BF16), bd(inv[k].astype(BF16)))).astype(BF16) for k in keys}
    au = {k: _dot(inv[k], jnp.concatenate([bd(blk(a_t, k)), bd(my[k][:cl].astype(BF16))], axis=1))
          for k in keys}
    a_hat = {k: au[k][:, :pw_].astype(BF16) for k in keys}
    u0_t = {k: jnp.transpose(au[k][:, pw_:]) for k in keys}
    ar = {k: jnp.concatenate([a_hat[k], blk(r_t, k)], axis=0) for k in keys}

    pairs = range(npair)
    for c in (range(nc - 1, -1, -1) if reverse else range(nc)):
        st = {p: s_ref[p] for p in pairs}
        stb = {p: st[p].astype(BF16) for p in pairs}
        res = {p: _dot_nt(ar[c, p], stb[p]) for p in pairs}
        res_t = {p: _dot_nt(stb[p], a_hat[c, p]) for p in pairs}
        ub = {p: (au[c, p][:, pw_:] + res[p][:cl]).astype(BF16) for p in pairs}
        ub_t = {p: (u0_t[c, p] + res_t[p]).astype(BF16) for p in pairs}
        y = {p: my[c, p][cl:] + res[p][cl:] + _dot(m_rb[c, p], bd(ub[p])) for p in pairs}
        for p in pairs:
            k = (c, p)
            decay = jnp.exp(tot_rows[c][:, lsl[k]])
            s_ref[p] = jnp.where(same_head, st[p] * decay + _dot(ub_t[p], blk(b_h, k)), 0.0) + vk[k]
        dy = {p: y[p] - head_sum(y[p]) * (1.0 / RK_HEAD_DIM) for p in pairs}
        var = {p: head_sum(dy[p] * dy[p]) * (1.0 / RK_HEAD_DIM) for p in pairs}
        for p in pairs:
            k = (c, p)
            yn = dy[p] * lax.rsqrt(var[p] + RK_GN_EPS) * lg_ref[:, lsl[k]] + lb_ref[:, lsl[k]]
            bonus = head_sum(blk(bonus_w, k)) * blk(vv, k).astype(F32)
            o_ref[rsl[k], lsl[k]] = (blk(gate_ref, k).astype(F32) * (yn + bonus)).astype(BF16)


def rwkv_mixer_layer(x, mod, p, ln_g, ln_b, tt=256):
    bsz, seq, d = x.shape
    tt = min(tt, seq)
    nt = seq // tt
    x_spec, prev_spec, next_spec = _tile_specs(seq, tt, d)
    tile = pl.BlockSpec((None, tt, d), lambda b, t: (b, t, 0))
    tile2 = pl.BlockSpec((2, None, tt, d), lambda b, t: (0, b, t, 0))
    rw, rg = p['w1'].shape[1] // 2, p['g1'].shape[1] // 2
    sh_bf = jax.ShapeDtypeStruct(x.shape, BF16)
    sh2_bf = jax.ShapeDtypeStruct((2,) + x.shape, BF16)
    sh2_f32 = jax.ShapeDtypeStruct((2,) + x.shape, F32)
    r, v, kkn, lw, km, bz, gate = pl.pallas_call(
        functools.partial(_rk_proj_kernel, nt=nt, tt=tt),
        grid=(bsz, nt),
        in_specs=[x_spec, prev_spec, next_spec, _mod_spec(d), _const_spec((6, d)),
                  _const_spec((3, d, d)),
                  _const_spec((d, 2 * rw)), _const_spec((2, rw, d)),
                  _const_spec((d, 2 * rw)), _const_spec((2, rw, d)),
                  _const_spec((d, 2 * rg)), _const_spec((2, rg, d)),
                  _const_spec((2, d)), _const_spec((2, d)), _const_spec((1, d)), _const_spec((1, d))],
        out_specs=[tile, tile, tile, tile2, tile2, tile2, tile2],
        out_shape=[sh_bf, sh_bf, sh_bf, sh2_f32, sh2_bf, sh2_bf, sh2_bf],
        compiler_params=_cparams("parallel", "arbitrary"),
        name="rwkv_proj",
    )(x, x, x, mod, p['mu'], p['w_rkv'], p['w1'], p['w2'], p['a1'], p['a2'], p['g1'], p['g2'],
      p['w0'], p['a0'], p['k_k'], p['k_a'])

    tb = min(RK_TOK_BLOCK, seq)
    hb = RK_LANE_BLOCK
    nb = seq // tb
    outs = []
    for z in range(2):
        rev = z == 1

        def tok(i, rev=rev):
            return nb - 1 - i if rev else i

        blk = pl.BlockSpec((None, tb, hb), lambda b, g, i: (b, tok(i), g))
        blk2 = pl.BlockSpec((None, None, tb, hb), lambda b, g, i, z=z: (z, b, tok(i), g))
        vec = pl.BlockSpec((1, hb), lambda b, g, i: (0, g))
        vec2 = pl.BlockSpec((None, 1, hb), lambda b, g, i, z=z: (z, 0, g))
        outs.append(pl.pallas_call(
            functools.partial(_rk_scan_kernel, reverse=rev),
            grid=(bsz, d // hb, nb),
            in_specs=[blk, blk, blk, blk2, blk2, blk2, blk2, vec, vec2, vec2],
            out_specs=blk,
            out_shape=sh_bf,
            scratch_shapes=[pltpu.VMEM((hb // (2 * RK_HEAD_DIM), 2 * RK_HEAD_DIM, 2 * RK_HEAD_DIM), F32)],
            compiler_params=_cparams("parallel", "parallel", "arbitrary"),
            name="rwkv_scan_rev" if rev else "rwkv_scan_fwd",
        )(r, v, kkn, lw, km, bz, gate, p['r_k'], p['lnx_g'], p['lnx_b']))
    return proj_residual_ln(outs, x, mod, p['w_o'], ln_g, ln_b)


def _prepare_params(w):
    d = D_MODEL
    bf = lambda a: a.astype(BF16)
    row = lambda a: a.reshape(1, d)
    p = {
        'w_ada': w['w_ada'], 'b_ada': w['b_ada'],
        'ln_g': w['ln_g'], 'ln_b': w['ln_b'],
        'conv_w_in': bf(w['conv_w_in']), 'conv_w': w['conv_w'], 'conv_w_out': bf(w['conv_w_out']),
        'na_w_qkv': bf(w['na_w_qkv']), 'na_w_o': bf(w['na_w_o']),
        'na_bias': [_na_bias_table(w['na_rpb'][j]) for j in range(w['na_rpb'].shape[0])],
        'ffn_w_gu': bf(w['ffn_w_gu']), 'ffn_w_down': bf(w['ffn_w_down']),
        'moe_router': [_pad_router(w['moe_w_router'][j], w['moe_b_router'][j])
                       for j in range(w['moe_w_router'].shape[0])],
        'moe_w_gu': bf(w['moe_w_gu']), 'moe_w_down': bf(w['moe_w_down']),
        'rk': [],
    }
    cat = lambda a: jnp.concatenate([a[0], a[1]], axis=1)
    for j in range(w['rk_mu'].shape[0]):
        p['rk'].append({
            'mu': w['rk_mu'][j], 'w_rkv': bf(w['rk_w_rkv'][j]),
            'w1': bf(cat(w['rk_w1'][j])), 'w2': bf(w['rk_w2'][j]),
            'a1': bf(cat(w['rk_a1'][j])), 'a2': bf(w['rk_a2'][j]),
            'g1': bf(cat(w['rk_g1'][j])), 'g2': bf(w['rk_g2'][j]),
            'w0': w['rk_w0'][j], 'a0': w['rk_a0'][j],
            'k_k': row(w['rk_k_k'][j]), 'k_a': row(w['rk_k_a'][j]),
            'r_k': w['rk_r_k'][j].reshape(1, d),
            'lnx_g': w['rk_lnx_g'][j].reshape(2, 1, d), 'lnx_b': w['rk_lnx_b'][j].reshape(2, 1, d),
            'w_o': bf(w['rk_w_o'][j]),
        })
    return p


def _trunk(x, mod_all, p):
    d = D_MODEL
    for i in range(DEPTH):
        mod = mod_all[i]
        lng = lambda s: p['ln_g'][i, s].reshape(1, d)
        lnb = lambda s: p['ln_b'][i, s].reshape(1, d)
        kind, j = i % 3, i // 3
        if kind == 0:
            x = conv_mixer_layer(x, mod, p['conv_w_in'][j], p['conv_w'][j], p['conv_w_out'][j],
                                 lng(0), lnb(0))
        elif kind == 1:
            x = na_mixer_layer(x, mod, p['na_w_qkv'][j], p['na_bias'][j], p['na_w_o'][j], lng(0), lnb(0))
        else:
            x = rwkv_mixer_layer(x, mod, p['rk'][j], lng(0), lnb(0))
        if i % 2 == 0:
            x = dense_ffn_layer(x, mod, p['ffn_w_gu'][i // 2], p['ffn_w_down'][i // 2], lng(1), lnb(1))
        else:
            wr, br = p['moe_router'][i // 2]
            x = moe_ffn_layer(x, mod, wr, br, p['moe_w_gu'][i // 2], p['moe_w_down'][i // 2],
                              lng(1), lnb(1))
    return x


def kernel(x_prompt, x_sample, c_prompt, c_sample, w_ada, b_ada, ln_g, ln_b, conv_w_in, conv_w, conv_w_out, na_w_qkv, na_rpb, na_w_o, rk_mu, rk_w_rkv, rk_w0, rk_w1, rk_w2, rk_a0, rk_a1, rk_a2, rk_g1, rk_g2, rk_k_k, rk_k_a, rk_r_k, rk_lnx_g, rk_lnx_b, rk_w_o, ffn_w_gu, ffn_w_down, moe_w_router, moe_b_router, moe_w_gu, moe_w_down):
    p = _prepare_params(dict(
        w_ada=w_ada, b_ada=b_ada, ln_g=ln_g, ln_b=ln_b,
        conv_w_in=conv_w_in, conv_w=conv_w, conv_w_out=conv_w_out,
        na_w_qkv=na_w_qkv, na_rpb=na_rpb, na_w_o=na_w_o,
        rk_mu=rk_mu, rk_w_rkv=rk_w_rkv, rk_w0=rk_w0, rk_w1=rk_w1, rk_w2=rk_w2,
        rk_a0=rk_a0, rk_a1=rk_a1, rk_a2=rk_a2, rk_g1=rk_g1, rk_g2=rk_g2,
        rk_k_k=rk_k_k, rk_k_a=rk_k_a, rk_r_k=rk_r_k, rk_lnx_g=rk_lnx_g, rk_lnx_b=rk_lnx_b,
        rk_w_o=rk_w_o, ffn_w_gu=ffn_w_gu, ffn_w_down=ffn_w_down,
        moe_w_router=moe_w_router, moe_b_router=moe_b_router,
        moe_w_gu=moe_w_gu, moe_w_down=moe_w_down))
    n_p = c_prompt.shape[0]
    mod_all = ada_modulation(jnp.concatenate([c_prompt, c_sample], axis=0), p['w_ada'], p['b_ada'])
    return (_trunk(x_prompt, mod_all[:, :n_p], p), _trunk(x_sample, mod_all[:, n_p:], p))
```

```python
import functools

import jax
import jax.numpy as jnp
from jax import lax
from jax.experimental import pallas as pl
from jax.experimental.pallas import tpu as pltpu

F32 = jnp.float32
BF16 = jnp.bfloat16

D_MODEL = 1024
DEPTH = 4
ALPHA = (2 * DEPTH) ** 0.25
LN_EPS = 1e-5

GRID_W = 64
NA_HEADS = 16
NA_HEAD_DIM = D_MODEL // NA_HEADS
NA_WIN_ROWS = 8
NA_WIN_COLS = 16
NEG_INF = -1e30

RK_HEAD_DIM = 64
RK_GN_EPS = 64e-5
RK_DECAY_SCALE = 0.6065306597126334
RK_CHUNK = 64
RK_TOK_BLOCK = 256
RK_LANE_BLOCK = 1024

N_EXPERTS = 8
ROUTER_LANES = 128

SUBLANES = 8
VMEM_LIMIT = 52 * 1024 * 1024
MOE_VMEM_LIMIT = 58 * 1024 * 1024


def _cparams(*sem):
    return pltpu.CompilerParams(dimension_semantics=sem, vmem_limit_bytes=VMEM_LIMIT)


def _const_spec(shape):
    nd = len(shape)
    return pl.BlockSpec(shape, lambda *_: (0,) * nd, pipeline_mode=pl.Buffered(1))


def _dot(a, b):
    return jnp.dot(a, b, preferred_element_type=F32)


def _dot_nt(a, b):
    return lax.dot_general(a, b, (((1,), (1,)), ((), ())), preferred_element_type=F32)


def _sigmoid(x):
    return 1.0 / (1.0 + jnp.exp(-x))


def _layer_norm(y, g, b):
    mu = jnp.mean(y, axis=-1, keepdims=True)
    d = y - mu
    var = jnp.mean(d * d, axis=-1, keepdims=True)
    return d * lax.rsqrt(var + LN_EPS) * g + b


def _ada_kernel(c_ref, w_ref, b_ref, o_ref):
    c = c_ref[...]
    s = (c * _sigmoid(c)).astype(BF16)
    o_ref[...] = _dot(s, w_ref[...].astype(BF16)) + b_ref[...]


def ada_modulation(c, w_ada, b_ada):
    bsz, d = c.shape
    depth = w_ada.shape[0]
    out = pl.pallas_call(
        _ada_kernel,
        grid=(depth, 6),
        in_specs=[
            pl.BlockSpec((bsz, d), lambda i, j: (0, 0)),
            pl.BlockSpec((None, d, d), lambda i, j: (i, 0, j)),
            pl.BlockSpec((None, None, 1, d), lambda i, j: (i, j, 0, 0)),
        ],
        out_specs=pl.BlockSpec((None, None, bsz, d), lambda i, j: (i, j, 0, 0)),
        out_shape=jax.ShapeDtypeStruct((depth, 6, bsz, d), F32),
        compiler_params=_cparams("arbitrary", "arbitrary"),
        name="ada_modulation",
    )(c, w_ada, b_ada.reshape(depth, 6, 1, d))
    return jnp.transpose(out, (0, 2, 1, 3))


def _tile_specs(seq, tt, d):
    nb8 = seq // SUBLANES
    per = tt // SUBLANES
    x_spec = pl.BlockSpec((None, tt, d), lambda b, t: (b, t, 0))
    prev_spec = pl.BlockSpec((None, SUBLANES, d), lambda b, t: (b, jnp.maximum(t * per - 1, 0), 0))
    next_spec = pl.BlockSpec((None, SUBLANES, d), lambda b, t: (b, jnp.minimum((t + 1) * per, nb8 - 1), 0))
    return x_spec, prev_spec, next_spec


def _mod_spec(d):
    return pl.BlockSpec((None, 6, d), lambda b, t: (b, 0, 0))


def _shifted(cur, prev_row, next_row, tt):
    row = lax.broadcasted_iota(jnp.int32, (tt, 1), 0)
    m1 = jnp.where(row == 0, prev_row, pltpu.roll(cur, 1, 0))
    p1 = jnp.where(row == tt - 1, next_row, pltpu.roll(cur, tt - 1, 0))
    return m1, p1


def _conv_kernel(x_ref, xp_ref, xn_ref, mod_ref, win_ref, cw_ref, wout_ref, lng_ref, lnb_ref,
                 o_ref, *, nt, tt, d):
    t = pl.program_id(1)
    sh, sc, gate = mod_ref[0:1, :], mod_ref[1:2, :], mod_ref[2:3, :]
    parts = [slice(i * tt // CONV_PARTS, (i + 1) * tt // CONV_PARTS) for i in range(CONV_PARTS)]
    xs = [x_ref[r, :] for r in parts]
    ps = [_dot((x * (1.0 + sc) + sh).astype(BF16), win_ref[...]) for x in xs]
    z = jnp.concatenate([p[:, d:2 * d] * p[:, 2 * d:] for p in ps], axis=0)
    halo = jnp.concatenate([xp_ref[...], xn_ref[...]], axis=0)
    hh = (halo * (1.0 + sc) + sh).astype(BF16)
    ph = _dot(hh, win_ref[:, d:])
    zh = ph[:, :d] * ph[:, d:]
    z_prev = jnp.where(t > 0, zh[SUBLANES - 1:SUBLANES, :], 0.0)
    z_next = jnp.where(t < nt - 1, zh[SUBLANES:SUBLANES + 1, :], 0.0)
    z_m1, z_p1 = _shifted(z, z_prev, z_next, tt)
    conv = z_m1 * cw_ref[0:1, :] + z * cw_ref[1:2, :] + z_p1 * cw_ref[2:3, :]
    mixes = [_dot((p[:, :d] * conv[r]).astype(BF16), wout_ref[...]) for p, r in zip(ps, parts)]
    for x, r, mix in zip(xs, parts, mixes):
        o_ref[r, :] = _layer_norm(ALPHA * x + gate * mix, lng_ref[...], lnb_ref[...])


CONV_PARTS = 4


def conv_mixer_layer(x, mod, w_in, conv_w, w_out, ln_g, ln_b, tt=1024):
    bsz, seq, d = x.shape
    tt = min(tt, seq)
    nt = seq // tt
    x_spec, prev_spec, next_spec = _tile_specs(seq, tt, d)
    return pl.pallas_call(
        functools.partial(_conv_kernel, nt=nt, tt=tt, d=d),
        grid=(bsz, nt),
        in_specs=[x_spec, prev_spec, next_spec, _mod_spec(d),
                  _const_spec((d, 3 * d)), _const_spec((3, d)), _const_spec((d, d)),
                  _const_spec((1, d)), _const_spec((1, d))],
        out_specs=pl.BlockSpec((None, tt, d), lambda b, t: (b, t, 0)),
        out_shape=jax.ShapeDtypeStruct(x.shape, F32),
        compiler_params=_cparams("parallel", "arbitrary"),
        name="conv_mixer",
    )(x, x, x, mod, w_in, conv_w, w_out, ln_g, ln_b)


def _ffn_kernel(x_ref, mod_ref, wgu_ref, wd_ref, lng_ref, lnb_ref, o_ref, *, ff, fc):
    sh, sc, gate = mod_ref[3:4, :], mod_ref[4:5, :], mod_ref[5:6, :]
    tt = x_ref.shape[0]
    parts = [slice(i * tt // FFN_PARTS, (i + 1) * tt // FFN_PARTS) for i in range(FFN_PARTS)]
    chunks = range(ff // fc)
    xs = [x_ref[r, :] for r in parts]
    hs = [(x * (1.0 + sc) + sh).astype(BF16) for x in xs]
    gs = [[_dot(h, wgu_ref[:, c * fc:(c + 1) * fc]) for c in chunks] for h in hs]
    us = [[_dot(h, wgu_ref[:, ff + c * fc:ff + (c + 1) * fc]) for c in chunks] for h in hs]
    acts = [[(g * _sigmoid(g) * u).astype(BF16) for g, u in zip(gr, ur)] for gr, ur in zip(gs, us)]
    for x, r, ar in zip(xs, parts, acts):
        acc = _dot(ar[0], wd_ref[0:fc, :])
        for c in chunks[1:]:
            acc = acc + _dot(ar[c], wd_ref[c * fc:(c + 1) * fc, :])
        o_ref[r, :] = _layer_norm(ALPHA * x + gate * acc, lng_ref[...], lnb_ref[...])


FFN_PARTS = 2


def dense_ffn_layer(x, mod, w_gu, w_down, ln_g, ln_b, tt=512):
    bsz, seq, d = x.shape
    tt = min(tt, seq)
    ff = w_down.shape[0]
    fc = ff
    return pl.pallas_call(
        functools.partial(_ffn_kernel, ff=ff, fc=fc),
        grid=(bsz, seq // tt),
        in_specs=[pl.BlockSpec((None, tt, d), lambda b, t: (b, t, 0)), _mod_spec(d),
                  _const_spec((d, 2 * ff)), _const_spec((ff, d)),
                  _const_spec((1, d)), _const_spec((1, d))],
        out_specs=pl.BlockSpec((None, tt, d), lambda b, t: (b, t, 0)),
        out_shape=jax.ShapeDtypeStruct(x.shape, F32),
        compiler_params=_cparams("parallel", "arbitrary"),
        name="dense_ffn",
    )(x, mod, w_gu, w_down, ln_g, ln_b)


MOE_ROW_CHUNK = 256
MOE_ROW_BLOCKS = (128, 256)


def _moe_kernel(x_ref, mod_ref, wr_ref, br_ref, wgu_ref, wd_ref, lng_ref, lnb_ref, o_ref,
                h_sc, comb_sc, acc_sc, rankc_sc, rankr_sc, maskr_sc, y_sc, *, fe, blocks):
    e = pl.program_id(2)
    tt = comb_sc.shape[0]
    lane = lax.broadcasted_iota(jnp.int32, comb_sc.shape, 1)

    @pl.when(e == 0)
    def _():
        sh, sc = mod_ref[3:4, :], mod_ref[4:5, :]
        h = x_ref[...] * (1.0 + sc) + sh
        h_hi = h.astype(BF16)
        h_sc[...] = h_hi
        h_lo = (h - h_hi.astype(F32)).astype(BF16)
        wr = wr_ref[...]
        w_hi = wr.astype(BF16)
        w_lo = (wr - w_hi.astype(F32)).astype(BF16)
        hh = _dot(h_hi, jnp.concatenate([w_hi, w_lo], axis=1))
        logits = hh[:, :ROUTER_LANES] + hh[:, ROUTER_LANES:] + _dot(h_lo, w_hi) + br_ref[...]
        m = jnp.max(logits, axis=-1, keepdims=True)
        ex = jnp.exp(logits - m)
        probs = ex / jnp.sum(ex, axis=-1, keepdims=True)
        valid = lane < N_EXPERTS
        p = jnp.where(valid, probs, -1.0)
        p1 = jnp.max(p, axis=-1, keepdims=True)
        i1 = jnp.min(jnp.where(p == p1, lane, ROUTER_LANES), axis=-1, keepdims=True)
        pr = jnp.where(lane == i1, -1.0, p)
        p2 = jnp.max(pr, axis=-1, keepdims=True)
        i2 = jnp.min(jnp.where(pr == p2, lane, ROUTER_LANES), axis=-1, keepdims=True)
        tot = p1 + p2
        comb = jnp.where(lane == i1, p1 / tot, 0.0) + jnp.where(lane == i2, p2 / tot, 0.0)
        comb_sc[...] = comb
        acc_sc[...] = jnp.zeros(acc_sc.shape, F32)
        y_sc[...] = jnp.zeros(y_sc.shape, BF16)
        routed = jnp.where(comb > 0.0, 1.0, 0.0).astype(BF16)
        q0 = lax.broadcasted_iota(jnp.int32, (ROUTER_LANES, ROUTER_LANES), 0)
        q1 = lax.broadcasted_iota(jnp.int32, (ROUTER_LANES, ROUTER_LANES), 1)
        routed_t = _dot_nt(jnp.where(q0 == q1, 1.0, 0.0).astype(BF16), routed)
        maskr_sc[...] = routed_t
        routed_tb = routed_t.astype(BF16)
        rc = min(MOE_ROW_CHUNK, tt)
        for c in range(tt // rc):
            r_i = lax.broadcasted_iota(jnp.int32, (rc, tt), 0) + c * rc
            c_i = lax.broadcasted_iota(jnp.int32, (rc, tt), 1)
            rankc_sc[c * rc:(c + 1) * rc, :] = _dot(jnp.where(c_i < r_i, 1.0, 0.0).astype(BF16), routed)
            r_j = lax.broadcasted_iota(jnp.int32, (tt, rc), 0)
            c_j = lax.broadcasted_iota(jnp.int32, (tt, rc), 1) + c * rc
            rankr_sc[:, c * rc:(c + 1) * rc] = _dot(routed_tb, jnp.where(r_j < c_j, 1.0, 0.0).astype(BF16))

    def expert(hb):
        g = _dot(hb, wgu_ref[:, :fe])
        u = _dot(hb, wgu_ref[:, fe:])
        act = (g * _sigmoid(g) * u).astype(BF16)
        return _dot(act, wd_ref[...])

    w_col = jnp.sum(jnp.where(lane == e, comb_sc[...], 0.0), axis=-1, keepdims=True)
    rk_col = jnp.sum(jnp.where(lane == e, rankc_sc[...], 0.0), axis=-1, keepdims=True).astype(jnp.int32)
    rk_row = rankr_sc[pl.ds(e, 1), :].astype(jnp.int32)
    mk_row = maskr_sc[pl.ds(e, 1), :]
    n_routed = jnp.sum(mk_row).astype(jnp.int32)
    half = tt // 2
    gb, sb = blocks

    def gather_block(j, carry):
        base = pl.multiple_of(j * gb, gb)
        sub = lax.broadcasted_iota(jnp.int32, (gb, tt), 0) + base
        pick = jnp.where((rk_row == sub) & (mk_row > 0.0), 1.0, 0.0).astype(BF16)
        y_sc[pl.ds(base, gb), :] = expert(_dot(pick, h_sc[...]).astype(BF16)).astype(BF16)
        return carry

    lax.fori_loop(0, (n_routed + gb - 1) // gb, gather_block, 0)

    for c in range(tt // sb):
        @pl.when(n_routed > c * sb)
        def _(c=c):
            for r0 in (0, half):
                rows = slice(r0, r0 + half)
                ln_i = lax.broadcasted_iota(jnp.int32, (half, sb), 1) + c * sb
                put = jnp.where((rk_col[rows] == ln_i) & (w_col[rows] > 0.0), 1.0, 0.0).astype(BF16)
                acc_sc[rows, :] += w_col[rows] * _dot(put, y_sc[c * sb:(c + 1) * sb, :])

    @pl.when(e == N_EXPERTS - 1)
    def _():
        gate = mod_ref[5:6, :]
        o_ref[...] = _layer_norm(ALPHA * x_ref[...] + gate * acc_sc[...], lng_ref[...], lnb_ref[...])


def moe_ffn_layer(x, mod, w_router, b_router, w_gu, w_down, ln_g, ln_b, tt=1024, blocks=MOE_ROW_BLOCKS):
    bsz, seq, d = x.shape
    tt = min(tt, seq)
    n_e, fe = w_down.shape[0], w_down.shape[1]
    return pl.pallas_call(
        functools.partial(_moe_kernel, fe=fe, blocks=tuple(min(b, tt) for b in blocks)),
        grid=(bsz, seq // tt, n_e),
        in_specs=[pl.BlockSpec((None, tt, d), lambda b, t, e: (b, t, 0)),
                  pl.BlockSpec((None, 6, d), lambda b, t, e: (b, 0, 0)),
                  _const_spec((d, ROUTER_LANES)), _const_spec((1, ROUTER_LANES)),
                  pl.BlockSpec((None, d, 2 * fe), lambda b, t, e: (e, 0, 0)),
                  pl.BlockSpec((None, fe, d), lambda b, t, e: (e, 0, 0)),
                  _const_spec((1, d)), _const_spec((1, d))],
        out_specs=pl.BlockSpec((None, tt, d), lambda b, t, e: (b, t, 0)),
        out_shape=jax.ShapeDtypeStruct(x.shape, F32),
        scratch_shapes=[pltpu.VMEM((tt, d), BF16), pltpu.VMEM((tt, ROUTER_LANES), F32),
                        pltpu.VMEM((tt, d), F32), pltpu.VMEM((tt, ROUTER_LANES), F32),
                        pltpu.VMEM((ROUTER_LANES, tt), F32), pltpu.VMEM((ROUTER_LANES, tt), F32),
                        pltpu.VMEM((tt, d), BF16)],
        compiler_params=pltpu.CompilerParams(
            dimension_semantics=("parallel", "arbitrary", "arbitrary"), vmem_limit_bytes=MOE_VMEM_LIMIT),
        name="moe_ffn",
    )(x, mod, w_router, b_router, w_gu, w_down, ln_g, ln_b)


def _pad_router(w_router, b_router):
    d, n_e = w_router.shape
    w = jnp.zeros((d, ROUTER_LANES), F32).at[:, :n_e].set(w_router)
    b = jnp.full((1, ROUTER_LANES), NEG_INF, F32).at[0, :n_e].set(b_router)
    return w, b


def _qkv_kernel(x_ref, mod_ref, w_ref, q_ref, k_ref, v_ref, *, d):
    sh, sc = mod_ref[0:1, :], mod_ref[1:2, :]
    h = (x_ref[...] * (1.0 + sc) + sh).astype(BF16)
    p = _dot(h, w_ref[...])
    q_ref[...] = (p[:, :d] * (NA_HEAD_DIM ** -0.5)).astype(BF16)
    k_ref[...] = p[:, d:2 * d].astype(BF16)
    v_ref[...] = p[:, 2 * d:].astype(BF16)


NA_ROWS_PER_STEP = 4


def _na_kernel(q_ref, k_ref, v_ref, *rest, rows):
    bias_refs, o_ref = rest[:-1], rest[-1]
    nk = NA_WIN_ROWS * GRID_W
    pw_ = 2 * NA_HEAD_DIM
    pairs = range(NA_HEADS // 2)
    lanes = [slice(p * pw_, (p + 1) * pw_) for p in pairs]
    head0 = lax.broadcasted_iota(jnp.int32, (GRID_W, pw_), 1) < NA_HEAD_DIM
    keys = [(i, p) for i in range(len(bias_refs)) for p in pairs]
    qrows = {i: slice(i * GRID_W, (i + 1) * GRID_W) for i in range(len(bias_refs))}
    start = {}
    for i in range(len(bias_refs)):
        r = pl.program_id(1) * len(bias_refs) + i
        r0 = jnp.clip(r - NA_WIN_ROWS // 2, 0, rows - NA_WIN_ROWS)
        start[i] = pl.multiple_of(r0 * GRID_W, GRID_W)
    q2 = {}
    for i, p in keys:
        q = q_ref[qrows[i], lanes[p]]
        z = jnp.zeros_like(q)
        q2[i, p] = jnp.concatenate([jnp.where(head0, q, z), jnp.where(head0, z, q)], axis=0)
    s = {(i, p): _dot_nt(q2[i, p], k_ref[pl.ds(start[i], nk), lanes[p]])
         + bias_refs[i][2 * p:2 * p + 2].reshape(2 * GRID_W, nk) for i, p in keys}
    m = {k: jnp.max(s[k], axis=-1, keepdims=True) for k in keys}
    e = {k: jnp.exp(s[k] - m[k]) for k in keys}
    l = {k: jnp.sum(e[k], axis=-1, keepdims=True) for k in keys}
    o = {(i, p): _dot(e[i, p].astype(BF16), v_ref[pl.ds(start[i], nk), lanes[p]]) / l[i, p] for i, p in keys}
    for i, p in keys:
        o_ref[qrows[i], lanes[p]] = jnp.where(head0, o[i, p][:GRID_W], o[i, p][GRID_W:]).astype(BF16)


def _na_bias_table(rpb):
    nh, nr, nc = rpb.shape
    w = GRID_W
    qc = jnp.arange(w)[:, None]
    kc = jnp.arange(w)[None, :]
    cs = jnp.clip(qc - NA_WIN_COLS // 2, 0, w - NA_WIN_COLS)
    col_ok = (kc >= cs) & (kc < cs + NA_WIN_COLS)
    lo = w - NA_WIN_COLS
    e = jnp.pad(rpb, ((0, 0), (0, 0), (lo, 2 * w - lo - nc)))
    flat = jnp.tile(e, (1, 1, w))
    toep = flat[:, :, w - 1:w - 1 + w * (2 * w - 1)].reshape(nh, nr, w, 2 * w - 1)[..., :w]
    toep = jnp.where(col_ok[None, None], toep, NEG_INF)
    tab = jnp.stack([toep[:, dl:dl + NA_WIN_ROWS] for dl in range(NA_WIN_ROWS)], axis=1)
    tab = jnp.transpose(tab, (0, 1, 3, 2, 4))
    return tab.reshape(nh, NA_WIN_ROWS, w, NA_WIN_ROWS * w)


def _proj_ln_kernel(*refs, n_in):
    a_refs = refs[:n_in]
    x_ref, mod_ref, w_ref, lng_ref, lnb_ref, o_ref = refs[n_in:]
    a = a_refs[0][...]
    if n_in == 2:
        a = (a.astype(F32) + a_refs[1][...].astype(F32)).astype(BF16)
    mix = _dot(a, w_ref[...])
    gate = mod_ref[2:3, :]
    o_ref[...] = _layer_norm(ALPHA * x_ref[...] + gate * mix, lng_ref[...], lnb_ref[...])


def proj_residual_ln(acts, x, mod, w_o, ln_g, ln_b, tt=1024):
    bsz, seq, d = x.shape
    tt = min(tt, seq)
    tile = pl.BlockSpec((None, tt, d), lambda b, t: (b, t, 0))
    return pl.pallas_call(
        functools.partial(_proj_ln_kernel, n_in=len(acts)),
        grid=(bsz, seq // tt),
        in_specs=[tile] * len(acts) + [tile, _mod_spec(d), _const_spec((d, d)),
                                       _const_spec((1, d)), _const_spec((1, d))],
        out_specs=tile,
        out_shape=jax.ShapeDtypeStruct(x.shape, F32),
        compiler_params=_cparams("parallel", "arbitrary"),
        name="proj_residual_ln",
    )(*acts, x, mod, w_o, ln_g, ln_b)


def na_mixer_layer(x, mod, w_qkv, bias_tab, w_o, ln_g, ln_b, tt=512):
    bsz, seq, d = x.shape
    tt = min(tt, seq)
    rows = seq // GRID_W
    assert rows >= NA_WIN_ROWS and seq % GRID_W == 0
    tile = pl.BlockSpec((None, tt, d), lambda b, t: (b, t, 0))
    q, k, v = pl.pallas_call(
        functools.partial(_qkv_kernel, d=d),
        grid=(bsz, seq // tt),
        in_specs=[tile, _mod_spec(d), _const_spec((d, 3 * d))],
        out_specs=[tile] * 3,
        out_shape=[jax.ShapeDtypeStruct(x.shape, BF16)] * 3,
        compiler_params=_cparams("parallel", "arbitrary"),
        name="na_qkv",
    )(x, mod, w_qkv)

    def delta(r):
        return jnp.clip(r - NA_WIN_ROWS // 2, 0, rows - NA_WIN_ROWS) - r + NA_WIN_ROWS - 1

    rps = NA_ROWS_PER_STEP
    assert rows % rps == 0
    row_spec = pl.BlockSpec((None, rps * GRID_W, d), lambda b, r: (b, r, 0))
    seq_spec = pl.BlockSpec((None, seq, d), lambda b, r: (b, 0, 0))
    bias_specs = [pl.BlockSpec((NA_HEADS, None, GRID_W, NA_WIN_ROWS * GRID_W),
                               lambda b, r, i=i: (0, delta(r * rps + i), 0, 0)) for i in range(rps)]
    att = pl.pallas_call(
        functools.partial(_na_kernel, rows=rows),
        grid=(bsz, rows // rps),
        in_specs=[row_spec, seq_spec, seq_spec] + bias_specs,
        out_specs=row_spec,
        out_shape=jax.ShapeDtypeStruct(x.shape, BF16),
        compiler_params=_cparams("parallel", "arbitrary"),
        name="na_attention",
    )(q, k, v, *([bias_tab] * rps))
    return proj_residual_ln([att], x, mod, w_o, ln_g, ln_b)


def _rk_proj_kernel(x_ref, xp_ref, xn_ref, mod_ref, mu_ref, wrkv_ref, w1_ref, w2_ref, a1_ref, a2_ref,
                    g1_ref, g2_ref, w0_ref, a0_ref, kk_ref, ka_ref,
                    r_o, v_o, kk_o, lw_o, km_o, b_o, gate_o, *, nt, tt):
    t = pl.program_id(1)
    sh, sc = mod_ref[0:1, :], mod_ref[1:2, :]
    h = x_ref[...] * (1.0 + sc) + sh
    h_prev = jnp.where(t > 0, xp_ref[SUBLANES - 1:SUBLANES, :] * (1.0 + sc) + sh, 0.0)
    h_next = jnp.where(t < nt - 1, xn_ref[0:1, :] * (1.0 + sc) + sh, 0.0)
    h_m1, h_p1 = _shifted(h, h_prev, h_next, tt)
    xx = 0.5 * (h_m1 + h_p1) - h

    h_b, xx_b, mu_b = h.astype(BF16), xx.astype(BF16), mu_ref[...].astype(BF16)

    def mixed(p):
        return h_b + xx_b * mu_b[p:p + 1, :]

    r = _dot(mixed(0), wrkv_ref[0])
    k = _dot(mixed(1), wrkv_ref[1])
    v = _dot(mixed(2), wrkv_ref[2])
    r_o[...] = r.astype(BF16)
    v_o[...] = v.astype(BF16)
    lw_in = jnp.tanh(_dot(mixed(3), w1_ref[...])).astype(BF16)
    a_in = _dot(mixed(4), a1_ref[...]).astype(BF16)
    g_in = _sigmoid(_dot(mixed(5), g1_ref[...])).astype(BF16)

    kk = k * kk_ref[...]
    kk_o[...] = kk.astype(BF16)
    rw = w1_ref.shape[1] // 2
    rg = g1_ref.shape[1] // 2
    for z in range(2):
        wl = w0_ref[z:z + 1, :] + _dot(lw_in[:, z * rw:(z + 1) * rw], w2_ref[z])
        lw_o[z] = (-RK_DECAY_SCALE) * _sigmoid(wl)
        a_lr = _sigmoid(a0_ref[z:z + 1, :] + _dot(a_in[:, z * rw:(z + 1) * rw], a2_ref[z]))
        km_o[z] = (k * (1.0 + (a_lr - 1.0) * ka_ref[...])).astype(BF16)
        b_o[z] = (kk * a_lr).astype(BF16)
        gate_o[z] = _dot(g_in[:, z * rg:(z + 1) * rg], g2_ref[z]).astype(BF16)


def _head_sum(x):
    head0 = lax.broadcasted_iota(jnp.int32, x.shape, 1) < RK_HEAD_DIM
    s0 = jnp.sum(jnp.where(head0, x, 0.0), axis=-1, keepdims=True)
    s1 = jnp.sum(jnp.where(head0, 0.0, x), axis=-1, keepdims=True)
    return jnp.where(head0, s0, s1)


def _rk_scan_kernel(r_ref, v_ref, kk_ref, lw_ref, km_ref, b_ref, gate_ref, rk_ref, lg_ref, lb_ref,
                    o_ref, s_ref, *, reverse):
    tb, hb = r_ref.shape
    cl = RK_CHUNK
    nc, nh = tb // cl, hb // RK_HEAD_DIM

    @pl.when(pl.program_id(2) == 0)
    def _():
        s_ref[...] = jnp.zeros(s_ref.shape, F32)

    ri = lax.broadcasted_iota(jnp.int32, (tb, tb), 0)
    ci = lax.broadcasted_iota(jnp.int32, (tb, tb), 1)
    ordered = (ci >= ri) if reverse else (ci <= ri)
    tri = jnp.where((ri // cl == ci // cl) & ordered, 1.0, 0.0).astype(BF16)
    lw = lw_ref[...]
    lw_hi = lw.astype(BF16)
    lw_lo = (lw - lw_hi.astype(F32)).astype(BF16)
    cum = _dot(tri, lw_hi) + _dot(tri, lw_lo)
    last = 0 if reverse else cl - 1
    tot_rows = [cum[c * cl + last:c * cl + last + 1, :] for c in range(nc)]
    tot = jnp.concatenate([jnp.broadcast_to(tr, (cl, hb)) for tr in tot_rows], axis=0)

    rr = r_ref[...].astype(F32)
    vv = v_ref[...]
    kk = kk_ref[...].astype(F32)
    pair = 2 * RK_HEAD_DIM
    rnorm = jnp.concatenate(
        [1.0 / jnp.maximum(jnp.sqrt(_head_sum(jnp.square(kk[:, i:i + pair]))), 1e-12) for i in range(0, hb, pair)],
        axis=1)
    kkn = kk * rnorm
    km = km_ref[...].astype(F32)
    bb = b_ref[...].astype(F32) * rnorm
    e_neg = jnp.exp(-cum)
    e_rem = jnp.exp(tot - cum)
    a_t = (-kkn * jnp.exp(cum - lw)).astype(BF16)
    r_t = (rr * jnp.exp(cum)).astype(BF16)
    b_t = (bb * e_neg).astype(BF16)
    k_t = (km * e_neg).astype(BF16)
    b_h = (bb * e_rem).astype(BF16)
    k_h = (km * e_rem).astype(BF16)
    bonus_w = rr * km * rk_ref[...]

    pw_ = 2 * RK_HEAD_DIM
    npair = hb // pw_
    t_i = lax.broadcasted_iota(jnp.int32, (cl, pw_), 0)
    s_i = lax.broadcasted_iota(jnp.int32, (cl, pw_), 1) % cl
    strict = (t_i < s_i) if reverse else (t_i > s_i)
    incl = (t_i <= s_i) if reverse else (t_i >= s_i)
    eye = jnp.where(t_i == s_i, 1.0, 0.0)
    head0 = lax.broadcasted_iota(jnp.int32, (cl, pw_), 1) < RK_HEAD_DIM
    q0 = lax.broadcasted_iota(jnp.int32, (pw_, pw_), 0)
    q1 = lax.broadcasted_iota(jnp.int32, (pw_, pw_), 1)
    same_head = (q0 // RK_HEAD_DIM) == (q1 // RK_HEAD_DIM)

    head_sum = _head_sum

    def bd(x):
        z = jnp.zeros_like(x)
        return jnp.concatenate([jnp.where(head0, x, z), jnp.where(head0, z, x)], axis=0)

    keys = [(c, p) for c in range(nc) for p in range(npair)]
    rsl = {k: slice(k[0] * cl, (k[0] + 1) * cl) for k in keys}
    lsl = {k: slice(k[1] * pw_, (k[1] + 1) * pw_) for k in keys}
    blk = lambda arr, k: arr[rsl[k], lsl[k]]

    gram = {k: _dot_nt(jnp.concatenate([blk(a_t, k), blk(r_t, k)], axis=0),
                       jnp.concatenate([bd(blk(b_t, k)), bd(blk(k_t, k))], axis=0)) for k in keys}
    m_ab = {k: jnp.where(strict, gram[k][:cl, :pw_], 0.0) for k in keys}
    m_ak = {k: jnp.where(strict, gram[k][:cl, pw_:], 0.0).astype(BF16) for k in keys}
    m_rb = {k: jnp.where(incl, gram[k][cl:, :pw_], 0.0).astype(BF16) for k in keys}
    m_rk = {k: jnp.where(incl, gram[k][cl:, pw_:], 0.0).astype(BF16) for k in keys}
    vbd = {k: bd(blk(vv, k)) for k in keys}
    my = {k: _dot(jnp.concatenate([m_ak[k], m_rk[k]], axis=0), vbd[k]) for k in keys}
    v_t = {k: jnp.transpose(blk(vv, k).astype(F32)).astype(BF16) for k in keys}
    vk = {k: jnp.where(same_head, _dot(v_t[k], blk(k_h, k)), 0.0) for k in keys}
    inv = {k: eye + m_ab[k] for k in keys}
    pwb = {k: m_ab[k].astype(BF16) for k in keys}
    pw = {k: _dot(pwb[k], bd(pwb[k])) for k in keys}
    n = 4
    while n < cl:
        pwb = {k: pw[k].astype(BF16) for k in keys}
        res = {k: _dot(pwb[k], jnp.concatenate([bd(pwb[k]), bd(inv[k].astype(BF16))], axis=1)) for k in keys}
        pw = {k: res[k][:, :pw_] for k in keys}
        inv = {k: inv[k] + res[k][:, pw_:] for k in keys}
        n *= 2
    inv = {k: (inv[k] + _dot(pw[k].astype(BF16), bd(inv[k].astype(BF16)))).astype(BF16) for k in keys}
    au = {k: _dot(inv[k], jnp.concatenate([bd(blk(a_t, k)), bd(my[k][:cl].astype(BF16))], axis=1))
          for k in keys}
    a_hat = {k: au[k][:, :pw_].astype(BF16) for k in keys}
    u0_t = {k: jnp.transpose(au[k][:, pw_:]) for k in keys}
    ar = {k: jnp.concatenate([a_hat[k], blk(r_t, k)], axis=0) for k in keys}

    pairs = range(npair)
    for c in (range(nc - 1, -1, -1) if reverse else range(nc)):
        st = {p: s_ref[p] for p in pairs}
        stb = {p: st[p].astype(BF16) for p in pairs}
        res = {p: _dot_nt(ar[c, p], stb[p]) for p in pairs}
        res_t = {p: _dot_nt(stb[p], a_hat[c, p]) for p in pairs}
        ub = {p: (au[c, p][:, pw_:] + res[p][:cl]).astype(BF16) for p in pairs}
        ub_t = {p: (u0_t[c, p] + res_t[p]).astype(BF16) for p in pairs}
        y = {p: my[c, p][cl:] + res[p][cl:] + _dot(m_rb[c, p], bd(ub[p])) for p in pairs}
        for p in pairs:
            k = (c, p)
            decay = jnp.exp(tot_rows[c][:, lsl[k]])
            s_ref[p] = jnp.where(same_head, st[p] * decay + _dot(ub_t[p], blk(b_h, k)), 0.0) + vk[k]
        dy = {p: y[p] - head_sum(y[p]) * (1.0 / RK_HEAD_DIM) for p in pairs}
        var = {p: head_sum(dy[p] * dy[p]) * (1.0 / RK_HEAD_DIM) for p in pairs}
        for p in pairs:
            k = (c, p)
            yn = dy[p] * lax.rsqrt(var[p] + RK_GN_EPS) * lg_ref[:, lsl[k]] + lb_ref[:, lsl[k]]
            bonus = head_sum(blk(bonus_w, k)) * blk(vv, k).astype(F32)
            o_ref[rsl[k], lsl[k]] = (blk(gate_ref, k).astype(F32) * (yn + bonus)).astype(BF16)


def rwkv_mixer_layer(x, mod, p, ln_g, ln_b, tt=256):
    bsz, seq, d = x.shape
    tt = min(tt, seq)
    nt = seq // tt
    x_spec, prev_spec, next_spec = _tile_specs(seq, tt, d)
    tile = pl.BlockSpec((None, tt, d), lambda b, t: (b, t, 0))
    tile2 = pl.BlockSpec((2, None, tt, d), lambda b, t: (0, b, t, 0))
    rw, rg = p['w1'].shape[1] // 2, p['g1'].shape[1] // 2
    sh_bf = jax.ShapeDtypeStruct(x.shape, BF16)
    sh2_bf = jax.ShapeDtypeStruct((2,) + x.shape, BF16)
    sh2_f32 = jax.ShapeDtypeStruct((2,) + x.shape, F32)
    r, v, kkn, lw, km, bz, gate = pl.pallas_call(
        functools.partial(_rk_proj_kernel, nt=nt, tt=tt),
        grid=(bsz, nt),
        in_specs=[x_spec, prev_spec, next_spec, _mod_spec(d), _const_spec((6, d)),
                  _const_spec((3, d, d)),
                  _const_spec((d, 2 * rw)), _const_spec((2, rw, d)),
                  _const_spec((d, 2 * rw)), _const_spec((2, rw, d)),
                  _const_spec((d, 2 * rg)), _const_spec((2, rg, d)),
                  _const_spec((2, d)), _const_spec((2, d)), _const_spec((1, d)), _const_spec((1, d))],
        out_specs=[tile, tile, tile, tile2, tile2, tile2, tile2],
        out_shape=[sh_bf, sh_bf, sh_bf, sh2_f32, sh2_bf, sh2_bf, sh2_bf],
        compiler_params=_cparams("parallel", "arbitrary"),
        name="rwkv_proj",
    )(x, x, x, mod, p['mu'], p['w_rkv'], p['w1'], p['w2'], p['a1'], p['a2'], p['g1'], p['g2'],
      p['w0'], p['a0'], p['k_k'], p['k_a'])

    tb = min(RK_TOK_BLOCK, seq)
    hb = RK_LANE_BLOCK
    nb = seq // tb
    outs = []
    for z in range(2):
        rev = z == 1

        def tok(i, rev=rev):
            return nb - 1 - i if rev else i

        blk = pl.BlockSpec((None, tb, hb), lambda b, g, i: (b, tok(i), g))
        blk2 = pl.BlockSpec((None, None, tb, hb), lambda b, g, i, z=z: (z, b, tok(i), g))
        vec = pl.BlockSpec((1, hb), lambda b, g, i: (0, g))
        vec2 = pl.BlockSpec((None, 1, hb), lambda b, g, i, z=z: (z, 0, g))
        outs.append(pl.pallas_call(
            functools.partial(_rk_scan_kernel, reverse=rev),
            grid=(bsz, d // hb, nb),
            in_specs=[blk, blk, blk, blk2, blk2, blk2, blk2, vec, vec2, vec2],
            out_specs=blk,
            out_shape=sh_bf,
            scratch_shapes=[pltpu.VMEM((hb // (2 * RK_HEAD_DIM), 2 * RK_HEAD_DIM, 2 * RK_HEAD_DIM), F32)],
            compiler_params=_cparams("parallel", "parallel", "arbitrary"),
            name="rwkv_scan_rev" if rev else "rwkv_scan_fwd",
        )(r, v, kkn, lw, km, bz, gate, p['r_k'], p['lnx_g'], p['lnx_b']))
    return proj_residual_ln(outs, x, mod, p['w_o'], ln_g, ln_b)


def _prepare_params(w):
    d = D_MODEL
    bf = lambda a: a.astype(BF16)
    row = lambda a: a.reshape(1, d)
    p = {
        'w_ada': w['w_ada'], 'b_ada': w['b_ada'],
        'ln_g': w['ln_g'], 'ln_b': w['ln_b'],
        'conv_w_in': bf(w['conv_w_in']), 'conv_w': w['conv_w'], 'conv_w_out': bf(w['conv_w_out']),
        'na_w_qkv': bf(w['na_w_qkv']), 'na_w_o': bf(w['na_w_o']),
        'na_bias': [_na_bias_table(w['na_rpb'][j]) for j in range(w['na_rpb'].shape[0])],
        'ffn_w_gu': bf(w['ffn_w_gu']), 'ffn_w_down': bf(w['ffn_w_down']),
        'moe_router': [_pad_router(w['moe_w_router'][j], w['moe_b_router'][j])
                       for j in range(w['moe_w_router'].shape[0])],
        'moe_w_gu': bf(w['moe_w_gu']), 'moe_w_down': bf(w['moe_w_down']),
        'rk': [],
    }
    cat = lambda a: jnp.concatenate([a[0], a[1]], axis=1)
    for j in range(w['rk_mu'].shape[0]):
        p['rk'].append({
            'mu': w['rk_mu'][j], 'w_rkv': bf(w['rk_w_rkv'][j]),
            'w1': bf(cat(w['rk_w1'][j])), 'w2': bf(w['rk_w2'][j]),
            'a1': bf(cat(w['rk_a1'][j])), 'a2': bf(w['rk_a2'][j]),
            'g1': bf(cat(w['rk_g1'][j])), 'g2': bf(w['rk_g2'][j]),
            'w0': w['rk_w0'][j], 'a0': w['rk_a0'][j],
            'k_k': row(w['rk_k_k'][j]), 'k_a': row(w['rk_k_a'][j]),
            'r_k': w['rk_r_k'][j].reshape(1, d),
            'lnx_g': w['rk_lnx_g'][j].reshape(2, 1, d), 'lnx_b': w['rk_lnx_b'][j].reshape(2, 1, d),
            'w_o': bf(w['rk_w_o'][j]),
        })
    return p


def _trunk(x, mod_all, p):
    d = D_MODEL
    for i in range(DEPTH):
        mod = mod_all[i]
        lng = lambda s: p['ln_g'][i, s].reshape(1, d)
        lnb = lambda s: p['ln_b'][i, s].reshape(1, d)
        kind, j = i % 3, i // 3
        if kind == 0:
            x = conv_mixer_layer(x, mod, p['conv_w_in'][j], p['conv_w'][j], p['conv_w_out'][j],
                                 lng(0), lnb(0))
        elif kind == 1:
            x = na_mixer_layer(x, mod, p['na_w_qkv'][j], p['na_bias'][j], p['na_w_o'][j], lng(0), lnb(0))
        else:
            x = rwkv_mixer_layer(x, mod, p['rk'][j], lng(0), lnb(0))
        if i % 2 == 0:
            x = dense_ffn_layer(x, mod, p['ffn_w_gu'][i // 2], p['ffn_w_down'][i // 2], lng(1), lnb(1))
        else:
            wr, br = p['moe_router'][i // 2]
            x = moe_ffn_layer(x, mod, wr, br, p['moe_w_gu'][i // 2], p['moe_w_down'][i // 2],
                              lng(1), lnb(1))
    return x


def kernel(x_prompt, x_sample, c_prompt, c_sample, w_ada, b_ada, ln_g, ln_b, conv_w_in, conv_w, conv_w_out, na_w_qkv, na_rpb, na_w_o, rk_mu, rk_w_rkv, rk_w0, rk_w1, rk_w2, rk_a0, rk_a1, rk_a2, rk_g1, rk_g2, rk_k_k, rk_k_a, rk_r_k, rk_lnx_g, rk_lnx_b, rk_w_o, ffn_w_gu, ffn_w_down, moe_w_router, moe_b_router, moe_w_gu, moe_w_down):
    p = _prepare_params(dict(
        w_ada=w_ada, b_ada=b_ada, ln_g=ln_g, ln_b=ln_b,
        conv_w_in=conv_w_in, conv_w=conv_w, conv_w_out=conv_w_out,
        na_w_qkv=na_w_qkv, na_rpb=na_rpb, na_w_o=na_w_o,
        rk_mu=rk_mu, rk_w_rkv=rk_w_rkv, rk_w0=rk_w0, rk_w1=rk_w1, rk_w2=rk_w2,
        rk_a0=rk_a0, rk_a1=rk_a1, rk_a2=rk_a2, rk_g1=rk_g1, rk_g2=rk_g2,
        rk_k_k=rk_k_k, rk_k_a=rk_k_a, rk_r_k=rk_r_k, rk_lnx_g=rk_lnx_g, rk_lnx_b=rk_lnx_b,
        rk_w_o=rk_w_o, ffn_w_gu=ffn_w_gu, ffn_w_down=ffn_w_down,
        moe_w_router=moe_w_router, moe_b_router=moe_b_router,
        moe_w_gu=moe_w_gu, moe_w_down=moe_w_down))
    n_p = c_prompt.shape[0]
    mod_all = ada_modulation(jnp.concatenate([c_prompt, c_sample], axis=0), p['w_ada'], p['b_ada'])
    return (_trunk(x_prompt, mod_all[:, :n_p], p), _trunk(x_sample, mod_all[:, n_p:], p))
```

```python
import functools

import jax
import jax.numpy as jnp
from jax import lax
from jax.experimental import pallas as pl
from jax.experimental.pallas import tpu as pltpu

F32 = jnp.float32
BF16 = jnp.bfloat16

D_MODEL = 1024
DEPTH = 4
ALPHA = (2 * DEPTH) ** 0.25
LN_EPS = 1e-5

GRID_W = 64
NA_HEADS = 16
NA_HEAD_DIM = D_MODEL // NA_HEADS
NA_WIN_ROWS = 8
NA_WIN_COLS = 16
NEG_INF = -1e30

RK_HEAD_DIM = 64
RK_GN_EPS = 64e-5
RK_DECAY_SCALE = 0.6065306597126334
RK_CHUNK = 64
RK_TOK_BLOCK = 256
RK_LANE_BLOCK = 1024

N_EXPERTS = 8
ROUTER_LANES = 128

SUBLANES = 8
VMEM_LIMIT = 52 * 1024 * 1024
MOE_VMEM_LIMIT = 58 * 1024 * 1024


def _cparams(*sem):
    return pltpu.CompilerParams(dimension_semantics=sem, vmem_limit_bytes=VMEM_LIMIT)


def _const_spec(shape):
    nd = len(shape)
    return pl.BlockSpec(shape, lambda *_: (0,) * nd, pipeline_mode=pl.Buffered(1))


def _dot(a, b):
    return jnp.dot(a, b, preferred_element_type=F32)


def _dot_nt(a, b):
    return lax.dot_general(a, b, (((1,), (1,)), ((), ())), preferred_element_type=F32)


def _sigmoid(x):
    return 1.0 / (1.0 + jnp.exp(-x))


def _layer_norm(y, g, b):
    mu = jnp.mean(y, axis=-1, keepdims=True)
    d = y - mu
    var = jnp.mean(d * d, axis=-1, keepdims=True)
    return d * lax.rsqrt(var + LN_EPS) * g + b


def _ada_kernel(c_ref, w_ref, b_ref, o_ref):
    c = c_ref[...]
    s = (c * _sigmoid(c)).astype(BF16)
    o_ref[...] = _dot(s, w_ref[...].astype(BF16)) + b_ref[...]


def ada_modulation(c, w_ada, b_ada):
    bsz, d = c.shape
    depth = w_ada.shape[0]
    out = pl.pallas_call(
        _ada_kernel,
        grid=(depth, 6),
        in_specs=[
            pl.BlockSpec((bsz, d), lambda i, j: (0, 0)),
            pl.BlockSpec((None, d, d), lambda i, j: (i, 0, j)),
            pl.BlockSpec((None, None, 1, d), lambda i, j: (i, j, 0, 0)),
        ],
        out_specs=pl.BlockSpec((None, None, bsz, d), lambda i, j: (i, j, 0, 0)),
        out_shape=jax.ShapeDtypeStruct((depth, 6, bsz, d), F32),
        compiler_params=_cparams("arbitrary", "arbitrary"),
        name="ada_modulation",
    )(c, w_ada, b_ada.reshape(depth, 6, 1, d))
    return jnp.transpose(out, (0, 2, 1, 3))


def _tile_specs(seq, tt, d):
    nb8 = seq // SUBLANES
    per = tt // SUBLANES
    x_spec = pl.BlockSpec((None, tt, d), lambda b, t: (b, t, 0))
    prev_spec = pl.BlockSpec((None, SUBLANES, d), lambda b, t: (b, jnp.maximum(t * per - 1, 0), 0))
    next_spec = pl.BlockSpec((None, SUBLANES, d), lambda b, t: (b, jnp.minimum((t + 1) * per, nb8 - 1), 0))
    return x_spec, prev_spec, next_spec


def _mod_spec(d):
    return pl.BlockSpec((None, 6, d), lambda b, t: (b, 0, 0))


def _shifted(cur, prev_row, next_row, tt):
    row = lax.broadcasted_iota(jnp.int32, (tt, 1), 0)
    m1 = jnp.where(row == 0, prev_row, pltpu.roll(cur, 1, 0))
    p1 = jnp.where(row == tt - 1, next_row, pltpu.roll(cur, tt - 1, 0))
    return m1, p1


def _conv_kernel(x_ref, xp_ref, xn_ref, mod_ref, win_ref, cw_ref, wout_ref, lng_ref, lnb_ref,
                 o_ref, *, nt, tt, d):
    t = pl.program_id(1)
    sh, sc, gate = mod_ref[0:1, :], mod_ref[1:2, :], mod_ref[2:3, :]
    parts = [slice(i * tt // CONV_PARTS, (i + 1) * tt // CONV_PARTS) for i in range(CONV_PARTS)]
    xs = [x_ref[r, :] for r in parts]
    ps = [_dot((x * (1.0 + sc) + sh).astype(BF16), win_ref[...]) for x in xs]
    z = jnp.concatenate([p[:, d:2 * d] * p[:, 2 * d:] for p in ps], axis=0)
    halo = jnp.concatenate([xp_ref[...], xn_ref[...]], axis=0)
    hh = (halo * (1.0 + sc) + sh).astype(BF16)
    ph = _dot(hh, win_ref[:, d:])
    zh = ph[:, :d] * ph[:, d:]
    z_prev = jnp.where(t > 0, zh[SUBLANES - 1:SUBLANES, :], 0.0)
    z_next = jnp.where(t < nt - 1, zh[SUBLANES:SUBLANES + 1, :], 0.0)
    z_m1, z_p1 = _shifted(z, z_prev, z_next, tt)
    conv = z_m1 * cw_ref[0:1, :] + z * cw_ref[1:2, :] + z_p1 * cw_ref[2:3, :]
    mixes = [_dot((p[:, :d] * conv[r]).astype(BF16), wout_ref[...]) for p, r in zip(ps, parts)]
    for x, r, mix in zip(xs, parts, mixes):
        o_ref[r, :] = _layer_norm(ALPHA * x + gate * mix, lng_ref[...], lnb_ref[...])


CONV_PARTS = 4


def conv_mixer_layer(x, mod, w_in, conv_w, w_out, ln_g, ln_b, tt=1024):
    bsz, seq, d = x.shape
    tt = min(tt, seq)
    nt = seq // tt
    x_spec, prev_spec, next_spec = _tile_specs(seq, tt, d)
    return pl.pallas_call(
        functools.partial(_conv_kernel, nt=nt, tt=tt, d=d),
        grid=(bsz, nt),
        in_specs=[x_spec, prev_spec, next_spec, _mod_spec(d),
                  _const_spec((d, 3 * d)), _const_spec((3, d)), _const_spec((d, d)),
                  _const_spec((1, d)), _const_spec((1, d))],
        out_specs=pl.BlockSpec((None, tt, d), lambda b, t: (b, t, 0)),
        out_shape=jax.ShapeDtypeStruct(x.shape, F32),
        compiler_params=_cparams("parallel", "arbitrary"),
        name="conv_mixer",
    )(x, x, x, mod, w_in, conv_w, w_out, ln_g, ln_b)


def _ffn_kernel(x_ref, mod_ref, wgu_ref, wd_ref, lng_ref, lnb_ref, o_ref, *, ff, fc):
    sh, sc, gate = mod_ref[3:4, :], mod_ref[4:5, :], mod_ref[5:6, :]
    tt = x_ref.shape[0]
    parts = [slice(i * tt // FFN_PARTS, (i + 1) * tt // FFN_PARTS) for i in range(FFN_PARTS)]
    chunks = range(ff // fc)
    xs = [x_ref[r, :] for r in parts]
    hs = [(x * (1.0 + sc) + sh).astype(BF16) for x in xs]
    gs = [[_dot(h, wgu_ref[:, c * fc:(c + 1) * fc]) for c in chunks] for h in hs]
    us = [[_dot(h, wgu_ref[:, ff + c * fc:ff + (c + 1) * fc]) for c in chunks] for h in hs]
    acts = [[(g * _sigmoid(g) * u).astype(BF16) for g, u in zip(gr, ur)] for gr, ur in zip(gs, us)]
    for x, r, ar in zip(xs, parts, acts):
        acc = _dot(ar[0], wd_ref[0:fc, :])
        for c in chunks[1:]:
            acc = acc + _dot(ar[c], wd_ref[c * fc:(c + 1) * fc, :])
        o_ref[r, :] = _layer_norm(ALPHA * x + gate * acc, lng_ref[...], lnb_ref[...])


FFN_PARTS = 2


def dense_ffn_layer(x, mod, w_gu, w_down, ln_g, ln_b, tt=512):
    bsz, seq, d = x.shape
    tt = min(tt, seq)
    ff = w_down.shape[0]
    fc = ff
    return pl.pallas_call(
        functools.partial(_ffn_kernel, ff=ff, fc=fc),
        grid=(bsz, seq // tt),
        in_specs=[pl.BlockSpec((None, tt, d), lambda b, t: (b, t, 0)), _mod_spec(d),
                  _const_spec((d, 2 * ff)), _const_spec((ff, d)),
                  _const_spec((1, d)), _const_spec((1, d))],
        out_specs=pl.BlockSpec((None, tt, d), lambda b, t: (b, t, 0)),
        out_shape=jax.ShapeDtypeStruct(x.shape, F32),
        compiler_params=_cparams("parallel", "arbitrary"),
        name="dense_ffn",
    )(x, mod, w_gu, w_down, ln_g, ln_b)


MOE_ROW_CHUNK = 256
MOE_ROW_BLOCKS = (128, 256)


def _moe_kernel(x_ref, mod_ref, wr_ref, br_ref, wgu_ref, wd_ref, lng_ref, lnb_ref, o_ref,
                h_sc, comb_sc, acc_sc, rankc_sc, rankr_sc, maskr_sc, y_sc, *, fe, blocks):
    e = pl.program_id(2)
    tt = comb_sc.shape[0]
    lane = lax.broadcasted_iota(jnp.int32, comb_sc.shape, 1)

    @pl.when(e == 0)
    def _():
        sh, sc = mod_ref[3:4, :], mod_ref[4:5, :]
        h = x_ref[...] * (1.0 + sc) + sh
        h_hi = h.astype(BF16)
        h_sc[...] = h_hi
        h_lo = (h - h_hi.astype(F32)).astype(BF16)
        wr = wr_ref[...]
        w_hi = wr.astype(BF16)
        w_lo = (wr - w_hi.astype(F32)).astype(BF16)
        hh = _dot(h_hi, jnp.concatenate([w_hi, w_lo], axis=1))
        logits = hh[:, :ROUTER_LANES] + hh[:, ROUTER_LANES:] + _dot(h_lo, w_hi) + br_ref[...]
        m = jnp.max(logits, axis=-1, keepdims=True)
        ex = jnp.exp(logits - m)
        probs = ex / jnp.sum(ex, axis=-1, keepdims=True)
        valid = lane < N_EXPERTS
        p = jnp.where(valid, probs, -1.0)
        p1 = jnp.max(p, axis=-1, keepdims=True)
        i1 = jnp.min(jnp.where(p == p1, lane, ROUTER_LANES), axis=-1, keepdims=True)
        pr = jnp.where(lane == i1, -1.0, p)
        p2 = jnp.max(pr, axis=-1, keepdims=True)
        i2 = jnp.min(jnp.where(pr == p2, lane, ROUTER_LANES), axis=-1, keepdims=True)
        tot = p1 + p2
        comb = jnp.where(lane == i1, p1 / tot, 0.0) + jnp.where(lane == i2, p2 / tot, 0.0)
        comb_sc[...] = comb
        acc_sc[...] = jnp.zeros(acc_sc.shape, F32)
        y_sc[...] = jnp.zeros(y_sc.shape, BF16)
        routed = jnp.where(comb > 0.0, 1.0, 0.0).astype(BF16)
        q0 = lax.broadcasted_iota(jnp.int32, (ROUTER_LANES, ROUTER_LANES), 0)
        q1 = lax.broadcasted_iota(jnp.int32, (ROUTER_LANES, ROUTER_LANES), 1)
        routed_t = _dot_nt(jnp.where(q0 == q1, 1.0, 0.0).astype(BF16), routed)
        maskr_sc[...] = routed_t
        routed_tb = routed_t.astype(BF16)
        rc = min(MOE_ROW_CHUNK, tt)
        for c in range(tt // rc):
            r_i = lax.broadcasted_iota(jnp.int32, (rc, tt), 0) + c * rc
            c_i = lax.broadcasted_iota(jnp.int32, (rc, tt), 1)
            rankc_sc[c * rc:(c + 1) * rc, :] = _dot(jnp.where(c_i < r_i, 1.0, 0.0).astype(BF16), routed)
            r_j = lax.broadcasted_iota(jnp.int32, (tt, rc), 0)
            c_j = lax.broadcasted_iota(jnp.int32, (tt, rc), 1) + c * rc
            rankr_sc[:, c * rc:(c + 1) * rc] = _dot(routed_tb, jnp.where(r_j < c_j, 1.0, 0.0).astype(BF16))

    def expert(hb):
        g = _dot(hb, wgu_ref[:, :fe])
        u = _dot(hb, wgu_ref[:, fe:])
        act = (g * _sigmoid(g) * u).astype(BF16)
        return _dot(act, wd_ref[...])

    w_col = jnp.sum(jnp.where(lane == e, comb_sc[...], 0.0), axis=-1, keepdims=True)
    rk_col = jnp.sum(jnp.where(lane == e, rankc_sc[...], 0.0), axis=-1, keepdims=True).astype(jnp.int32)
    rk_row = rankr_sc[pl.ds(e, 1), :].astype(jnp.int32)
    mk_row = maskr_sc[pl.ds(e, 1), :]
    n_routed = jnp.sum(mk_row).astype(jnp.int32)
    half = tt // 2
    gb, sb = blocks

    def gather_block(j, carry):
        base = pl.multiple_of(j * gb, gb)
        sub = lax.broadcasted_iota(jnp.int32, (gb, tt), 0) + base
        pick = jnp.where((rk_row == sub) & (mk_row > 0.0), 1.0, 0.0).astype(BF16)
        y_sc[pl.ds(base, gb), :] = expert(_dot(pick, h_sc[...]).astype(BF16)).astype(BF16)
        return carry

    lax.fori_loop(0, (n_routed + gb - 1) // gb, gather_block, 0)

    for c in range(tt // sb):
        @pl.when(n_routed > c * sb)
        def _(c=c):
            for r0 in (0, half):
                rows = slice(r0, r0 + half)
                ln_i = lax.broadcasted_iota(jnp.int32, (half, sb), 1) + c * sb
                put = jnp.where((rk_col[rows] == ln_i) & (w_col[rows] > 0.0), 1.0, 0.0).astype(BF16)
                acc_sc[rows, :] += w_col[rows] * _dot(put, y_sc[c * sb:(c + 1) * sb, :])

    @pl.when(e == N_EXPERTS - 1)
    def _():
        gate = mod_ref[5:6, :]
        o_ref[...] = _layer_norm(ALPHA * x_ref[...] + gate * acc_sc[...], lng_ref[...], lnb_ref[...])


def moe_ffn_layer(x, mod, w_router, b_router, w_gu, w_down, ln_g, ln_b, tt=1024, blocks=MOE_ROW_BLOCKS):
    bsz, seq, d = x.shape
    tt = min(tt, seq)
    n_e, fe = w_down.shape[0], w_down.shape[1]
    return pl.pallas_call(
        functools.partial(_moe_kernel, fe=fe, blocks=tuple(min(b, tt) for b in blocks)),
        grid=(bsz, seq // tt, n_e),
        in_specs=[pl.BlockSpec((None, tt, d), lambda b, t, e: (b, t, 0)),
                  pl.BlockSpec((None, 6, d), lambda b, t, e: (b, 0, 0)),
                  _const_spec((d, ROUTER_LANES)), _const_spec((1, ROUTER_LANES)),
                  pl.BlockSpec((None, d, 2 * fe), lambda b, t, e: (e, 0, 0)),
                  pl.BlockSpec((None, fe, d), lambda b, t, e: (e, 0, 0)),
                  _const_spec((1, d)), _const_spec((1, d))],
        out_specs=pl.BlockSpec((None, tt, d), lambda b, t, e: (b, t, 0)),
        out_shape=jax.ShapeDtypeStruct(x.shape, F32),
        scratch_shapes=[pltpu.VMEM((tt, d), BF16), pltpu.VMEM((tt, ROUTER_LANES), F32),
                        pltpu.VMEM((tt, d), F32), pltpu.VMEM((tt, ROUTER_LANES), F32),
                        pltpu.VMEM((ROUTER_LANES, tt), F32), pltpu.VMEM((ROUTER_LANES, tt), F32),
                        pltpu.VMEM((tt, d), BF16)],
        compiler_params=pltpu.CompilerParams(
            dimension_semantics=("parallel", "arbitrary", "arbitrary"), vmem_limit_bytes=MOE_VMEM_LIMIT),
        name="moe_ffn",
    )(x, mod, w_router, b_router, w_gu, w_down, ln_g, ln_b)


def _pad_router(w_router, b_router):
    d, n_e = w_router.shape
    w = jnp.zeros((d, ROUTER_LANES), F32).at[:, :n_e].set(w_router)
    b = jnp.full((1, ROUTER_LANES), NEG_INF, F32).at[0, :n_e].set(b_router)
    return w, b


def _qkv_kernel(x_ref, mod_ref, w_ref, q_ref, k_ref, v_ref, *, d):
    sh, sc = mod_ref[0:1, :], mod_ref[1:2, :]
    h = (x_ref[...] * (1.0 + sc) + sh).astype(BF16)
    p = _dot(h, w_ref[...])
    q_ref[...] = (p[:, :d] * (NA_HEAD_DIM ** -0.5)).astype(BF16)
    k_ref[...] = p[:, d:2 * d].astype(BF16)
    v_ref[...] = p[:, 2 * d:].astype(BF16)


NA_ROWS_PER_STEP = 4


def _na_kernel(q_ref, k_ref, v_ref, *rest, rows):
    bias_refs, o_ref = rest[:-1], rest[-1]
    nk = NA_WIN_ROWS * GRID_W
    pw_ = 2 * NA_HEAD_DIM
    pairs = range(NA_HEADS // 2)
    lanes = [slice(p * pw_, (p + 1) * pw_) for p in pairs]
    head0 = lax.broadcasted_iota(jnp.int32, (GRID_W, pw_), 1) < NA_HEAD_DIM
    keys = [(i, p) for i in range(len(bias_refs)) for p in pairs]
    qrows = {i: slice(i * GRID_W, (i + 1) * GRID_W) for i in range(len(bias_refs))}
    start = {}
    for i in range(len(bias_refs)):
        r = pl.program_id(1) * len(bias_refs) + i
        r0 = jnp.clip(r - NA_WIN_ROWS // 2, 0, rows - NA_WIN_ROWS)
        start[i] = pl.multiple_of(r0 * GRID_W, GRID_W)
    q2 = {}
    for i, p in keys:
        q = q_ref[qrows[i], lanes[p]]
        z = jnp.zeros_like(q)
        q2[i, p] = jnp.concatenate([jnp.where(head0, q, z), jnp.where(head0, z, q)], axis=0)
    s = {(i, p): _dot_nt(q2[i, p], k_ref[pl.ds(start[i], nk), lanes[p]])
         + bias_refs[i][2 * p:2 * p + 2].reshape(2 * GRID_W, nk) for i, p in keys}
    m = {k: jnp.max(s[k], axis=-1, keepdims=True) for k in keys}
    e = {k: jnp.exp(s[k] - m[k]) for k in keys}
    l = {k: jnp.sum(e[k], axis=-1, keepdims=True) for k in keys}
    o = {(i, p): _dot(e[i, p].astype(BF16), v_ref[pl.ds(start[i], nk), lanes[p]]) / l[i, p] for i, p in keys}
    for i, p in keys:
        o_ref[qrows[i], lanes[p]] = jnp.where(head0, o[i, p][:GRID_W], o[i, p][GRID_W:]).astype(BF16)


def _na_bias_table(rpb):
    nh, nr, nc = rpb.shape
    w = GRID_W
    qc = jnp.arange(w)[:, None]
    kc = jnp.arange(w)[None, :]
    cs = jnp.clip(qc - NA_WIN_COLS // 2, 0, w - NA_WIN_COLS)
    col_ok = (kc >= cs) & (kc < cs + NA_WIN_COLS)
    lo = w - NA_WIN_COLS
    e = jnp.pad(rpb, ((0, 0), (0, 0), (lo, 2 * w - lo - nc)))
    flat = jnp.tile(e, (1, 1, w))
    toep = flat[:, :, w - 1:w - 1 + w * (2 * w - 1)].reshape(nh, nr, w, 2 * w - 1)[..., :w]
    toep = jnp.where(col_ok[None, None], toep, NEG_INF)
    tab = jnp.stack([toep[:, dl:dl + NA_WIN_ROWS] for dl in range(NA_WIN_ROWS)], axis=1)
    tab = jnp.transpose(tab, (0, 1, 3, 2, 4))
    return tab.reshape(nh, NA_WIN_ROWS, w, NA_WIN_ROWS * w)


def _proj_ln_kernel(*refs, n_in):
    a_refs = refs[:n_in]
    x_ref, mod_ref, w_ref, lng_ref, lnb_ref, o_ref = refs[n_in:]
    a = a_refs[0][...]
    if n_in == 2:
        a = (a.astype(F32) + a_refs[1][...].astype(F32)).astype(BF16)
    mix = _dot(a, w_ref[...])
    gate = mod_ref[2:3, :]
    o_ref[...] = _layer_norm(ALPHA * x_ref[...] + gate * mix, lng_ref[...], lnb_ref[...])


def proj_residual_ln(acts, x, mod, w_o, ln_g, ln_b, tt=1024):
    bsz, seq, d = x.shape
    tt = min(tt, seq)
    tile = pl.BlockSpec((None, tt, d), lambda b, t: (b, t, 0))
    return pl.pallas_call(
        functools.partial(_proj_ln_kernel, n_in=len(acts)),
        grid=(bsz, seq // tt),
        in_specs=[tile] * len(acts) + [tile, _mod_spec(d), _const_spec((d, d)),
                                       _const_spec((1, d)), _const_spec((1, d))],
        out_specs=tile,
        out_shape=jax.ShapeDtypeStruct(x.shape, F32),
        compiler_params=_cparams("parallel", "arbitrary"),
        name="proj_residual_ln",
    )(*acts, x, mod, w_o, ln_g, ln_b)


def na_mixer_layer(x, mod, w_qkv, bias_tab, w_o, ln_g, ln_b, tt=512):
    bsz, seq, d = x.shape
    tt = min(tt, seq)
    rows = seq // GRID_W
    assert rows >= NA_WIN_ROWS and seq % GRID_W == 0
    tile = pl.BlockSpec((None, tt, d), lambda b, t: (b, t, 0))
    q, k, v = pl.pallas_call(
        functools.partial(_qkv_kernel, d=d),
        grid=(bsz, seq // tt),
        in_specs=[tile, _mod_spec(d), _const_spec((d, 3 * d))],
        out_specs=[tile] * 3,
        out_shape=[jax.ShapeDtypeStruct(x.shape, BF16)] * 3,
        compiler_params=_cparams("parallel", "arbitrary"),
        name="na_qkv",
    )(x, mod, w_qkv)

    def delta(r):
        return jnp.clip(r - NA_WIN_ROWS // 2, 0, rows - NA_WIN_ROWS) - r + NA_WIN_ROWS - 1

    rps = NA_ROWS_PER_STEP
    assert rows % rps == 0
    row_spec = pl.BlockSpec((None, rps * GRID_W, d), lambda b, r: (b, r, 0))
    seq_spec = pl.BlockSpec((None, seq, d), lambda b, r: (b, 0, 0))
    bias_specs = [pl.BlockSpec((NA_HEADS, None, GRID_W, NA_WIN_ROWS * GRID_W),
                               lambda b, r, i=i: (0, delta(r * rps + i), 0, 0)) for i in range(rps)]
    att = pl.pallas_call(
        functools.partial(_na_kernel, rows=rows),
        grid=(bsz, rows // rps),
        in_specs=[row_spec, seq_spec, seq_spec] + bias_specs,
        out_specs=row_spec,
        out_shape=jax.ShapeDtypeStruct(x.shape, BF16),
        compiler_params=_cparams("parallel", "arbitrary"),
        name="na_attention",
    )(q, k, v, *([bias_tab] * rps))
    return proj_residual_ln([att], x, mod, w_o, ln_g, ln_b)


def _rk_proj_kernel(x_ref, xp_ref, xn_ref, mod_ref, mu_ref, wrkv_ref, w1_ref, w2_ref, a1_ref, a2_ref,
                    g1_ref, g2_ref, w0_ref, a0_ref, kk_ref, ka_ref,
                    r_o, v_o, kk_o, lw_o, km_o, b_o, gate_o, *, nt, tt):
    t = pl.program_id(1)
    sh, sc = mod_ref[0:1, :], mod_ref[1:2, :]
    h = x_ref[...] * (1.0 + sc) + sh
    h_prev = jnp.where(t > 0, xp_ref[SUBLANES - 1:SUBLANES, :] * (1.0 + sc) + sh, 0.0)
    h_next = jnp.where(t < nt - 1, xn_ref[0:1, :] * (1.0 + sc) + sh, 0.0)
    h_m1, h_p1 = _shifted(h, h_prev, h_next, tt)
    xx = 0.5 * (h_m1 + h_p1) - h

    h_b, xx_b, mu_b = h.astype(BF16), xx.astype(BF16), mu_ref[...].astype(BF16)

    def mixed(p):
        return h_b + xx_b * mu_b[p:p + 1, :]

    r = _dot(mixed(0), wrkv_ref[0])
    k = _dot(mixed(1), wrkv_ref[1])
    v = _dot(mixed(2), wrkv_ref[2])
    r_o[...] = r.astype(BF16)
    v_o[...] = v.astype(BF16)
    lw_in = jnp.tanh(_dot(mixed(3), w1_ref[...])).astype(BF16)
    a_in = _dot(mixed(4), a1_ref[...]).astype(BF16)
    g_in = _sigmoid(_dot(mixed(5), g1_ref[...])).astype(BF16)

    kk = k * kk_ref[...]
    kk_o[...] = kk.astype(BF16)
    rw = w1_ref.shape[1] // 2
    rg = g1_ref.shape[1] // 2
    for z in range(2):
        wl = w0_ref[z:z + 1, :] + _dot(lw_in[:, z * rw:(z + 1) * rw], w2_ref[z])
        lw_o[z] = (-RK_DECAY_SCALE) * _sigmoid(wl)
        a_lr = _sigmoid(a0_ref[z:z + 1, :] + _dot(a_in[:, z * rw:(z + 1) * rw], a2_ref[z]))
        km_o[z] = (k * (1.0 + (a_lr - 1.0) * ka_ref[...])).astype(BF16)
        b_o[z] = (kk * a_lr).astype(BF16)
        gate_o[z] = _dot(g_in[:, z * rg:(z + 1) * rg], g2_ref[z]).astype(BF16)


def _head_sum(x):
    head0 = lax.broadcasted_iota(jnp.int32, x.shape, 1) < RK_HEAD_DIM
    s0 = jnp.sum(jnp.where(head0, x, 0.0), axis=-1, keepdims=True)
    s1 = jnp.sum(jnp.where(head0, 0.0, x), axis=-1, keepdims=True)
    return jnp.where(head0, s0, s1)


def _rk_scan_kernel(r_ref, v_ref, kk_ref, lw_ref, km_ref, b_ref, gate_ref, rk_ref, lg_ref, lb_ref,
                    o_ref, s_ref, *, reverse):
    tb, hb = r_ref.shape
    cl = RK_CHUNK
    nc, nh = tb // cl, hb // RK_HEAD_DIM

    @pl.when(pl.program_id(2) == 0)
    def _():
        s_ref[...] = jnp.zeros(s_ref.shape, F32)

    ri = lax.broadcasted_iota(jnp.int32, (tb, tb), 0)
    ci = lax.broadcasted_iota(jnp.int32, (tb, tb), 1)
    ordered = (ci >= ri) if reverse else (ci <= ri)
    tri = jnp.where((ri // cl == ci // cl) & ordered, 1.0, 0.0).astype(BF16)
    lw = lw_ref[...]
    lw_hi = lw.astype(BF16)
    lw_lo = (lw - lw_hi.astype(F32)).astype(BF16)
    cum = _dot(tri, lw_hi) + _dot(tri, lw_lo)
    last = 0 if reverse else cl - 1
    tot_rows = [cum[c * cl + last:c * cl + last + 1, :] for c in range(nc)]
    tot = jnp.concatenate([jnp.broadcast_to(tr, (cl, hb)) for tr in tot_rows], axis=0)

    rr = r_ref[...].astype(F32)
    vv = v_ref[...]
    kk = kk_ref[...].astype(F32)
    pair = 2 * RK_HEAD_DIM
    rnorm = jnp.concatenate(
        [1.0 / jnp.maximum(jnp.sqrt(_head_sum(jnp.square(kk[:, i:i + pair]))), 1e-12) for i in range(0, hb, pair)],
        axis=1)
    kkn = kk * rnorm
    km = km_ref[...].astype(F32)
    bb = b_ref[...].astype(F32) * rnorm
    e_neg = jnp.exp(-cum)
    e_rem = jnp.exp(tot - cum)
    a_t = (-kkn * jnp.exp(cum - lw)).astype(BF16)
    r_t = (rr * jnp.exp(cum)).astype(BF16)
    b_t = (bb * e_neg).astype(BF16)
    k_t = (km * e_neg).astype(BF16)
    b_h = (bb * e_rem).astype(BF16)
    k_h = (km * e_rem).astype(BF16)
    bonus_w = rr * km * rk_ref[...]

    pw_ = 2 * RK_HEAD_DIM
    npair = hb // pw_
    t_i = lax.broadcasted_iota(jnp.int32, (cl, pw_), 0)
    s_i = lax.broadcasted_iota(jnp.int32, (cl, pw_), 1) % cl
    strict = (t_i < s_i) if reverse else (t_i > s_i)
    incl = (t_i <= s_i) if reverse else (t_i >= s_i)
    eye = jnp.where(t_i == s_i, 1.0, 0.0)
    head0 = lax.broadcasted_iota(jnp.int32, (cl, pw_), 1) < RK_HEAD_DIM
    q0 = lax.broadcasted_iota(jnp.int32, (pw_, pw_), 0)
    q1 = lax.broadcasted_iota(jnp.int32, (pw_, pw_), 1)
    same_head = (q0 // RK_HEAD_DIM) == (q1 // RK_HEAD_DIM)

    head_sum = _head_sum

    def bd(x):
        z = jnp.zeros_like(x)
        return jnp.concatenate([jnp.where(head0, x, z), jnp.where(head0, z, x)], axis=0)

    keys = [(c, p) for c in range(nc) for p in range(npair)]
    rsl = {k: slice(k[0] * cl, (k[0] + 1) * cl) for k in keys}
    lsl = {k: slice(k[1] * pw_, (k[1] + 1) * pw_) for k in keys}
    blk = lambda arr, k: arr[rsl[k], lsl[k]]

    gram = {k: _dot_nt(jnp.concatenate([blk(a_t, k), blk(r_t, k)], axis=0),
                       jnp.concatenate([bd(blk(b_t, k)), bd(blk(k_t, k))], axis=0)) for k in keys}
    m_ab = {k: jnp.where(strict, gram[k][:cl, :pw_], 0.0) for k in keys}
    m_ak = {k: jnp.where(strict, gram[k][:cl, pw_:], 0.0).astype(BF16) for k in keys}
    m_rb = {k: jnp.where(incl, gram[k][cl:, :pw_], 0.0).astype(BF16) for k in keys}
    m_rk = {k: jnp.where(incl, gram[k][cl:, pw_:], 0.0).astype(BF16) for k in keys}
    vbd = {k: bd(blk(vv, k)) for k in keys}
    my = {k: _dot(jnp.concatenate([m_ak[k], m_rk[k]], axis=0), vbd[k]) for k in keys}
    v_t = {k: jnp.transpose(blk(vv, k).astype(F32)).astype(BF16) for k in keys}
    vk = {k: jnp.where(same_head, _dot(v_t[k], blk(k_h, k)), 0.0) for k in keys}
    inv = {k: eye + m_ab[k] for k in keys}
    pwb = {k: m_ab[k].astype(BF16) for k in keys}
    pw = {k: _dot(pwb[k], bd(pwb[k])) for k in keys}
    n = 4
    while n < cl:
        pwb = {k: pw[k].astype(BF16) for k in keys}
        res = {k: _dot(pwb[k], jnp.concatenate([bd(pwb[k]), bd(inv[k].astype(BF16))], axis=1)) for k in keys}
        pw = {k: res[k][:, :pw_] for k in keys}
        inv = {k: inv[k] + res[k][:, pw_:] for k in keys}
        n *= 2
    inv = {k: (inv[k] + _dot(pw[k].astype(BF16), bd(inv[k].astype(BF16)))).astype(BF16) for k in keys}
    au = {k: _dot(inv[k], jnp.concatenate([bd(blk(a_t, k)), bd(my[k][:cl].astype(BF16))], axis=1))
          for k in keys}
    a_hat = {k: au[k][:, :pw_].astype(BF16) for k in keys}
    u0_t = {k: jnp.transpose(au[k][:, pw_:]) for k in keys}
    ar = {k: jnp.concatenate([a_hat[k], blk(r_t, k)], axis=0) for k in keys}

    pairs = range(npair)
    for c in (range(nc - 1, -1, -1) if reverse else range(nc)):
        st = {p: s_ref[p] for p in pairs}
        stb = {p: st[p].astype(BF16) for p in pairs}
        res = {p: _dot_nt(ar[c, p], stb[p]) for p in pairs}
        res_t = {p: _dot_nt(stb[p], a_hat[c, p]) for p in pairs}
        ub = {p: (au[c, p][:, pw_:] + res[p][:cl]).astype(BF16) for p in pairs}
        ub_t = {p: (u0_t[c, p] + res_t[p]).astype(BF16) for p in pairs}
        y = {p: my[c, p][cl:] + res[p][cl:] + _dot(m_rb[c, p], bd(ub[p])) for p in pairs}
        for p in pairs:
            k = (c, p)
            decay = jnp.exp(tot_rows[c][:, lsl[k]])
            s_ref[p] = jnp.where(same_head, st[p] * decay + _dot(ub_t[p], blk(b_h, k)), 0.0) + vk[k]
        dy = {p: y[p] - head_sum(y[p]) * (1.0 / RK_HEAD_DIM) for p in pairs}
        var = {p: head_sum(dy[p] * dy[p]) * (1.0 / RK_HEAD_DIM) for p in pairs}
        for p in pairs:
            k = (c, p)
            yn = dy[p] * lax.rsqrt(var[p] + RK_GN_EPS) * lg_ref[:, lsl[k]] + lb_ref[:, lsl[k]]
            bonus = head_sum(blk(bonus_w, k)) * blk(vv, k).astype(F32)
            o_ref[rsl[k], lsl[k]] = (blk(gate_ref, k).astype(F32) * (yn + bonus)).astype(BF16)


def rwkv_mixer_layer(x, mod, p, ln_g, ln_b, tt=256):
    bsz, seq, d = x.shape
    tt = min(tt, seq)
    nt = seq // tt
    x_spec, prev_spec, next_spec = _tile_specs(seq, tt, d)
    tile = pl.BlockSpec((None, tt, d), lambda b, t: (b, t, 0))
    tile2 = pl.BlockSpec((2, None, tt, d), lambda b, t: (0, b, t, 0))
    rw, rg = p['w1'].shape[1] // 2, p['g1'].shape[1] // 2
    sh_bf = jax.ShapeDtypeStruct(x.shape, BF16)
    sh2_bf = jax.ShapeDtypeStruct((2,) + x.shape, BF16)
    sh2_f32 = jax.ShapeDtypeStruct((2,) + x.shape, F32)
    r, v, kkn, lw, km, bz, gate = pl.pallas_call(
        functools.partial(_rk_proj_kernel, nt=nt, tt=tt),
        grid=(bsz, nt),
        in_specs=[x_spec, prev_spec, next_spec, _mod_spec(d), _const_spec((6, d)),
                  _const_spec((3, d, d)),
                  _const_spec((d, 2 * rw)), _const_spec((2, rw, d)),
                  _const_spec((d, 2 * rw)), _const_spec((2, rw, d)),
                  _const_spec((d, 2 * rg)), _const_spec((2, rg, d)),
                  _const_spec((2, d)), _const_spec((2, d)), _const_spec((1, d)), _const_spec((1, d))],
        out_specs=[tile, tile, tile, tile2, tile2, tile2, tile2],
        out_shape=[sh_bf, sh_bf, sh_bf, sh2_f32, sh2_bf, sh2_bf, sh2_bf],
        compiler_params=_cparams("parallel", "arbitrary"),
        name="rwkv_proj",
    )(x, x, x, mod, p['mu'], p['w_rkv'], p['w1'], p['w2'], p['a1'], p['a2'], p['g1'], p['g2'],
      p['w0'], p['a0'], p['k_k'], p['k_a'])

    tb = min(RK_TOK_BLOCK, seq)
    hb = RK_LANE_BLOCK
    nb = seq // tb
    outs = []
    for z in range(2):
        rev = z == 1

        def tok(i, rev=rev):
            return nb - 1 - i if rev else i

        blk = pl.BlockSpec((None, tb, hb), lambda b, g, i: (b, tok(i), g))
        blk2 = pl.BlockSpec((None, None, tb, hb), lambda b, g, i, z=z: (z, b, tok(i), g))
        vec = pl.BlockSpec((1, hb), lambda b, g, i: (0, g))
        vec2 = pl.BlockSpec((None, 1, hb), lambda b, g, i, z=z: (z, 0, g))
        outs.append(pl.pallas_call(
            functools.partial(_rk_scan_kernel, reverse=rev),
            grid=(bsz, d // hb, nb),
            in_specs=[blk, blk, blk, blk2, blk2, blk2, blk2, vec, vec2, vec2],
            out_specs=blk,
            out_shape=sh_bf,
            scratch_shapes=[pltpu.VMEM((hb // (2 * RK_HEAD_DIM), 2 * RK_HEAD_DIM, 2 * RK_HEAD_DIM), F32)],
            compiler_params=_cparams("parallel", "parallel", "arbitrary"),
            name="rwkv_scan_rev" if rev else "rwkv_scan_fwd",
        )(r, v, kkn, lw, km, bz, gate, p['r_k'], p['lnx_g'], p['lnx_b']))
    return proj_residual_ln(outs, x, mod, p['w_o'], ln_g, ln_b)


def _prepare_params(w):
    d = D_MODEL
    bf = lambda a: a.astype(BF16)
    per_layer = lambda a: [bf(a[j]) for j in range(a.shape[0])]
    row = lambda a: a.reshape(1, d)
    p = {
        'w_ada': w['w_ada'], 'b_ada': w['b_ada'],
        'ln_g': w['ln_g'], 'ln_b': w['ln_b'],
        'conv_w_in': per_layer(w['conv_w_in']), 'conv_w': w['conv_w'], 'conv_w_out': per_layer(w['conv_w_out']),
        'na_w_qkv': per_layer(w['na_w_qkv']), 'na_w_o': per_layer(w['na_w_o']),
        'na_bias': [_na_bias_table(w['na_rpb'][j]) for j in range(w['na_rpb'].shape[0])],
        'ffn_w_gu': per_layer(w['ffn_w_gu']), 'ffn_w_down': per_layer(w['ffn_w_down']),
        'moe_router': [_pad_router(w['moe_w_router'][j], w['moe_b_router'][j])
                       for j in range(w['moe_w_router'].shape[0])],
        'moe_w_gu': per_layer(w['moe_w_gu']), 'moe_w_down': per_layer(w['moe_w_down']),
        'rk': [],
    }
    cat = lambda a: jnp.concatenate([a[0], a[1]], axis=1)
    for j in range(w['rk_mu'].shape[0]):
        p['rk'].append({
            'mu': w['rk_mu'][j], 'w_rkv': bf(w['rk_w_rkv'][j]),
            'w1': bf(cat(w['rk_w1'][j])), 'w2': bf(w['rk_w2'][j]),
            'a1': bf(cat(w['rk_a1'][j])), 'a2': bf(w['rk_a2'][j]),
            'g1': bf(cat(w['rk_g1'][j])), 'g2': bf(w['rk_g2'][j]),
            'w0': w['rk_w0'][j], 'a0': w['rk_a0'][j],
            'k_k': row(w['rk_k_k'][j]), 'k_a': row(w['rk_k_a'][j]),
            'r_k': w['rk_r_k'][j].reshape(1, d),
            'lnx_g': w['rk_lnx_g'][j].reshape(2, 1, d), 'lnx_b': w['rk_lnx_b'][j].reshape(2, 1, d),
            'w_o': bf(w['rk_w_o'][j]),
        })
    return p


def _trunk(x, mod_all, p):
    d = D_MODEL
    for i in range(DEPTH):
        mod = mod_all[i]
        lng = lambda s: p['ln_g'][i, s].reshape(1, d)
        lnb = lambda s: p['ln_b'][i, s].reshape(1, d)
        kind, j = i % 3, i // 3
        if kind == 0:
            x = conv_mixer_layer(x, mod, p['conv_w_in'][j], p['conv_w'][j], p['conv_w_out'][j],
                                 lng(0), lnb(0))
        elif kind == 1:
            x = na_mixer_layer(x, mod, p['na_w_qkv'][j], p['na_bias'][j], p['na_w_o'][j], lng(0), lnb(0))
        else:
            x = rwkv_mixer_layer(x, mod, p['rk'][j], lng(0), lnb(0))
        if i % 2 == 0:
            x = dense_ffn_layer(x, mod, p['ffn_w_gu'][i // 2], p['ffn_w_down'][i // 2], lng(1), lnb(1))
        else:
            wr, br = p['moe_router'][i // 2]
            x = moe_ffn_layer(x, mod, wr, br, p['moe_w_gu'][i // 2], p['moe_w_down'][i // 2],
                              lng(1), lnb(1))
    return x


def kernel(x_prompt, x_sample, c_prompt, c_sample, w_ada, b_ada, ln_g, ln_b, conv_w_in, conv_w, conv_w_out, na_w_qkv, na_rpb, na_w_o, rk_mu, rk_w_rkv, rk_w0, rk_w1, rk_w2, rk_a0, rk_a1, rk_a2, rk_g1, rk_g2, rk_k_k, rk_k_a, rk_r_k, rk_lnx_g, rk_lnx_b, rk_w_o, ffn_w_gu, ffn_w_down, moe_w_router, moe_b_router, moe_w_gu, moe_w_down):
    p = _prepare_params(dict(
        w_ada=w_ada, b_ada=b_ada, ln_g=ln_g, ln_b=ln_b,
        conv_w_in=conv_w_in, conv_w=conv_w, conv_w_out=conv_w_out,
        na_w_qkv=na_w_qkv, na_rpb=na_rpb, na_w_o=na_w_o,
        rk_mu=rk_mu, rk_w_rkv=rk_w_rkv, rk_w0=rk_w0, rk_w1=rk_w1, rk_w2=rk_w2,
        rk_a0=rk_a0, rk_a1=rk_a1, rk_a2=rk_a2, rk_g1=rk_g1, rk_g2=rk_g2,
        rk_k_k=rk_k_k, rk_k_a=rk_k_a, rk_r_k=rk_r_k, rk_lnx_g=rk_lnx_g, rk_lnx_b=rk_lnx_b,
        rk_w_o=rk_w_o, ffn_w_gu=ffn_w_gu, ffn_w_down=ffn_w_down,
        moe_w_router=moe_w_router, moe_b_router=moe_b_router,
        moe_w_gu=moe_w_gu, moe_w_down=moe_w_down))
    n_p = c_prompt.shape[0]
    mod_all = ada_modulation(jnp.concatenate([c_prompt, c_sample], axis=0), p['w_ada'], p['b_ada'])
    return (_trunk(x_prompt, mod_all[:, :n_p], p), _trunk(x_sample, mod_all[:, n_p:], p))
```

```python
import functools

import jax
import jax.numpy as jnp
from jax import lax
from jax.experimental import pallas as pl
from jax.experimental.pallas import tpu as pltpu

F32 = jnp.float32
BF16 = jnp.bfloat16

D_MODEL = 1024
DEPTH = 4
ALPHA = (2 * DEPTH) ** 0.25
LN_EPS = 1e-5

GRID_W = 64
NA_HEADS = 16
NA_HEAD_DIM = D_MODEL // NA_HEADS
NA_WIN_ROWS = 8
NA_WIN_COLS = 16
NEG_INF = -1e30

RK_HEAD_DIM = 64
RK_GN_EPS = 64e-5
RK_DECAY_SCALE = 0.6065306597126334
RK_CHUNK = 64
RK_TOK_BLOCK = 256
RK_LANE_BLOCK = 1024

N_EXPERTS = 8
ROUTER_LANES = 128

SUBLANES = 8
VMEM_LIMIT = 52 * 1024 * 1024
MOE_VMEM_LIMIT = 58 * 1024 * 1024


def _cparams(*sem):
    return pltpu.CompilerParams(dimension_semantics=sem, vmem_limit_bytes=VMEM_LIMIT)


def _const_spec(shape):
    nd = len(shape)
    return pl.BlockSpec(shape, lambda *_: (0,) * nd, pipeline_mode=pl.Buffered(1))


def _dot(a, b):
    return jnp.dot(a, b, preferred_element_type=F32)


def _dot_nt(a, b):
    return lax.dot_general(a, b, (((1,), (1,)), ((), ())), preferred_element_type=F32)


def _sigmoid(x):
    return 1.0 / (1.0 + jnp.exp(-x))


def _layer_norm(y, g, b):
    mu = jnp.mean(y, axis=-1, keepdims=True)
    d = y - mu
    var = jnp.mean(d * d, axis=-1, keepdims=True)
    return d * lax.rsqrt(var + LN_EPS) * g + b


def _ada_kernel(c_ref, w_ref, b_ref, o_ref):
    c = c_ref[...]
    s = (c * _sigmoid(c)).astype(BF16)
    o_ref[...] = _dot(s, w_ref[...].astype(BF16)) + b_ref[...]


def ada_modulation(c, w_ada, b_ada):
    bsz, d = c.shape
    depth = w_ada.shape[0]
    out = pl.pallas_call(
        _ada_kernel,
        grid=(depth, 6),
        in_specs=[
            pl.BlockSpec((bsz, d), lambda i, j: (0, 0)),
            pl.BlockSpec((None, d, d), lambda i, j: (i, 0, j)),
            pl.BlockSpec((None, None, 1, d), lambda i, j: (i, j, 0, 0)),
        ],
        out_specs=pl.BlockSpec((None, None, bsz, d), lambda i, j: (i, j, 0, 0)),
        out_shape=jax.ShapeDtypeStruct((depth, 6, bsz, d), F32),
        compiler_params=_cparams("arbitrary", "arbitrary"),
        name="ada_modulation",
    )(c, w_ada, b_ada.reshape(depth, 6, 1, d))
    return jnp.transpose(out, (0, 2, 1, 3))


def _tile_specs(seq, tt, d):
    nb8 = seq // SUBLANES
    per = tt // SUBLANES
    x_spec = pl.BlockSpec((None, tt, d), lambda b, t: (b, t, 0))
    prev_spec = pl.BlockSpec((None, SUBLANES, d), lambda b, t: (b, jnp.maximum(t * per - 1, 0), 0))
    next_spec = pl.BlockSpec((None, SUBLANES, d), lambda b, t: (b, jnp.minimum((t + 1) * per, nb8 - 1), 0))
    return x_spec, prev_spec, next_spec


def _mod_spec(d):
    return pl.BlockSpec((None, 6, d), lambda b, t: (b, 0, 0))


def _shifted(cur, prev_row, next_row, tt):
    row = lax.broadcasted_iota(jnp.int32, (tt, 1), 0)
    m1 = jnp.where(row == 0, prev_row, pltpu.roll(cur, 1, 0))
    p1 = jnp.where(row == tt - 1, next_row, pltpu.roll(cur, tt - 1, 0))
    return m1, p1


def _conv_kernel(x_ref, xp_ref, xn_ref, mod_ref, win_ref, cw_ref, wout_ref, lng_ref, lnb_ref,
                 o_ref, *, nt, tt, d):
    t = pl.program_id(1)
    sh, sc, gate = mod_ref[0:1, :], mod_ref[1:2, :], mod_ref[2:3, :]
    parts = [slice(i * tt // CONV_PARTS, (i + 1) * tt // CONV_PARTS) for i in range(CONV_PARTS)]
    xs = [x_ref[r, :] for r in parts]
    ps = [_dot((x * (1.0 + sc) + sh).astype(BF16), win_ref[...]) for x in xs]
    z = jnp.concatenate([p[:, d:2 * d] * p[:, 2 * d:] for p in ps], axis=0)
    halo = jnp.concatenate([xp_ref[...], xn_ref[...]], axis=0)
    hh = (halo * (1.0 + sc) + sh).astype(BF16)
    ph = _dot(hh, win_ref[:, d:])
    zh = ph[:, :d] * ph[:, d:]
    z_prev = jnp.where(t > 0, zh[SUBLANES - 1:SUBLANES, :], 0.0)
    z_next = jnp.where(t < nt - 1, zh[SUBLANES:SUBLANES + 1, :], 0.0)
    z_m1, z_p1 = _shifted(z, z_prev, z_next, tt)
    conv = z_m1 * cw_ref[0:1, :] + z * cw_ref[1:2, :] + z_p1 * cw_ref[2:3, :]
    mixes = [_dot((p[:, :d] * conv[r]).astype(BF16), wout_ref[...]) for p, r in zip(ps, parts)]
    for x, r, mix in zip(xs, parts, mixes):
        o_ref[r, :] = _layer_norm(ALPHA * x + gate * mix, lng_ref[...], lnb_ref[...])


CONV_PARTS = 4


def conv_mixer_layer(x, mod, w_in, conv_w, w_out, ln_g, ln_b, tt=1024):
    bsz, seq, d = x.shape
    tt = min(tt, seq)
    nt = seq // tt
    x_spec, prev_spec, next_spec = _tile_specs(seq, tt, d)
    return pl.pallas_call(
        functools.partial(_conv_kernel, nt=nt, tt=tt, d=d),
        grid=(bsz, nt),
        in_specs=[x_spec, prev_spec, next_spec, _mod_spec(d),
                  _const_spec((d, 3 * d)), _const_spec((3, d)), _const_spec((d, d)),
                  _const_spec((1, d)), _const_spec((1, d))],
        out_specs=pl.BlockSpec((None, tt, d), lambda b, t: (b, t, 0)),
        out_shape=jax.ShapeDtypeStruct(x.shape, F32),
        compiler_params=_cparams("parallel", "arbitrary"),
        name="conv_mixer",
    )(x, x, x, mod, w_in, conv_w, w_out, ln_g, ln_b)


def _ffn_kernel(x_ref, mod_ref, wgu_ref, wd_ref, lng_ref, lnb_ref, o_ref, *, ff, fc):
    sh, sc, gate = mod_ref[3:4, :], mod_ref[4:5, :], mod_ref[5:6, :]
    tt = x_ref.shape[0]
    parts = [slice(i * tt // FFN_PARTS, (i + 1) * tt // FFN_PARTS) for i in range(FFN_PARTS)]
    chunks = range(ff // fc)
    xs = [x_ref[r, :] for r in parts]
    hs = [(x * (1.0 + sc) + sh).astype(BF16) for x in xs]
    gs = [[_dot(h, wgu_ref[:, c * fc:(c + 1) * fc]) for c in chunks] for h in hs]
    us = [[_dot(h, wgu_ref[:, ff + c * fc:ff + (c + 1) * fc]) for c in chunks] for h in hs]
    acts = [[(g * _sigmoid(g) * u).astype(BF16) for g, u in zip(gr, ur)] for gr, ur in zip(gs, us)]
    for x, r, ar in zip(xs, parts, acts):
        acc = _dot(ar[0], wd_ref[0:fc, :])
        for c in chunks[1:]:
            acc = acc + _dot(ar[c], wd_ref[c * fc:(c + 1) * fc, :])
        o_ref[r, :] = _layer_norm(ALPHA * x + gate * acc, lng_ref[...], lnb_ref[...])


FFN_PARTS = 2


def dense_ffn_layer(x, mod, w_gu, w_down, ln_g, ln_b, tt=512):
    bsz, seq, d = x.shape
    tt = min(tt, seq)
    ff = w_down.shape[0]
    fc = ff
    return pl.pallas_call(
        functools.partial(_ffn_kernel, ff=ff, fc=fc),
        grid=(bsz, seq // tt),
        in_specs=[pl.BlockSpec((None, tt, d), lambda b, t: (b, t, 0)), _mod_spec(d),
                  _const_spec((d, 2 * ff)), _const_spec((ff, d)),
                  _const_spec((1, d)), _const_spec((1, d))],
        out_specs=pl.BlockSpec((None, tt, d), lambda b, t: (b, t, 0)),
        out_shape=jax.ShapeDtypeStruct(x.shape, F32),
        compiler_params=_cparams("parallel", "arbitrary"),
        name="dense_ffn",
    )(x, mod, w_gu, w_down, ln_g, ln_b)


MOE_ROW_CHUNK = 256
MOE_ROW_BLOCKS = (128, 256)


def _moe_kernel(x_ref, mod_ref, wr_ref, br_ref, wgu_ref, wd_ref, lng_ref, lnb_ref, o_ref,
                h_sc, comb_sc, acc_sc, rankc_sc, rankr_sc, maskr_sc, y_sc, *, fe, blocks):
    e = pl.program_id(2)
    tt = comb_sc.shape[0]
    lane = lax.broadcasted_iota(jnp.int32, comb_sc.shape, 1)

    @pl.when(e == 0)
    def _():
        sh, sc = mod_ref[3:4, :], mod_ref[4:5, :]
        h = x_ref[...] * (1.0 + sc) + sh
        h_hi = h.astype(BF16)
        h_sc[...] = h_hi
        h_lo = (h - h_hi.astype(F32)).astype(BF16)
        wr = wr_ref[...]
        w_hi = wr.astype(BF16)
        w_lo = (wr - w_hi.astype(F32)).astype(BF16)
        hh = _dot(h_hi, jnp.concatenate([w_hi, w_lo], axis=1))
        logits = hh[:, :ROUTER_LANES] + hh[:, ROUTER_LANES:] + _dot(h_lo, w_hi) + br_ref[...]
        m = jnp.max(logits, axis=-1, keepdims=True)
        ex = jnp.exp(logits - m)
        probs = ex / jnp.sum(ex, axis=-1, keepdims=True)
        valid = lane < N_EXPERTS
        p = jnp.where(valid, probs, -1.0)
        p1 = jnp.max(p, axis=-1, keepdims=True)
        i1 = jnp.min(jnp.where(p == p1, lane, ROUTER_LANES), axis=-1, keepdims=True)
        pr = jnp.where(lane == i1, -1.0, p)
        p2 = jnp.max(pr, axis=-1, keepdims=True)
        i2 = jnp.min(jnp.where(pr == p2, lane, ROUTER_LANES), axis=-1, keepdims=True)
        tot = p1 + p2
        comb = jnp.where(lane == i1, p1 / tot, 0.0) + jnp.where(lane == i2, p2 / tot, 0.0)
        comb_sc[...] = comb
        acc_sc[...] = jnp.zeros(acc_sc.shape, F32)
        y_sc[...] = jnp.zeros(y_sc.shape, BF16)
        routed = jnp.where(comb > 0.0, 1.0, 0.0).astype(BF16)
        q0 = lax.broadcasted_iota(jnp.int32, (ROUTER_LANES, ROUTER_LANES), 0)
        q1 = lax.broadcasted_iota(jnp.int32, (ROUTER_LANES, ROUTER_LANES), 1)
        routed_t = _dot_nt(jnp.where(q0 == q1, 1.0, 0.0).astype(BF16), routed)
        maskr_sc[...] = routed_t
        routed_tb = routed_t.astype(BF16)
        rc = min(MOE_ROW_CHUNK, tt)
        for c in range(tt // rc):
            r_i = lax.broadcasted_iota(jnp.int32, (rc, tt), 0) + c * rc
            c_i = lax.broadcasted_iota(jnp.int32, (rc, tt), 1)
            rankc_sc[c * rc:(c + 1) * rc, :] = _dot(jnp.where(c_i < r_i, 1.0, 0.0).astype(BF16), routed)
            r_j = lax.broadcasted_iota(jnp.int32, (tt, rc), 0)
            c_j = lax.broadcasted_iota(jnp.int32, (tt, rc), 1) + c * rc
            rankr_sc[:, c * rc:(c + 1) * rc] = _dot(routed_tb, jnp.where(r_j < c_j, 1.0, 0.0).astype(BF16))

    def expert(hb):
        g = _dot(hb, wgu_ref[:, :fe])
        u = _dot(hb, wgu_ref[:, fe:])
        act = (g * _sigmoid(g) * u).astype(BF16)
        return _dot(act, wd_ref[...])

    w_col = jnp.sum(jnp.where(lane == e, comb_sc[...], 0.0), axis=-1, keepdims=True)
    rk_col = jnp.sum(jnp.where(lane == e, rankc_sc[...], 0.0), axis=-1, keepdims=True).astype(jnp.int32)
    rk_row = rankr_sc[pl.ds(e, 1), :].astype(jnp.int32)
    mk_row = maskr_sc[pl.ds(e, 1), :]
    n_routed = jnp.sum(mk_row).astype(jnp.int32)
    half = tt // 2
    gb, sb = blocks

    def gather_block(j, carry):
        base = pl.multiple_of(j * gb, gb)
        sub = lax.broadcasted_iota(jnp.int32, (gb, tt), 0) + base
        pick = jnp.where((rk_row == sub) & (mk_row > 0.0), 1.0, 0.0).astype(BF16)
        y_sc[pl.ds(base, gb), :] = expert(_dot(pick, h_sc[...]).astype(BF16)).astype(BF16)
        return carry

    lax.fori_loop(0, (n_routed + gb - 1) // gb, gather_block, 0)

    for c in range(tt // sb):
        @pl.when(n_routed > c * sb)
        def _(c=c):
            for r0 in (0, half):
                rows = slice(r0, r0 + half)
                ln_i = lax.broadcasted_iota(jnp.int32, (half, sb), 1) + c * sb
                put = jnp.where((rk_col[rows] == ln_i) & (w_col[rows] > 0.0), 1.0, 0.0).astype(BF16)
                acc_sc[rows, :] += w_col[rows] * _dot(put, y_sc[c * sb:(c + 1) * sb, :])

    @pl.when(e == N_EXPERTS - 1)
    def _():
        gate = mod_ref[5:6, :]
        o_ref[...] = _layer_norm(ALPHA * x_ref[...] + gate * acc_sc[...], lng_ref[...], lnb_ref[...])


def moe_ffn_layer(x, mod, w_router, b_router, w_gu, w_down, ln_g, ln_b, tt=1024, blocks=MOE_ROW_BLOCKS):
    bsz, seq, d = x.shape
    tt = min(tt, seq)
    n_e, fe = w_down.shape[0], w_down.shape[1]
    return pl.pallas_call(
        functools.partial(_moe_kernel, fe=fe, blocks=tuple(min(b, tt) for b in blocks)),
        grid=(bsz, seq // tt, n_e),
        in_specs=[pl.BlockSpec((None, tt, d), lambda b, t, e: (b, t, 0)),
                  pl.BlockSpec((None, 6, d), lambda b, t, e: (b, 0, 0)),
                  _const_spec((d, ROUTER_LANES)), _const_spec((1, ROUTER_LANES)),
                  pl.BlockSpec((None, d, 2 * fe), lambda b, t, e: (e, 0, 0)),
                  pl.BlockSpec((None, fe, d), lambda b, t, e: (e, 0, 0)),
                  _const_spec((1, d)), _const_spec((1, d))],
        out_specs=pl.BlockSpec((None, tt, d), lambda b, t, e: (b, t, 0)),
        out_shape=jax.ShapeDtypeStruct(x.shape, F32),
        scratch_shapes=[pltpu.VMEM((tt, d), BF16), pltpu.VMEM((tt, ROUTER_LANES), F32),
                        pltpu.VMEM((tt, d), F32), pltpu.VMEM((tt, ROUTER_LANES), F32),
                        pltpu.VMEM((ROUTER_LANES, tt), F32), pltpu.VMEM((ROUTER_LANES, tt), F32),
                        pltpu.VMEM((tt, d), BF16)],
        compiler_params=pltpu.CompilerParams(
            dimension_semantics=("parallel", "arbitrary", "arbitrary"), vmem_limit_bytes=MOE_VMEM_LIMIT),
        name="moe_ffn",
    )(x, mod, w_router, b_router, w_gu, w_down, ln_g, ln_b)


def _pad_router(w_router, b_router):
    d, n_e = w_router.shape
    w = jnp.zeros((d, ROUTER_LANES), F32).at[:, :n_e].set(w_router)
    b = jnp.full((1, ROUTER_LANES), NEG_INF, F32).at[0, :n_e].set(b_router)
    return w, b


def _qkv_kernel(x_ref, mod_ref, w_ref, q_ref, k_ref, v_ref, *, d):
    sh, sc = mod_ref[0:1, :], mod_ref[1:2, :]
    h = (x_ref[...] * (1.0 + sc) + sh).astype(BF16)
    p = _dot(h, w_ref[...])
    q_ref[...] = (p[:, :d] * (NA_HEAD_DIM ** -0.5)).astype(BF16)
    k_ref[...] = p[:, d:2 * d].astype(BF16)
    v_ref[...] = p[:, 2 * d:].astype(BF16)


NA_ROWS_PER_STEP = 4


def _na_kernel(q_ref, k_ref, v_ref, *rest, rows):
    bias_refs, o_ref = rest[:-1], rest[-1]
    nk = NA_WIN_ROWS * GRID_W
    pw_ = 2 * NA_HEAD_DIM
    pairs = range(NA_HEADS // 2)
    lanes = [slice(p * pw_, (p + 1) * pw_) for p in pairs]
    head0 = lax.broadcasted_iota(jnp.int32, (GRID_W, pw_), 1) < NA_HEAD_DIM
    keys = [(i, p) for i in range(len(bias_refs)) for p in pairs]
    qrows = {i: slice(i * GRID_W, (i + 1) * GRID_W) for i in range(len(bias_refs))}
    start = {}
    for i in range(len(bias_refs)):
        r = pl.program_id(1) * len(bias_refs) + i
        r0 = jnp.clip(r - NA_WIN_ROWS // 2, 0, rows - NA_WIN_ROWS)
        start[i] = pl.multiple_of(r0 * GRID_W, GRID_W)
    q2 = {}
    for i, p in keys:
        q = q_ref[qrows[i], lanes[p]]
        z = jnp.zeros_like(q)
        q2[i, p] = jnp.concatenate([jnp.where(head0, q, z), jnp.where(head0, z, q)], axis=0)
    s = {(i, p): _dot_nt(q2[i, p], k_ref[pl.ds(start[i], nk), lanes[p]])
         + bias_refs[i][2 * p:2 * p + 2].reshape(2 * GRID_W, nk) for i, p in keys}
    m = {k: jnp.max(s[k], axis=-1, keepdims=True) for k in keys}
    e = {k: jnp.exp(s[k] - m[k]) for k in keys}
    l = {k: jnp.sum(e[k], axis=-1, keepdims=True) for k in keys}
    o = {(i, p): _dot(e[i, p].astype(BF16), v_ref[pl.ds(start[i], nk), lanes[p]]) / l[i, p] for i, p in keys}
    for i, p in keys:
        o_ref[qrows[i], lanes[p]] = jnp.where(head0, o[i, p][:GRID_W], o[i, p][GRID_W:]).astype(BF16)


def _na_bias_table(rpb):
    nh, nr, nc = rpb.shape
    w = GRID_W
    qc = jnp.arange(w)[:, None]
    kc = jnp.arange(w)[None, :]
    cs = jnp.clip(qc - NA_WIN_COLS // 2, 0, w - NA_WIN_COLS)
    col_ok = (kc >= cs) & (kc < cs + NA_WIN_COLS)
    lo = w - NA_WIN_COLS
    e = jnp.pad(rpb, ((0, 0), (0, 0), (lo, 2 * w - lo - nc)))
    flat = jnp.tile(e, (1, 1, w))
    toep = flat[:, :, w - 1:w - 1 + w * (2 * w - 1)].reshape(nh, nr, w, 2 * w - 1)[..., :w]
    toep = jnp.where(col_ok[None, None], toep, NEG_INF)
    tab = jnp.stack([toep[:, dl:dl + NA_WIN_ROWS] for dl in range(NA_WIN_ROWS)], axis=1)
    tab = jnp.transpose(tab, (0, 1, 3, 2, 4))
    return tab.reshape(nh, NA_WIN_ROWS, w, NA_WIN_ROWS * w)


def _proj_ln_kernel(*refs, n_in):
    a_refs = refs[:n_in]
    x_ref, mod_ref, w_ref, lng_ref, lnb_ref, o_ref = refs[n_in:]
    a = a_refs[0][...]
    if n_in == 2:
        a = (a.astype(F32) + a_refs[1][...].astype(F32)).astype(BF16)
    mix = _dot(a, w_ref[...])
    gate = mod_ref[2:3, :]
    o_ref[...] = _layer_norm(ALPHA * x_ref[...] + gate * mix, lng_ref[...], lnb_ref[...])


def proj_residual_ln(acts, x, mod, w_o, ln_g, ln_b, tt=1024):
    bsz, seq, d = x.shape
    tt = min(tt, seq)
    tile = pl.BlockSpec((None, tt, d), lambda b, t: (b, t, 0))
    return pl.pallas_call(
        functools.partial(_proj_ln_kernel, n_in=len(acts)),
        grid=(bsz, seq // tt),
        in_specs=[tile] * len(acts) + [tile, _mod_spec(d), _const_spec((d, d)),
                                       _const_spec((1, d)), _const_spec((1, d))],
        out_specs=tile,
        out_shape=jax.ShapeDtypeStruct(x.shape, F32),
        compiler_params=_cparams("parallel", "arbitrary"),
        name="proj_residual_ln",
    )(*acts, x, mod, w_o, ln_g, ln_b)


def na_mixer_layer(x, mod, w_qkv, bias_tab, w_o, ln_g, ln_b, tt=512):
    bsz, seq, d = x.shape
    tt = min(tt, seq)
    rows = seq // GRID_W
    assert rows >= NA_WIN_ROWS and seq % GRID_W == 0
    tile = pl.BlockSpec((None, tt, d), lambda b, t: (b, t, 0))
    q, k, v = pl.pallas_call(
        functools.partial(_qkv_kernel, d=d),
        grid=(bsz, seq // tt),
        in_specs=[tile, _mod_spec(d), _const_spec((d, 3 * d))],
        out_specs=[tile] * 3,
        out_shape=[jax.ShapeDtypeStruct(x.shape, BF16)] * 3,
        compiler_params=_cparams("parallel", "arbitrary"),
        name="na_qkv",
    )(x, mod, w_qkv)

    def delta(r):
        return jnp.clip(r - NA_WIN_ROWS // 2, 0, rows - NA_WIN_ROWS) - r + NA_WIN_ROWS - 1

    rps = NA_ROWS_PER_STEP
    assert rows % rps == 0
    row_spec = pl.BlockSpec((None, rps * GRID_W, d), lambda b, r: (b, r, 0))
    seq_spec = pl.BlockSpec((None, seq, d), lambda b, r: (b, 0, 0))
    bias_specs = [pl.BlockSpec((NA_HEADS, None, GRID_W, NA_WIN_ROWS * GRID_W),
                               lambda b, r, i=i: (0, delta(r * rps + i), 0, 0)) for i in range(rps)]
    att = pl.pallas_call(
        functools.partial(_na_kernel, rows=rows),
        grid=(bsz, rows // rps),
        in_specs=[row_spec, seq_spec, seq_spec] + bias_specs,
        out_specs=row_spec,
        out_shape=jax.ShapeDtypeStruct(x.shape, BF16),
        compiler_params=_cparams("parallel", "arbitrary"),
        name="na_attention",
    )(q, k, v, *([bias_tab] * rps))
    return proj_residual_ln([att], x, mod, w_o, ln_g, ln_b)


def _rk_proj_kernel(x_ref, xp_ref, xn_ref, mod_ref, mu_ref, wrkv_ref, w1_ref, w2_ref, a1_ref, a2_ref,
                    g1_ref, g2_ref, w0_ref, a0_ref, kk_ref, ka_ref,
                    r_o, v_o, kk_o, lw_o, km_o, b_o, gate_o, *, nt, tt):
    t = pl.program_id(1)
    sh, sc = mod_ref[0:1, :], mod_ref[1:2, :]
    h = x_ref[...] * (1.0 + sc) + sh
    h_prev = jnp.where(t > 0, xp_ref[SUBLANES - 1:SUBLANES, :] * (1.0 + sc) + sh, 0.0)
    h_next = jnp.where(t < nt - 1, xn_ref[0:1, :] * (1.0 + sc) + sh, 0.0)
    h_m1, h_p1 = _shifted(h, h_prev, h_next, tt)
    xx = 0.5 * (h_m1 + h_p1) - h

    h_b, xx_b, mu_b = h.astype(BF16), xx.astype(BF16), mu_ref[...].astype(BF16)

    def mixed(p):
        return h_b + xx_b * mu_b[p:p + 1, :]

    r = _dot(mixed(0), wrkv_ref[0])
    k = _dot(mixed(1), wrkv_ref[1])
    v = _dot(mixed(2), wrkv_ref[2])
    r_o[...] = r.astype(BF16)
    v_o[...] = v.astype(BF16)
    lw_in = jnp.tanh(_dot(mixed(3), w1_ref[...])).astype(BF16)
    a_in = _dot(mixed(4), a1_ref[...]).astype(BF16)
    g_in = _sigmoid(_dot(mixed(5), g1_ref[...])).astype(BF16)

    kk = k * kk_ref[...]
    kk_o[...] = kk.astype(BF16)
    rw = w1_ref.shape[1] // 2
    rg = g1_ref.shape[1] // 2
    for z in range(2):
        wl = w0_ref[z:z + 1, :] + _dot(lw_in[:, z * rw:(z + 1) * rw], w2_ref[z])
        lw_o[z] = (-RK_DECAY_SCALE) * _sigmoid(wl)
        a_lr = _sigmoid(a0_ref[z:z + 1, :] + _dot(a_in[:, z * rw:(z + 1) * rw], a2_ref[z]))
        km_o[z] = (k * (1.0 + (a_lr - 1.0) * ka_ref[...])).astype(BF16)
        b_o[z] = (kk * a_lr).astype(BF16)
        gate_o[z] = _dot(g_in[:, z * rg:(z + 1) * rg], g2_ref[z]).astype(BF16)


def _head_sum(x):
    head0 = lax.broadcasted_iota(jnp.int32, x.shape, 1) < RK_HEAD_DIM
    s0 = jnp.sum(jnp.where(head0, x, 0.0), axis=-1, keepdims=True)
    s1 = jnp.sum(jnp.where(head0, 0.0, x), axis=-1, keepdims=True)
    return jnp.where(head0, s0, s1)


def _rk_scan_kernel(r_ref, v_ref, kk_ref, lw_ref, km_ref, b_ref, gate_ref, rk_ref, lg_ref, lb_ref,
                    *rest, reverse):
    o_ref, s_ref = rest[-2:]
    prev_ref = rest[0] if len(rest) == 3 else None
    tb, hb = r_ref.shape
    cl = RK_CHUNK
    nc, nh = tb // cl, hb // RK_HEAD_DIM

    @pl.when(pl.program_id(2) == 0)
    def _():
        s_ref[...] = jnp.zeros(s_ref.shape, F32)

    ri = lax.broadcasted_iota(jnp.int32, (tb, tb), 0)
    ci = lax.broadcasted_iota(jnp.int32, (tb, tb), 1)
    ordered = (ci >= ri) if reverse else (ci <= ri)
    tri = jnp.where((ri // cl == ci // cl) & ordered, 1.0, 0.0).astype(BF16)
    lw = lw_ref[...]
    lw_hi = lw.astype(BF16)
    lw_lo = (lw - lw_hi.astype(F32)).astype(BF16)
    cum = _dot(tri, lw_hi) + _dot(tri, lw_lo)
    last = 0 if reverse else cl - 1
    tot_rows = [cum[c * cl + last:c * cl + last + 1, :] for c in range(nc)]
    tot = jnp.concatenate([jnp.broadcast_to(tr, (cl, hb)) for tr in tot_rows], axis=0)

    rr = r_ref[...].astype(F32)
    vv = v_ref[...]
    kk = kk_ref[...].astype(F32)
    pair = 2 * RK_HEAD_DIM
    rnorm = jnp.concatenate(
        [1.0 / jnp.maximum(jnp.sqrt(_head_sum(jnp.square(kk[:, i:i + pair]))), 1e-12) for i in range(0, hb, pair)],
        axis=1)
    kkn = kk * rnorm
    km = km_ref[...].astype(F32)
    bb = b_ref[...].astype(F32) * rnorm
    e_neg = jnp.exp(-cum)
    e_rem = jnp.exp(tot - cum)
    a_t = (-kkn * jnp.exp(cum - lw)).astype(BF16)
    r_t = (rr * jnp.exp(cum)).astype(BF16)
    b_t = (bb * e_neg).astype(BF16)
    k_t = (km * e_neg).astype(BF16)
    b_h = (bb * e_rem).astype(BF16)
    k_h = (km * e_rem).astype(BF16)
    bonus_w = rr * km * rk_ref[...]

    pw_ = 2 * RK_HEAD_DIM
    npair = hb // pw_
    t_i = lax.broadcasted_iota(jnp.int32, (cl, pw_), 0)
    s_i = lax.broadcasted_iota(jnp.int32, (cl, pw_), 1) % cl
    strict = (t_i < s_i) if reverse else (t_i > s_i)
    incl = (t_i <= s_i) if reverse else (t_i >= s_i)
    eye = jnp.where(t_i == s_i, 1.0, 0.0)
    head0 = lax.broadcasted_iota(jnp.int32, (cl, pw_), 1) < RK_HEAD_DIM
    q0 = lax.broadcasted_iota(jnp.int32, (pw_, pw_), 0)
    q1 = lax.broadcasted_iota(jnp.int32, (pw_, pw_), 1)
    same_head = (q0 // RK_HEAD_DIM) == (q1 // RK_HEAD_DIM)

    head_sum = _head_sum

    def bd(x):
        z = jnp.zeros_like(x)
        return jnp.concatenate([jnp.where(head0, x, z), jnp.where(head0, z, x)], axis=0)

    keys = [(c, p) for c in range(nc) for p in range(npair)]
    rsl = {k: slice(k[0] * cl, (k[0] + 1) * cl) for k in keys}
    lsl = {k: slice(k[1] * pw_, (k[1] + 1) * pw_) for k in keys}
    blk = lambda arr, k: arr[rsl[k], lsl[k]]

    gram = {k: _dot_nt(jnp.concatenate([blk(a_t, k), blk(r_t, k)], axis=0),
                       jnp.concatenate([bd(blk(b_t, k)), bd(blk(k_t, k))], axis=0)) for k in keys}
    m_ab = {k: jnp.where(strict, gram[k][:cl, :pw_], 0.0) for k in keys}
    m_ak = {k: jnp.where(strict, gram[k][:cl, pw_:], 0.0).astype(BF16) for k in keys}
    m_rb = {k: jnp.where(incl, gram[k][cl:, :pw_], 0.0).astype(BF16) for k in keys}
    m_rk = {k: jnp.where(incl, gram[k][cl:, pw_:], 0.0).astype(BF16) for k in keys}
    vbd = {k: bd(blk(vv, k)) for k in keys}
    my = {k: _dot(jnp.concatenate([m_ak[k], m_rk[k]], axis=0), vbd[k]) for k in keys}
    v_t = {k: jnp.transpose(blk(vv, k).astype(F32)).astype(BF16) for k in keys}
    vk = {k: jnp.where(same_head, _dot(v_t[k], blk(k_h, k)), 0.0) for k in keys}
    inv = {k: eye + m_ab[k] for k in keys}
    pwb = {k: m_ab[k].astype(BF16) for k in keys}
    pw = {k: _dot(pwb[k], bd(pwb[k])) for k in keys}
    n = 4
    while n < cl:
        pwb = {k: pw[k].astype(BF16) for k in keys}
        res = {k: _dot(pwb[k], jnp.concatenate([bd(pwb[k]), bd(inv[k].astype(BF16))], axis=1)) for k in keys}
        pw = {k: res[k][:, :pw_] for k in keys}
        inv = {k: inv[k] + res[k][:, pw_:] for k in keys}
        n *= 2
    inv = {k: (inv[k] + _dot(pw[k].astype(BF16), bd(inv[k].astype(BF16)))).astype(BF16) for k in keys}
    au = {k: _dot(inv[k], jnp.concatenate([bd(blk(a_t, k)), bd(my[k][:cl].astype(BF16))], axis=1))
          for k in keys}
    a_hat = {k: au[k][:, :pw_].astype(BF16) for k in keys}
    u0_t = {k: jnp.transpose(au[k][:, pw_:]) for k in keys}
    ar = {k: jnp.concatenate([a_hat[k], blk(r_t, k)], axis=0) for k in keys}

    pairs = range(npair)
    for c in (range(nc - 1, -1, -1) if reverse else range(nc)):
        st = {p: s_ref[p] for p in pairs}
        stb = {p: st[p].astype(BF16) for p in pairs}
        res = {p: _dot_nt(ar[c, p], stb[p]) for p in pairs}
        res_t = {p: _dot_nt(stb[p], a_hat[c, p]) for p in pairs}
        ub = {p: (au[c, p][:, pw_:] + res[p][:cl]).astype(BF16) for p in pairs}
        ub_t = {p: (u0_t[c, p] + res_t[p]).astype(BF16) for p in pairs}
        y = {p: my[c, p][cl:] + res[p][cl:] + _dot(m_rb[c, p], bd(ub[p])) for p in pairs}
        for p in pairs:
            k = (c, p)
            decay = jnp.exp(tot_rows[c][:, lsl[k]])
            s_ref[p] = jnp.where(same_head, st[p] * decay + _dot(ub_t[p], blk(b_h, k)), 0.0) + vk[k]
        dy = {p: y[p] - head_sum(y[p]) * (1.0 / RK_HEAD_DIM) for p in pairs}
        var = {p: head_sum(dy[p] * dy[p]) * (1.0 / RK_HEAD_DIM) for p in pairs}
        for p in pairs:
            k = (c, p)
            yn = dy[p] * lax.rsqrt(var[p] + RK_GN_EPS) * lg_ref[:, lsl[k]] + lb_ref[:, lsl[k]]
            bonus = head_sum(blk(bonus_w, k)) * blk(vv, k).astype(F32)
            out = blk(gate_ref, k).astype(F32) * (yn + bonus)
            if prev_ref is not None:
                out = out + blk(prev_ref, k).astype(F32)
            o_ref[rsl[k], lsl[k]] = out.astype(BF16)


def rwkv_mixer_layer(x, mod, p, ln_g, ln_b, tt=256):
    bsz, seq, d = x.shape
    tt = min(tt, seq)
    nt = seq // tt
    x_spec, prev_spec, next_spec = _tile_specs(seq, tt, d)
    tile = pl.BlockSpec((None, tt, d), lambda b, t: (b, t, 0))
    tile2 = pl.BlockSpec((2, None, tt, d), lambda b, t: (0, b, t, 0))
    rw, rg = p['w1'].shape[1] // 2, p['g1'].shape[1] // 2
    sh_bf = jax.ShapeDtypeStruct(x.shape, BF16)
    sh2_bf = jax.ShapeDtypeStruct((2,) + x.shape, BF16)
    sh2_f32 = jax.ShapeDtypeStruct((2,) + x.shape, F32)
    r, v, kkn, lw, km, bz, gate = pl.pallas_call(
        functools.partial(_rk_proj_kernel, nt=nt, tt=tt),
        grid=(bsz, nt),
        in_specs=[x_spec, prev_spec, next_spec, _mod_spec(d), _const_spec((6, d)),
                  _const_spec((3, d, d)),
                  _const_spec((d, 2 * rw)), _const_spec((2, rw, d)),
                  _const_spec((d, 2 * rw)), _const_spec((2, rw, d)),
                  _const_spec((d, 2 * rg)), _const_spec((2, rg, d)),
                  _const_spec((2, d)), _const_spec((2, d)), _const_spec((1, d)), _const_spec((1, d))],
        out_specs=[tile, tile, tile, tile2, tile2, tile2, tile2],
        out_shape=[sh_bf, sh_bf, sh_bf, sh2_f32, sh2_bf, sh2_bf, sh2_bf],
        compiler_params=_cparams("parallel", "arbitrary"),
        name="rwkv_proj",
    )(x, x, x, mod, p['mu'], p['w_rkv'], p['w1'], p['w2'], p['a1'], p['a2'], p['g1'], p['g2'],
      p['w0'], p['a0'], p['k_k'], p['k_a'])

    tb = min(RK_TOK_BLOCK, seq)
    hb = RK_LANE_BLOCK
    nb = seq // tb
    outs = []
    for z in range(2):
        rev = z == 1

        def tok(i, rev=rev):
            return nb - 1 - i if rev else i

        blk = pl.BlockSpec((None, tb, hb), lambda b, g, i: (b, tok(i), g))
        blk2 = pl.BlockSpec((None, None, tb, hb), lambda b, g, i, z=z: (z, b, tok(i), g))
        vec = pl.BlockSpec((1, hb), lambda b, g, i: (0, g))
        vec2 = pl.BlockSpec((None, 1, hb), lambda b, g, i, z=z: (z, 0, g))
        outs.append(pl.pallas_call(
            functools.partial(_rk_scan_kernel, reverse=rev),
            grid=(bsz, d // hb, nb),
            in_specs=[blk, blk, blk, blk2, blk2, blk2, blk2, vec, vec2, vec2] + [blk] * len(outs),
            out_specs=blk,
            out_shape=sh_bf,
            scratch_shapes=[pltpu.VMEM((hb // (2 * RK_HEAD_DIM), 2 * RK_HEAD_DIM, 2 * RK_HEAD_DIM), F32)],
            compiler_params=_cparams("parallel", "parallel", "arbitrary"),
            name="rwkv_scan_rev" if rev else "rwkv_scan_fwd",
        )(r, v, kkn, lw, km, bz, gate, p['r_k'], p['lnx_g'], p['lnx_b'], *outs))
    return proj_residual_ln(outs[-1:], x, mod, p['w_o'], ln_g, ln_b)


def _prepare_params(w):
    d = D_MODEL
    bf = lambda a: a.astype(BF16)
    per_layer = lambda a: [bf(a[j]) for j in range(a.shape[0])]
    row = lambda a: a.reshape(1, d)
    p = {
        'w_ada': w['w_ada'], 'b_ada': w['b_ada'],
        'ln_g': w['ln_g'], 'ln_b': w['ln_b'],
        'conv_w_in': per_layer(w['conv_w_in']), 'conv_w': w['conv_w'], 'conv_w_out': per_layer(w['conv_w_out']),
        'na_w_qkv': per_layer(w['na_w_qkv']), 'na_w_o': per_layer(w['na_w_o']),
        'na_bias': [_na_bias_table(w['na_rpb'][j]) for j in range(w['na_rpb'].shape[0])],
        'ffn_w_gu': per_layer(w['ffn_w_gu']), 'ffn_w_down': per_layer(w['ffn_w_down']),
        'moe_router': [_pad_router(w['moe_w_router'][j], w['moe_b_router'][j])
                       for j in range(w['moe_w_router'].shape[0])],
        'moe_w_gu': per_layer(w['moe_w_gu']), 'moe_w_down': per_layer(w['moe_w_down']),
        'rk': [],
    }
    cat = lambda a: jnp.concatenate([a[0], a[1]], axis=1)
    for j in range(w['rk_mu'].shape[0]):
        p['rk'].append({
            'mu': w['rk_mu'][j], 'w_rkv': bf(w['rk_w_rkv'][j]),
            'w1': bf(cat(w['rk_w1'][j])), 'w2': bf(w['rk_w2'][j]),
            'a1': bf(cat(w['rk_a1'][j])), 'a2': bf(w['rk_a2'][j]),
            'g1': bf(cat(w['rk_g1'][j])), 'g2': bf(w['rk_g2'][j]),
            'w0': w['rk_w0'][j], 'a0': w['rk_a0'][j],
            'k_k': row(w['rk_k_k'][j]), 'k_a': row(w['rk_k_a'][j]),
            'r_k': w['rk_r_k'][j].reshape(1, d),
            'lnx_g': w['rk_lnx_g'][j].reshape(2, 1, d), 'lnx_b': w['rk_lnx_b'][j].reshape(2, 1, d),
            'w_o': bf(w['rk_w_o'][j]),
        })
    return p


def _trunk(x, mod_all, p):
    d = D_MODEL
    for i in range(DEPTH):
        mod = mod_all[i]
        lng = lambda s: p['ln_g'][i, s].reshape(1, d)
        lnb = lambda s: p['ln_b'][i, s].reshape(1, d)
        kind, j = i % 3, i // 3
        if kind == 0:
            x = conv_mixer_layer(x, mod, p['conv_w_in'][j], p['conv_w'][j], p['conv_w_out'][j],
                                 lng(0), lnb(0))
        elif kind == 1:
            x = na_mixer_layer(x, mod, p['na_w_qkv'][j], p['na_bias'][j], p['na_w_o'][j], lng(0), lnb(0))
        else:
            x = rwkv_mixer_layer(x, mod, p['rk'][j], lng(0), lnb(0))
        if i % 2 == 0:
            x = dense_ffn_layer(x, mod, p['ffn_w_gu'][i // 2], p['ffn_w_down'][i // 2], lng(1), lnb(1))
        else:
            wr, br = p['moe_router'][i // 2]
            x = moe_ffn_layer(x, mod, wr, br, p['moe_w_gu'][i // 2], p['moe_w_down'][i // 2],
                              lng(1), lnb(1))
    return x


def kernel(x_prompt, x_sample, c_prompt, c_sample, w_ada, b_ada, ln_g, ln_b, conv_w_in, conv_w, conv_w_out, na_w_qkv, na_rpb, na_w_o, rk_mu, rk_w_rkv, rk_w0, rk_w1, rk_w2, rk_a0, rk_a1, rk_a2, rk_g1, rk_g2, rk_k_k, rk_k_a, rk_r_k, rk_lnx_g, rk_lnx_b, rk_w_o, ffn_w_gu, ffn_w_down, moe_w_router, moe_b_router, moe_w_gu, moe_w_down):
    p = _prepare_params(dict(
        w_ada=w_ada, b_ada=b_ada, ln_g=ln_g, ln_b=ln_b,
        conv_w_in=conv_w_in, conv_w=conv_w, conv_w_out=conv_w_out,
        na_w_qkv=na_w_qkv, na_rpb=na_rpb, na_w_o=na_w_o,
        rk_mu=rk_mu, rk_w_rkv=rk_w_rkv, rk_w0=rk_w0, rk_w1=rk_w1, rk_w2=rk_w2,
        rk_a0=rk_a0, rk_a1=rk_a1, rk_a2=rk_a2, rk_g1=rk_g1, rk_g2=rk_g2,
        rk_k_k=rk_k_k, rk_k_a=rk_k_a, rk_r_k=rk_r_k, rk_lnx_g=rk_lnx_g, rk_lnx_b=rk_lnx_b,
        rk_w_o=rk_w_o, ffn_w_gu=ffn_w_gu, ffn_w_down=ffn_w_down,
        moe_w_router=moe_w_router, moe_b_router=moe_b_router,
        moe_w_gu=moe_w_gu, moe_w_down=moe_w_down))
    n_p = c_prompt.shape[0]
    mod_all = ada_modulation(jnp.concatenate([c_prompt, c_sample], axis=0), p['w_ada'], p['b_ada'])
    return (_trunk(x_prompt, mod_all[:, :n_p], p), _trunk(x_sample, mod_all[:, n_p:], p))
```
